```python
import jax, jax.numpy as jnp
from jax import lax
import numpy as np

D_MODEL = 2048
BATCH = 8
SEQ = 2048
DEPTH = 2

D_MIX = D_MODEL
D_LRU = D_MIX // 2
D_RWKV = D_MIX - D_LRU
LRU_HEADS = 4
LRU_BLOCK = D_LRU // LRU_HEADS
LRU_CONV = 4
LRU_C = 8.0
RWKV_HEAD = 64
RWKV_HEADS = D_RWKV // RWKV_HEAD
LORA_W = 64
LORA_A = 64
LORA_V = 32
LORA_G = 160
N_SHIFT = 3 * D_RWKV + LORA_W + LORA_A + LORA_G
D_IN = 2 * D_LRU + N_SHIFT
D_FF = 3 * D_MODEL
FFN_CONV = 3
D_PLE = 256
RMS_EPS = 1e-6
LNX_EPS = 64e-5

kernel_name = 'hybrid_rglru_rwkv7_parallel_heads'


def rmsnorm(x, g):
    xf = x.astype(jnp.float32)
    y = xf * lax.rsqrt(jnp.mean(xf * xf, axis=-1, keepdims=True) + RMS_EPS)
    return y.astype(x.dtype) * g


def causal_dwconv(x, w, b):
    K = w.shape[0]
    S = x.shape[1]
    xp = jnp.pad(x, ((0, 0), (K - 1, 0), (0, 0)))
    out = xp[:, K - 1:K - 1 + S] * w[K - 1] + b
    for j in range(K - 1):
        out = out + xp[:, j:j + S] * w[j]
    return out


def token_shift(z):
    return jnp.pad(z, ((0, 0), (1, 0), (0, 0)))[:, :-1]


def _linear_combine(left, right):
    a_l, b_l = left
    a_r, b_r = right
    return a_l * a_r, a_r * b_l + b_r


def rg_lru(xc, wx, bx, wa, ba, lam):
    B, S, _ = xc.shape
    xh = xc.reshape(B, S, LRU_HEADS, LRU_BLOCK)
    gate_x = jax.nn.sigmoid(jnp.einsum('bshi,hij->bshj', xh, wx).reshape(B, S, D_LRU) + bx)
    gate_a = jax.nn.sigmoid(jnp.einsum('bshi,hij->bshj', xh, wa).reshape(B, S, D_LRU) + ba)
    log_a = (-LRU_C * gate_a * jax.nn.softplus(-lam)).astype(jnp.float32)
    a = jnp.exp(log_a)
    mult = jnp.sqrt(1.0 - jnp.exp(2.0 * log_a))
    mult = jnp.where((jnp.arange(S) == 0)[None, :, None], 1.0, mult)
    b_in = (xc * gate_x).astype(jnp.float32) * mult
    _, h = lax.associative_scan(_linear_combine, (a, b_in), axis=1)
    return h.astype(xc.dtype)


def wkv7_scan(r, decay, k, v, kk, kka):
    B, S, H, N = r.shape

    def step(state, inp):
        r_t, w_t, k_t, v_t, kk_t, b_t = inp
        sa = jnp.einsum('bhvk,bhk->bhv', state, -kk_t)
        state = (state * w_t[:, :, None, :] + sa[..., None] * b_t[:, :, None, :]
                 + v_t[..., None] * k_t[:, :, None, :])
        y = jnp.einsum('bhvk,bhk->bhv', state, r_t)
        return state, y

    s0 = jnp.zeros((B, H, N, N), jnp.float32)
    xs = tuple(jnp.swapaxes(t, 0, 1) for t in (r, decay, k, v, kk, kka))
    _, ys = lax.scan(step, s0, xs)
    return jnp.swapaxes(ys, 0, 1)


def setup_inputs(seed: int = 0) -> dict:
    key = jax.random.key(seed)
    ks = iter(jax.random.split(key, 64))
    f32 = jnp.float32
    L = DEPTH
    Lv = DEPTH - 1

    def nrm(shape, scale):
        return scale * jax.random.normal(next(ks), shape, f32)

    def gain(shape, base=1.0):
        return base + 0.02 * jax.random.normal(next(ks), shape, f32)

    x = jax.random.normal(next(ks), (BATCH, SEQ, D_MODEL), f32)
    p = jax.random.normal(next(ks), (DEPTH, BATCH, SEQ, D_PLE), f32)

    a_target = jax.random.uniform(next(ks), (L, D_LRU), f32, 0.9, 0.999)
    s = a_target ** (1.0 / LRU_C)
    lru_lambda = jnp.log(s) - jnp.log1p(-s)

    ratio = jnp.arange(D_RWKV, dtype=f32) / (D_RWKV - 1)
    rwkv_w0 = (-5.5 + 5.0 * ratio ** 0.9)[None, :] + nrm((L, D_RWKV), 0.1)

    return {
        'x': x,
        'p': p,
        'ln_mix': gain((L, D_MODEL)),
        'w_in': nrm((L, D_MODEL, D_IN), D_MODEL ** -0.5),
        'w_in_vres': nrm((Lv, D_MODEL, LORA_V), D_MODEL ** -0.5),
        'mu_shift': jax.random.uniform(next(ks), (L, N_SHIFT), f32),
        'mu_shift_vres': jax.random.uniform(next(ks), (Lv, LORA_V), f32),
        'conv_a_w': nrm((L, LRU_CONV, D_LRU), LRU_CONV ** -0.5),
        'conv_a_b': nrm((L, D_LRU), 0.02),
        'lru_wx': nrm((L, LRU_HEADS, LRU_BLOCK, LRU_BLOCK), LRU_BLOCK ** -0.5),
        'lru_bx': nrm((L, D_LRU), 0.02),
        'lru_wa': nrm((L, LRU_HEADS, LRU_BLOCK, LRU_BLOCK), LRU_BLOCK ** -0.5),
        'lru_ba': nrm((L, D_LRU), 0.02),
        'lru_lambda': lru_lambda,
        'lru_norm': gain((L, D_LRU)),
        'rwkv_w0': rwkv_w0,
        'rwkv_w2': nrm((L, LORA_W, D_RWKV), 0.5 * LORA_W ** -0.5),
        'rwkv_a0': nrm((L, D_RWKV), 0.1),
        'rwkv_a2': nrm((L, LORA_A, D_RWKV), LORA_A ** -0.5),
        'rwkv_v0': gain((Lv, D_RWKV)),
        'rwkv_v2': nrm((Lv, LORA_V, D_RWKV), LORA_V ** -0.5),
        'rwkv_g2': nrm((L, LORA_G, D_RWKV), LORA_G ** -0.5),
        'rwkv_kk': gain((L, D_RWKV), 0.85),
        'rwkv_ka': gain((L, D_RWKV)),
        'rwkv_rk': nrm((L, RWKV_HEADS, RWKV_HEAD), 0.1),
        'rwkv_lnx_w': gain((L, D_RWKV)),
        'rwkv_lnx_b': nrm((L, D_RWKV), 0.02),
        'w_o': nrm((L, D_MIX, D_MODEL), D_MIX ** -0.5),
        'ln_ffn': gain((L, D_MODEL)),
        'w_gate': nrm((L, D_MODEL, D_FF), D_MODEL ** -0.5),
        'w_up': nrm((L, D_MODEL, D_FF), D_MODEL ** -0.5),
        'conv_f_w': nrm((L, FFN_CONV, D_FF), FFN_CONV ** -0.5),
        'conv_f_b': nrm((L, D_FF), 0.02),
        'w_down': nrm((L, D_FF, D_MODEL), D_FF ** -0.5),
        'ln_ple': gain((L, D_MODEL)),
        'w_ple_gate': nrm((L, D_MODEL, D_MODEL), D_MODEL ** -0.5),
        'w_ple_proj': nrm((L, D_PLE, D_MODEL), D_PLE ** -0.5),
        'ln_ple_post': gain((L, D_MODEL)),
        'ln_final': gain((D_MODEL,)),
    }


def reference(x, p, ln_mix, w_in, w_in_vres, mu_shift, mu_shift_vres, conv_a_w, conv_a_b,
              lru_wx, lru_bx, lru_wa, lru_ba, lru_lambda, lru_norm,
              rwkv_w0, rwkv_w2, rwkv_a0, rwkv_a2, rwkv_v0, rwkv_v2, rwkv_g2,
              rwkv_kk, rwkv_ka, rwkv_rk, rwkv_lnx_w, rwkv_lnx_b, w_o,
              ln_ffn, w_gate, w_up, conv_f_w, conv_f_b, w_down,
              ln_ple, w_ple_gate, w_ple_proj, ln_ple_post, ln_final):
    B, S, _ = x.shape
    H, N = RWKV_HEADS, RWKV_HEAD

    def heads(t):
        return t.reshape(B, S, H, N)

    h = x
    v_first = None
    for i in range(DEPTH):
        u = rmsnorm(h, ln_mix[i])
        if i == 0:
            w_cat, mu = w_in[0], mu_shift[0]
        else:
            w_cat = jnp.concatenate([w_in[i], w_in_vres[i - 1]], axis=1)
            mu = jnp.concatenate([mu_shift[i], mu_shift_vres[i - 1]], axis=0)
        z = u @ w_cat

        xb = causal_dwconv(z[..., :D_LRU], conv_a_w[i], conv_a_b[i])
        yb = jax.nn.gelu(z[..., D_LRU:2 * D_LRU])
        hl = rg_lru(xb, lru_wx[i], lru_bx[i], lru_wa[i], lru_ba[i], lru_lambda[i])
        out_a = rmsnorm(hl * yb, lru_norm[i])

        zr = z[..., 2 * D_LRU:]
        zr = zr + (token_shift(zr) - zr) * mu
        r = zr[..., :D_RWKV]
        k = zr[..., D_RWKV:2 * D_RWKV]
        v = zr[..., 2 * D_RWKV:3 * D_RWKV]
        o = 3 * D_RWKV
        wl = zr[..., o:o + LORA_W]
        o += LORA_W
        al = zr[..., o:o + LORA_A]
        o += LORA_A
        gl = zr[..., o:o + LORA_G]

        w_log = -jax.nn.softplus(-(rwkv_w0[i] + jnp.tanh(wl) @ rwkv_w2[i])) - 0.5
        decay = jnp.exp(-jnp.exp(w_log.astype(jnp.float32)))
        a = jax.nn.sigmoid(rwkv_a0[i] + al @ rwkv_a2[i])
        g = jax.nn.sigmoid(gl) @ rwkv_g2[i]
        if i == 0:
            v_first = v
        else:
            vl = zr[..., N_SHIFT:]
            v = v + (v_first - v) * jax.nn.sigmoid(rwkv_v0[i - 1] + vl @ rwkv_v2[i - 1])

        kk = heads((k * rwkv_kk[i]).astype(jnp.float32))
        kk = kk / jnp.maximum(jnp.sqrt(jnp.sum(kk * kk, axis=-1, keepdims=True)), 1e-12)
        k = k * (1.0 + (a - 1.0) * rwkv_ka[i])
        rh, kh, vh = heads(r), heads(k), heads(v)
        ah = heads(a.astype(jnp.float32))
        y = wkv7_scan(rh.astype(jnp.float32), heads(decay), kh.astype(jnp.float32),
                      vh.astype(jnp.float32), kk, kk * ah)
        mean = jnp.mean(y, axis=-1, keepdims=True)
        var = jnp.mean(jnp.square(y - mean), axis=-1, keepdims=True)
        yn = ((y - mean) * lax.rsqrt(var + LNX_EPS)).reshape(B, S, D_RWKV).astype(x.dtype)
        yn = yn * rwkv_lnx_w[i] + rwkv_lnx_b[i]
        bonus = (jnp.sum(rh * kh * rwkv_rk[i], axis=-1, keepdims=True) * vh).reshape(B, S, D_RWKV)
        out_b = (yn + bonus) * g

        h = h + jnp.concatenate([out_a, out_b], axis=-1) @ w_o[i]

        u = rmsnorm(h, ln_ffn[i])
        gate = causal_dwconv(u @ w_gate[i], conv_f_w[i], conv_f_b[i])
        h = h + (jax.nn.gelu(gate) * (u @ w_up[i])) @ w_down[i]

        u = rmsnorm(h, ln_ple[i])
        e = jax.nn.sigmoid(u @ w_ple_gate[i]) * (p[i] @ w_ple_proj[i])
        h = h + rmsnorm(e, ln_ple_post[i])

    return rmsnorm(h, ln_final)
```

```python
import functools
import math

import jax
import jax.numpy as jnp
from jax import lax
from jax.experimental import pallas as pl
from jax.experimental.pallas import tpu as pltpu

F32 = jnp.float32
BF16 = jnp.bfloat16

D_MODEL = 2048
D_LRU = 1024
D_RWKV = 1024
LRU_HEADS = 4
LRU_BLOCK = 256
LRU_CONV = 4
LRU_C = 8.0
HEAD = 64
LORA_W = 64
LORA_A = 64
LORA_V = 32
LORA_G = 160
D_MAIN = 2 * D_LRU + 3 * D_RWKV
D_LORA = 384
D_FF = 3 * D_MODEL
FFN_CONV = 3
D_PLE = 256
RMS_EPS = 1e-6
LNX_EPS = 64e-5

V7X_VMEM_BYTES = 64 * 1024 * 1024
SUBLANES = 8
LANES = 128

CHUNK = 64
PAIR = 2 * HEAD
HI = lax.Precision.HIGHEST


def _vmem_limit(nbytes):
    return int(min(V7X_VMEM_BYTES - 8 * 1024 * 1024, nbytes + 16 * 1024 * 1024))


def _rms(x, g):
    return x * lax.rsqrt(jnp.mean(x * x, axis=-1, keepdims=True) + RMS_EPS) * g


def _gelu(x):
    c = math.sqrt(2.0 / math.pi)
    return 0.5 * x * (1.0 + jnp.tanh(c * (x + 0.044715 * (x * x * x))))


def _sigmoid(x):
    return 1.0 / (1.0 + jnp.exp(-x))


def _softplus(x):
    return jnp.maximum(x, 0.0) + jnp.log1p(jnp.exp(-jnp.abs(x)))


def _shift_rows(x, d, prev8):
    rolled = pltpu.roll(x, d, axis=0)
    prev = pltpu.roll(prev8, d, axis=0)
    row = lax.broadcasted_iota(jnp.int32, prev8.shape, 0)
    top = jnp.where(row < d, prev, rolled[:SUBLANES])
    return jnp.concatenate([top, rolled[SUBLANES:]], axis=0)


def _inproj_kernel(x_ref, g_ref, wm_ref, wl_ref, zm_ref, zl_ref, u_ref):
    @pl.when(pl.program_id(1) == 0)
    def _():
        u_ref[...] = _rms(x_ref[...], g_ref[...]).astype(BF16)
        zl_ref[...] = jnp.dot(u_ref[...], wl_ref[...], preferred_element_type=F32)

    zm_ref[...] = jnp.dot(u_ref[...], wm_ref[...], preferred_element_type=F32)


def _inproj(h, g, w_main, w_lora, tm=512, tn=1024):
    T = h.shape[0]
    nbytes = 2 * (tm * D_MODEL * 4 + D_MODEL * tn * 2 + D_MODEL * D_LORA * 2
                  + tm * tn * 4 + tm * D_LORA * 4) + tm * D_MODEL * 2
    return pl.pallas_call(
        _inproj_kernel,
        grid=(T // tm, D_MAIN // tn),
        in_specs=[
            pl.BlockSpec((tm, D_MODEL), lambda i, j: (i, 0)),
            pl.BlockSpec((1, D_MODEL), lambda i, j: (0, 0)),
            pl.BlockSpec((D_MODEL, tn), lambda i, j: (0, j)),
            pl.BlockSpec((D_MODEL, D_LORA), lambda i, j: (0, 0)),
        ],
        out_specs=[
            pl.BlockSpec((tm, tn), lambda i, j: (i, j)),
            pl.BlockSpec((tm, D_LORA), lambda i, j: (i, 0)),
        ],
        out_shape=[
            jax.ShapeDtypeStruct((T, D_MAIN), F32),
            jax.ShapeDtypeStruct((T, D_LORA), F32),
        ],
        scratch_shapes=[pltpu.VMEM((tm, D_MODEL), BF16)],
        compiler_params=pltpu.CompilerParams(
            dimension_semantics=("arbitrary", "arbitrary"),
            vmem_limit_bytes=_vmem_limit(nbytes)),
        name="inproj",
    )(h, g, w_main, w_lora)


def _lru_kernel(xb_ref, yb_ref, cw_ref, cb_ref, wx_ref, bx_ref, wa_ref, ba_ref,
                lam_ref, nrm_ref, o_ref, tail_ref, h_ref):
    t = pl.program_id(1)

    @pl.when(t == 0)
    def _():
        tail_ref[...] = jnp.zeros_like(tail_ref)
        h_ref[...] = jnp.zeros_like(h_ref)

    x = xb_ref[...]
    ts = x.shape[0]
    tail = tail_ref[...]
    cw = cw_ref[...]
    xc = x * cw[LRU_CONV - 1:LRU_CONV] + cb_ref[...]
    for d in range(1, LRU_CONV):
        xc = xc + _shift_rows(x, d, tail) * cw[LRU_CONV - 1 - d:LRU_CONV - d]
    tail_ref[...] = x[ts - SUBLANES:]

    xcb = xc.astype(BF16)
    gx, ga = [], []
    for hd in range(LRU_HEADS):
        blk = xcb[:, hd * LRU_BLOCK:(hd + 1) * LRU_BLOCK]
        gx.append(jnp.dot(blk, wx_ref[hd], preferred_element_type=F32))
        ga.append(jnp.dot(blk, wa_ref[hd], preferred_element_type=F32))
    gate_x = _sigmoid(jnp.concatenate(gx, axis=-1) + bx_ref[...])
    gate_a = _sigmoid(jnp.concatenate(ga, axis=-1) + ba_ref[...])
    log_a = (-LRU_C) * gate_a * _softplus(-lam_ref[...])
    a = jnp.exp(log_a)
    mult = jnp.sqrt(1.0 - jnp.exp(2.0 * log_a))
    row = lax.broadcasted_iota(jnp.int32, (ts, 1), 0)
    mult = jnp.where(jnp.logical_and(row == 0, t == 0), 1.0, mult)
    b = xc * gate_x * mult

    d = 1
    while d < ts:
        keep = row >= d
        a_sh = jnp.where(keep, pltpu.roll(a, d, axis=0), 1.0)
        b_sh = jnp.where(keep, pltpu.roll(b, d, axis=0), 0.0)
        b = a * b_sh + b
        a = a * a_sh
        d *= 2
    h = a * h_ref[0:1, :] + b
    h_ref[0:1, :] = h[ts - 1:ts]

    y = h * _gelu(yb_ref[...])
    o_ref[...] = _rms(y, nrm_ref[...]).astype(o_ref.dtype)


def _lru(zm, cw, cb, wx, bx, wa, ba, lam, nrm, batch, seq, ts=256):
    T = zm.shape[0]
    nt = seq // ts
    vec = pl.BlockSpec((1, D_LRU), lambda b, t: (0, 0))
    mat = pl.BlockSpec((LRU_HEADS, LRU_BLOCK, LRU_BLOCK), lambda b, t: (0, 0, 0))
    nbytes = 2 * (2 * ts * D_LRU * 4 + ts * D_LRU * 2) + 24 * ts * D_LRU * 4
    return pl.pallas_call(
        _lru_kernel,
        grid=(batch, nt),
        in_specs=[
            pl.BlockSpec((ts, D_LRU), lambda b, t: (b * nt + t, 0)),
            pl.BlockSpec((ts, D_LRU), lambda b, t: (b * nt + t, 1)),
            pl.BlockSpec((LRU_CONV, D_LRU), lambda b, t: (0, 0)),
            vec, mat, vec, mat, vec, vec, vec,
        ],
        out_specs=pl.BlockSpec((ts, D_LRU), lambda b, t: (b * nt + t, 0)),
        out_shape=jax.ShapeDtypeStruct((T, D_LRU), BF16),
        scratch_shapes=[pltpu.VMEM((SUBLANES, D_LRU), F32),
                        pltpu.VMEM((SUBLANES, D_LRU), F32)],
        compiler_params=pltpu.CompilerParams(
            dimension_semantics=("arbitrary", "arbitrary"),
            vmem_limit_bytes=_vmem_limit(nbytes)),
        name="rglru",
    )(zm, zm, cw, cb, wx, bx, wa, ba, lam, nrm)


def _seg_sum(x, lane_lo):
    s0 = jnp.sum(jnp.where(lane_lo, x, 0.0), axis=-1, keepdims=True)
    s1 = jnp.sum(jnp.where(lane_lo, 0.0, x), axis=-1, keepdims=True)
    return jnp.where(lane_lo, s0, s1)


def _dot_nt(a, b):
    return lax.dot_general(a, b, (((1,), (1,)), ((), ())), precision=HI,
                           preferred_element_type=F32)


def _dot_tn(a, b):
    return lax.dot_general(a, b, (((0,), (0,)), ((), ())), precision=HI,
                           preferred_element_type=F32)


def _dot_hi(a, b):
    return jnp.dot(a, b, precision=HI, preferred_element_type=F32)


def _rwkv_kernel(has_vres, *refs):
    if has_vres:
        (r_ref, k_ref, v_ref, zl_ref, vf_ref, mur_ref, muk_ref, muv_ref, mul_ref,
         w0_ref, w2_ref, a0_ref, a2_ref, g2_ref, v0_ref, v2_ref,
         kkw_ref, ka_ref, rk_ref, lnw_ref, lnb_ref, o_ref) = refs
        vfo_ref = None
    else:
        (r_ref, k_ref, v_ref, zl_ref, mur_ref, muk_ref, muv_ref, mul_ref,
         w0_ref, w2_ref, a0_ref, a2_ref, g2_ref,
         kkw_ref, ka_ref, rk_ref, lnw_ref, lnb_ref, o_ref, vfo_ref) = refs
        vf_ref = v0_ref = v2_ref = None

    C = CHUNK
    nchunk = r_ref.shape[0] // C
    lane_lo = lax.broadcasted_iota(jnp.int32, (1, PAIR), 1) < HEAD
    ri = lax.broadcasted_iota(jnp.int32, (C, C), 0)
    ci = lax.broadcasted_iota(jnp.int32, (C, C), 1)
    tri_incl = (ri >= ci).astype(F32)
    lower_strict = ri > ci
    lower_incl = ri >= ci
    row0 = lax.broadcasted_iota(jnp.int32, (C, 1), 0) == 0

    def shift_lerp(cur, prev_row, mu):
        sh = jnp.where(row0, prev_row, pltpu.roll(cur, 1, axis=0))
        return cur + (sh - cur) * mu

    def body(c, carry):
        pr, pk, pv, pz, s0, s1 = carry
        rows = pl.ds(pl.multiple_of(c * C, C), C)
        r_raw, k_raw, v_raw, z_raw = r_ref[rows, :], k_ref[rows, :], v_ref[rows, :], zl_ref[rows, :]
        r = shift_lerp(r_raw, pr, mur_ref[...])
        k = shift_lerp(k_raw, pk, muk_ref[...])
        v = shift_lerp(v_raw, pv, muv_ref[...])
        zl = shift_lerp(z_raw, pz, mul_ref[...])

        z01 = zl[:, 0:LANES]
        wpre = w0_ref[...] + _dot_hi(jnp.tanh(z01), w2_ref[...])
        w_log = -_softplus(-wpre) - 0.5
        logw = -jnp.exp(w_log)
        a = _sigmoid(a0_ref[...] + _dot_hi(z01, a2_ref[...]))
        g = _dot_hi(_sigmoid(zl[:, LANES:3 * LANES]), g2_ref[...])
        if has_vres:
            mix = _sigmoid(v0_ref[...] + _dot_hi(zl[:, 2 * LANES:3 * LANES], v2_ref[...]))
            v = v + (vf_ref[rows, :] - v) * mix
        else:
            vfo_ref[rows, :] = v

        kk = k * kkw_ref[...]
        kk = kk / jnp.maximum(jnp.sqrt(_seg_sum(kk * kk, lane_lo)), 1e-12)
        k2 = k * (1.0 + (a - 1.0) * ka_ref[...])
        bb = kk * a

        cum = _dot_hi(tri_incl, logw)
        p_in = jnp.exp(cum)
        p_ex = jnp.exp(cum - logw)
        p_inv = jnp.exp(-cum)
        p_end = p_in[C - 1:C, :]
        rt = r * p_in
        at = -kk * p_ex
        bt = bb * p_inv
        kt = k2 * p_inv
        bhat = bt * p_end
        khat = kt * p_end

        ys, s_new = [], []
        for hh, st in ((0, s0), (1, s1)):
            sl = slice(hh * HEAD, (hh + 1) * HEAD)
            ar = jnp.concatenate([at[:, sl], rt[:, sl]], axis=0)
            sc_b = _dot_nt(ar, bt[:, sl])
            sc_k = _dot_nt(ar, kt[:, sl])
            from_state = _dot_nt(ar, st)
            m_ab = jnp.where(lower_strict, sc_b[:C], 0.0)
            m_ak = jnp.where(lower_strict, sc_k[:C], 0.0)
            a_rb = jnp.where(lower_incl, sc_b[C:], 0.0)
            a_rk = jnp.where(lower_incl, sc_k[C:], 0.0)
            vh = v[:, sl]
            u = from_state[:C] + _dot_hi(m_ak, vh)
            m = m_ab
            nstep = int(math.log2(C))
            for i in range(nstep):
                u = u + _dot_hi(m, u)
                if i + 1 < nstep:
                    m = _dot_hi(m, m)
            ys.append(from_state[C:] + _dot_hi(a_rb, u) + _dot_hi(a_rk, vh))
            s_new.append(st * p_end[:, sl] + _dot_tn(u, bhat[:, sl]) + _dot_tn(vh, khat[:, sl]))

        y = jnp.concatenate(ys, axis=-1)
        mean = _seg_sum(y, lane_lo) * (1.0 / HEAD)
        yc = y - mean
        var = _seg_sum(yc * yc, lane_lo) * (1.0 / HEAD)
        yn = yc * lax.rsqrt(var + LNX_EPS) * lnw_ref[...] + lnb_ref[...]
        bonus = _seg_sum(r * k2 * rk_ref[...], lane_lo) * v
        o_ref[rows, :] = ((yn + bonus) * g).astype(o_ref.dtype)
        return (r_raw[C - 1:C], k_raw[C - 1:C], v_raw[C - 1:C], z_raw[C - 1:C],
                s_new[0], s_new[1])

    zrow = jnp.zeros((1, PAIR), F32)
    init = (zrow, zrow, zrow, jnp.zeros((1, D_LORA), F32),
            jnp.zeros((HEAD, HEAD), F32), jnp.zeros((HEAD, HEAD), F32))
    lax.fori_loop(0, nchunk, body, init)


def _rwkv(zm, zl, vfirst, prm, batch, seq):
    T = zm.shape[0]
    npair = D_RWKV // PAIR
    col0 = 2 * D_LRU // PAIR
    has_vres = vfirst is not None

    def col(off):
        return pl.BlockSpec((seq, PAIR), lambda b, p: (b, off + p))

    vecp = pl.BlockSpec((1, PAIR), lambda b, p: (0, p))
    vec_r = pl.BlockSpec((1, PAIR), lambda b, p: (0, p))
    vec_k = pl.BlockSpec((1, PAIR), lambda b, p: (0, npair + p))
    vec_v = pl.BlockSpec((1, PAIR), lambda b, p: (0, 2 * npair + p))
    vec_l = pl.BlockSpec((1, D_LORA), lambda b, p: (0, 0))

    def lora(rows):
        return pl.BlockSpec((rows, PAIR), lambda b, p: (0, p))

    in_specs = [col(col0), col(col0 + npair), col(col0 + 2 * npair),
                pl.BlockSpec((seq, D_LORA), lambda b, p: (b, 0))]
    args = [zm, zm, zm, zl]
    if has_vres:
        in_specs.append(pl.BlockSpec((seq, PAIR), lambda b, p: (b, p)))
        args.append(vfirst)
    in_specs += [vec_r, vec_k, vec_v, vec_l, vecp, lora(LANES), vecp, lora(LANES), lora(2 * LANES)]
    args += [prm["mu_rkv"], prm["mu_rkv"], prm["mu_rkv"], prm["mu_lora"],
             prm["w0"], prm["w2"], prm["a0"], prm["a2"], prm["g2"]]
    if has_vres:
        in_specs += [vecp, lora(LANES)]
        args += [prm["v0"], prm["v2"]]
    in_specs += [vecp] * 5
    args += [prm["kk"], prm["ka"], prm["rk"], prm["lnw"], prm["lnb"]]

    out_blk = pl.BlockSpec((seq, PAIR), lambda b, p: (b, p))
    if has_vres:
        out_specs = out_blk
        out_shape = jax.ShapeDtypeStruct((T, D_RWKV), BF16)
    else:
        out_specs = [out_blk, out_blk]
        out_shape = [jax.ShapeDtypeStruct((T, D_RWKV), BF16),
                     jax.ShapeDtypeStruct((T, D_RWKV), F32)]
    nbytes = 2 * seq * (5 * PAIR * 4 + D_LORA * 4 + PAIR * 2)
    res = pl.pallas_call(
        functools.partial(_rwkv_kernel, has_vres),
        grid=(batch, npair),
        in_specs=in_specs,
        out_specs=out_specs,
        out_shape=out_shape,
        compiler_params=pltpu.CompilerParams(
            dimension_semantics=("arbitrary", "arbitrary"),
            vmem_limit_bytes=_vmem_limit(nbytes)),
        name="rwkv7",
    )(*args)
    if has_vres:
        return res, vfirst
    return res[0], res[1]


def _oproj_kernel(h_ref, a_ref, b_ref, wa_ref, wb_ref, o_ref):
    acc = jnp.dot(a_ref[...], wa_ref[...], preferred_element_type=F32)
    acc = acc + jnp.dot(b_ref[...], wb_ref[...], preferred_element_type=F32)
    o_ref[...] = h_ref[...] + acc


def _oproj(h, oa, ob, wo, tm=512):
    T = h.shape[0]
    nbytes = 2 * (2 * tm * D_MODEL * 4 + 2 * tm * D_LRU * 2 + 2 * D_LRU * D_MODEL * 2)
    return pl.pallas_call(
        _oproj_kernel,
        grid=(T // tm,),
        in_specs=[
            pl.BlockSpec((tm, D_MODEL), lambda i: (i, 0)),
            pl.BlockSpec((tm, D_LRU), lambda i: (i, 0)),
            pl.BlockSpec((tm, D_RWKV), lambda i: (i, 0)),
            pl.BlockSpec((D_LRU, D_MODEL), lambda i: (0, 0)),
            pl.BlockSpec((D_RWKV, D_MODEL), lambda i: (1, 0)),
        ],
        out_specs=pl.BlockSpec((tm, D_MODEL), lambda i: (i, 0)),
        out_shape=jax.ShapeDtypeStruct((T, D_MODEL), F32),
        compiler_params=pltpu.CompilerParams(
            dimension_semantics=("arbitrary",),
            vmem_limit_bytes=_vmem_limit(nbytes)),
        name="oproj",
    )(h, oa, ob, wo, wo)


def _ffn_kernel(tiles_per_seq, h_ref, g_ref, wg_ref, wu_ref, cw_ref, cb_ref, wd_ref,
                o_ref, u_ref, acc_ref, tail_ref):
    i = pl.program_id(0)
    j = pl.program_id(1)

    @pl.when(j == 0)
    def _():
        u_ref[...] = _rms(h_ref[...], g_ref[...]).astype(BF16)
        acc_ref[...] = jnp.zeros_like(acc_ref)

    @pl.when(i % tiles_per_seq == 0)
    def _():
        tail_ref[j] = jnp.zeros(tail_ref.shape[1:], F32)

    u = u_ref[...]
    gate = jnp.dot(u, wg_ref[...], preferred_element_type=F32)
    tm = gate.shape[0]
    tail = tail_ref[j]
    cw = cw_ref[...]
    conv = gate * cw[FFN_CONV - 1:FFN_CONV] + cb_ref[...]
    for d in range(1, FFN_CONV):
        conv = conv + _shift_rows(gate, d, tail) * cw[FFN_CONV - 1 - d:FFN_CONV - d]
    tail_ref[j] = gate[tm - SUBLANES:]
    up = jnp.dot(u, wu_ref[...], preferred_element_type=F32)
    act = (_gelu(conv) * up).astype(BF16)
    acc_ref[...] += jnp.dot(act, wd_ref[...], preferred_element_type=F32)

    @pl.when(j == pl.num_programs(1) - 1)
    def _():
        o_ref[...] = h_ref[...] + acc_ref[...]


def _ffn(h, g, wg, wu, cw, cb, wd, seq, tm=512, tf=512):
    T = h.shape[0]
    nf = D_FF // tf
    nbytes = (2 * (2 * tm * D_MODEL * 4 + 3 * D_MODEL * tf * 2) + tm * D_MODEL * 6
              + nf * SUBLANES * tf * 4 + 6 * tm * tf * 4)
    return pl.pallas_call(
        functools.partial(_ffn_kernel, seq // tm),
        grid=(T // tm, nf),
        in_specs=[
            pl.BlockSpec((tm, D_MODEL), lambda i, j: (i, 0)),
            pl.BlockSpec((1, D_MODEL), lambda i, j: (0, 0)),
            pl.BlockSpec((D_MODEL, tf), lambda i, j: (0, j)),
            pl.BlockSpec((D_MODEL, tf), lambda i, j: (0, j)),
            pl.BlockSpec((FFN_CONV, tf), lambda i, j: (0, j)),
            pl.BlockSpec((1, tf), lambda i, j: (0, j)),
            pl.BlockSpec((tf, D_MODEL), lambda i, j: (j, 0)),
        ],
        out_specs=pl.BlockSpec((tm, D_MODEL), lambda i, j: (i, 0)),
        out_shape=jax.ShapeDtypeStruct((T, D_MODEL), F32),
        scratch_shapes=[pltpu.VMEM((tm, D_MODEL), BF16),
                        pltpu.VMEM((tm, D_MODEL), F32),
                        pltpu.VMEM((nf, SUBLANES, tf), F32)],
        compiler_params=pltpu.CompilerParams(
            dimension_semantics=("arbitrary", "arbitrary"),
            vmem_limit_bytes=_vmem_limit(nbytes)),
        name="ffn",
    )(h, g, wg, wu, cw, cb, wd)


def _ple_kernel(final, h_ref, p_ref, g_ref, wg_ref, wp_ref, gp_ref, gf_ref, o_ref):
    h = h_ref[...]
    u = _rms(h, g_ref[...]).astype(BF16)
    gate = _sigmoid(jnp.dot(u, wg_ref[...], preferred_element_type=F32))
    proj = jnp.dot(p_ref[...].astype(BF16), wp_ref[...], preferred_element_type=F32)
    out = h + _rms(gate * proj, gp_ref[...])
    if final:
        out = _rms(out, gf_ref[...])
    o_ref[...] = out


def _ple(h, p, g, wg, wp, gp, gf, final, tm=512):
    T = h.shape[0]
    vec = pl.BlockSpec((1, D_MODEL), lambda i: (0, 0))
    nbytes = 2 * (2 * tm * D_MODEL * 4 + tm * D_PLE * 4 + D_MODEL * D_MODEL * 2
                  + D_PLE * D_MODEL * 2) + 4 * tm * D_MODEL * 4
    return pl.pallas_call(
        functools.partial(_ple_kernel, final),
        grid=(T // tm,),
        in_specs=[
            pl.BlockSpec((tm, D_MODEL), lambda i: (i, 0)),
            pl.BlockSpec((tm, D_PLE), lambda i: (i, 0)),
            vec,
            pl.BlockSpec((D_MODEL, D_MODEL), lambda i: (0, 0)),
            pl.BlockSpec((D_PLE, D_MODEL), lambda i: (0, 0)),
            vec, vec,
        ],
        out_specs=pl.BlockSpec((tm, D_MODEL), lambda i: (i, 0)),
        out_shape=jax.ShapeDtypeStruct((T, D_MODEL), F32),
        compiler_params=pltpu.CompilerParams(
            dimension_semantics=("arbitrary",),
            vmem_limit_bytes=_vmem_limit(nbytes)),
        name="ple",
    )(h, p, g, wg, wp, gp, gf)


def _row(v):
    return v.reshape(1, -1).astype(F32)


def _pad_rows(w, top, total):
    return jnp.pad(w, ((top, total - top - w.shape[0]), (0, 0)))


def kernel(x, p, ln_mix, w_in, w_in_vres, mu_shift, mu_shift_vres, conv_a_w, conv_a_b, lru_wx, lru_bx, lru_wa, lru_ba, lru_lambda, lru_norm, rwkv_w0, rwkv_w2, rwkv_a0, rwkv_a2, rwkv_v0, rwkv_v2, rwkv_g2, rwkv_kk, rwkv_ka, rwkv_rk, rwkv_lnx_w, rwkv_lnx_b, w_o, ln_ffn, w_gate, w_up, conv_f_w, conv_f_b, w_down, ln_ple, w_ple_gate, w_ple_proj, ln_ple_post, ln_final):
    batch, seq, _ = x.shape
    depth = w_in.shape[0]
    T = batch * seq
    h = x.reshape(T, D_MODEL)
    n_lora = LORA_W + LORA_A + LORA_G
    vfirst = None
    for i in range(depth):
        w_main = w_in[i][:, :D_MAIN].astype(BF16)
        lora_cols = [w_in[i][:, D_MAIN:]]
        mu_l = [mu_shift[i][3 * D_RWKV:]]
        if i > 0:
            lora_cols.append(w_in_vres[i - 1])
            mu_l.append(mu_shift_vres[i - 1])
        w_lora = jnp.concatenate(lora_cols, axis=1)
        w_lora = jnp.pad(w_lora, ((0, 0), (0, D_LORA - w_lora.shape[1]))).astype(BF16)
        mu_lora = jnp.concatenate(mu_l, axis=0)
        mu_lora = jnp.pad(mu_lora, (0, D_LORA - mu_lora.shape[0]))

        zm, zl = _inproj(h, _row(ln_mix[i]), w_main, w_lora)

        out_a = _lru(zm, conv_a_w[i], _row(conv_a_b[i]), lru_wx[i].astype(BF16), _row(lru_bx[i]),
                     lru_wa[i].astype(BF16), _row(lru_ba[i]), _row(lru_lambda[i]),
                     _row(lru_norm[i]), batch, seq)

        prm = {
            "mu_rkv": _row(mu_shift[i][:3 * D_RWKV]),
            "mu_lora": _row(mu_lora),
            "w0": _row(rwkv_w0[i]),
            "w2": _pad_rows(rwkv_w2[i], 0, LANES),
            "a0": _row(rwkv_a0[i]),
            "a2": _pad_rows(rwkv_a2[i], LORA_W, LANES),
            "g2": _pad_rows(rwkv_g2[i], 0, 2 * LANES),
            "kk": _row(rwkv_kk[i]), "ka": _row(rwkv_ka[i]), "rk": _row(rwkv_rk[i]),
            "lnw": _row(rwkv_lnx_w[i]), "lnb": _row(rwkv_lnx_b[i]),
        }
        if i > 0:
            prm["v0"] = _row(rwkv_v0[i - 1])
            prm["v2"] = _pad_rows(rwkv_v2[i - 1], n_lora - 2 * LANES, LANES)
        out_b, vfirst = _rwkv(zm, zl, vfirst, prm, batch, seq)

        h = _oproj(h, out_a, out_b, w_o[i].astype(BF16))
        h = _ffn(h, _row(ln_ffn[i]), w_gate[i].astype(BF16), w_up[i].astype(BF16),
                 conv_f_w[i], _row(conv_f_b[i]), w_down[i].astype(BF16), seq)
        h = _ple(h, p[i].reshape(T, D_PLE), _row(ln_ple[i]), w_ple_gate[i].astype(BF16),
                 w_ple_proj[i].astype(BF16), _row(ln_ple_post[i]), _row(ln_final),
                 final=(i == depth - 1))
    return h.reshape(batch, seq, D_MODEL)
```

```python
import functools
import math

import jax
import jax.numpy as jnp
from jax import lax
from jax.experimental import pallas as pl
from jax.experimental.pallas import tpu as pltpu

F32 = jnp.float32
BF16 = jnp.bfloat16

D_MODEL = 2048
D_LRU = 1024
D_RWKV = 1024
LRU_HEADS = 4
LRU_BLOCK = 256
LRU_CONV = 4
LRU_C = 8.0
HEAD = 64
N_HEADS = D_RWKV // HEAD
LORA_W = 64
LORA_A = 64
LORA_V = 32
LORA_G = 160
D_MAIN = 2 * D_LRU + 3 * D_RWKV
D_LORA = 384
D_FF = 3 * D_MODEL
FFN_CONV = 3
D_PLE = 256
RMS_EPS = 1e-6
LNX_EPS = 64e-5

V7X_VMEM_BYTES = 64 * 1024 * 1024
SUBLANES = 8
LANES = 128

CHUNK = 64
PAIR = 2 * HEAD
HEAD_GROUP = 16
RWKV_ROWS = 512


def _vmem_limit(nbytes):
    return int(min(V7X_VMEM_BYTES - 8 * 1024 * 1024, nbytes + 16 * 1024 * 1024))


def _rms(x, g):
    return x * lax.rsqrt(jnp.mean(x * x, axis=-1, keepdims=True) + RMS_EPS) * g


def _gelu(x):
    c = math.sqrt(2.0 / math.pi)
    return 0.5 * x * (1.0 + jnp.tanh(c * (x + 0.044715 * (x * x * x))))


def _sigmoid(x):
    return 1.0 / (1.0 + jnp.exp(-x))


def _softplus(x):
    return jnp.maximum(x, 0.0) + jnp.log1p(jnp.exp(-jnp.abs(x)))


def _shift_rows(x, d, prev8):
    rolled = pltpu.roll(x, d, axis=0)
    prev = pltpu.roll(prev8, d, axis=0)
    row = lax.broadcasted_iota(jnp.int32, prev8.shape, 0)
    top = jnp.where(row < d, prev, rolled[:SUBLANES])
    return jnp.concatenate([top, rolled[SUBLANES:]], axis=0)


def _inproj_kernel(x_ref, g_ref, wm_ref, wl_ref, zm_ref, zl_ref, u_ref):
    @pl.when(pl.program_id(1) == 0)
    def _():
        u_ref[...] = _rms(x_ref[...], g_ref[...]).astype(BF16)
        zl_ref[...] = jnp.dot(u_ref[...], wl_ref[...], preferred_element_type=F32)

    zm_ref[...] = jnp.dot(u_ref[...], wm_ref[...], preferred_element_type=F32)


def _inproj(h, g, w_main, w_lora, tm=512, tn=1024):
    T = h.shape[0]
    nbytes = 2 * (tm * D_MODEL * 4 + D_MODEL * tn * 2 + D_MODEL * D_LORA * 2
                  + tm * tn * 4 + tm * D_LORA * 4) + tm * D_MODEL * 2
    return pl.pallas_call(
        _inproj_kernel,
        grid=(T // tm, D_MAIN // tn),
        in_specs=[
            pl.BlockSpec((tm, D_MODEL), lambda i, j: (i, 0)),
            pl.BlockSpec((1, D_MODEL), lambda i, j: (0, 0)),
            pl.BlockSpec((D_MODEL, tn), lambda i, j: (0, j)),
            pl.BlockSpec((D_MODEL, D_LORA), lambda i, j: (0, 0)),
        ],
        out_specs=[
            pl.BlockSpec((tm, tn), lambda i, j: (i, j)),
            pl.BlockSpec((tm, D_LORA), lambda i, j: (i, 0)),
        ],
        out_shape=[
            jax.ShapeDtypeStruct((T, D_MAIN), F32),
            jax.ShapeDtypeStruct((T, D_LORA), F32),
        ],
        scratch_shapes=[pltpu.VMEM((tm, D_MODEL), BF16)],
        compiler_params=pltpu.CompilerParams(
            dimension_semantics=("arbitrary", "arbitrary"),
            vmem_limit_bytes=_vmem_limit(nbytes)),
        name="inproj",
    )(h, g, w_main, w_lora)


def _lru_kernel(xb_ref, yb_ref, cw_ref, cb_ref, wx_ref, bx_ref, wa_ref, ba_ref,
                lam_ref, nrm_ref, o_ref, tail_ref, h_ref):
    t = pl.program_id(1)

    @pl.when(t == 0)
    def _():
        tail_ref[...] = jnp.zeros_like(tail_ref)
        h_ref[...] = jnp.zeros_like(h_ref)

    x = xb_ref[...]
    ts = x.shape[0]
    tail = tail_ref[...]
    cw = cw_ref[...]
    xc = x * cw[LRU_CONV - 1:LRU_CONV] + cb_ref[...]
    for d in range(1, LRU_CONV):
        xc = xc + _shift_rows(x, d, tail) * cw[LRU_CONV - 1 - d:LRU_CONV - d]
    tail_ref[...] = x[ts - SUBLANES:]

    xcb = xc.astype(BF16)
    gx, ga = [], []
    for hd in range(LRU_HEADS):
        blk = xcb[:, hd * LRU_BLOCK:(hd + 1) * LRU_BLOCK]
        gx.append(jnp.dot(blk, wx_ref[hd], preferred_element_type=F32))
        ga.append(jnp.dot(blk, wa_ref[hd], preferred_element_type=F32))
    gate_x = _sigmoid(jnp.concatenate(gx, axis=-1) + bx_ref[...])
    gate_a = _sigmoid(jnp.concatenate(ga, axis=-1) + ba_ref[...])
    log_a = (-LRU_C) * gate_a * _softplus(-lam_ref[...])
    a = jnp.exp(log_a)
    mult = jnp.sqrt(1.0 - jnp.exp(2.0 * log_a))
    row = lax.broadcasted_iota(jnp.int32, (ts, 1), 0)
    mult = jnp.where(jnp.logical_and(row == 0, t == 0), 1.0, mult)
    b = xc * gate_x * mult

    d = 1
    while d < ts:
        keep = row >= d
        a_sh = jnp.where(keep, pltpu.roll(a, d, axis=0), 1.0)
        b_sh = jnp.where(keep, pltpu.roll(b, d, axis=0), 0.0)
        b = a * b_sh + b
        a = a * a_sh
        d *= 2
    h = a * h_ref[0:1, :] + b
    h_ref[0:1, :] = h[ts - 1:ts]

    y = h * _gelu(yb_ref[...])
    o_ref[...] = _rms(y, nrm_ref[...]).astype(o_ref.dtype)


def _lru(zm, cw, cb, wx, bx, wa, ba, lam, nrm, batch, seq, ts=256):
    T = zm.shape[0]
    nt = seq // ts
    vec = pl.BlockSpec((1, D_LRU), lambda b, t: (0, 0))
    mat = pl.BlockSpec((LRU_HEADS, LRU_BLOCK, LRU_BLOCK), lambda b, t: (0, 0, 0))
    nbytes = 2 * (2 * ts * D_LRU * 4 + ts * D_LRU * 2) + 24 * ts * D_LRU * 4
    return pl.pallas_call(
        _lru_kernel,
        grid=(batch, nt),
        in_specs=[
            pl.BlockSpec((ts, D_LRU), lambda b, t: (b * nt + t, 0)),
            pl.BlockSpec((ts, D_LRU), lambda b, t: (b * nt + t, 1)),
            pl.BlockSpec((LRU_CONV, D_LRU), lambda b, t: (0, 0)),
            vec, mat, vec, mat, vec, vec, vec,
        ],
        out_specs=pl.BlockSpec((ts, D_LRU), lambda b, t: (b * nt + t, 0)),
        out_shape=jax.ShapeDtypeStruct((T, D_LRU), BF16),
        scratch_shapes=[pltpu.VMEM((SUBLANES, D_LRU), F32),
                        pltpu.VMEM((SUBLANES, D_LRU), F32)],
        compiler_params=pltpu.CompilerParams(
            dimension_semantics=("arbitrary", "arbitrary"),
            vmem_limit_bytes=_vmem_limit(nbytes)),
        name="rglru",
    )(zm, zm, cw, cb, wx, bx, wa, ba, lam, nrm)


def _mm(a, b):
    return jnp.dot(a.astype(BF16), b.astype(BF16), preferred_element_type=F32)


def _mm_nt(a, b):
    return lax.dot_general(a.astype(BF16), b.astype(BF16), (((1,), (1,)), ((), ())),
                           preferred_element_type=F32)


def _mm_tn(a, b):
    return lax.dot_general(a.astype(BF16), b.astype(BF16), (((0,), (0,)), ((), ())),
                           preferred_element_type=F32)


def _split3(x):
    hi = x.astype(BF16)
    r1 = x - hi.astype(F32)
    mid = r1.astype(BF16)
    lo = (r1 - mid.astype(F32)).astype(BF16)
    return hi, mid, lo


def _seg_sum(x):
    lane_lo = lax.broadcasted_iota(jnp.int32, (1, PAIR), 1) < HEAD
    out = []
    for p in range(x.shape[1] // PAIR):
        t = x[:, p * PAIR:(p + 1) * PAIR]
        s0 = jnp.sum(jnp.where(lane_lo, t, 0.0), axis=-1, keepdims=True)
        s1 = jnp.sum(jnp.where(lane_lo, 0.0, t), axis=-1, keepdims=True)
        out.append(jnp.where(lane_lo, s0, s1))
    return jnp.concatenate(out, axis=-1)


def _rwkv_kernel(has_vres, *refs):
    if has_vres:
        (r_ref, k_ref, v_ref, zl_ref, vf_ref, mur_ref, muk_ref, muv_ref, mul_ref,
         w0_ref, w2_ref, a0_ref, a2_ref, g2_ref, v0_ref, v2_ref,
         kkw_ref, ka_ref, rk_ref, lnw_ref, lnb_ref, o_ref, s_ref, prev_ref, prevz_ref) = refs
        vfo_ref = None
    else:
        (r_ref, k_ref, v_ref, zl_ref, mur_ref, muk_ref, muv_ref, mul_ref,
         w0_ref, w2_ref, a0_ref, a2_ref, g2_ref,
         kkw_ref, ka_ref, rk_ref, lnw_ref, lnb_ref, o_ref, vfo_ref,
         s_ref, prev_ref, prevz_ref) = refs
        vf_ref = v0_ref = v2_ref = None

    C = CHUNK
    nchunk = r_ref.shape[0] // C
    nh = r_ref.shape[1] // HEAD

    @pl.when(pl.program_id(2) == 0)
    def _():
        s_ref[...] = jnp.zeros_like(s_ref)
        prev_ref[...] = jnp.zeros_like(prev_ref)
        prevz_ref[...] = jnp.zeros_like(prevz_ref)

    ri3 = lax.broadcasted_iota(jnp.int32, (C, 3 * C), 0)
    ci3 = lax.broadcasted_iota(jnp.int32, (C, 3 * C), 1) % C
    tri3 = (ri3 >= ci3).astype(BF16)
    ri2 = lax.broadcasted_iota(jnp.int32, (2 * C, 2 * C), 0)
    ci2 = lax.broadcasted_iota(jnp.int32, (2 * C, 2 * C), 1) % C
    keep2 = jnp.where(ri2 < C, ri2, ri2 - C + 1) > ci2
    row0 = lax.broadcasted_iota(jnp.int32, (C, 1), 0) == 0
    zeros_h = jnp.zeros((C, HEAD), BF16)

    def shift_lerp(cur, prev_row, mu):
        sh = jnp.where(row0, prev_row, pltpu.roll(cur, 1, axis=0))
        return cur + (sh - cur) * mu

    def body(c, carry):
        rows = pl.ds(pl.multiple_of(c * C, C), C)
        r_raw, k_raw, v_raw, z_raw = r_ref[rows, :], k_ref[rows, :], v_ref[rows, :], zl_ref[rows, :]
        r = shift_lerp(r_raw, prev_ref[0:1, :], mur_ref[...])
        k = shift_lerp(k_raw, prev_ref[1:2, :], muk_ref[...])
        v = shift_lerp(v_raw, prev_ref[2:3, :], muv_ref[...])
        zl = shift_lerp(z_raw, prevz_ref[0:1, :], mul_ref[...])
        prev_ref[0:1, :] = r_raw[C - 1:C]
        prev_ref[1:2, :] = k_raw[C - 1:C]
        prev_ref[2:3, :] = v_raw[C - 1:C]
        prevz_ref[0:1, :] = z_raw[C - 1:C]

        z01 = zl[:, 0:LANES]
        wpre = w0_ref[...] + _mm(jnp.tanh(z01), w2_ref[...])
        w_log = -_softplus(-wpre) - 0.5
        logw = -jnp.exp(w_log)
        a = _sigmoid(a0_ref[...] + _mm(z01, a2_ref[...]))
        g = _mm(_sigmoid(zl[:, LANES:3 * LANES]), g2_ref[...])
        if has_vres:
            mix = _sigmoid(v0_ref[...] + _mm(zl[:, 2 * LANES:3 * LANES], v2_ref[...]))
            v = v + (vf_ref[rows, :] - v) * mix
        else:
            vfo_ref[rows, :] = v

        kk = k * kkw_ref[...]
        kk = kk / jnp.maximum(jnp.sqrt(_seg_sum(kk * kk)), 1e-12)
        k2 = k * (1.0 + (a - 1.0) * ka_ref[...])
        bb = kk * a

        cum = jnp.dot(tri3, jnp.concatenate(_split3(logw), axis=0), preferred_element_type=F32)
        p_in = jnp.exp(cum)
        p_ex = jnp.exp(cum - logw)
        p_inv = jnp.exp(-cum)
        p_end = p_in[C - 1:C, :]
        rt = r * p_in
        at = -kk * p_ex
        bt = bb * p_inv
        kt = k2 * p_inv
        rt_b, at_b, bt_b, kt_b, v_b = (t.astype(BF16) for t in (rt, at, bt, kt, v))
        bhat_b = (bt * p_end).astype(BF16)
        khat_b = (kt * p_end).astype(BF16)

        heads = range(nh)
        sls = [slice(hh * HEAD, (hh + 1) * HEAD) for hh in heads]
        sc_b, m, vh_b, d = [], [], [], []
        for sl in sls:
            ar = jnp.concatenate([at_b[:, sl], rt_b[:, sl]], axis=0)
            bk = jnp.concatenate([bt_b[:, sl], kt_b[:, sl]], axis=0)
            sc = jnp.where(keep2, _mm_nt(ar, bk), 0.0)
            sc_b.append(sc.astype(BF16))
            m.append(sc[:C, :C])
            vh_b.append(v_b[:, sl])
        for hh in heads:
            zv = jnp.concatenate([zeros_h, vh_b[hh]], axis=0)
            x_loc = jnp.dot(sc_b[hh][:C], zv, preferred_element_type=F32)
            d.append(jnp.concatenate([at[:, sls[hh]], x_loc], axis=-1))
        nstep = int(math.log2(C))
        for i in range(nstep):
            for hh in heads:
                m_b = m[hh].astype(BF16)
                if i + 1 < nstep:
                    rhs = jnp.concatenate([d[hh].astype(BF16), m_b], axis=-1)
                    out = jnp.dot(m_b, rhs, preferred_element_type=F32)
                    d[hh] = d[hh] + out[:, :2 * HEAD]
                    m[hh] = out[:, 2 * HEAD:]
                else:
                    d[hh] = d[hh] + jnp.dot(m_b, d[hh].astype(BF16), preferred_element_type=F32)
        o1, wz = [], []
        for hh in heads:
            gmat = jnp.concatenate(
                [d[hh].astype(BF16), jnp.concatenate([zeros_h, vh_b[hh]], axis=-1)], axis=0)
            o1.append(jnp.dot(sc_b[hh][C:], gmat, preferred_element_type=F32))
            bkh = jnp.concatenate([bhat_b[:, sls[hh]], khat_b[:, sls[hh]]], axis=0)
            wz.append(_mm_tn(gmat, bkh))
        ys = []
        for hh in heads:
            rbar = rt[:, sls[hh]] + o1[hh][:, :HEAD]
            st = s_ref[hh]
            ys.append(_mm_nt(rbar, st) + o1[hh][:, HEAD:])
            s_ref[hh] = st * p_end[:, sls[hh]] + _mm(st, wz[hh][:HEAD]) + wz[hh][HEAD:]

        y = jnp.concatenate(ys, axis=-1)
        mean = _seg_sum(y) * (1.0 / HEAD)
        yc = y - mean
        var = _seg_sum(yc * yc) * (1.0 / HEAD)
        yn = yc * lax.rsqrt(var + LNX_EPS) * lnw_ref[...] + lnb_ref[...]
        bonus = _seg_sum(r * k2 * rk_ref[...]) * v
        o_ref[rows, :] = ((yn + bonus) * g).astype(o_ref.dtype)
        return carry

    lax.fori_loop(0, nchunk, body, 0)


def _rwkv(zm, zl, vfirst, prm, batch, seq, hg=HEAD_GROUP, ts=RWKV_ROWS):
    T = zm.shape[0]
    W = hg * HEAD
    ng = D_RWKV // W
    nt = seq // ts
    col0 = 2 * D_LRU // W
    has_vres = vfirst is not None

    def col(off):
        return pl.BlockSpec((ts, W), lambda b, g, t: (b * nt + t, off + g))

    vecg = pl.BlockSpec((1, W), lambda b, g, t: (0, g))
    vec_k = pl.BlockSpec((1, W), lambda b, g, t: (0, ng + g))
    vec_v = pl.BlockSpec((1, W), lambda b, g, t: (0, 2 * ng + g))
    vec_l = pl.BlockSpec((1, D_LORA), lambda b, g, t: (0, 0))

    def lora(rows):
        return pl.BlockSpec((rows, W), lambda b, g, t: (0, g))

    in_specs = [col(col0), col(col0 + ng), col(col0 + 2 * ng),
                pl.BlockSpec((ts, D_LORA), lambda b, g, t: (b * nt + t, 0))]
    args = [zm, zm, zm, zl]
    if has_vres:
        in_specs.append(col(0))
        args.append(vfirst)
    in_specs += [vecg, vec_k, vec_v, vec_l, vecg, lora(LANES), vecg, lora(LANES), lora(2 * LANES)]
    args += [prm["mu_rkv"], prm["mu_rkv"], prm["mu_rkv"], prm["mu_lora"],
             prm["w0"], prm["w2"], prm["a0"], prm["a2"], prm["g2"]]
    if has_vres:
        in_specs += [vecg, lora(LANES)]
        args += [prm["v0"], prm["v2"]]
    in_specs += [vecg] * 5
    args += [prm["kk"], prm["ka"], prm["rk"], prm["lnw"], prm["lnb"]]

    if has_vres:
        out_specs = col(0)
        out_shape = jax.ShapeDtypeStruct((T, D_RWKV), BF16)
    else:
        out_specs = [col(0), col(0)]
        out_shape = [jax.ShapeDtypeStruct((T, D_RWKV), BF16),
                     jax.ShapeDtypeStruct((T, D_RWKV), F32)]
    nbytes = 2 * ts * (5 * W * 4 + D_LORA * 4 + W * 2) + hg * HEAD * HEAD * 4
    res = pl.pallas_call(
        functools.partial(_rwkv_kernel, has_vres),
        grid=(batch, ng, nt),
        in_specs=in_specs,
        out_specs=out_specs,
        out_shape=out_shape,
        scratch_shapes=[pltpu.VMEM((hg, HEAD, HEAD), F32),
                        pltpu.VMEM((SUBLANES, W), F32),
                        pltpu.VMEM((SUBLANES, D_LORA), F32)],
        compiler_params=pltpu.CompilerParams(
            dimension_semantics=("arbitrary", "arbitrary", "arbitrary"),
            vmem_limit_bytes=_vmem_limit(nbytes)),
        name="rwkv7",
    )(*args)
    if has_vres:
        return res, vfirst
    return res[0], res[1]


def _oproj_kernel(h_ref, a_ref, b_ref, wa_ref, wb_ref, o_ref):
    acc = jnp.dot(a_ref[...], wa_ref[...], preferred_element_type=F32)
    acc = acc + jnp.dot(b_ref[...], wb_ref[...], preferred_element_type=F32)
    o_ref[...] = h_ref[...] + acc


def _oproj(h, oa, ob, wo, tm=512):
    T = h.shape[0]
    nbytes = 2 * (2 * tm * D_MODEL * 4 + 2 * tm * D_LRU * 2 + 2 * D_LRU * D_MODEL * 2)
    return pl.pallas_call(
        _oproj_kernel,
        grid=(T // tm,),
        in_specs=[
            pl.BlockSpec((tm, D_MODEL), lambda i: (i, 0)),
            pl.BlockSpec((tm, D_LRU), lambda i: (i, 0)),
            pl.BlockSpec((tm, D_RWKV), lambda i: (i, 0)),
            pl.BlockSpec((D_LRU, D_MODEL), lambda i: (0, 0)),
            pl.BlockSpec((D_RWKV, D_MODEL), lambda i: (1, 0)),
        ],
        out_specs=pl.BlockSpec((tm, D_MODEL), lambda i: (i, 0)),
        out_shape=jax.ShapeDtypeStruct((T, D_MODEL), F32),
        compiler_params=pltpu.CompilerParams(
            dimension_semantics=("arbitrary",),
            vmem_limit_bytes=_vmem_limit(nbytes)),
        name="oproj",
    )(h, oa, ob, wo, wo)


def _ffn_kernel(tiles_per_seq, h_ref, g_ref, wg_ref, wu_ref, cw_ref, cb_ref, wd_ref,
                o_ref, u_ref, acc_ref, tail_ref):
    i = pl.program_id(0)
    j = pl.program_id(1)

    @pl.when(j == 0)
    def _():
        u_ref[...] = _rms(h_ref[...], g_ref[...]).astype(BF16)
        acc_ref[...] = jnp.zeros_like(acc_ref)

    @pl.when(i % tiles_per_seq == 0)
    def _():
        tail_ref[j] = jnp.zeros(tail_ref.shape[1:], F32)

    u = u_ref[...]
    gate = jnp.dot(u, wg_ref[...], preferred_element_type=F32)
    tm = gate.shape[0]
    tail = tail_ref[j]
    cw = cw_ref[...]
    conv = gate * cw[FFN_CONV - 1:FFN_CONV] + cb_ref[...]
    for d in range(1, FFN_CONV):
        conv = conv + _shift_rows(gate, d, tail) * cw[FFN_CONV - 1 - d:FFN_CONV - d]
    tail_ref[j] = gate[tm - SUBLANES:]
    up = jnp.dot(u, wu_ref[...], preferred_element_type=F32)
    act = (_gelu(conv) * up).astype(BF16)
    acc_ref[...] += jnp.dot(act, wd_ref[...], preferred_element_type=F32)

    @pl.when(j == pl.num_programs(1) - 1)
    def _():
        o_ref[...] = h_ref[...] + acc_ref[...]


def _ffn(h, g, wg, wu, cw, cb, wd, seq, tm=512, tf=512):
    T = h.shape[0]
    nf = D_FF // tf
    nbytes = (2 * (2 * tm * D_MODEL * 4 + 3 * D_MODEL * tf * 2) + tm * D_MODEL * 6
              + nf * SUBLANES * tf * 4 + 6 * tm * tf * 4)
    return pl.pallas_call(
        functools.partial(_ffn_kernel, seq // tm),
        grid=(T // tm, nf),
        in_specs=[
            pl.BlockSpec((tm, D_MODEL), lambda i, j: (i, 0)),
            pl.BlockSpec((1, D_MODEL), lambda i, j: (0, 0)),
            pl.BlockSpec((D_MODEL, tf), lambda i, j: (0, j)),
            pl.BlockSpec((D_MODEL, tf), lambda i, j: (0, j)),
            pl.BlockSpec((FFN_CONV, tf), lambda i, j: (0, j)),
            pl.BlockSpec((1, tf), lambda i, j: (0, j)),
            pl.BlockSpec((tf, D_MODEL), lambda i, j: (j, 0)),
        ],
        out_specs=pl.BlockSpec((tm, D_MODEL), lambda i, j: (i, 0)),
        out_shape=jax.ShapeDtypeStruct((T, D_MODEL), F32),
        scratch_shapes=[pltpu.VMEM((tm, D_MODEL), BF16),
                        pltpu.VMEM((tm, D_MODEL), F32),
                        pltpu.VMEM((nf, SUBLANES, tf), F32)],
        compiler_params=pltpu.CompilerParams(
            dimension_semantics=("arbitrary", "arbitrary"),
            vmem_limit_bytes=_vmem_limit(nbytes)),
        name="ffn",
    )(h, g, wg, wu, cw, cb, wd)


def _ple_kernel(final, h_ref, p_ref, g_ref, wg_ref, wp_ref, gp_ref, gf_ref, o_ref):
    h = h_ref[...]
    u = _rms(h, g_ref[...]).astype(BF16)
    gate = _sigmoid(jnp.dot(u, wg_ref[...], preferred_element_type=F32))
    proj = jnp.dot(p_ref[...].astype(BF16), wp_ref[...], preferred_element_type=F32)
    out = h + _rms(gate * proj, gp_ref[...])
    if final:
        out = _rms(out, gf_ref[...])
    o_ref[...] = out


def _ple(h, p, g, wg, wp, gp, gf, final, tm=512):
    T = h.shape[0]
    vec = pl.BlockSpec((1, D_MODEL), lambda i: (0, 0))
    nbytes = 2 * (2 * tm * D_MODEL * 4 + tm * D_PLE * 4 + D_MODEL * D_MODEL * 2
                  + D_PLE * D_MODEL * 2) + 4 * tm * D_MODEL * 4
    return pl.pallas_call(
        functools.partial(_ple_kernel, final),
        grid=(T // tm,),
        in_specs=[
            pl.BlockSpec((tm, D_MODEL), lambda i: (i, 0)),
            pl.BlockSpec((tm, D_PLE), lambda i: (i, 0)),
            vec,
            pl.BlockSpec((D_MODEL, D_MODEL), lambda i: (0, 0)),
            pl.BlockSpec((D_PLE, D_MODEL), lambda i: (0, 0)),
            vec, vec,
        ],
        out_specs=pl.BlockSpec((tm, D_MODEL), lambda i: (i, 0)),
        out_shape=jax.ShapeDtypeStruct((T, D_MODEL), F32),
        compiler_params=pltpu.CompilerParams(
            dimension_semantics=("arbitrary",),
            vmem_limit_bytes=_vmem_limit(nbytes)),
        name="ple",
    )(h, p, g, wg, wp, gp, gf)


def _row(v):
    return v.reshape(1, -1).astype(F32)


def _pad_rows(w, top, total):
    return jnp.pad(w, ((top, total - top - w.shape[0]), (0, 0)))


def kernel(x, p, ln_mix, w_in, w_in_vres, mu_shift, mu_shift_vres, conv_a_w, conv_a_b, lru_wx, lru_bx, lru_wa, lru_ba, lru_lambda, lru_norm, rwkv_w0, rwkv_w2, rwkv_a0, rwkv_a2, rwkv_v0, rwkv_v2, rwkv_g2, rwkv_kk, rwkv_ka, rwkv_rk, rwkv_lnx_w, rwkv_lnx_b, w_o, ln_ffn, w_gate, w_up, conv_f_w, conv_f_b, w_down, ln_ple, w_ple_gate, w_ple_proj, ln_ple_post, ln_final):
    batch, seq, _ = x.shape
    depth = w_in.shape[0]
    T = batch * seq
    h = x.reshape(T, D_MODEL)
    n_lora = LORA_W + LORA_A + LORA_G
    vfirst = None
    for i in range(depth):
        w_main = w_in[i][:, :D_MAIN].astype(BF16)
        lora_cols = [w_in[i][:, D_MAIN:]]
        mu_l = [mu_shift[i][3 * D_RWKV:]]
        if i > 0:
            lora_cols.append(w_in_vres[i - 1])
            mu_l.append(mu_shift_vres[i - 1])
        w_lora = jnp.concatenate(lora_cols, axis=1)
        w_lora = jnp.pad(w_lora, ((0, 0), (0, D_LORA - w_lora.shape[1]))).astype(BF16)
        mu_lora = jnp.concatenate(mu_l, axis=0)
        mu_lora = jnp.pad(mu_lora, (0, D_LORA - mu_lora.shape[0]))

        zm, zl = _inproj(h, _row(ln_mix[i]), w_main, w_lora)

        out_a = _lru(zm, conv_a_w[i], _row(conv_a_b[i]), lru_wx[i].astype(BF16), _row(lru_bx[i]),
                     lru_wa[i].astype(BF16), _row(lru_ba[i]), _row(lru_lambda[i]),
                     _row(lru_norm[i]), batch, seq)

        prm = {
            "mu_rkv": _row(mu_shift[i][:3 * D_RWKV]),
            "mu_lora": _row(mu_lora),
            "w0": _row(rwkv_w0[i]),
            "w2": _pad_rows(rwkv_w2[i], 0, LANES),
            "a0": _row(rwkv_a0[i]),
            "a2": _pad_rows(rwkv_a2[i], LORA_W, LANES),
            "g2": _pad_rows(rwkv_g2[i], 0, 2 * LANES),
            "kk": _row(rwkv_kk[i]), "ka": _row(rwkv_ka[i]), "rk": _row(rwkv_rk[i]),
            "lnw": _row(rwkv_lnx_w[i]), "lnb": _row(rwkv_lnx_b[i]),
        }
        if i > 0:
            prm["v0"] = _row(rwkv_v0[i - 1])
            prm["v2"] = _pad_rows(rwkv_v2[i - 1], n_lora - 2 * LANES, LANES)
        out_b, vfirst = _rwkv(zm, zl, vfirst, prm, batch, seq)

        h = _oproj(h, out_a, out_b, w_o[i].astype(BF16))
        h = _ffn(h, _row(ln_ffn[i]), w_gate[i].astype(BF16), w_up[i].astype(BF16),
                 conv_f_w[i], _row(conv_f_b[i]), w_down[i].astype(BF16), seq)
        h = _ple(h, p[i].reshape(T, D_PLE), _row(ln_ple[i]), w_ple_gate[i].astype(BF16),
                 w_ple_proj[i].astype(BF16), _row(ln_ple_post[i]), _row(ln_final),
                 final=(i == depth - 1))
    return h.reshape(batch, seq, D_MODEL)
```

```python
import functools
import math

import jax
import jax.numpy as jnp
from jax import lax
from jax.experimental import pallas as pl
from jax.experimental.pallas import tpu as pltpu

F32 = jnp.float32
BF16 = jnp.bfloat16

D_MODEL = 2048
D_LRU = 1024
D_RWKV = 1024
LRU_HEADS = 4
LRU_BLOCK = 256
LRU_CONV = 4
LRU_C = 8.0
HEAD = 64
N_HEADS = D_RWKV // HEAD
LORA_W = 64
LORA_A = 64
LORA_V = 32
LORA_G = 160
D_MAIN = 2 * D_LRU + 3 * D_RWKV
D_LORA = 384
D_FF = 3 * D_MODEL
FFN_CONV = 3
D_PLE = 256
RMS_EPS = 1e-6
LNX_EPS = 64e-5

V7X_VMEM_BYTES = 64 * 1024 * 1024
SUBLANES = 8
LANES = 128

CHUNK = 64
PAIR = 2 * HEAD
HEAD_GROUP = 16
RWKV_ROWS = 512


def _vmem_limit(nbytes):
    return int(min(V7X_VMEM_BYTES - 8 * 1024 * 1024, nbytes + 16 * 1024 * 1024))


def _rms(x, g):
    return x * lax.rsqrt(jnp.mean(x * x, axis=-1, keepdims=True) + RMS_EPS) * g


def _gelu(x):
    c = math.sqrt(2.0 / math.pi)
    return 0.5 * x * (1.0 + jnp.tanh(c * (x + 0.044715 * (x * x * x))))


def _sigmoid(x):
    return 1.0 / (1.0 + jnp.exp(-x))


def _softplus(x):
    return jnp.maximum(x, 0.0) + jnp.log1p(jnp.exp(-jnp.abs(x)))


def _shift_rows(x, d, prev8):
    rolled = pltpu.roll(x, d, axis=0)
    prev = pltpu.roll(prev8, d, axis=0)
    row = lax.broadcasted_iota(jnp.int32, prev8.shape, 0)
    top = jnp.where(row < d, prev, rolled[:SUBLANES])
    return jnp.concatenate([top, rolled[SUBLANES:]], axis=0)


def _inproj_kernel(x_ref, g_ref, wm_ref, wl_ref, zm_ref, zl_ref, u_ref):
    @pl.when(pl.program_id(1) == 0)
    def _():
        u_ref[...] = _rms(x_ref[...], g_ref[...]).astype(BF16)
        zl_ref[...] = jnp.dot(u_ref[...], wl_ref[...], preferred_element_type=F32)

    zm_ref[...] = jnp.dot(u_ref[...], wm_ref[...], preferred_element_type=F32)


def _inproj(h, g, w_in, layer, w_lora, tm=1024, tn=1024):
    T = h.shape[0]
    nbytes = 2 * (tm * D_MODEL * 4 + D_MODEL * tn * 2 + D_MODEL * D_LORA * 2
                  + tm * tn * 4 + tm * D_LORA * 4) + tm * D_MODEL * 2
    return pl.pallas_call(
        _inproj_kernel,
        grid=(T // tm, D_MAIN // tn),
        in_specs=[
            pl.BlockSpec((tm, D_MODEL), lambda i, j: (i, 0)),
            pl.BlockSpec((1, D_MODEL), lambda i, j: (0, 0)),
            pl.BlockSpec((None, D_MODEL, tn), lambda i, j: (layer, 0, j)),
            pl.BlockSpec((D_MODEL, D_LORA), lambda i, j: (0, 0)),
        ],
        out_specs=[
            pl.BlockSpec((tm, tn), lambda i, j: (i, j)),
            pl.BlockSpec((tm, D_LORA), lambda i, j: (i, 0)),
        ],
        out_shape=[
            jax.ShapeDtypeStruct((T, D_MAIN), F32),
            jax.ShapeDtypeStruct((T, D_LORA), F32),
        ],
        scratch_shapes=[pltpu.VMEM((tm, D_MODEL), BF16)],
        compiler_params=pltpu.CompilerParams(
            dimension_semantics=("arbitrary", "arbitrary"),
            vmem_limit_bytes=_vmem_limit(nbytes)),
        name="inproj",
    )(h, g, w_in, w_lora)


def _lru_kernel(xb_ref, yb_ref, cw_ref, cb_ref, wx_ref, bx_ref, wa_ref, ba_ref,
                lam_ref, nrm_ref, o_ref, tail_ref, h_ref):
    t = pl.program_id(1)

    @pl.when(t == 0)
    def _():
        tail_ref[...] = jnp.zeros_like(tail_ref)
        h_ref[...] = jnp.zeros_like(h_ref)

    x = xb_ref[...]
    ts = x.shape[0]
    tail = tail_ref[...]
    cw = cw_ref[...]
    xc = x * cw[LRU_CONV - 1:LRU_CONV] + cb_ref[...]
    for d in range(1, LRU_CONV):
        xc = xc + _shift_rows(x, d, tail) * cw[LRU_CONV - 1 - d:LRU_CONV - d]
    tail_ref[...] = x[ts - SUBLANES:]

    xcb = xc.astype(BF16)
    gx, ga = [], []
    for hd in range(LRU_HEADS):
        blk = xcb[:, hd * LRU_BLOCK:(hd + 1) * LRU_BLOCK]
        gx.append(jnp.dot(blk, wx_ref[hd], preferred_element_type=F32))
        ga.append(jnp.dot(blk, wa_ref[hd], preferred_element_type=F32))
    gate_x = _sigmoid(jnp.concatenate(gx, axis=-1) + bx_ref[...])
    gate_a = _sigmoid(jnp.concatenate(ga, axis=-1) + ba_ref[...])
    log_a = (-LRU_C) * gate_a * _softplus(-lam_ref[...])
    a = jnp.exp(log_a)
    mult = jnp.sqrt(1.0 - a * a)
    row = lax.broadcasted_iota(jnp.int32, (ts, 1), 0)
    mult = jnp.where(jnp.logical_and(row == 0, t == 0), 1.0, mult)
    b = xc * gate_x * mult

    ngroup = ts // SUBLANES
    a = a.reshape(ngroup, SUBLANES, D_LRU)
    b = b.reshape(ngroup, SUBLANES, D_LRU)
    sub = lax.broadcasted_iota(jnp.int32, (1, SUBLANES, 1), 1)
    d = 1
    while d < SUBLANES:
        keep = sub >= d
        a_sh = jnp.where(keep, pltpu.roll(a, d, axis=1), 1.0)
        b_sh = jnp.where(keep, pltpu.roll(b, d, axis=1), 0.0)
        b = a * b_sh + b
        a = a * a_sh
        d *= 2
    carry = h_ref[0:1, :]
    hs = []
    for grp in range(ngroup):
        hg = a[grp] * carry + b[grp]
        hs.append(hg)
        carry = hg[SUBLANES - 1:SUBLANES]
    h = jnp.concatenate(hs, axis=0)
    h_ref[0:1, :] = carry

    y = h * _gelu(yb_ref[...])
    o_ref[...] = _rms(y, nrm_ref[...]).astype(o_ref.dtype)


def _lru(zm, cw, cb, wx, bx, wa, ba, lam, nrm, batch, seq, ts=256):
    T = zm.shape[0]
    nt = seq // ts
    vec = pl.BlockSpec((1, D_LRU), lambda b, t: (0, 0))
    mat = pl.BlockSpec((LRU_HEADS, LRU_BLOCK, LRU_BLOCK), lambda b, t: (0, 0, 0))
    nbytes = 2 * (2 * ts * D_LRU * 4 + ts * D_LRU * 2) + 24 * ts * D_LRU * 4
    return pl.pallas_call(
        _lru_kernel,
        grid=(batch, nt),
        in_specs=[
            pl.BlockSpec((ts, D_LRU), lambda b, t: (b * nt + t, 0)),
            pl.BlockSpec((ts, D_LRU), lambda b, t: (b * nt + t, 1)),
            pl.BlockSpec((LRU_CONV, D_LRU), lambda b, t: (0, 0)),
            vec, mat, vec, mat, vec, vec, vec,
        ],
        out_specs=pl.BlockSpec((ts, D_LRU), lambda b, t: (b * nt + t, 0)),
        out_shape=jax.ShapeDtypeStruct((T, D_LRU), BF16),
        scratch_shapes=[pltpu.VMEM((SUBLANES, D_LRU), F32),
                        pltpu.VMEM((SUBLANES, D_LRU), F32)],
        compiler_params=pltpu.CompilerParams(
            dimension_semantics=("arbitrary", "arbitrary"),
            vmem_limit_bytes=_vmem_limit(nbytes)),
        name="rglru",
    )(zm, zm, cw, cb, wx, bx, wa, ba, lam, nrm)


def _mm(a, b):
    return jnp.dot(a.astype(BF16), b.astype(BF16), preferred_element_type=F32)


def _mm_nt(a, b):
    return lax.dot_general(a.astype(BF16), b.astype(BF16), (((1,), (1,)), ((), ())),
                           preferred_element_type=F32)


def _mm_tn(a, b):
    return lax.dot_general(a.astype(BF16), b.astype(BF16), (((0,), (0,)), ((), ())),
                           preferred_element_type=F32)


def _split3(x):
    hi = x.astype(BF16)
    r1 = x - hi.astype(F32)
    mid = r1.astype(BF16)
    lo = (r1 - mid.astype(F32)).astype(BF16)
    return hi, mid, lo


def _seg_sum(x):
    lane_lo = lax.broadcasted_iota(jnp.int32, (1, PAIR), 1) < HEAD
    out = []
    for p in range(x.shape[1] // PAIR):
        t = x[:, p * PAIR:(p + 1) * PAIR]
        s0 = jnp.sum(jnp.where(lane_lo, t, 0.0), axis=-1, keepdims=True)
        s1 = jnp.sum(jnp.where(lane_lo, 0.0, t), axis=-1, keepdims=True)
        out.append(jnp.where(lane_lo, s0, s1))
    return jnp.concatenate(out, axis=-1)


def _rwkv_kernel(has_vres, *refs):
    if has_vres:
        (r_ref, k_ref, v_ref, zl_ref, vf_ref, mur_ref, muk_ref, muv_ref, mul_ref,
         w0_ref, w2_ref, a0_ref, a2_ref, g2_ref, v0_ref, v2_ref,
         kkw_ref, ka_ref, rk_ref, lnw_ref, lnb_ref, o_ref, s_ref, prev_ref, prevz_ref) = refs
        vfo_ref = None
    else:
        (r_ref, k_ref, v_ref, zl_ref, mur_ref, muk_ref, muv_ref, mul_ref,
         w0_ref, w2_ref, a0_ref, a2_ref, g2_ref,
         kkw_ref, ka_ref, rk_ref, lnw_ref, lnb_ref, o_ref, vfo_ref,
         s_ref, prev_ref, prevz_ref) = refs
        vf_ref = v0_ref = v2_ref = None

    C = CHUNK
    nchunk = r_ref.shape[0] // C
    nh = r_ref.shape[1] // HEAD

    @pl.when(pl.program_id(2) == 0)
    def _():
        s_ref[...] = jnp.zeros_like(s_ref)
        prev_ref[...] = jnp.zeros_like(prev_ref)
        prevz_ref[...] = jnp.zeros_like(prevz_ref)

    ri3 = lax.broadcasted_iota(jnp.int32, (C, 3 * C), 0)
    ci3 = lax.broadcasted_iota(jnp.int32, (C, 3 * C), 1) % C
    tri3 = (ri3 >= ci3).astype(BF16)
    ri2 = lax.broadcasted_iota(jnp.int32, (2 * C, 2 * C), 0)
    ci2 = lax.broadcasted_iota(jnp.int32, (2 * C, 2 * C), 1) % C
    keep2 = jnp.where(ri2 < C, ri2, ri2 - C + 1) > ci2
    row0 = lax.broadcasted_iota(jnp.int32, (C, 1), 0) == 0
    zeros_h = jnp.zeros((C, HEAD), BF16)

    def shift_lerp(cur, prev_row, mu):
        sh = jnp.where(row0, prev_row, pltpu.roll(cur, 1, axis=0))
        return cur + (sh - cur) * mu

    def chunk_rows(c):
        return pl.ds(pl.multiple_of(c * C, C), C)

    def prep(c):
        rows = chunk_rows(c)
        first = c == 0
        before = pl.ds(jnp.maximum(c * C - 1, 0), 1)

        def lerp(ref, carried, mu):
            return shift_lerp(ref[rows, :], jnp.where(first, carried, ref[before, :]), mu)

        r = lerp(r_ref, prev_ref[0:1, :], mur_ref[...])
        k = lerp(k_ref, prev_ref[1:2, :], muk_ref[...])
        v = lerp(v_ref, prev_ref[2:3, :], muv_ref[...])
        zl = lerp(zl_ref, prevz_ref[0:1, :], mul_ref[...])

        z01 = zl[:, 0:LANES]
        wpre = w0_ref[...] + _mm(jnp.tanh(z01), w2_ref[...])
        w_log = -_softplus(-wpre) - 0.5
        logw = -jnp.exp(w_log)
        a = _sigmoid(a0_ref[...] + _mm(z01, a2_ref[...]))
        g = _mm(_sigmoid(zl[:, LANES:3 * LANES]), g2_ref[...])
        if has_vres:
            mix = _sigmoid(v0_ref[...] + _mm(zl[:, 2 * LANES:3 * LANES], v2_ref[...]))
            v = v + (vf_ref[rows, :] - v) * mix
        else:
            vfo_ref[rows, :] = v

        kk = k * kkw_ref[...]
        kk = kk / jnp.maximum(jnp.sqrt(_seg_sum(kk * kk)), 1e-12)
        k2 = k * (1.0 + (a - 1.0) * ka_ref[...])
        bb = kk * a

        cum = jnp.dot(tri3, jnp.concatenate(_split3(logw), axis=0), preferred_element_type=F32)
        p_in = jnp.exp(cum)
        p_ex = jnp.exp(cum - logw)
        p_inv = jnp.exp(-cum)
        p_end = p_in[C - 1:C, :]
        rt = r * p_in
        at = -kk * p_ex
        bt = bb * p_inv
        kt = k2 * p_inv
        rt_b, at_b, bt_b, kt_b, v_b = (t.astype(BF16) for t in (rt, at, bt, kt, v))
        bhat_b = (bt * p_end).astype(BF16)
        khat_b = (kt * p_end).astype(BF16)
        bonus = _seg_sum(r * k2 * rk_ref[...]) * v
        return (rt_b, at_b, bt_b, kt_b, v_b, bhat_b, khat_b, at, rt, p_end), (bonus, g)

    def heads_stage(local):
        rt_b, at_b, bt_b, kt_b, v_b, bhat_b, khat_b, at, rt, p_end = local
        heads = range(nh)
        sls = [slice(hh * HEAD, (hh + 1) * HEAD) for hh in heads]
        sc_b, m, vh_b, d = [], [], [], []
        for sl in sls:
            ar = jnp.concatenate([at_b[:, sl], rt_b[:, sl]], axis=0)
            bk = jnp.concatenate([bt_b[:, sl], kt_b[:, sl]], axis=0)
            sc = jnp.where(keep2, _mm_nt(ar, bk), 0.0)
            sc_b.append(sc.astype(BF16))
            m.append(sc[:C, :C])
            vh_b.append(v_b[:, sl])
        for hh in heads:
            zv = jnp.concatenate([zeros_h, vh_b[hh]], axis=0)
            x_loc = jnp.dot(sc_b[hh][:C], zv, preferred_element_type=F32)
            d.append(jnp.concatenate([at[:, sls[hh]], x_loc], axis=-1))
        nstep = int(math.log2(C))
        for i in range(nstep):
            for hh in heads:
                m_b = m[hh].astype(BF16)
                if i + 1 < nstep:
                    rhs = jnp.concatenate([d[hh].astype(BF16), m_b], axis=-1)
                    prod = jnp.dot(m_b, rhs, preferred_element_type=F32)
                    d[hh] = d[hh] + prod[:, :2 * HEAD]
                    m[hh] = prod[:, 2 * HEAD:]
                else:
                    d[hh] = d[hh] + jnp.dot(m_b, d[hh].astype(BF16), preferred_element_type=F32)
        o1, wz = [], []
        for hh in heads:
            gmat = jnp.concatenate(
                [d[hh].astype(BF16), jnp.concatenate([zeros_h, vh_b[hh]], axis=-1)], axis=0)
            o1.append(jnp.dot(sc_b[hh][C:], gmat, preferred_element_type=F32))
            bkh = jnp.concatenate([bhat_b[:, sls[hh]], khat_b[:, sls[hh]]], axis=0)
            wz.append(_mm_tn(gmat, bkh))
        ys = []
        for hh in heads:
            rbar = rt[:, sls[hh]] + o1[hh][:, :HEAD]
            st = s_ref[hh]
            ys.append(_mm_nt(rbar, st) + o1[hh][:, HEAD:])
            s_ref[hh] = st * p_end[:, sls[hh]] + _mm(st, wz[hh][:HEAD]) + wz[hh][HEAD:]

        return jnp.concatenate(ys, axis=-1)

    def tail(c, y, post):
        bonus, g = post
        mean = _seg_sum(y) * (1.0 / HEAD)
        yc = y - mean
        var = _seg_sum(yc * yc) * (1.0 / HEAD)
        yn = yc * lax.rsqrt(var + LNX_EPS) * lnw_ref[...] + lnb_ref[...]
        o_ref[chunk_rows(c), :] = ((yn + bonus) * g).astype(o_ref.dtype)

    def body(i, carry):
        local, y_prev, post_prev, post_cur = carry
        tail(jnp.maximum(i - 1, 0), y_prev, post_prev)
        local_next, post_next = prep(jnp.minimum(i + 1, nchunk - 1))
        y = heads_stage(local)
        return local_next, y, post_cur, post_next

    local0, post0 = prep(0)
    zeros_w = jnp.zeros((C, r_ref.shape[1]), F32)
    _, y_last, post_last, _ = lax.fori_loop(
        0, nchunk, body, (local0, zeros_w, (zeros_w, zeros_w), post0))
    tail(nchunk - 1, y_last, post_last)

    last = pl.ds(r_ref.shape[0] - 1, 1)
    prev_ref[0:1, :] = r_ref[last, :]
    prev_ref[1:2, :] = k_ref[last, :]
    prev_ref[2:3, :] = v_ref[last, :]
    prevz_ref[0:1, :] = zl_ref[last, :]


def _rwkv(zm, zl, vfirst, prm, batch, seq, hg=HEAD_GROUP, ts=RWKV_ROWS):
    T = zm.shape[0]
    W = hg * HEAD
    ng = D_RWKV // W
    nt = seq // ts
    col0 = 2 * D_LRU // W
    has_vres = vfirst is not None

    def col(off):
        return pl.BlockSpec((ts, W), lambda b, g, t: (b * nt + t, off + g))

    vecg = pl.BlockSpec((1, W), lambda b, g, t: (0, g))
    vec_k = pl.BlockSpec((1, W), lambda b, g, t: (0, ng + g))
    vec_v = pl.BlockSpec((1, W), lambda b, g, t: (0, 2 * ng + g))
    vec_l = pl.BlockSpec((1, D_LORA), lambda b, g, t: (0, 0))

    def lora(rows):
        return pl.BlockSpec((rows, W), lambda b, g, t: (0, g))

    in_specs = [col(col0), col(col0 + ng), col(col0 + 2 * ng),
                pl.BlockSpec((ts, D_LORA), lambda b, g, t: (b * nt + t, 0))]
    args = [zm, zm, zm, zl]
    if has_vres:
        in_specs.append(col(0))
        args.append(vfirst)
    in_specs += [vecg, vec_k, vec_v, vec_l, vecg, lora(LANES), vecg, lora(LANES), lora(2 * LANES)]
    args += [prm["mu_rkv"], prm["mu_rkv"], prm["mu_rkv"], prm["mu_lora"],
             prm["w0"], prm["w2"], prm["a0"], prm["a2"], prm["g2"]]
    if has_vres:
        in_specs += [vecg, lora(LANES)]
        args += [prm["v0"], prm["v2"]]
    in_specs += [vecg] * 5
    args += [prm["kk"], prm["ka"], prm["rk"], prm["lnw"], prm["lnb"]]

    if has_vres:
        out_specs = col(0)
        out_shape = jax.ShapeDtypeStruct((T, D_RWKV), BF16)
    else:
        out_specs = [col(0), col(0)]
        out_shape = [jax.ShapeDtypeStruct((T, D_RWKV), BF16),
                     jax.ShapeDtypeStruct((T, D_RWKV), F32)]
    nbytes = 2 * ts * (5 * W * 4 + D_LORA * 4 + W * 2) + hg * HEAD * HEAD * 4
    res = pl.pallas_call(
        functools.partial(_rwkv_kernel, has_vres),
        grid=(batch, ng, nt),
        in_specs=in_specs,
        out_specs=out_specs,
        out_shape=out_shape,
        scratch_shapes=[pltpu.VMEM((hg, HEAD, HEAD), F32),
                        pltpu.VMEM((SUBLANES, W), F32),
                        pltpu.VMEM((SUBLANES, D_LORA), F32)],
        compiler_params=pltpu.CompilerParams(
            dimension_semantics=("arbitrary", "arbitrary", "arbitrary"),
            vmem_limit_bytes=_vmem_limit(nbytes)),
        name="rwkv7",
    )(*args)
    if has_vres:
        return res, vfirst
    return res[0], res[1]


def _oproj_kernel(h_ref, a_ref, b_ref, wa_ref, wb_ref, o_ref):
    acc = jnp.dot(a_ref[...], wa_ref[...], preferred_element_type=F32)
    acc = acc + jnp.dot(b_ref[...], wb_ref[...], preferred_element_type=F32)
    o_ref[...] = h_ref[...] + acc


def _oproj(h, oa, ob, wo, layer, tm=512):
    T = h.shape[0]
    nbytes = 2 * (2 * tm * D_MODEL * 4 + 2 * tm * D_LRU * 2 + 2 * D_LRU * D_MODEL * 2)
    return pl.pallas_call(
        _oproj_kernel,
        grid=(T // tm,),
        in_specs=[
            pl.BlockSpec((tm, D_MODEL), lambda i: (i, 0)),
            pl.BlockSpec((tm, D_LRU), lambda i: (i, 0)),
            pl.BlockSpec((tm, D_RWKV), lambda i: (i, 0)),
            pl.BlockSpec((None, D_LRU, D_MODEL), lambda i: (layer, 0, 0)),
            pl.BlockSpec((None, D_RWKV, D_MODEL), lambda i: (layer, 1, 0)),
        ],
        out_specs=pl.BlockSpec((tm, D_MODEL), lambda i: (i, 0)),
        out_shape=jax.ShapeDtypeStruct((T, D_MODEL), F32),
        compiler_params=pltpu.CompilerParams(
            dimension_semantics=("arbitrary",),
            vmem_limit_bytes=_vmem_limit(nbytes)),
        name="oproj",
    )(h, oa, ob, wo, wo)


def _ffn_kernel(tiles_per_seq, h_ref, g_ref, wg_ref, wu_ref, cw_ref, cb_ref, wd_ref,
                o_ref, u_ref, acc_ref, tail_ref):
    i = pl.program_id(0)
    j = pl.program_id(1)

    @pl.when(j == 0)
    def _():
        u_ref[...] = _rms(h_ref[...], g_ref[...]).astype(BF16)
        acc_ref[...] = jnp.zeros_like(acc_ref)

    @pl.when(i % tiles_per_seq == 0)
    def _():
        tail_ref[j] = jnp.zeros(tail_ref.shape[1:], F32)

    u = u_ref[...]
    gate = jnp.dot(u, wg_ref[...], preferred_element_type=F32)
    tm = gate.shape[0]
    tail = tail_ref[j]
    cw = cw_ref[...]
    conv = gate * cw[FFN_CONV - 1:FFN_CONV] + cb_ref[...]
    for d in range(1, FFN_CONV):
        conv = conv + _shift_rows(gate, d, tail) * cw[FFN_CONV - 1 - d:FFN_CONV - d]
    tail_ref[j] = gate[tm - SUBLANES:]
    up = jnp.dot(u, wu_ref[...], preferred_element_type=F32)
    act = (_gelu(conv) * up).astype(BF16)
    acc_ref[...] += jnp.dot(act, wd_ref[...], preferred_element_type=F32)

    @pl.when(j == pl.num_programs(1) - 1)
    def _():
        o_ref[...] = h_ref[...] + acc_ref[...]


def _ffn(h, g, wg, wu, cw, cb, wd, layer, seq, tm=512, tf=768):
    T = h.shape[0]
    nf = D_FF // tf
    nbytes = (2 * (2 * tm * D_MODEL * 4 + 3 * D_MODEL * tf * 2) + tm * D_MODEL * 6
              + nf * SUBLANES * tf * 4 + 6 * tm * tf * 4)
    return pl.pallas_call(
        functools.partial(_ffn_kernel, seq // tm),
        grid=(T // tm, nf),
        in_specs=[
            pl.BlockSpec((tm, D_MODEL), lambda i, j: (i, 0)),
            pl.BlockSpec((1, D_MODEL), lambda i, j: (0, 0)),
            pl.BlockSpec((None, D_MODEL, tf), lambda i, j: (layer, 0, j)),
            pl.BlockSpec((None, D_MODEL, tf), lambda i, j: (layer, 0, j)),
            pl.BlockSpec((FFN_CONV, tf), lambda i, j: (0, j)),
            pl.BlockSpec((1, tf), lambda i, j: (0, j)),
            pl.BlockSpec((None, tf, D_MODEL), lambda i, j: (layer, j, 0)),
        ],
        out_specs=pl.BlockSpec((tm, D_MODEL), lambda i, j: (i, 0)),
        out_shape=jax.ShapeDtypeStruct((T, D_MODEL), F32),
        scratch_shapes=[pltpu.VMEM((tm, D_MODEL), BF16),
                        pltpu.VMEM((tm, D_MODEL), F32),
                        pltpu.VMEM((nf, SUBLANES, tf), F32)],
        compiler_params=pltpu.CompilerParams(
            dimension_semantics=("arbitrary", "arbitrary"),
            vmem_limit_bytes=_vmem_limit(nbytes)),
        name="ffn",
    )(h, g, wg, wu, cw, cb, wd)


def _ple_kernel(final, h_ref, p_ref, g_ref, wg_ref, wp_ref, gp_ref, gf_ref, o_ref):
    h = h_ref[...]
    u = _rms(h, g_ref[...]).astype(BF16)
    gate = _sigmoid(jnp.dot(u, wg_ref[...], preferred_element_type=F32))
    proj = jnp.dot(p_ref[...].astype(BF16), wp_ref[...], preferred_element_type=F32)
    out = h + _rms(gate * proj, gp_ref[...])
    if final:
        out = _rms(out, gf_ref[...])
    o_ref[...] = out


def _ple(h, p, g, wg, layer, wp, gp, gf, final, tm=512):
    T = h.shape[0]
    vec = pl.BlockSpec((1, D_MODEL), lambda i: (0, 0))
    nbytes = 2 * (2 * tm * D_MODEL * 4 + tm * D_PLE * 4 + D_MODEL * D_MODEL * 2
                  + D_PLE * D_MODEL * 2) + 4 * tm * D_MODEL * 4
    return pl.pallas_call(
        functools.partial(_ple_kernel, final),
        grid=(T // tm,),
        in_specs=[
            pl.BlockSpec((tm, D_MODEL), lambda i: (i, 0)),
            pl.BlockSpec((tm, D_PLE), lambda i: (i, 0)),
            vec,
            pl.BlockSpec((None, D_MODEL, D_MODEL), lambda i: (layer, 0, 0)),
            pl.BlockSpec((D_PLE, D_MODEL), lambda i: (0, 0)),
            vec, vec,
        ],
        out_specs=pl.BlockSpec((tm, D_MODEL), lambda i: (i, 0)),
        out_shape=jax.ShapeDtypeStruct((T, D_MODEL), F32),
        compiler_params=pltpu.CompilerParams(
            dimension_semantics=("arbitrary",),
            vmem_limit_bytes=_vmem_limit(nbytes)),
        name="ple",
    )(h, p, g, wg, wp, gp, gf)


def _row(v):
    return v.reshape(1, -1).astype(F32)


def _pad_rows(w, top, total):
    return jnp.pad(w, ((top, total - top - w.shape[0]), (0, 0)))


def kernel(x, p, ln_mix, w_in, w_in_vres, mu_shift, mu_shift_vres, conv_a_w, conv_a_b, lru_wx, lru_bx, lru_wa, lru_ba, lru_lambda, lru_norm, rwkv_w0, rwkv_w2, rwkv_a0, rwkv_a2, rwkv_v0, rwkv_v2, rwkv_g2, rwkv_kk, rwkv_ka, rwkv_rk, rwkv_lnx_w, rwkv_lnx_b, w_o, ln_ffn, w_gate, w_up, conv_f_w, conv_f_b, w_down, ln_ple, w_ple_gate, w_ple_proj, ln_ple_post, ln_final):
    batch, seq, _ = x.shape
    depth = w_in.shape[0]
    T = batch * seq
    h = x.reshape(T, D_MODEL)
    n_lora = LORA_W + LORA_A + LORA_G
    vfirst = None
    w_in_b, w_o_b, w_gate_b, w_up_b, w_down_b, w_ple_gate_b = (
        w.astype(BF16) for w in (w_in, w_o, w_gate, w_up, w_down, w_ple_gate))
    for i in range(depth):
        lora_cols = [w_in[i][:, D_MAIN:]]
        mu_l = [mu_shift[i][3 * D_RWKV:]]
        if i > 0:
            lora_cols.append(w_in_vres[i - 1])
            mu_l.append(mu_shift_vres[i - 1])
        w_lora = jnp.concatenate(lora_cols, axis=1)
        w_lora = jnp.pad(w_lora, ((0, 0), (0, D_LORA - w_lora.shape[1]))).astype(BF16)
        mu_lora = jnp.concatenate(mu_l, axis=0)
        mu_lora = jnp.pad(mu_lora, (0, D_LORA - mu_lora.shape[0]))

        zm, zl = _inproj(h, _row(ln_mix[i]), w_in_b, i, w_lora)

        out_a = _lru(zm, conv_a_w[i], _row(conv_a_b[i]), lru_wx[i].astype(BF16), _row(lru_bx[i]),
                     lru_wa[i].astype(BF16), _row(lru_ba[i]), _row(lru_lambda[i]),
                     _row(lru_norm[i]), batch, seq)

        prm = {
            "mu_rkv": _row(mu_shift[i][:3 * D_RWKV]),
            "mu_lora": _row(mu_lora),
            "w0": _row(rwkv_w0[i]),
            "w2": _pad_rows(rwkv_w2[i], 0, LANES),
            "a0": _row(rwkv_a0[i]),
            "a2": _pad_rows(rwkv_a2[i], LORA_W, LANES),
            "g2": _pad_rows(rwkv_g2[i], 0, 2 * LANES),
            "kk": _row(rwkv_kk[i]), "ka": _row(rwkv_ka[i]), "rk": _row(rwkv_rk[i]),
            "lnw": _row(rwkv_lnx_w[i]), "lnb": _row(rwkv_lnx_b[i]),
        }
        if i > 0:
            prm["v0"] = _row(rwkv_v0[i - 1])
            prm["v2"] = _pad_rows(rwkv_v2[i - 1], n_lora - 2 * LANES, LANES)
        out_b, vfirst = _rwkv(zm, zl, vfirst, prm, batch, seq)

        h = _oproj(h, out_a, out_b, w_o_b, i)
        h = _ffn(h, _row(ln_ffn[i]), w_gate_b, w_up_b, conv_f_w[i], _row(conv_f_b[i]),
                 w_down_b, i, seq)
        h = _ple(h, p[i].reshape(T, D_PLE), _row(ln_ple[i]), w_ple_gate_b, i,
                 w_ple_proj[i].astype(BF16), _row(ln_ple_post[i]), _row(ln_final),
                 final=(i == depth - 1))
    return h.reshape(batch, seq, D_MODEL)
```

```python
import functools
import math

import jax
import jax.numpy as jnp
from jax import lax
from jax.experimental import pallas as pl
from jax.experimental.pallas import tpu as pltpu

F32 = jnp.float32
BF16 = jnp.bfloat16

D_MODEL = 2048
D_LRU = 1024
D_RWKV = 1024
LRU_HEADS = 4
LRU_BLOCK = 256
LRU_CONV = 4
LRU_C = 8.0
HEAD = 64
N_HEADS = D_RWKV // HEAD
LORA_W = 64
LORA_A = 64
LORA_V = 32
LORA_G = 160
D_MAIN = 2 * D_LRU + 3 * D_RWKV
D_LORA = 384
D_FF = 3 * D_MODEL
FFN_CONV = 3
D_PLE = 256
RMS_EPS = 1e-6
LNX_EPS = 64e-5

V7X_VMEM_BYTES = 64 * 1024 * 1024
SUBLANES = 8
LANES = 128

CHUNK = 64
PAIR = 2 * HEAD
CHUNK_GROUP = 1
HEAD_GROUP = 16
RWKV_ROWS = 1024


def _vmem_limit(nbytes):
    return int(min(V7X_VMEM_BYTES - 4 * 1024 * 1024, nbytes + 16 * 1024 * 1024))


def _rms(x, g):
    return x * lax.rsqrt(jnp.mean(x * x, axis=-1, keepdims=True) + RMS_EPS) * g


def _gelu(x):
    c = math.sqrt(2.0 / math.pi)
    return 0.5 * x * (1.0 + jnp.tanh(c * (x + 0.044715 * (x * x * x))))


def _sigmoid(x):
    return 1.0 / (1.0 + jnp.exp(-x))


def _softplus(x):
    return jnp.maximum(x, 0.0) + jnp.log1p(jnp.exp(-jnp.abs(x)))


def _shift_rows(x, d, prev8):
    rolled = pltpu.roll(x, d, axis=0)
    prev = pltpu.roll(prev8, d, axis=0)
    row = lax.broadcasted_iota(jnp.int32, prev8.shape, 0)
    top = jnp.where(row < d, prev, rolled[:SUBLANES])
    return jnp.concatenate([top, rolled[SUBLANES:]], axis=0)


def _inproj_kernel(x_ref, g_ref, wm_ref, wl_ref, zm_ref, zl_ref, u_ref):
    @pl.when(pl.program_id(1) == 0)
    def _():
        u_ref[...] = _rms(x_ref[...], g_ref[...]).astype(BF16)
        zl_ref[...] = jnp.dot(u_ref[...], wl_ref[...], preferred_element_type=F32)

    zm_ref[...] = jnp.dot(u_ref[...], wm_ref[...], preferred_element_type=F32)


def _inproj(h, g, w_in, layer, w_lora, tm=1024, tn=1024):
    T = h.shape[0]
    nbytes = 2 * (tm * D_MODEL * 4 + D_MODEL * tn * 2 + D_MODEL * D_LORA * 2
                  + tm * tn * 4 + tm * D_LORA * 4) + tm * D_MODEL * 2
    return pl.pallas_call(
        _inproj_kernel,
        grid=(T // tm, D_MAIN // tn),
        in_specs=[
            pl.BlockSpec((tm, D_MODEL), lambda i, j: (i, 0)),
            pl.BlockSpec((1, D_MODEL), lambda i, j: (0, 0)),
            pl.BlockSpec((None, D_MODEL, tn), lambda i, j: (layer, 0, j)),
            pl.BlockSpec((D_MODEL, D_LORA), lambda i, j: (0, 0)),
        ],
        out_specs=[
            pl.BlockSpec((tm, tn), lambda i, j: (i, j)),
            pl.BlockSpec((tm, D_LORA), lambda i, j: (i, 0)),
        ],
        out_shape=[
            jax.ShapeDtypeStruct((T, D_MAIN), F32),
            jax.ShapeDtypeStruct((T, D_LORA), F32),
        ],
        scratch_shapes=[pltpu.VMEM((tm, D_MODEL), BF16)],
        compiler_params=pltpu.CompilerParams(
            dimension_semantics=("arbitrary", "arbitrary"),
            vmem_limit_bytes=_vmem_limit(nbytes)),
        name="inproj",
    )(h, g, w_in, w_lora)


def _lru_kernel(xb_ref, yb_ref, cw_ref, cb_ref, wx_ref, bx_ref, wa_ref, ba_ref,
                lam_ref, nrm_ref, o_ref, tail_ref, h_ref):
    t = pl.program_id(1)

    @pl.when(t == 0)
    def _():
        tail_ref[...] = jnp.zeros_like(tail_ref)
        h_ref[...] = jnp.zeros_like(h_ref)

    x = xb_ref[...]
    ts = x.shape[0]
    tail = tail_ref[...]
    cw = cw_ref[...]
    xc = x * cw[LRU_CONV - 1:LRU_CONV] + cb_ref[...]
    for d in range(1, LRU_CONV):
        xc = xc + _shift_rows(x, d, tail) * cw[LRU_CONV - 1 - d:LRU_CONV - d]
    tail_ref[...] = x[ts - SUBLANES:]

    xcb = xc.astype(BF16)
    gx, ga = [], []
    for hd in range(LRU_HEADS):
        blk = xcb[:, hd * LRU_BLOCK:(hd + 1) * LRU_BLOCK]
        gx.append(jnp.dot(blk, wx_ref[hd], preferred_element_type=F32))
        ga.append(jnp.dot(blk, wa_ref[hd], preferred_element_type=F32))
    gate_x = _sigmoid(jnp.concatenate(gx, axis=-1) + bx_ref[...])
    gate_a = _sigmoid(jnp.concatenate(ga, axis=-1) + ba_ref[...])
    log_a = (-LRU_C) * gate_a * _softplus(-lam_ref[...])
    a = jnp.exp(log_a)
    mult = jnp.sqrt(1.0 - a * a)
    row = lax.broadcasted_iota(jnp.int32, (ts, 1), 0)
    mult = jnp.where(jnp.logical_and(row == 0, t == 0), 1.0, mult)
    b = xc * gate_x * mult

    ngroup = ts // SUBLANES
    a = a.reshape(ngroup, SUBLANES, D_LRU)
    b = b.reshape(ngroup, SUBLANES, D_LRU)
    sub = lax.broadcasted_iota(jnp.int32, (1, SUBLANES, 1), 1)
    d = 1
    while d < SUBLANES:
        keep = sub >= d
        a_sh = jnp.where(keep, pltpu.roll(a, d, axis=1), 1.0)
        b_sh = jnp.where(keep, pltpu.roll(b, d, axis=1), 0.0)
        b = a * b_sh + b
        a = a * a_sh
        d *= 2
    carry = h_ref[0:1, :]
    hs = []
    for grp in range(ngroup):
        hg = a[grp] * carry + b[grp]
        hs.append(hg)
        carry = hg[SUBLANES - 1:SUBLANES]
    h = jnp.concatenate(hs, axis=0)
    h_ref[0:1, :] = carry

    y = h * _gelu(yb_ref[...])
    o_ref[...] = _rms(y, nrm_ref[...]).astype(o_ref.dtype)


def _lru(zm, cw, cb, wx, bx, wa, ba, lam, nrm, batch, seq, ts=256):
    T = zm.shape[0]
    nt = seq // ts
    vec = pl.BlockSpec((1, D_LRU), lambda b, t: (0, 0))
    mat = pl.BlockSpec((LRU_HEADS, LRU_BLOCK, LRU_BLOCK), lambda b, t: (0, 0, 0))
    nbytes = 2 * (2 * ts * D_LRU * 4 + ts * D_LRU * 2) + 24 * ts * D_LRU * 4
    return pl.pallas_call(
        _lru_kernel,
        grid=(batch, nt),
        in_specs=[
            pl.BlockSpec((ts, D_LRU), lambda b, t: (b * nt + t, 0)),
            pl.BlockSpec((ts, D_LRU), lambda b, t: (b * nt + t, 1)),
            pl.BlockSpec((LRU_CONV, D_LRU), lambda b, t: (0, 0)),
            vec, mat, vec, mat, vec, vec, vec,
        ],
        out_specs=pl.BlockSpec((ts, D_LRU), lambda b, t: (b * nt + t, 0)),
        out_shape=jax.ShapeDtypeStruct((T, D_LRU), BF16),
        scratch_shapes=[pltpu.VMEM((SUBLANES, D_LRU), F32),
                        pltpu.VMEM((SUBLANES, D_LRU), F32)],
        compiler_params=pltpu.CompilerParams(
            dimension_semantics=("arbitrary", "arbitrary"),
            vmem_limit_bytes=_vmem_limit(nbytes)),
        name="rglru",
    )(zm, zm, cw, cb, wx, bx, wa, ba, lam, nrm)


def _mm(a, b):
    return jnp.dot(a.astype(BF16), b.astype(BF16), preferred_element_type=F32)


def _mm_nt(a, b):
    return lax.dot_general(a.astype(BF16), b.astype(BF16), (((1,), (1,)), ((), ())),
                           preferred_element_type=F32)


def _mm_tn(a, b):
    return lax.dot_general(a.astype(BF16), b.astype(BF16), (((0,), (0,)), ((), ())),
                           preferred_element_type=F32)


def _split3(x):
    hi = x.astype(BF16)
    r1 = x - hi.astype(F32)
    mid = r1.astype(BF16)
    lo = (r1 - mid.astype(F32)).astype(BF16)
    return hi, mid, lo


def _seg_sum(x):
    lane_lo = lax.broadcasted_iota(jnp.int32, (1, PAIR), 1) < HEAD
    out = []
    for p in range(x.shape[1] // PAIR):
        t = x[:, p * PAIR:(p + 1) * PAIR]
        s0 = jnp.sum(jnp.where(lane_lo, t, 0.0), axis=-1, keepdims=True)
        s1 = jnp.sum(jnp.where(lane_lo, 0.0, t), axis=-1, keepdims=True)
        out.append(jnp.where(lane_lo, s0, s1))
    return jnp.concatenate(out, axis=-1)


def _rwkv_kernel(has_vres, *refs):
    if has_vres:
        (r_ref, k_ref, v_ref, zl_ref, vf_ref, mur_ref, muk_ref, muv_ref, mul_ref,
         w0_ref, w2_ref, a0_ref, a2_ref, g2_ref, v0_ref, v2_ref,
         kkw_ref, ka_ref, rk_ref, lnw_ref, lnb_ref, o_ref, s_ref, prev_ref, prevz_ref) = refs
        vfo_ref = None
    else:
        (r_ref, k_ref, v_ref, zl_ref, mur_ref, muk_ref, muv_ref, mul_ref,
         w0_ref, w2_ref, a0_ref, a2_ref, g2_ref,
         kkw_ref, ka_ref, rk_ref, lnw_ref, lnb_ref, o_ref, vfo_ref,
         s_ref, prev_ref, prevz_ref) = refs
        vf_ref = v0_ref = v2_ref = None

    C = CHUNK
    nchunk = r_ref.shape[0] // C
    nh = r_ref.shape[1] // HEAD

    @pl.when(pl.program_id(2) == 0)
    def _():
        s_ref[...] = jnp.zeros_like(s_ref)
        prev_ref[...] = jnp.zeros_like(prev_ref)
        prevz_ref[...] = jnp.zeros_like(prevz_ref)

    ri3 = lax.broadcasted_iota(jnp.int32, (C, 3 * C), 0)
    ci3 = lax.broadcasted_iota(jnp.int32, (C, 3 * C), 1) % C
    tri3 = (ri3 >= ci3).astype(BF16)
    ri2 = lax.broadcasted_iota(jnp.int32, (2 * C, 2 * C), 0)
    ci2 = lax.broadcasted_iota(jnp.int32, (2 * C, 2 * C), 1) % C
    keep2 = jnp.where(ri2 < C, ri2, ri2 - C + 1) > ci2
    row0 = lax.broadcasted_iota(jnp.int32, (C, 1), 0) == 0
    zeros_h = jnp.zeros((C, HEAD), BF16)

    def shift_lerp(cur, prev_row, mu):
        sh = jnp.where(row0, prev_row, pltpu.roll(cur, 1, axis=0))
        return cur + (sh - cur) * mu

    def chunk_rows(c):
        return pl.ds(pl.multiple_of(c * C, C), C)

    def prep(c):
        rows = chunk_rows(c)
        first = c == 0
        before = pl.ds(jnp.maximum(c * C - 1, 0), 1)

        def lerp(ref, carried, mu):
            return shift_lerp(ref[rows, :], jnp.where(first, carried, ref[before, :]), mu)

        r = lerp(r_ref, prev_ref[0:1, :], mur_ref[...])
        k = lerp(k_ref, prev_ref[1:2, :], muk_ref[...])
        v = lerp(v_ref, prev_ref[2:3, :], muv_ref[...])
        zl = lerp(zl_ref, prevz_ref[0:1, :], mul_ref[...])

        z01 = zl[:, 0:LANES]
        wpre = w0_ref[...] + _mm(jnp.tanh(z01), w2_ref[...])
        w_log = -_softplus(-wpre) - 0.5
        logw = -jnp.exp(w_log)
        a = _sigmoid(a0_ref[...] + _mm(z01, a2_ref[...]))
        g = _mm(_sigmoid(zl[:, LANES:3 * LANES]), g2_ref[...])
        if has_vres:
            mix = _sigmoid(v0_ref[...] + _mm(zl[:, 2 * LANES:3 * LANES], v2_ref[...]))
            v = v + (vf_ref[rows, :] - v) * mix
        else:
            vfo_ref[rows, :] = v

        kk = k * kkw_ref[...]
        kk = kk / jnp.maximum(jnp.sqrt(_seg_sum(kk * kk)), 1e-12)
        k2 = k * (1.0 + (a - 1.0) * ka_ref[...])
        bb = kk * a

        cum = jnp.dot(tri3, jnp.concatenate(_split3(logw), axis=0), preferred_element_type=F32)
        p_in = jnp.exp(cum)
        p_ex = jnp.exp(cum - logw)
        p_inv = jnp.exp(-cum)
        p_end = p_in[C - 1:C, :]
        rt = r * p_in
        at = -kk * p_ex
        bt = bb * p_inv
        kt = k2 * p_inv
        rt_b, at_b, bt_b, kt_b, v_b = (t.astype(BF16) for t in (rt, at, bt, kt, v))
        bhat_b = (bt * p_end).astype(BF16)
        khat_b = (kt * p_end).astype(BF16)
        bonus = _seg_sum(r * k2 * rk_ref[...]) * v
        return (rt_b, at_b, bt_b, kt_b, v_b, bhat_b, khat_b, at, rt, p_end), (bonus, g)

    def heads_stage(locals_):
        pairs = [(j, hh) for j in range(len(locals_)) for hh in range(nh)]
        sls = [slice(hh * HEAD, (hh + 1) * HEAD) for hh in range(nh)]
        sc_b, m, vh_b, d = {}, {}, {}, {}
        for j, hh in pairs:
            rt_b, at_b, bt_b, kt_b, v_b = locals_[j][:5]
            sl = sls[hh]
            ar = jnp.concatenate([at_b[:, sl], rt_b[:, sl]], axis=0)
            bk = jnp.concatenate([bt_b[:, sl], kt_b[:, sl]], axis=0)
            sc = jnp.where(keep2, _mm_nt(ar, bk), 0.0)
            sc_b[j, hh] = sc.astype(BF16)
            m[j, hh] = sc[:C, :C]
            vh_b[j, hh] = v_b[:, sl]
        for j, hh in pairs:
            at = locals_[j][7]
            zv = jnp.concatenate([zeros_h, vh_b[j, hh]], axis=0)
            x_loc = jnp.dot(sc_b[j, hh][:C], zv, preferred_element_type=F32)
            d[j, hh] = jnp.concatenate([at[:, sls[hh]], x_loc], axis=-1)
        nstep = int(math.log2(C))
        for i in range(nstep):
            for p in pairs:
                m_b = m[p].astype(BF16)
                if i + 1 < nstep:
                    rhs = jnp.concatenate([d[p].astype(BF16), m_b], axis=-1)
                    prod = jnp.dot(m_b, rhs, preferred_element_type=F32)
                    d[p] = d[p] + prod[:, :2 * HEAD]
                    m[p] = prod[:, 2 * HEAD:]
                else:
                    d[p] = d[p] + jnp.dot(m_b, d[p].astype(BF16), preferred_element_type=F32)
        o1, wz = {}, {}
        for j, hh in pairs:
            bhat_b, khat_b = locals_[j][5:7]
            gmat = jnp.concatenate(
                [d[j, hh].astype(BF16), jnp.concatenate([zeros_h, vh_b[j, hh]], axis=-1)],
                axis=0)
            o1[j, hh] = jnp.dot(sc_b[j, hh][C:], gmat, preferred_element_type=F32)
            bkh = jnp.concatenate([bhat_b[:, sls[hh]], khat_b[:, sls[hh]]], axis=0)
            wz[j, hh] = _mm_tn(gmat, bkh)
        states = [s_ref[hh] for hh in range(nh)]
        ys = []
        for j in range(len(locals_)):
            rt, p_end = locals_[j][8:10]
            yj = []
            for hh in range(nh):
                rbar = rt[:, sls[hh]] + o1[j, hh][:, :HEAD]
                st = states[hh]
                yj.append(_mm_nt(rbar, st) + o1[j, hh][:, HEAD:])
                states[hh] = (st * p_end[:, sls[hh]] + _mm(st, wz[j, hh][:HEAD])
                              + wz[j, hh][HEAD:])
            ys.append(jnp.concatenate(yj, axis=-1))
        for hh in range(nh):
            s_ref[hh] = states[hh]
        return ys

    def tail(c, y, post):
        bonus, g = post
        mean = _seg_sum(y) * (1.0 / HEAD)
        yc = y - mean
        var = _seg_sum(yc * yc) * (1.0 / HEAD)
        yn = yc * lax.rsqrt(var + LNX_EPS) * lnw_ref[...] + lnb_ref[...]
        o_ref[chunk_rows(c), :] = ((yn + bonus) * g).astype(o_ref.dtype)

    ngroup = nchunk // CHUNK_GROUP
    members = range(CHUNK_GROUP)

    def prep_group(gi):
        done = [prep(gi * CHUNK_GROUP + j) for j in members]
        return tuple(l for l, _ in done), tuple(p for _, p in done)

    def tail_group(gi, ys, posts):
        for j in members:
            tail(gi * CHUNK_GROUP + j, ys[j], posts[j])

    def body(i, carry):
        locals_, y_prev, post_prev, post_cur = carry
        tail_group(jnp.maximum(i - 1, 0), y_prev, post_prev)
        locals_next, post_next = prep_group(jnp.minimum(i + 1, ngroup - 1))
        ys = tuple(heads_stage(locals_))
        return locals_next, ys, post_cur, post_next

    locals0, post0 = prep_group(0)
    zeros_w = jnp.zeros((C, r_ref.shape[1]), F32)
    zeros_y = tuple(zeros_w for _ in members)
    zeros_post = tuple((zeros_w, zeros_w) for _ in members)
    _, y_last, post_last, _ = lax.fori_loop(
        0, ngroup, body, (locals0, zeros_y, zeros_post, post0))
    tail_group(ngroup - 1, y_last, post_last)

    last = pl.ds(r_ref.shape[0] - 1, 1)
    prev_ref[0:1, :] = r_ref[last, :]
    prev_ref[1:2, :] = k_ref[last, :]
    prev_ref[2:3, :] = v_ref[last, :]
    prevz_ref[0:1, :] = zl_ref[last, :]


def _rwkv(zm, zl, vfirst, prm, batch, seq, hg=HEAD_GROUP, ts=RWKV_ROWS):
    T = zm.shape[0]
    W = hg * HEAD
    ng = D_RWKV // W
    nt = seq // ts
    col0 = 2 * D_LRU // W
    has_vres = vfirst is not None

    def col(off):
        return pl.BlockSpec((ts, W), lambda b, g, t: (b * nt + t, off + g))

    vecg = pl.BlockSpec((1, W), lambda b, g, t: (0, g))
    vec_k = pl.BlockSpec((1, W), lambda b, g, t: (0, ng + g))
    vec_v = pl.BlockSpec((1, W), lambda b, g, t: (0, 2 * ng + g))
    vec_l = pl.BlockSpec((1, D_LORA), lambda b, g, t: (0, 0))

    def lora(rows):
        return pl.BlockSpec((rows, W), lambda b, g, t: (0, g))

    in_specs = [col(col0), col(col0 + ng), col(col0 + 2 * ng),
                pl.BlockSpec((ts, D_LORA), lambda b, g, t: (b * nt + t, 0))]
    args = [zm, zm, zm, zl]
    if has_vres:
        in_specs.append(col(0))
        args.append(vfirst)
    in_specs += [vecg, vec_k, vec_v, vec_l, vecg, lora(LANES), vecg, lora(LANES), lora(2 * LANES)]
    args += [prm["mu_rkv"], prm["mu_rkv"], prm["mu_rkv"], prm["mu_lora"],
             prm["w0"], prm["w2"], prm["a0"], prm["a2"], prm["g2"]]
    if has_vres:
        in_specs += [vecg, lora(LANES)]
        args += [prm["v0"], prm["v2"]]
    in_specs += [vecg] * 5
    args += [prm["kk"], prm["ka"], prm["rk"], prm["lnw"], prm["lnb"]]

    if has_vres:
        out_specs = col(0)
        out_shape = jax.ShapeDtypeStruct((T, D_RWKV), BF16)
    else:
        out_specs = [col(0), col(0)]
        out_shape = [jax.ShapeDtypeStruct((T, D_RWKV), BF16),
                     jax.ShapeDtypeStruct((T, D_RWKV), F32)]
    nbytes = 2 * ts * (5 * W * 4 + D_LORA * 4 + W * 2) + hg * HEAD * HEAD * 4
    res = pl.pallas_call(
        functools.partial(_rwkv_kernel, has_vres),
        grid=(batch, ng, nt),
        in_specs=in_specs,
        out_specs=out_specs,
        out_shape=out_shape,
        scratch_shapes=[pltpu.VMEM((hg, HEAD, HEAD), F32),
                        pltpu.VMEM((SUBLANES, W), F32),
                        pltpu.VMEM((SUBLANES, D_LORA), F32)],
        compiler_params=pltpu.CompilerParams(
            dimension_semantics=("arbitrary", "arbitrary", "arbitrary"),
            vmem_limit_bytes=_vmem_limit(nbytes)),
        name="rwkv7",
    )(*args)
    if has_vres:
        return res, vfirst
    return res[0], res[1]


def _oproj_kernel(h_ref, a_ref, b_ref, wa_ref, wb_ref, o_ref):
    acc = jnp.dot(a_ref[...], wa_ref[...], preferred_element_type=F32)
    acc = acc + jnp.dot(b_ref[...], wb_ref[...], preferred_element_type=F32)
    o_ref[...] = h_ref[...] + acc


def _oproj(h, oa, ob, wo, layer, tm=512):
    T = h.shape[0]
    nbytes = 2 * (2 * tm * D_MODEL * 4 + 2 * tm * D_LRU * 2 + 2 * D_LRU * D_MODEL * 2)
    return pl.pallas_call(
        _oproj_kernel,
        grid=(T // tm,),
        in_specs=[
            pl.BlockSpec((tm, D_MODEL), lambda i: (i, 0)),
            pl.BlockSpec((tm, D_LRU), lambda i: (i, 0)),
            pl.BlockSpec((tm, D_RWKV), lambda i: (i, 0)),
            pl.BlockSpec((None, D_LRU, D_MODEL), lambda i: (layer, 0, 0)),
            pl.BlockSpec((None, D_RWKV, D_MODEL), lambda i: (layer, 1, 0)),
        ],
        out_specs=pl.BlockSpec((tm, D_MODEL), lambda i: (i, 0)),
        out_shape=jax.ShapeDtypeStruct((T, D_MODEL), F32),
        compiler_params=pltpu.CompilerParams(
            dimension_semantics=("arbitrary",),
            vmem_limit_bytes=_vmem_limit(nbytes)),
        name="oproj",
    )(h, oa, ob, wo, wo)


def _ffn_kernel(tiles_per_seq, h_ref, g_ref, wg_ref, wu_ref, cw_ref, cb_ref, wd_ref,
                o_ref, u_ref, acc_ref, tail_ref):
    i = pl.program_id(0)
    j = pl.program_id(1)

    @pl.when(j == 0)
    def _():
        u_ref[...] = _rms(h_ref[...], g_ref[...]).astype(BF16)
        acc_ref[...] = jnp.zeros_like(acc_ref)

    @pl.when(i % tiles_per_seq == 0)
    def _():
        tail_ref[j] = jnp.zeros(tail_ref.shape[1:], F32)

    u = u_ref[...]
    gate = jnp.dot(u, wg_ref[...], preferred_element_type=F32)
    tm = gate.shape[0]
    tail = tail_ref[j]
    cw = cw_ref[...]
    conv = gate * cw[FFN_CONV - 1:FFN_CONV] + cb_ref[...]
    for d in range(1, FFN_CONV):
        conv = conv + _shift_rows(gate, d, tail) * cw[FFN_CONV - 1 - d:FFN_CONV - d]
    tail_ref[j] = gate[tm - SUBLANES:]
    up = jnp.dot(u, wu_ref[...], preferred_element_type=F32)
    act = (_gelu(conv) * up).astype(BF16)
    acc_ref[...] += jnp.dot(act, wd_ref[...], preferred_element_type=F32)

    @pl.when(j == pl.num_programs(1) - 1)
    def _():
        o_ref[...] = h_ref[...] + acc_ref[...]


def _ffn(h, g, wg, wu, cw, cb, wd, layer, seq, tm=512, tf=1024):
    T = h.shape[0]
    nf = D_FF // tf
    nbytes = (2 * (2 * tm * D_MODEL * 4 + 3 * D_MODEL * tf * 2) + tm * D_MODEL * 6
              + nf * SUBLANES * tf * 4 + 6 * tm * tf * 4)
    return pl.pallas_call(
        functools.partial(_ffn_kernel, seq // tm),
        grid=(T // tm, nf),
        in_specs=[
            pl.BlockSpec((tm, D_MODEL), lambda i, j: (i, 0)),
            pl.BlockSpec((1, D_MODEL), lambda i, j: (0, 0)),
            pl.BlockSpec((None, D_MODEL, tf), lambda i, j: (layer, 0, j)),
            pl.BlockSpec((None, D_MODEL, tf), lambda i, j: (layer, 0, j)),
            pl.BlockSpec((FFN_CONV, tf), lambda i, j: (0, j)),
            pl.BlockSpec((1, tf), lambda i, j: (0, j)),
            pl.BlockSpec((None, tf, D_MODEL), lambda i, j: (layer, j, 0)),
        ],
        out_specs=pl.BlockSpec((tm, D_MODEL), lambda i, j: (i, 0)),
        out_shape=jax.ShapeDtypeStruct((T, D_MODEL), F32),
        scratch_shapes=[pltpu.VMEM((tm, D_MODEL), BF16),
                        pltpu.VMEM((tm, D_MODEL), F32),
                        pltpu.VMEM((nf, SUBLANES, tf), F32)],
        compiler_params=pltpu.CompilerParams(
            dimension_semantics=("arbitrary", "arbitrary"),
            vmem_limit_bytes=_vmem_limit(nbytes)),
        name="ffn",
    )(h, g, wg, wu, cw, cb, wd)


def _ple_kernel(final, h_ref, p_ref, g_ref, wg_ref, wp_ref, gp_ref, gf_ref, o_ref):
    h = h_ref[...]
    u = _rms(h, g_ref[...]).astype(BF16)
    gate = _sigmoid(jnp.dot(u, wg_ref[...], preferred_element_type=F32))
    proj = jnp.dot(p_ref[...].astype(BF16), wp_ref[...], preferred_element_type=F32)
    out = h + _rms(gate * proj, gp_ref[...])
    if final:
        out = _rms(out, gf_ref[...])
    o_ref[...] = out


def _ple(h, p, g, wg, layer, wp, gp, gf, final, tm=512):
    T = h.shape[0]
    vec = pl.BlockSpec((1, D_MODEL), lambda i: (0, 0))
    nbytes = 2 * (2 * tm * D_MODEL * 4 + tm * D_PLE * 4 + D_MODEL * D_MODEL * 2
                  + D_PLE * D_MODEL * 2) + 4 * tm * D_MODEL * 4
    return pl.pallas_call(
        functools.partial(_ple_kernel, final),
        grid=(T // tm,),
        in_specs=[
            pl.BlockSpec((tm, D_MODEL), lambda i: (i, 0)),
            pl.BlockSpec((tm, D_PLE), lambda i: (i, 0)),
            vec,
            pl.BlockSpec((None, D_MODEL, D_MODEL), lambda i: (layer, 0, 0)),
            pl.BlockSpec((D_PLE, D_MODEL), lambda i: (0, 0)),
            vec, vec,
        ],
        out_specs=pl.BlockSpec((tm, D_MODEL), lambda i: (i, 0)),
        out_shape=jax.ShapeDtypeStruct((T, D_MODEL), F32),
        compiler_params=pltpu.CompilerParams(
            dimension_semantics=("arbitrary",),
            vmem_limit_bytes=_vmem_limit(nbytes)),
        name="ple",
    )(h, p, g, wg, wp, gp, gf)


def _row(v):
    return v.reshape(1, -1).astype(F32)


def _pad_rows(w, top, total):
    return jnp.pad(w, ((top, total - top - w.shape[0]), (0, 0)))


def kernel(x, p, ln_mix, w_in, w_in_vres, mu_shift, mu_shift_vres, conv_a_w, conv_a_b, lru_wx, lru_bx, lru_wa, lru_ba, lru_lambda, lru_norm, rwkv_w0, rwkv_w2, rwkv_a0, rwkv_a2, rwkv_v0, rwkv_v2, rwkv_g2, rwkv_kk, rwkv_ka, rwkv_rk, rwkv_lnx_w, rwkv_lnx_b, w_o, ln_ffn, w_gate, w_up, conv_f_w, conv_f_b, w_down, ln_ple, w_ple_gate, w_ple_proj, ln_ple_post, ln_final):
    batch, seq, _ = x.shape
    depth = w_in.shape[0]
    T = batch * seq
    h = x.reshape(T, D_MODEL)
    n_lora = LORA_W + LORA_A + LORA_G
    vfirst = None
    w_in_b, w_o_b, w_gate_b, w_up_b, w_down_b, w_ple_gate_b = (
        w.astype(BF16) for w in (w_in[:, :, :D_MAIN], w_o, w_gate, w_up, w_down, w_ple_gate))
    for i in range(depth):
        lora_cols = [w_in[i][:, D_MAIN:]]
        mu_l = [mu_shift[i][3 * D_RWKV:]]
        if i > 0:
            lora_cols.append(w_in_vres[i - 1])
            mu_l.append(mu_shift_vres[i - 1])
        w_lora = jnp.concatenate(lora_cols, axis=1)
        w_lora = jnp.pad(w_lora, ((0, 0), (0, D_LORA - w_lora.shape[1]))).astype(BF16)
        mu_lora = jnp.concatenate(mu_l, axis=0)
        mu_lora = jnp.pad(mu_lora, (0, D_LORA - mu_lora.shape[0]))

        zm, zl = _inproj(h, _row(ln_mix[i]), w_in_b, i, w_lora)

        out_a = _lru(zm, conv_a_w[i], _row(conv_a_b[i]), lru_wx[i].astype(BF16), _row(lru_bx[i]),
                     lru_wa[i].astype(BF16), _row(lru_ba[i]), _row(lru_lambda[i]),
                     _row(lru_norm[i]), batch, seq)

        prm = {
            "mu_rkv": _row(mu_shift[i][:3 * D_RWKV]),
            "mu_lora": _row(mu_lora),
            "w0": _row(rwkv_w0[i]),
            "w2": _pad_rows(rwkv_w2[i], 0, LANES),
            "a0": _row(rwkv_a0[i]),
            "a2": _pad_rows(rwkv_a2[i], LORA_W, LANES),
            "g2": _pad_rows(rwkv_g2[i], 0, 2 * LANES),
            "kk": _row(rwkv_kk[i]), "ka": _row(rwkv_ka[i]), "rk": _row(rwkv_rk[i]),
            "lnw": _row(rwkv_lnx_w[i]), "lnb": _row(rwkv_lnx_b[i]),
        }
        if i > 0:
            prm["v0"] = _row(rwkv_v0[i - 1])
            prm["v2"] = _pad_rows(rwkv_v2[i - 1], n_lora - 2 * LANES, LANES)
        out_b, vfirst = _rwkv(zm, zl, vfirst, prm, batch, seq)

        h = _oproj(h, out_a, out_b, w_o_b, i)
        h = _ffn(h, _row(ln_ffn[i]), w_gate_b, w_up_b, conv_f_w[i], _row(conv_f_b[i]),
                 w_down_b, i, seq)
        h = _ple(h, p[i].reshape(T, D_PLE), _row(ln_ple[i]), w_ple_gate_b, i,
                 w_ple_proj[i].astype(BF16), _row(ln_ple_post[i]), _row(ln_final),
                 final=(i == depth - 1))
    return h.reshape(batch, seq, D_MODEL)
```

```python
import functools
import math

import jax
import jax.numpy as jnp
from jax import lax
from jax.experimental import pallas as pl
from jax.experimental.pallas import tpu as pltpu

F32 = jnp.float32
BF16 = jnp.bfloat16

D_MODEL = 2048
D_LRU = 1024
D_RWKV = 1024
LRU_HEADS = 4
LRU_BLOCK = 256
LRU_CONV = 4
LRU_C = 8.0
HEAD = 64
N_HEADS = D_RWKV // HEAD
LORA_W = 64
LORA_A = 64
LORA_V = 32
LORA_G = 160
D_MAIN = 2 * D_LRU + 3 * D_RWKV
D_LORA = 384
D_FF = 3 * D_MODEL
FFN_CONV = 3
D_PLE = 256
RMS_EPS = 1e-6
LNX_EPS = 64e-5

V7X_VMEM_BYTES = 64 * 1024 * 1024
SUBLANES = 8
LANES = 128

CHUNK = 64
PAIR = 2 * HEAD
LRU_ROWS = 256
HEAD_GROUP = 16
RWKV_ROWS = 1024


def _vmem_limit(nbytes):
    return int(min(V7X_VMEM_BYTES - 4 * 1024 * 1024, nbytes + 16 * 1024 * 1024))


def _rms(x, g):
    return x * lax.rsqrt(jnp.mean(x * x, axis=-1, keepdims=True) + RMS_EPS) * g


def _gelu(x):
    c = math.sqrt(2.0 / math.pi)
    return 0.5 * x * (1.0 + jnp.tanh(c * (x + 0.044715 * (x * x * x))))


def _sigmoid(x):
    return 1.0 / (1.0 + jnp.exp(-x))


def _softplus(x):
    return jnp.maximum(x, 0.0) + jnp.log1p(jnp.exp(-jnp.abs(x)))


def _shift_rows(x, d, prev8):
    rolled = pltpu.roll(x, d, axis=0)
    prev = pltpu.roll(prev8, d, axis=0)
    row = lax.broadcasted_iota(jnp.int32, prev8.shape, 0)
    top = jnp.where(row < d, prev, rolled[:SUBLANES])
    return jnp.concatenate([top, rolled[SUBLANES:]], axis=0)


def _inproj_kernel(x_ref, g_ref, wm_ref, wl_ref, zm_ref, zl_ref, u_ref):
    @pl.when(pl.program_id(1) == 0)
    def _():
        u_ref[...] = _rms(x_ref[...], g_ref[...]).astype(BF16)
        zl_ref[...] = jnp.dot(u_ref[...], wl_ref[...], preferred_element_type=F32)

    zm_ref[...] = jnp.dot(u_ref[...], wm_ref[...], preferred_element_type=F32)


def _inproj(h, g, w_in, layer, w_lora, tm=1024, tn=1024):
    T = h.shape[0]
    nbytes = 2 * (tm * D_MODEL * 4 + D_MODEL * tn * 2 + D_MODEL * D_LORA * 2
                  + tm * tn * 4 + tm * D_LORA * 4) + tm * D_MODEL * 2
    return pl.pallas_call(
        _inproj_kernel,
        grid=(T // tm, D_MAIN // tn),
        in_specs=[
            pl.BlockSpec((tm, D_MODEL), lambda i, j: (i, 0)),
            pl.BlockSpec((1, D_MODEL), lambda i, j: (0, 0)),
            pl.BlockSpec((None, D_MODEL, tn), lambda i, j: (layer, 0, j)),
            pl.BlockSpec((D_MODEL, D_LORA), lambda i, j: (0, 0)),
        ],
        out_specs=[
            pl.BlockSpec((tm, tn), lambda i, j: (i, j)),
            pl.BlockSpec((tm, D_LORA), lambda i, j: (i, 0)),
        ],
        out_shape=[
            jax.ShapeDtypeStruct((T, D_MAIN), F32),
            jax.ShapeDtypeStruct((T, D_LORA), F32),
        ],
        scratch_shapes=[pltpu.VMEM((tm, D_MODEL), BF16)],
        compiler_params=pltpu.CompilerParams(
            dimension_semantics=("arbitrary", "arbitrary"),
            vmem_limit_bytes=_vmem_limit(nbytes)),
        name="inproj",
    )(h, g, w_in, w_lora)


def _lru_rows(xb_ref, yb_ref, rows, tail, carry, seq_start, cw, cb, wx_ref, bx, wa_ref, ba,
              sp_lam, nrm, between):
    nrows = rows.stop - rows.start
    ngroup = nrows // SUBLANES
    sub = lax.broadcasted_iota(jnp.int32, (1, SUBLANES, 1), 1)
    ys, tails, carries, ss = [], [], [], 0.0
    for hd in range(LRU_HEADS):
        cols = slice(hd * LRU_BLOCK, (hd + 1) * LRU_BLOCK)
        x = xb_ref[rows, cols]
        xc = x * cw[LRU_CONV - 1:LRU_CONV, cols] + cb[:, cols]
        for d in range(1, LRU_CONV):
            xc = xc + _shift_rows(x, d, tail[:, cols]) * cw[LRU_CONV - 1 - d:LRU_CONV - d, cols]
        tails.append(x[nrows - SUBLANES:])

        xcb = xc.astype(BF16)
        gate_x = _sigmoid(jnp.dot(xcb, wx_ref[hd], preferred_element_type=F32) + bx[:, cols])
        gate_a = _sigmoid(jnp.dot(xcb, wa_ref[hd], preferred_element_type=F32) + ba[:, cols])
        log_a = (-LRU_C) * gate_a * sp_lam[:, cols]
        a = jnp.exp(log_a)
        mult = jnp.sqrt(1.0 - a * a)
        if seq_start is not None:
            row = lax.broadcasted_iota(jnp.int32, (nrows, 1), 0)
            mult = jnp.where(jnp.logical_and(row == 0, seq_start), 1.0, mult)
        b = xc * gate_x * mult

        a = a.reshape(ngroup, SUBLANES, LRU_BLOCK)
        b = b.reshape(ngroup, SUBLANES, LRU_BLOCK)
        d = 1
        while d < SUBLANES:
            keep = sub >= d
            a_sh = jnp.where(keep, pltpu.roll(a, d, axis=1), 1.0)
            b_sh = jnp.where(keep, pltpu.roll(b, d, axis=1), 0.0)
            b = a * b_sh + b
            a = a * a_sh
            d *= 2
        hcar = carry[:, cols]
        hs = []
        for grp in range(ngroup):
            hg = a[grp] * hcar + b[grp]
            hs.append(hg)
            hcar = hg[SUBLANES - 1:SUBLANES]
        carries.append(hcar)

        y = jnp.concatenate(hs, axis=0) * _gelu(yb_ref[rows, cols])
        ss = ss + jnp.sum(y * y, axis=-1, keepdims=True)
        ys.append(y)
        between(hd)
    scale = lax.rsqrt(ss * (1.0 / D_LRU) + RMS_EPS)
    out = jnp.concatenate(ys, axis=-1) * scale * nrm
    return out.astype(BF16), jnp.concatenate(tails, axis=-1), jnp.concatenate(carries, axis=-1)


def _lru_oproj_kernel(xb_ref, yb_ref, ob_ref, h_ref, cw_ref, cb_ref, wx_ref, bx_ref, wa_ref,
                      ba_ref, lam_ref, nrm_ref, woa_ref, wob_ref, o_ref, tail_ref, carry_ref):
    t = pl.program_id(1)

    @pl.when(t == 0)
    def _():
        tail_ref[...] = jnp.zeros_like(tail_ref)
        carry_ref[...] = jnp.zeros_like(carry_ref)

    sp_lam = _softplus(-lam_ref[...])
    tail, carry = tail_ref[...], carry_ref[0:1, :]
    ts = xb_ref.shape[0]
    ncol = D_MODEL // LRU_HEADS
    pieces = {}

    def project(name, lhs, w_ref):
        def step(hd):
            cols = slice(hd * ncol, (hd + 1) * ncol)
            pieces.setdefault(name, []).append(
                jnp.dot(lhs, w_ref[:, cols], preferred_element_type=F32))
        return step

    between = project("b", ob_ref[...], wob_ref)
    for r0 in range(0, ts, LRU_ROWS):
        out_a, tail, carry = _lru_rows(
            xb_ref, yb_ref, slice(r0, r0 + LRU_ROWS), tail, carry, (t == 0) if r0 == 0 else None,
            cw_ref[...], cb_ref[...], wx_ref, bx_ref[...], wa_ref, ba_ref[...], sp_lam,
            nrm_ref[...], between)
        between = project(("a", r0), out_a, woa_ref)
    for hd in range(LRU_HEADS):
        between(hd)
    tail_ref[...] = tail
    carry_ref[0:1, :] = carry
    acc_a = jnp.concatenate(
        [jnp.concatenate(pieces["a", r0], axis=-1) for r0 in range(0, ts, LRU_ROWS)], axis=0)
    o_ref[...] = h_ref[...] + jnp.concatenate(pieces["b"], axis=-1) + acc_a


def _lru_oproj(zm, out_b, h, cw, cb, wx, bx, wa, ba, lam, nrm, wo, layer, batch, seq, ts=512):
    T = zm.shape[0]
    nt = seq // ts
    vec = pl.BlockSpec((1, D_LRU), lambda b, t: (0, 0))
    mat = pl.BlockSpec((LRU_HEADS, LRU_BLOCK, LRU_BLOCK), lambda b, t: (0, 0, 0))

    def rows(width, col):
        return pl.BlockSpec((ts, width), lambda b, t: (b * nt + t, col))

    nbytes = (2 * (2 * ts * D_LRU * 4 + ts * D_RWKV * 2 + 2 * ts * D_MODEL * 4
                   + 2 * D_LRU * D_MODEL * 2) + 16 * LRU_ROWS * D_LRU * 4 + 2 * ts * D_MODEL * 4)
    return pl.pallas_call(
        _lru_oproj_kernel,
        grid=(batch, nt),
        in_specs=[
            rows(D_LRU, 0), rows(D_LRU, 1), rows(D_RWKV, 0), rows(D_MODEL, 0),
            pl.BlockSpec((LRU_CONV, D_LRU), lambda b, t: (0, 0)),
            vec, mat, vec, mat, vec, vec, vec,
            pl.BlockSpec((None, D_LRU, D_MODEL), lambda b, t: (layer, 0, 0)),
            pl.BlockSpec((None, D_RWKV, D_MODEL), lambda b, t: (layer, 1, 0)),
        ],
        out_specs=rows(D_MODEL, 0),
        out_shape=jax.ShapeDtypeStruct((T, D_MODEL), F32),
        scratch_shapes=[pltpu.VMEM((SUBLANES, D_LRU), F32),
                        pltpu.VMEM((SUBLANES, D_LRU), F32)],
        compiler_params=pltpu.CompilerParams(
            dimension_semantics=("arbitrary", "arbitrary"),
            vmem_limit_bytes=_vmem_limit(nbytes)),
        name="lru_oproj",
    )(zm, zm, out_b, h, cw, cb, wx, bx, wa, ba, lam, nrm, wo, wo)


def _mm(a, b):
    return jnp.dot(a.astype(BF16), b.astype(BF16), preferred_element_type=F32)


def _mm_nt(a, b):
    return lax.dot_general(a.astype(BF16), b.astype(BF16), (((1,), (1,)), ((), ())),
                           preferred_element_type=F32)


def _mm_tn(a, b):
    return lax.dot_general(a.astype(BF16), b.astype(BF16), (((0,), (0,)), ((), ())),
                           preferred_element_type=F32)


def _split3(x):
    hi = x.astype(BF16)
    r1 = x - hi.astype(F32)
    mid = r1.astype(BF16)
    lo = (r1 - mid.astype(F32)).astype(BF16)
    return hi, mid, lo


def _seg_sum(x):
    lane_lo = lax.broadcasted_iota(jnp.int32, (1, PAIR), 1) < HEAD
    out = []
    for p in range(x.shape[1] // PAIR):
        t = x[:, p * PAIR:(p + 1) * PAIR]
        s0 = jnp.sum(jnp.where(lane_lo, t, 0.0), axis=-1, keepdims=True)
        s1 = jnp.sum(jnp.where(lane_lo, 0.0, t), axis=-1, keepdims=True)
        out.append(jnp.where(lane_lo, s0, s1))
    return jnp.concatenate(out, axis=-1)


def _rwkv_kernel(has_vres, *refs):
    if has_vres:
        (r_ref, k_ref, v_ref, zl_ref, vf_ref, mur_ref, muk_ref, muv_ref, mul_ref,
         w0_ref, w2_ref, a0_ref, a2_ref, g2_ref, v0_ref, v2_ref,
         kkw_ref, ka_ref, rk_ref, lnw_ref, lnb_ref, o_ref, s_ref, prev_ref, prevz_ref) = refs
        vfo_ref = None
    else:
        (r_ref, k_ref, v_ref, zl_ref, mur_ref, muk_ref, muv_ref, mul_ref,
         w0_ref, w2_ref, a0_ref, a2_ref, g2_ref,
         kkw_ref, ka_ref, rk_ref, lnw_ref, lnb_ref, o_ref, vfo_ref,
         s_ref, prev_ref, prevz_ref) = refs
        vf_ref = v0_ref = v2_ref = None

    C = CHUNK
    nchunk = r_ref.shape[0] // C
    nh = r_ref.shape[1] // HEAD

    @pl.when(pl.program_id(2) == 0)
    def _():
        s_ref[...] = jnp.zeros_like(s_ref)
        prev_ref[...] = jnp.zeros_like(prev_ref)
        prevz_ref[...] = jnp.zeros_like(prevz_ref)

    ri3 = lax.broadcasted_iota(jnp.int32, (C, 3 * C), 0)
    ci3 = lax.broadcasted_iota(jnp.int32, (C, 3 * C), 1) % C
    tri3 = (ri3 >= ci3).astype(BF16)
    ri2 = lax.broadcasted_iota(jnp.int32, (2 * C, 2 * C), 0)
    ci2 = lax.broadcasted_iota(jnp.int32, (2 * C, 2 * C), 1) % C
    keep2 = jnp.where(ri2 < C, ri2, ri2 - C + 1) > ci2
    row0 = lax.broadcasted_iota(jnp.int32, (C, 1), 0) == 0
    zeros_h = jnp.zeros((C, HEAD), BF16)

    def shift_lerp(cur, prev_row, mu):
        sh = jnp.where(row0, prev_row, pltpu.roll(cur, 1, axis=0))
        return cur + (sh - cur) * mu

    def chunk_rows(c):
        return pl.ds(pl.multiple_of(c * C, C), C)

    def prep(c):
        rows = chunk_rows(c)
        first = c == 0
        before = pl.ds(jnp.maximum(c * C - 1, 0), 1)

        def lerp(ref, carried, mu):
            return shift_lerp(ref[rows, :], jnp.where(first, carried, ref[before, :]), mu)

        r = lerp(r_ref, prev_ref[0:1, :], mur_ref[...])
        k = lerp(k_ref, prev_ref[1:2, :], muk_ref[...])
        v = lerp(v_ref, prev_ref[2:3, :], muv_ref[...])
        zl = lerp(zl_ref, prevz_ref[0:1, :], mul_ref[...])

        z01 = zl[:, 0:LANES]
        wpre = w0_ref[...] + _mm(jnp.tanh(z01), w2_ref[...])
        w_log = -_softplus(-wpre) - 0.5
        logw = -jnp.exp(w_log)
        a = _sigmoid(a0_ref[...] + _mm(z01, a2_ref[...]))
        g = _mm(_sigmoid(zl[:, LANES:3 * LANES]), g2_ref[...])
        if has_vres:
            mix = _sigmoid(v0_ref[...] + _mm(zl[:, 2 * LANES:3 * LANES], v2_ref[...]))
            v = v + (vf_ref[rows, :] - v) * mix
        else:
            vfo_ref[rows, :] = v

        kk = k * kkw_ref[...]
        kk = kk / jnp.maximum(jnp.sqrt(_seg_sum(kk * kk)), 1e-12)
        k2 = k * (1.0 + (a - 1.0) * ka_ref[...])
        bb = kk * a

        cum = jnp.dot(tri3, jnp.concatenate(_split3(logw), axis=0), preferred_element_type=F32)
        p_in = jnp.exp(cum)
        p_ex = jnp.exp(cum - logw)
        p_inv = jnp.exp(-cum)
        p_end = p_in[C - 1:C, :]
        rt = r * p_in
        at = -kk * p_ex
        bt = bb * p_inv
        kt = k2 * p_inv
        rt_b, at_b, bt_b, kt_b, v_b = (t.astype(BF16) for t in (rt, at, bt, kt, v))
        bhat_b = (bt * p_end).astype(BF16)
        khat_b = (kt * p_end).astype(BF16)
        bonus = _seg_sum(r * k2 * rk_ref[...]) * v
        return (rt_b, at_b, bt_b, kt_b, v_b, bhat_b, khat_b, at, rt, p_end), (bonus, g)

    def heads_stage(local):
        rt_b, at_b, bt_b, kt_b, v_b, bhat_b, khat_b, at, rt, p_end = local
        heads = range(nh)
        sls = [slice(hh * HEAD, (hh + 1) * HEAD) for hh in heads]
        sc_b, m, vh_b, d = [], [], [], []
        for sl in sls:
            ar = jnp.concatenate([at_b[:, sl], rt_b[:, sl]], axis=0)
            bk = jnp.concatenate([bt_b[:, sl], kt_b[:, sl]], axis=0)
            sc = jnp.where(keep2, _mm_nt(ar, bk), 0.0)
            sc_b.append(sc.astype(BF16))
            m.append(sc[:C, :C])
            vh_b.append(v_b[:, sl])
        for hh in heads:
            zv = jnp.concatenate([zeros_h, vh_b[hh]], axis=0)
            x_loc = jnp.dot(sc_b[hh][:C], zv, preferred_element_type=F32)
            d.append(jnp.concatenate([at[:, sls[hh]], x_loc], axis=-1))
        nstep = int(math.log2(C))
        for i in range(nstep):
            for hh in heads:
                m_b = m[hh].astype(BF16)
                if i + 1 < nstep:
                    rhs = jnp.concatenate([d[hh].astype(BF16), m_b], axis=-1)
                    prod = jnp.dot(m_b, rhs, preferred_element_type=F32)
                    d[hh] = d[hh] + prod[:, :2 * HEAD]
                    m[hh] = prod[:, 2 * HEAD:]
                else:
                    d[hh] = d[hh] + jnp.dot(m_b, d[hh].astype(BF16), preferred_element_type=F32)
        o1, wz = [], []
        for hh in heads:
            gmat = jnp.concatenate(
                [d[hh].astype(BF16), jnp.concatenate([zeros_h, vh_b[hh]], axis=-1)], axis=0)
            o1.append(jnp.dot(sc_b[hh][C:], gmat, preferred_element_type=F32))
            bkh = jnp.concatenate([bhat_b[:, sls[hh]], khat_b[:, sls[hh]]], axis=0)
            wz.append(_mm_tn(gmat, bkh))
        ys = []
        for hh in heads:
            rbar = rt[:, sls[hh]] + o1[hh][:, :HEAD]
            st = s_ref[hh]
            ys.append(_mm_nt(rbar, st) + o1[hh][:, HEAD:])
            s_ref[hh] = st * p_end[:, sls[hh]] + _mm(st, wz[hh][:HEAD]) + wz[hh][HEAD:]
        return jnp.concatenate(ys, axis=-1)

    def tail(c, y, post):
        bonus, g = post
        mean = _seg_sum(y) * (1.0 / HEAD)
        yc = y - mean
        var = _seg_sum(yc * yc) * (1.0 / HEAD)
        yn = yc * lax.rsqrt(var + LNX_EPS) * lnw_ref[...] + lnb_ref[...]
        o_ref[chunk_rows(c), :] = ((yn + bonus) * g).astype(o_ref.dtype)

    def body(i, carry):
        local, y_prev, post_prev, post_cur = carry
        tail(jnp.maximum(i - 1, 0), y_prev, post_prev)
        local_next, post_next = prep(jnp.minimum(i + 1, nchunk - 1))
        y = heads_stage(local)
        return local_next, y, post_cur, post_next

    local0, post0 = prep(0)
    zeros_w = jnp.zeros((C, r_ref.shape[1]), F32)
    _, y_last, post_last, _ = lax.fori_loop(
        0, nchunk, body, (local0, zeros_w, (zeros_w, zeros_w), post0))
    tail(nchunk - 1, y_last, post_last)

    last = pl.ds(r_ref.shape[0] - 1, 1)
    prev_ref[0:1, :] = r_ref[last, :]
    prev_ref[1:2, :] = k_ref[last, :]
    prev_ref[2:3, :] = v_ref[last, :]
    prevz_ref[0:1, :] = zl_ref[last, :]


def _rwkv(zm, zl, vfirst, prm, batch, seq, hg=HEAD_GROUP, ts=RWKV_ROWS):
    T = zm.shape[0]
    W = hg * HEAD
    ng = D_RWKV // W
    nt = seq // ts
    col0 = 2 * D_LRU // W
    has_vres = vfirst is not None

    def col(off):
        return pl.BlockSpec((ts, W), lambda b, g, t: (b * nt + t, off + g))

    vecg = pl.BlockSpec((1, W), lambda b, g, t: (0, g))
    vec_k = pl.BlockSpec((1, W), lambda b, g, t: (0, ng + g))
    vec_v = pl.BlockSpec((1, W), lambda b, g, t: (0, 2 * ng + g))
    vec_l = pl.BlockSpec((1, D_LORA), lambda b, g, t: (0, 0))

    def lora(rows):
        return pl.BlockSpec((rows, W), lambda b, g, t: (0, g))

    in_specs = [col(col0), col(col0 + ng), col(col0 + 2 * ng),
                pl.BlockSpec((ts, D_LORA), lambda b, g, t: (b * nt + t, 0))]
    args = [zm, zm, zm, zl]
    if has_vres:
        in_specs.append(col(0))
        args.append(vfirst)
    in_specs += [vecg, vec_k, vec_v, vec_l, vecg, lora(LANES), vecg, lora(LANES), lora(2 * LANES)]
    args += [prm["mu_rkv"], prm["mu_rkv"], prm["mu_rkv"], prm["mu_lora"],
             prm["w0"], prm["w2"], prm["a0"], prm["a2"], prm["g2"]]
    if has_vres:
        in_specs += [vecg, lora(LANES)]
        args += [prm["v0"], prm["v2"]]
    in_specs += [vecg] * 5
    args += [prm["kk"], prm["ka"], prm["rk"], prm["lnw"], prm["lnb"]]

    if has_vres:
        out_specs = col(0)
        out_shape = jax.ShapeDtypeStruct((T, D_RWKV), BF16)
    else:
        out_specs = [col(0), col(0)]
        out_shape = [jax.ShapeDtypeStruct((T, D_RWKV), BF16),
                     jax.ShapeDtypeStruct((T, D_RWKV), F32)]
    nbytes = 2 * ts * (5 * W * 4 + D_LORA * 4 + W * 2) + hg * HEAD * HEAD * 4
    res = pl.pallas_call(
        functools.partial(_rwkv_kernel, has_vres),
        grid=(batch, ng, nt),
        in_specs=in_specs,
        out_specs=out_specs,
        out_shape=out_shape,
        scratch_shapes=[pltpu.VMEM((hg, HEAD, HEAD), F32),
                        pltpu.VMEM((SUBLANES, W), F32),
                        pltpu.VMEM((SUBLANES, D_LORA), F32)],
        compiler_params=pltpu.CompilerParams(
            dimension_semantics=("arbitrary", "arbitrary", "arbitrary"),
            vmem_limit_bytes=_vmem_limit(nbytes)),
        name="rwkv7",
    )(*args)
    if has_vres:
        return res, vfirst
    return res[0], res[1]


def _ffn_kernel(tiles_per_seq, h_ref, g_ref, wg_ref, wu_ref, cw_ref, cb_ref, wd_ref,
                o_ref, u_ref, acc_ref, tail_ref):
    i = pl.program_id(0)
    j = pl.program_id(1)

    @pl.when(j == 0)
    def _():
        u_ref[...] = _rms(h_ref[...], g_ref[...]).astype(BF16)
        acc_ref[...] = jnp.zeros_like(acc_ref)

    @pl.when(i % tiles_per_seq == 0)
    def _():
        tail_ref[j] = jnp.zeros(tail_ref.shape[1:], F32)

    u = u_ref[...]
    gate = jnp.dot(u, wg_ref[...], preferred_element_type=F32)
    tm = gate.shape[0]
    tail = tail_ref[j]
    cw = cw_ref[...]
    conv = gate * cw[FFN_CONV - 1:FFN_CONV] + cb_ref[...]
    for d in range(1, FFN_CONV):
        conv = conv + _shift_rows(gate, d, tail) * cw[FFN_CONV - 1 - d:FFN_CONV - d]
    tail_ref[j] = gate[tm - SUBLANES:]
    up = jnp.dot(u, wu_ref[...], preferred_element_type=F32)
    act = (_gelu(conv) * up).astype(BF16)
    acc_ref[...] += jnp.dot(act, wd_ref[...], preferred_element_type=F32)

    @pl.when(j == pl.num_programs(1) - 1)
    def _():
        o_ref[...] = h_ref[...] + acc_ref[...]


def _ffn(h, g, wg, wu, cw, cb, wd, layer, seq, tm=512, tf=1024):
    T = h.shape[0]
    nf = D_FF // tf
    nbytes = (2 * (2 * tm * D_MODEL * 4 + 3 * D_MODEL * tf * 2) + tm * D_MODEL * 6
              + nf * SUBLANES * tf * 4 + 6 * tm * tf * 4)
    return pl.pallas_call(
        functools.partial(_ffn_kernel, seq // tm),
        grid=(T // tm, nf),
        in_specs=[
            pl.BlockSpec((tm, D_MODEL), lambda i, j: (i, 0)),
            pl.BlockSpec((1, D_MODEL), lambda i, j: (0, 0)),
            pl.BlockSpec((None, D_MODEL, tf), lambda i, j: (layer, 0, j)),
            pl.BlockSpec((None, D_MODEL, tf), lambda i, j: (layer, 0, j)),
            pl.BlockSpec((FFN_CONV, tf), lambda i, j: (0, j)),
            pl.BlockSpec((1, tf), lambda i, j: (0, j)),
            pl.BlockSpec((None, tf, D_MODEL), lambda i, j: (layer, j, 0)),
        ],
        out_specs=pl.BlockSpec((tm, D_MODEL), lambda i, j: (i, 0)),
        out_shape=jax.ShapeDtypeStruct((T, D_MODEL), F32),
        scratch_shapes=[pltpu.VMEM((tm, D_MODEL), BF16),
                        pltpu.VMEM((tm, D_MODEL), F32),
                        pltpu.VMEM((nf, SUBLANES, tf), F32)],
        compiler_params=pltpu.CompilerParams(
            dimension_semantics=("arbitrary", "arbitrary"),
            vmem_limit_bytes=_vmem_limit(nbytes)),
        name="ffn",
    )(h, g, wg, wu, cw, cb, wd)


def _ple_kernel(final, h_ref, p_ref, g_ref, wg_ref, wp_ref, gp_ref, gf_ref, o_ref):
    h = h_ref[...]
    u = _rms(h, g_ref[...]).astype(BF16)
    gate = _sigmoid(jnp.dot(u, wg_ref[...], preferred_element_type=F32))
    proj = jnp.dot(p_ref[...].astype(BF16), wp_ref[...], preferred_element_type=F32)
    out = h + _rms(gate * proj, gp_ref[...])
    if final:
        out = _rms(out, gf_ref[...])
    o_ref[...] = out


def _ple(h, p, g, wg, layer, wp, gp, gf, final, tm=512):
    T = h.shape[0]
    vec = pl.BlockSpec((1, D_MODEL), lambda i: (0, 0))
    nbytes = 2 * (2 * tm * D_MODEL * 4 + tm * D_PLE * 4 + D_MODEL * D_MODEL * 2
                  + D_PLE * D_MODEL * 2) + 4 * tm * D_MODEL * 4
    return pl.pallas_call(
        functools.partial(_ple_kernel, final),
        grid=(T // tm,),
        in_specs=[
            pl.BlockSpec((tm, D_MODEL), lambda i: (i, 0)),
            pl.BlockSpec((tm, D_PLE), lambda i: (i, 0)),
            vec,
            pl.BlockSpec((None, D_MODEL, D_MODEL), lambda i: (layer, 0, 0)),
            pl.BlockSpec((D_PLE, D_MODEL), lambda i: (0, 0)),
            vec, vec,
        ],
        out_specs=pl.BlockSpec((tm, D_MODEL), lambda i: (i, 0)),
        out_shape=jax.ShapeDtypeStruct((T, D_MODEL), F32),
        compiler_params=pltpu.CompilerParams(
            dimension_semantics=("arbitrary",),
            vmem_limit_bytes=_vmem_limit(nbytes)),
        name="ple",
    )(h, p, g, wg, wp, gp, gf)


def _row(v):
    return v.reshape(1, -1).astype(F32)


def _pad_rows(w, top, total):
    return jnp.pad(w, ((top, total - top - w.shape[0]), (0, 0)))


def kernel(x, p, ln_mix, w_in, w_in_vres, mu_shift, mu_shift_vres, conv_a_w, conv_a_b, lru_wx, lru_bx, lru_wa, lru_ba, lru_lambda, lru_norm, rwkv_w0, rwkv_w2, rwkv_a0, rwkv_a2, rwkv_v0, rwkv_v2, rwkv_g2, rwkv_kk, rwkv_ka, rwkv_rk, rwkv_lnx_w, rwkv_lnx_b, w_o, ln_ffn, w_gate, w_up, conv_f_w, conv_f_b, w_down, ln_ple, w_ple_gate, w_ple_proj, ln_ple_post, ln_final):
    batch, seq, _ = x.shape
    depth = w_in.shape[0]
    T = batch * seq
    h = x.reshape(T, D_MODEL)
    n_lora = LORA_W + LORA_A + LORA_G
    vfirst = None
    w_in_b, w_o_b, w_gate_b, w_up_b, w_down_b, w_ple_gate_b = (
        w.astype(BF16) for w in (w_in[:, :, :D_MAIN], w_o, w_gate, w_up, w_down, w_ple_gate))
    for i in range(depth):
        lora_cols = [w_in[i][:, D_MAIN:]]
        mu_l = [mu_shift[i][3 * D_RWKV:]]
        if i > 0:
            lora_cols.append(w_in_vres[i - 1])
            mu_l.append(mu_shift_vres[i - 1])
        w_lora = jnp.concatenate(lora_cols, axis=1)
        w_lora = jnp.pad(w_lora, ((0, 0), (0, D_LORA - w_lora.shape[1]))).astype(BF16)
        mu_lora = jnp.concatenate(mu_l, axis=0)
        mu_lora = jnp.pad(mu_lora, (0, D_LORA - mu_lora.shape[0]))

        zm, zl = _inproj(h, _row(ln_mix[i]), w_in_b, i, w_lora)

        prm = {
            "mu_rkv": _row(mu_shift[i][:3 * D_RWKV]),
            "mu_lora": _row(mu_lora),
            "w0": _row(rwkv_w0[i]),
            "w2": _pad_rows(rwkv_w2[i], 0, LANES),
            "a0": _row(rwkv_a0[i]),
            "a2": _pad_rows(rwkv_a2[i], LORA_W, LANES),
            "g2": _pad_rows(rwkv_g2[i], 0, 2 * LANES),
            "kk": _row(rwkv_kk[i]), "ka": _row(rwkv_ka[i]), "rk": _row(rwkv_rk[i]),
            "lnw": _row(rwkv_lnx_w[i]), "lnb": _row(rwkv_lnx_b[i]),
        }
        if i > 0:
            prm["v0"] = _row(rwkv_v0[i - 1])
            prm["v2"] = _pad_rows(rwkv_v2[i - 1], n_lora - 2 * LANES, LANES)
        out_b, vfirst = _rwkv(zm, zl, vfirst, prm, batch, seq)

        h = _lru_oproj(zm, out_b, h, conv_a_w[i], _row(conv_a_b[i]), lru_wx[i].astype(BF16),
                       _row(lru_bx[i]), lru_wa[i].astype(BF16), _row(lru_ba[i]),
                       _row(lru_lambda[i]), _row(lru_norm[i]), w_o_b, i, batch, seq)
        h = _ffn(h, _row(ln_ffn[i]), w_gate_b, w_up_b, conv_f_w[i], _row(conv_f_b[i]),
                 w_down_b, i, seq)
        h = _ple(h, p[i].reshape(T, D_PLE), _row(ln_ple[i]), w_ple_gate_b, i,
                 w_ple_proj[i].astype(BF16), _row(ln_ple_post[i]), _row(ln_final),
                 final=(i == depth - 1))
    return h.reshape(batch, seq, D_MODEL)
```

```python
import functools
import math

import jax
import jax.numpy as jnp
from jax import lax
from jax.experimental import pallas as pl
from jax.experimental.pallas import tpu as pltpu

F32 = jnp.float32
BF16 = jnp.bfloat16

D_MODEL = 2048
D_LRU = 1024
D_RWKV = 1024
LRU_HEADS = 4
LRU_BLOCK = 256
LRU_CONV = 4
LRU_C = 8.0
HEAD = 64
N_HEADS = D_RWKV // HEAD
LORA_W = 64
LORA_A = 64
LORA_V = 32
LORA_G = 160
D_MAIN = 2 * D_LRU + 3 * D_RWKV
D_LORA = 384
D_FF = 3 * D_MODEL
FFN_CONV = 3
D_PLE = 256
RMS_EPS = 1e-6
LNX_EPS = 64e-5

V7X_VMEM_BYTES = 64 * 1024 * 1024
SUBLANES = 8
LANES = 128

CHUNK = 64
PAIR = 2 * HEAD
LRU_ROWS = 256
HEAD_GROUP = 16
RWKV_ROWS = 1024


def _vmem_limit(nbytes):
    return int(min(V7X_VMEM_BYTES - 4 * 1024 * 1024, nbytes + 16 * 1024 * 1024))


def _rms(x, g):
    return x * lax.rsqrt(jnp.mean(x * x, axis=-1, keepdims=True) + RMS_EPS) * g


def _gelu(x):
    c = math.sqrt(2.0 / math.pi)
    return 0.5 * x * (1.0 + jnp.tanh(c * (x + 0.044715 * (x * x * x))))


def _sigmoid(x):
    return 1.0 / (1.0 + jnp.exp(-x))


def _softplus(x):
    return jnp.maximum(x, 0.0) + jnp.log1p(jnp.exp(-jnp.abs(x)))


def _shift_rows(x, d, prev8):
    rolled = pltpu.roll(x, d, axis=0)
    prev = pltpu.roll(prev8, d, axis=0)
    row = lax.broadcasted_iota(jnp.int32, prev8.shape, 0)
    top = jnp.where(row < d, prev, rolled[:SUBLANES])
    return jnp.concatenate([top, rolled[SUBLANES:]], axis=0)


def _inproj_kernel(x_ref, g_ref, wm_ref, wl_ref, zm_ref, zl_ref, u_ref):
    @pl.when(pl.program_id(1) == 0)
    def _():
        u_ref[...] = _rms(x_ref[...], g_ref[...]).astype(BF16)
        zl_ref[...] = jnp.dot(u_ref[...], wl_ref[...], preferred_element_type=F32)

    zm_ref[...] = jnp.dot(u_ref[...], wm_ref[...], preferred_element_type=F32)


def _inproj(h, g, w_in, layer, w_lora, tm=1024, tn=1024):
    T = h.shape[0]
    nbytes = 2 * (tm * D_MODEL * 4 + D_MODEL * tn * 2 + D_MODEL * D_LORA * 2
                  + tm * tn * 4 + tm * D_LORA * 4) + tm * D_MODEL * 2
    return pl.pallas_call(
        _inproj_kernel,
        grid=(T // tm, D_MAIN // tn),
        in_specs=[
            pl.BlockSpec((tm, D_MODEL), lambda i, j: (i, 0)),
            pl.BlockSpec((1, D_MODEL), lambda i, j: (0, 0)),
            pl.BlockSpec((None, D_MODEL, tn), lambda i, j: (layer, 0, j)),
            pl.BlockSpec((D_MODEL, D_LORA), lambda i, j: (0, 0)),
        ],
        out_specs=[
            pl.BlockSpec((tm, tn), lambda i, j: (i, j)),
            pl.BlockSpec((tm, D_LORA), lambda i, j: (i, 0)),
        ],
        out_shape=[
            jax.ShapeDtypeStruct((T, D_MAIN), F32),
            jax.ShapeDtypeStruct((T, D_LORA), F32),
        ],
        scratch_shapes=[pltpu.VMEM((tm, D_MODEL), BF16)],
        compiler_params=pltpu.CompilerParams(
            dimension_semantics=("arbitrary", "arbitrary"),
            vmem_limit_bytes=_vmem_limit(nbytes)),
        name="inproj",
    )(h, g, w_in, w_lora)


def _lru_rows(xb_ref, yb_ref, rows, tail, carry, seq_start, cw, cb, wx_ref, bx, wa_ref, ba,
              sp_lam, nrm, between):
    nrows = rows.stop - rows.start
    ngroup = nrows // SUBLANES
    sub = lax.broadcasted_iota(jnp.int32, (1, SUBLANES, 1), 1)
    ys, tails, carries, ss = [], [], [], 0.0
    for hd in range(LRU_HEADS):
        cols = slice(hd * LRU_BLOCK, (hd + 1) * LRU_BLOCK)
        x = xb_ref[rows, cols]
        xc = x * cw[LRU_CONV - 1:LRU_CONV, cols] + cb[:, cols]
        for d in range(1, LRU_CONV):
            xc = xc + _shift_rows(x, d, tail[:, cols]) * cw[LRU_CONV - 1 - d:LRU_CONV - d, cols]
        tails.append(x[nrows - SUBLANES:])

        xcb = xc.astype(BF16)
        gate_x = _sigmoid(jnp.dot(xcb, wx_ref[hd], preferred_element_type=F32) + bx[:, cols])
        gate_a = _sigmoid(jnp.dot(xcb, wa_ref[hd], preferred_element_type=F32) + ba[:, cols])
        log_a = (-LRU_C) * gate_a * sp_lam[:, cols]
        a = jnp.exp(log_a)
        mult = jnp.sqrt(1.0 - a * a)
        if seq_start is not None:
            row = lax.broadcasted_iota(jnp.int32, (nrows, 1), 0)
            mult = jnp.where(jnp.logical_and(row == 0, seq_start), 1.0, mult)
        b = xc * gate_x * mult

        a = a.reshape(ngroup, SUBLANES, LRU_BLOCK)
        b = b.reshape(ngroup, SUBLANES, LRU_BLOCK)
        d = 1
        while d < SUBLANES:
            keep = sub >= d
            a_sh = jnp.where(keep, pltpu.roll(a, d, axis=1), 1.0)
            b_sh = jnp.where(keep, pltpu.roll(b, d, axis=1), 0.0)
            b = a * b_sh + b
            a = a * a_sh
            d *= 2
        hcar = carry[:, cols]
        hs = []
        for grp in range(ngroup):
            hg = a[grp] * hcar + b[grp]
            hs.append(hg)
            hcar = hg[SUBLANES - 1:SUBLANES]
        carries.append(hcar)

        y = jnp.concatenate(hs, axis=0) * _gelu(yb_ref[rows, cols])
        ss = ss + jnp.sum(y * y, axis=-1, keepdims=True)
        ys.append(y)
        between(hd)
    scale = lax.rsqrt(ss * (1.0 / D_LRU) + RMS_EPS)
    out = jnp.concatenate(ys, axis=-1) * scale * nrm
    return out.astype(BF16), jnp.concatenate(tails, axis=-1), jnp.concatenate(carries, axis=-1)


def _lru_oproj_kernel(xb_ref, yb_ref, ob_ref, h_ref, cw_ref, cb_ref, wx_ref, bx_ref, wa_ref,
                      ba_ref, lam_ref, nrm_ref, woa_ref, wob_ref, o_ref, tail_ref, carry_ref):
    t = pl.program_id(1)

    @pl.when(t == 0)
    def _():
        tail_ref[...] = jnp.zeros_like(tail_ref)
        carry_ref[...] = jnp.zeros_like(carry_ref)

    sp_lam = _softplus(-lam_ref[...])
    tail, carry = tail_ref[...], carry_ref[0:1, :]
    ts = xb_ref.shape[0]
    ncol = D_MODEL // LRU_HEADS
    pieces = {}

    def project(name, lhs, w_ref):
        def step(hd):
            cols = slice(hd * ncol, (hd + 1) * ncol)
            pieces.setdefault(name, []).append(
                jnp.dot(lhs, w_ref[:, cols], preferred_element_type=F32))
        return step

    between = project("b", ob_ref[...], wob_ref)
    for r0 in range(0, ts, LRU_ROWS):
        out_a, tail, carry = _lru_rows(
            xb_ref, yb_ref, slice(r0, r0 + LRU_ROWS), tail, carry, (t == 0) if r0 == 0 else None,
            cw_ref[...], cb_ref[...], wx_ref, bx_ref[...], wa_ref, ba_ref[...], sp_lam,
            nrm_ref[...], between)
        between = project(("a", r0), out_a, woa_ref)
    for hd in range(LRU_HEADS):
        between(hd)
    tail_ref[...] = tail
    carry_ref[0:1, :] = carry
    acc_a = jnp.concatenate(
        [jnp.concatenate(pieces["a", r0], axis=-1) for r0 in range(0, ts, LRU_ROWS)], axis=0)
    o_ref[...] = h_ref[...] + jnp.concatenate(pieces["b"], axis=-1) + acc_a


def _lru_oproj(zm, out_b, h, cw, cb, wx, bx, wa, ba, lam, nrm, wo, layer, batch, seq, ts=512):
    T = zm.shape[0]
    nt = seq // ts
    vec = pl.BlockSpec((1, D_LRU), lambda b, t: (0, 0))
    mat = pl.BlockSpec((LRU_HEADS, LRU_BLOCK, LRU_BLOCK), lambda b, t: (0, 0, 0))

    def rows(width, col):
        return pl.BlockSpec((ts, width), lambda b, t: (b * nt + t, col))

    nbytes = (2 * (2 * ts * D_LRU * 4 + ts * D_RWKV * 2 + 2 * ts * D_MODEL * 4
                   + 2 * D_LRU * D_MODEL * 2) + 16 * LRU_ROWS * D_LRU * 4 + 2 * ts * D_MODEL * 4)
    return pl.pallas_call(
        _lru_oproj_kernel,
        grid=(batch, nt),
        in_specs=[
            rows(D_LRU, 0), rows(D_LRU, 1), rows(D_RWKV, 0), rows(D_MODEL, 0),
            pl.BlockSpec((LRU_CONV, D_LRU), lambda b, t: (0, 0)),
            vec, mat, vec, mat, vec, vec, vec,
            pl.BlockSpec((None, D_LRU, D_MODEL), lambda b, t: (layer, 0, 0)),
            pl.BlockSpec((None, D_RWKV, D_MODEL), lambda b, t: (layer, 1, 0)),
        ],
        out_specs=rows(D_MODEL, 0),
        out_shape=jax.ShapeDtypeStruct((T, D_MODEL), F32),
        scratch_shapes=[pltpu.VMEM((SUBLANES, D_LRU), F32),
                        pltpu.VMEM((SUBLANES, D_LRU), F32)],
        compiler_params=pltpu.CompilerParams(
            dimension_semantics=("arbitrary", "arbitrary"),
            vmem_limit_bytes=_vmem_limit(nbytes)),
        name="lru_oproj",
    )(zm, zm, out_b, h, cw, cb, wx, bx, wa, ba, lam, nrm, wo, wo)


def _mm(a, b):
    return jnp.dot(a.astype(BF16), b.astype(BF16), preferred_element_type=F32)


def _mm_nt(a, b):
    return lax.dot_general(a.astype(BF16), b.astype(BF16), (((1,), (1,)), ((), ())),
                           preferred_element_type=F32)


def _mm_tn(a, b):
    return lax.dot_general(a.astype(BF16), b.astype(BF16), (((0,), (0,)), ((), ())),
                           preferred_element_type=F32)


def _split3(x):
    hi = x.astype(BF16)
    r1 = x - hi.astype(F32)
    mid = r1.astype(BF16)
    lo = (r1 - mid.astype(F32)).astype(BF16)
    return hi, mid, lo


def _seg_sum(x):
    lane_lo = lax.broadcasted_iota(jnp.int32, (1, PAIR), 1) < HEAD
    out = []
    for p in range(x.shape[1] // PAIR):
        t = x[:, p * PAIR:(p + 1) * PAIR]
        s0 = jnp.sum(jnp.where(lane_lo, t, 0.0), axis=-1, keepdims=True)
        s1 = jnp.sum(jnp.where(lane_lo, 0.0, t), axis=-1, keepdims=True)
        out.append(jnp.where(lane_lo, s0, s1))
    return jnp.concatenate(out, axis=-1)


def _rwkv_kernel(has_vres, *refs):
    if has_vres:
        (r_ref, k_ref, v_ref, zl_ref, vf_ref, mur_ref, muk_ref, muv_ref, mul_ref,
         w0_ref, w2_ref, a0_ref, a2_ref, g2_ref, v0_ref, v2_ref,
         kkw_ref, ka_ref, rk_ref, lnw_ref, lnb_ref, o_ref, s_ref, prev_ref, prevz_ref) = refs
        vfo_ref = None
    else:
        (r_ref, k_ref, v_ref, zl_ref, mur_ref, muk_ref, muv_ref, mul_ref,
         w0_ref, w2_ref, a0_ref, a2_ref, g2_ref,
         kkw_ref, ka_ref, rk_ref, lnw_ref, lnb_ref, o_ref, vfo_ref,
         s_ref, prev_ref, prevz_ref) = refs
        vf_ref = v0_ref = v2_ref = None

    C = CHUNK
    nchunk = r_ref.shape[0] // C
    nh = r_ref.shape[1] // HEAD

    @pl.when(pl.program_id(2) == 0)
    def _():
        s_ref[...] = jnp.zeros_like(s_ref)
        prev_ref[...] = jnp.zeros_like(prev_ref)
        prevz_ref[...] = jnp.zeros_like(prevz_ref)

    ri3 = lax.broadcasted_iota(jnp.int32, (C, 3 * C), 0)
    ci3 = lax.broadcasted_iota(jnp.int32, (C, 3 * C), 1) % C
    tri3 = (ri3 >= ci3).astype(BF16)
    ri2 = lax.broadcasted_iota(jnp.int32, (2 * C, 2 * C), 0)
    ci2 = lax.broadcasted_iota(jnp.int32, (2 * C, 2 * C), 1) % C
    keep2 = jnp.where(ri2 < C, ri2, ri2 - C + 1) > ci2
    row0 = lax.broadcasted_iota(jnp.int32, (C, 1), 0) == 0
    zeros_h = jnp.zeros((C, HEAD), BF16)

    def shift_lerp(cur, prev_row, mu):
        sh = jnp.where(row0, prev_row, pltpu.roll(cur, 1, axis=0))
        return cur + (sh - cur) * mu

    def chunk_rows(c):
        return pl.ds(pl.multiple_of(c * C, C), C)

    def prep(c, out):
        rows = chunk_rows(c)
        first = c == 0
        before = pl.ds(jnp.maximum(c * C - 1, 0), 1)

        def lerp(ref, carried, mu):
            return shift_lerp(ref[rows, :], jnp.where(first, carried, ref[before, :]), mu)

        zl = lerp(zl_ref, prevz_ref[0:1, :], mul_ref[...])
        z01 = zl[:, 0:LANES]
        wpre = w0_ref[...] + _mm(jnp.tanh(z01), w2_ref[...])
        apre = a0_ref[...] + _mm(z01, a2_ref[...])
        g = _mm(_sigmoid(zl[:, LANES:3 * LANES]), g2_ref[...])
        if has_vres:
            mpre = v0_ref[...] + _mm(zl[:, 2 * LANES:3 * LANES], v2_ref[...])
        yield
        w_log = -_softplus(-wpre) - 0.5
        logw = -jnp.exp(w_log)
        cum = jnp.dot(tri3, jnp.concatenate(_split3(logw), axis=0), preferred_element_type=F32)
        yield
        r = lerp(r_ref, prev_ref[0:1, :], mur_ref[...])
        k = lerp(k_ref, prev_ref[1:2, :], muk_ref[...])
        a = _sigmoid(apre)
        yield
        v = lerp(v_ref, prev_ref[2:3, :], muv_ref[...])
        if has_vres:
            v = v + (vf_ref[rows, :] - v) * _sigmoid(mpre)
        else:
            vfo_ref[rows, :] = v
        yield
        kk = k * kkw_ref[...]
        kk = kk / jnp.maximum(jnp.sqrt(_seg_sum(kk * kk)), 1e-12)
        yield
        k2 = k * (1.0 + (a - 1.0) * ka_ref[...])
        bb = kk * a
        bonus = _seg_sum(r * k2 * rk_ref[...]) * v
        yield
        p_in = jnp.exp(cum)
        p_ex = jnp.exp(cum - logw)
        p_inv = jnp.exp(-cum)
        p_end = p_in[C - 1:C, :]
        yield
        rt = r * p_in
        at = -kk * p_ex
        rt_b, at_b, v_b = rt.astype(BF16), at.astype(BF16), v.astype(BF16)
        yield
        bt = bb * p_inv
        kt = k2 * p_inv
        bt_b, kt_b = bt.astype(BF16), kt.astype(BF16)
        yield
        bhat_b = (bt * p_end).astype(BF16)
        khat_b = (kt * p_end).astype(BF16)
        out["local"] = (rt_b, at_b, bt_b, kt_b, v_b, bhat_b, khat_b, at, rt, p_end)
        out["post"] = (bonus, g)

    def heads_stage(local, out):
        rt_b, at_b, bt_b, kt_b, v_b, bhat_b, khat_b, at, rt, p_end = local
        heads = range(nh)
        sls = [slice(hh * HEAD, (hh + 1) * HEAD) for hh in heads]
        sc_b, m, vh_b, d = [], [], [], []
        for sl in sls:
            ar = jnp.concatenate([at_b[:, sl], rt_b[:, sl]], axis=0)
            bk = jnp.concatenate([bt_b[:, sl], kt_b[:, sl]], axis=0)
            sc = jnp.where(keep2, _mm_nt(ar, bk), 0.0)
            sc_b.append(sc.astype(BF16))
            m.append(sc[:C, :C])
            vh_b.append(v_b[:, sl])
        yield
        for hh in heads:
            zv = jnp.concatenate([zeros_h, vh_b[hh]], axis=0)
            x_loc = jnp.dot(sc_b[hh][:C], zv, preferred_element_type=F32)
            d.append(jnp.concatenate([at[:, sls[hh]], x_loc], axis=-1))
        yield
        nstep = int(math.log2(C))
        for i in range(nstep):
            for hh in heads:
                m_b = m[hh].astype(BF16)
                if i + 1 < nstep:
                    rhs = jnp.concatenate([d[hh].astype(BF16), m_b], axis=-1)
                    prod = jnp.dot(m_b, rhs, preferred_element_type=F32)
                    d[hh] = d[hh] + prod[:, :2 * HEAD]
                    m[hh] = prod[:, 2 * HEAD:]
                else:
                    d[hh] = d[hh] + jnp.dot(m_b, d[hh].astype(BF16), preferred_element_type=F32)
            yield
        o1, wz = [], []
        for hh in heads:
            gmat = jnp.concatenate(
                [d[hh].astype(BF16), jnp.concatenate([zeros_h, vh_b[hh]], axis=-1)], axis=0)
            o1.append(jnp.dot(sc_b[hh][C:], gmat, preferred_element_type=F32))
            bkh = jnp.concatenate([bhat_b[:, sls[hh]], khat_b[:, sls[hh]]], axis=0)
            wz.append(_mm_tn(gmat, bkh))
        yield
        ys = []
        for hh in heads:
            rbar = rt[:, sls[hh]] + o1[hh][:, :HEAD]
            st = s_ref[hh]
            ys.append(_mm_nt(rbar, st) + o1[hh][:, HEAD:])
            s_ref[hh] = st * p_end[:, sls[hh]] + _mm(st, wz[hh][:HEAD]) + wz[hh][HEAD:]
        out["y"] = jnp.concatenate(ys, axis=-1)

    def tail(c, y, post):
        bonus, g = post
        mean = _seg_sum(y) * (1.0 / HEAD)
        yc = y - mean
        yield
        var = _seg_sum(yc * yc) * (1.0 / HEAD)
        yield
        yn = yc * lax.rsqrt(var + LNX_EPS) * lnw_ref[...] + lnb_ref[...]
        o_ref[chunk_rows(c), :] = ((yn + bonus) * g).astype(o_ref.dtype)

    def run_interleaved(*gens):
        alive = list(gens)
        while alive:
            for gen in list(alive):
                if next(gen, "done") == "done":
                    alive.remove(gen)

    def body(i, carry):
        local, y_prev, post_prev, post_cur = carry
        out = {}
        run_interleaved(heads_stage(local, out),
                        tail(jnp.maximum(i - 1, 0), y_prev, post_prev),
                        prep(jnp.minimum(i + 1, nchunk - 1), out))
        return out["local"], out["y"], post_cur, out["post"]

    first = {}
    run_interleaved(prep(0, first))
    zeros_w = jnp.zeros((C, r_ref.shape[1]), F32)
    _, y_last, post_last, _ = lax.fori_loop(
        0, nchunk, body, (first["local"], zeros_w, (zeros_w, zeros_w), first["post"]))
    run_interleaved(tail(nchunk - 1, y_last, post_last))

    last = pl.ds(r_ref.shape[0] - 1, 1)
    prev_ref[0:1, :] = r_ref[last, :]
    prev_ref[1:2, :] = k_ref[last, :]
    prev_ref[2:3, :] = v_ref[last, :]
    prevz_ref[0:1, :] = zl_ref[last, :]


def _rwkv(zm, zl, vfirst, prm, batch, seq, hg=HEAD_GROUP, ts=RWKV_ROWS):
    T = zm.shape[0]
    W = hg * HEAD
    ng = D_RWKV // W
    nt = seq // ts
    col0 = 2 * D_LRU // W
    has_vres = vfirst is not None

    def col(off):
        return pl.BlockSpec((ts, W), lambda b, g, t: (b * nt + t, off + g))

    vecg = pl.BlockSpec((1, W), lambda b, g, t: (0, g))
    vec_k = pl.BlockSpec((1, W), lambda b, g, t: (0, ng + g))
    vec_v = pl.BlockSpec((1, W), lambda b, g, t: (0, 2 * ng + g))
    vec_l = pl.BlockSpec((1, D_LORA), lambda b, g, t: (0, 0))

    def lora(rows):
        return pl.BlockSpec((rows, W), lambda b, g, t: (0, g))

    in_specs = [col(col0), col(col0 + ng), col(col0 + 2 * ng),
                pl.BlockSpec((ts, D_LORA), lambda b, g, t: (b * nt + t, 0))]
    args = [zm, zm, zm, zl]
    if has_vres:
        in_specs.append(col(0))
        args.append(vfirst)
    in_specs += [vecg, vec_k, vec_v, vec_l, vecg, lora(LANES), vecg, lora(LANES), lora(2 * LANES)]
    args += [prm["mu_rkv"], prm["mu_rkv"], prm["mu_rkv"], prm["mu_lora"],
             prm["w0"], prm["w2"], prm["a0"], prm["a2"], prm["g2"]]
    if has_vres:
        in_specs += [vecg, lora(LANES)]
        args += [prm["v0"], prm["v2"]]
    in_specs += [vecg] * 5
    args += [prm["kk"], prm["ka"], prm["rk"], prm["lnw"], prm["lnb"]]

    if has_vres:
        out_specs = col(0)
        out_shape = jax.ShapeDtypeStruct((T, D_RWKV), BF16)
    else:
        out_specs = [col(0), col(0)]
        out_shape = [jax.ShapeDtypeStruct((T, D_RWKV), BF16),
                     jax.ShapeDtypeStruct((T, D_RWKV), F32)]
    nbytes = 2 * ts * (5 * W * 4 + D_LORA * 4 + W * 2) + hg * HEAD * HEAD * 4
    res = pl.pallas_call(
        functools.partial(_rwkv_kernel, has_vres),
        grid=(batch, ng, nt),
        in_specs=in_specs,
        out_specs=out_specs,
        out_shape=out_shape,
        scratch_shapes=[pltpu.VMEM((hg, HEAD, HEAD), F32),
                        pltpu.VMEM((SUBLANES, W), F32),
                        pltpu.VMEM((SUBLANES, D_LORA), F32)],
        compiler_params=pltpu.CompilerParams(
            dimension_semantics=("arbitrary", "arbitrary", "arbitrary"),
            vmem_limit_bytes=_vmem_limit(nbytes)),
        name="rwkv7",
    )(*args)
    if has_vres:
        return res, vfirst
    return res[0], res[1]


def _ffn_kernel(tiles_per_seq, h_ref, g_ref, wg_ref, wu_ref, cw_ref, cb_ref, wd_ref,
                o_ref, u_ref, acc_ref, tail_ref):
    i = pl.program_id(0)
    j = pl.program_id(1)

    @pl.when(j == 0)
    def _():
        u_ref[...] = _rms(h_ref[...], g_ref[...]).astype(BF16)
        acc_ref[...] = jnp.zeros_like(acc_ref)

    @pl.when(i % tiles_per_seq == 0)
    def _():
        tail_ref[j] = jnp.zeros(tail_ref.shape[1:], F32)

    u = u_ref[...]
    gate = jnp.dot(u, wg_ref[...], preferred_element_type=F32)
    tm = gate.shape[0]
    tail = tail_ref[j]
    cw = cw_ref[...]
    conv = gate * cw[FFN_CONV - 1:FFN_CONV] + cb_ref[...]
    for d in range(1, FFN_CONV):
        conv = conv + _shift_rows(gate, d, tail) * cw[FFN_CONV - 1 - d:FFN_CONV - d]
    tail_ref[j] = gate[tm - SUBLANES:]
    up = jnp.dot(u, wu_ref[...], preferred_element_type=F32)
    act = (_gelu(conv) * up).astype(BF16)
    acc_ref[...] += jnp.dot(act, wd_ref[...], preferred_element_type=F32)

    @pl.when(j == pl.num_programs(1) - 1)
    def _():
        o_ref[...] = h_ref[...] + acc_ref[...]


def _ffn(h, g, wg, wu, cw, cb, wd, layer, seq, tm=512, tf=1024):
    T = h.shape[0]
    nf = D_FF // tf
    nbytes = (2 * (2 * tm * D_MODEL * 4 + 3 * D_MODEL * tf * 2) + tm * D_MODEL * 6
              + nf * SUBLANES * tf * 4 + 6 * tm * tf * 4)
    return pl.pallas_call(
        functools.partial(_ffn_kernel, seq // tm),
        grid=(T // tm, nf),
        in_specs=[
            pl.BlockSpec((tm, D_MODEL), lambda i, j: (i, 0)),
            pl.BlockSpec((1, D_MODEL), lambda i, j: (0, 0)),
            pl.BlockSpec((None, D_MODEL, tf), lambda i, j: (layer, 0, j)),
            pl.BlockSpec((None, D_MODEL, tf), lambda i, j: (layer, 0, j)),
            pl.BlockSpec((FFN_CONV, tf), lambda i, j: (0, j)),
            pl.BlockSpec((1, tf), lambda i, j: (0, j)),
            pl.BlockSpec((None, tf, D_MODEL), lambda i, j: (layer, j, 0)),
        ],
        out_specs=pl.BlockSpec((tm, D_MODEL), lambda i, j: (i, 0)),
        out_shape=jax.ShapeDtypeStruct((T, D_MODEL), F32),
        scratch_shapes=[pltpu.VMEM((tm, D_MODEL), BF16),
                        pltpu.VMEM((tm, D_MODEL), F32),
                        pltpu.VMEM((nf, SUBLANES, tf), F32)],
        compiler_params=pltpu.CompilerParams(
            dimension_semantics=("arbitrary", "arbitrary"),
            vmem_limit_bytes=_vmem_limit(nbytes)),
        name="ffn",
    )(h, g, wg, wu, cw, cb, wd)


def _ple_kernel(final, h_ref, p_ref, g_ref, wg_ref, wp_ref, gp_ref, gf_ref, o_ref):
    h = h_ref[...]
    u = _rms(h, g_ref[...]).astype(BF16)
    gate = _sigmoid(jnp.dot(u, wg_ref[...], preferred_element_type=F32))
    proj = jnp.dot(p_ref[...].astype(BF16), wp_ref[...], preferred_element_type=F32)
    out = h + _rms(gate * proj, gp_ref[...])
    if final:
        out = _rms(out, gf_ref[...])
    o_ref[...] = out


def _ple(h, p, g, wg, layer, wp, gp, gf, final, tm=512):
    T = h.shape[0]
    vec = pl.BlockSpec((1, D_MODEL), lambda i: (0, 0))
    nbytes = 2 * (2 * tm * D_MODEL * 4 + tm * D_PLE * 4 + D_MODEL * D_MODEL * 2
                  + D_PLE * D_MODEL * 2) + 4 * tm * D_MODEL * 4
    return pl.pallas_call(
        functools.partial(_ple_kernel, final),
        grid=(T // tm,),
        in_specs=[
            pl.BlockSpec((tm, D_MODEL), lambda i: (i, 0)),
            pl.BlockSpec((tm, D_PLE), lambda i: (i, 0)),
            vec,
            pl.BlockSpec((None, D_MODEL, D_MODEL), lambda i: (layer, 0, 0)),
            pl.BlockSpec((D_PLE, D_MODEL), lambda i: (0, 0)),
            vec, vec,
        ],
        out_specs=pl.BlockSpec((tm, D_MODEL), lambda i: (i, 0)),
        out_shape=jax.ShapeDtypeStruct((T, D_MODEL), F32),
        compiler_params=pltpu.CompilerParams(
            dimension_semantics=("arbitrary",),
            vmem_limit_bytes=_vmem_limit(nbytes)),
        name="ple",
    )(h, p, g, wg, wp, gp, gf)


def _row(v):
    return v.reshape(1, -1).astype(F32)


def _pad_rows(w, top, total):
    return jnp.pad(w, ((top, total - top - w.shape[0]), (0, 0)))


def kernel(x, p, ln_mix, w_in, w_in_vres, mu_shift, mu_shift_vres, conv_a_w, conv_a_b, lru_wx, lru_bx, lru_wa, lru_ba, lru_lambda, lru_norm, rwkv_w0, rwkv_w2, rwkv_a0, rwkv_a2, rwkv_v0, rwkv_v2, rwkv_g2, rwkv_kk, rwkv_ka, rwkv_rk, rwkv_lnx_w, rwkv_lnx_b, w_o, ln_ffn, w_gate, w_up, conv_f_w, conv_f_b, w_down, ln_ple, w_ple_gate, w_ple_proj, ln_ple_post, ln_final):
    batch, seq, _ = x.shape
    depth = w_in.shape[0]
    T = batch * seq
    h = x.reshape(T, D_MODEL)
    n_lora = LORA_W + LORA_A + LORA_G
    vfirst = None
    w_in_b, w_o_b, w_gate_b, w_up_b, w_down_b, w_ple_gate_b = (
        w.astype(BF16) for w in (w_in[:, :, :D_MAIN], w_o, w_gate, w_up, w_down, w_ple_gate))
    for i in range(depth):
        lora_cols = [w_in[i][:, D_MAIN:]]
        mu_l = [mu_shift[i][3 * D_RWKV:]]
        if i > 0:
            lora_cols.append(w_in_vres[i - 1])
            mu_l.append(mu_shift_vres[i - 1])
        w_lora = jnp.concatenate(lora_cols, axis=1)
        w_lora = jnp.pad(w_lora, ((0, 0), (0, D_LORA - w_lora.shape[1]))).astype(BF16)
        mu_lora = jnp.concatenate(mu_l, axis=0)
        mu_lora = jnp.pad(mu_lora, (0, D_LORA - mu_lora.shape[0]))

        zm, zl = _inproj(h, _row(ln_mix[i]), w_in_b, i, w_lora)

        prm = {
            "mu_rkv": _row(mu_shift[i][:3 * D_RWKV]),
            "mu_lora": _row(mu_lora),
            "w0": _row(rwkv_w0[i]),
            "w2": _pad_rows(rwkv_w2[i], 0, LANES),
            "a0": _row(rwkv_a0[i]),
            "a2": _pad_rows(rwkv_a2[i], LORA_W, LANES),
            "g2": _pad_rows(rwkv_g2[i], 0, 2 * LANES),
            "kk": _row(rwkv_kk[i]), "ka": _row(rwkv_ka[i]), "rk": _row(rwkv_rk[i]),
            "lnw": _row(rwkv_lnx_w[i]), "lnb": _row(rwkv_lnx_b[i]),
        }
        if i > 0:
            prm["v0"] = _row(rwkv_v0[i - 1])
            prm["v2"] = _pad_rows(rwkv_v2[i - 1], n_lora - 2 * LANES, LANES)
        out_b, vfirst = _rwkv(zm, zl, vfirst, prm, batch, seq)

        h = _lru_oproj(zm, out_b, h, conv_a_w[i], _row(conv_a_b[i]), lru_wx[i].astype(BF16),
                       _row(lru_bx[i]), lru_wa[i].astype(BF16), _row(lru_ba[i]),
                       _row(lru_lambda[i]), _row(lru_norm[i]), w_o_b, i, batch, seq)
        h = _ffn(h, _row(ln_ffn[i]), w_gate_b, w_up_b, conv_f_w[i], _row(conv_f_b[i]),
                 w_down_b, i, seq)
        h = _ple(h, p[i].reshape(T, D_PLE), _row(ln_ple[i]), w_ple_gate_b, i,
                 w_ple_proj[i].astype(BF16), _row(ln_ple_post[i]), _row(ln_final),
                 final=(i == depth - 1))
    return h.reshape(batch, seq, D_MODEL)
```

```python
import functools
import math

import jax
import jax.numpy as jnp
from jax import lax
from jax.experimental import pallas as pl
from jax.experimental.pallas import tpu as pltpu

F32 = jnp.float32
BF16 = jnp.bfloat16

D_MODEL = 2048
D_LRU = 1024
D_RWKV = 1024
LRU_HEADS = 4
LRU_BLOCK = 256
LRU_CONV = 4
LRU_C = 8.0
HEAD = 64
N_HEADS = D_RWKV // HEAD
LORA_W = 64
LORA_A = 64
LORA_V = 32
LORA_G = 160
D_MAIN = 2 * D_LRU + 3 * D_RWKV
D_LORA = 384
D_FF = 3 * D_MODEL
FFN_CONV = 3
D_PLE = 256
RMS_EPS = 1e-6
LNX_EPS = 64e-5

V7X_VMEM_BYTES = 64 * 1024 * 1024
SUBLANES = 8
LANES = 128
ROW_GROUP = 16

CHUNK = 64
PAIR = 2 * HEAD
LRU_ROWS = 256
HEAD_GROUP = 16
RWKV_ROWS = 1024


def _vmem_limit(nbytes):
    return int(min(V7X_VMEM_BYTES - 4 * 1024 * 1024, nbytes + 16 * 1024 * 1024))


def _rms(x, g):
    return x * lax.rsqrt(jnp.mean(x * x, axis=-1, keepdims=True) + RMS_EPS) * g


def _gelu(x):
    c = math.sqrt(2.0 / math.pi)
    return 0.5 * x * (1.0 + jnp.tanh(c * (x + 0.044715 * (x * x * x))))


def _sigmoid(x):
    return 1.0 / (1.0 + jnp.exp(-x))


def _softplus(x):
    return jnp.maximum(x, 0.0) + jnp.log1p(jnp.exp(-jnp.abs(x)))


def _shift_rows(x, d, prev8):
    rolled = pltpu.roll(x, d, axis=0)
    prev = pltpu.roll(prev8, d, axis=0)
    row = lax.broadcasted_iota(jnp.int32, prev8.shape, 0)
    top = jnp.where(row < d, prev, rolled[:SUBLANES])
    return jnp.concatenate([top, rolled[SUBLANES:]], axis=0)


def _inproj_kernel(x_ref, g_ref, wm_ref, wl_ref, zm_ref, zl_ref, u_ref):
    @pl.when(pl.program_id(1) == 0)
    def _():
        u_ref[...] = _rms(x_ref[...], g_ref[...]).astype(BF16)
        zl_ref[...] = jnp.dot(u_ref[...], wl_ref[...], preferred_element_type=F32)

    zm_ref[...] = jnp.dot(u_ref[...], wm_ref[...],
                          preferred_element_type=F32).astype(zm_ref.dtype)


def _inproj(h, g, w_in, layer, w_lora, tm=1024, tn=1024):
    T = h.shape[0]
    nbytes = 2 * (tm * D_MODEL * 4 + D_MODEL * tn * 2 + D_MODEL * D_LORA * 2
                  + tm * tn * 2 + tm * D_LORA * 4) + tm * D_MODEL * 2
    return pl.pallas_call(
        _inproj_kernel,
        grid=(T // tm, D_MAIN // tn),
        in_specs=[
            pl.BlockSpec((tm, D_MODEL), lambda i, j: (i, 0)),
            pl.BlockSpec((1, D_MODEL), lambda i, j: (0, 0)),
            pl.BlockSpec((None, D_MODEL, tn), lambda i, j: (layer, 0, j)),
            pl.BlockSpec((D_MODEL, D_LORA), lambda i, j: (0, 0)),
        ],
        out_specs=[
            pl.BlockSpec((tm, tn), lambda i, j: (i, j)),
            pl.BlockSpec((tm, D_LORA), lambda i, j: (i, 0)),
        ],
        out_shape=[
            jax.ShapeDtypeStruct((T, D_MAIN), BF16),
            jax.ShapeDtypeStruct((T, D_LORA), F32),
        ],
        scratch_shapes=[pltpu.VMEM((tm, D_MODEL), BF16)],
        compiler_params=pltpu.CompilerParams(
            dimension_semantics=("arbitrary", "arbitrary"),
            vmem_limit_bytes=_vmem_limit(nbytes)),
        name="inproj",
    )(h, g, w_in, w_lora)


def _lru_rows(xb_ref, yb_ref, rows, tail, carry, seq_start, cw, cb, wx_ref, bx, wa_ref, ba,
              sp_lam, nrm, between):
    nrows = rows.stop - rows.start
    ngroup = nrows // SUBLANES
    sub = lax.broadcasted_iota(jnp.int32, (1, SUBLANES, 1), 1)
    ys, tails, carries, ss = [], [], [], 0.0
    for hd in range(LRU_HEADS):
        cols = slice(hd * LRU_BLOCK, (hd + 1) * LRU_BLOCK)
        x = xb_ref[rows, cols].astype(F32)
        xc = x * cw[LRU_CONV - 1:LRU_CONV, cols] + cb[:, cols]
        for d in range(1, LRU_CONV):
            xc = xc + _shift_rows(x, d, tail[:, cols]) * cw[LRU_CONV - 1 - d:LRU_CONV - d, cols]
        tails.append(x[nrows - SUBLANES:])

        xcb = xc.astype(BF16)
        gate_x = _sigmoid(jnp.dot(xcb, wx_ref[hd], preferred_element_type=F32) + bx[:, cols])
        gate_a = _sigmoid(jnp.dot(xcb, wa_ref[hd], preferred_element_type=F32) + ba[:, cols])
        log_a = (-LRU_C) * gate_a * sp_lam[:, cols]
        a = jnp.exp(log_a)
        mult = jnp.sqrt(1.0 - a * a)
        if seq_start is not None:
            row = lax.broadcasted_iota(jnp.int32, (nrows, 1), 0)
            mult = jnp.where(jnp.logical_and(row == 0, seq_start), 1.0, mult)
        b = xc * gate_x * mult

        a = a.reshape(ngroup, SUBLANES, LRU_BLOCK)
        b = b.reshape(ngroup, SUBLANES, LRU_BLOCK)
        d = 1
        while d < SUBLANES:
            keep = sub >= d
            a_sh = jnp.where(keep, pltpu.roll(a, d, axis=1), 1.0)
            b_sh = jnp.where(keep, pltpu.roll(b, d, axis=1), 0.0)
            b = a * b_sh + b
            a = a * a_sh
            d *= 2
        hcar = carry[:, cols]
        hs = []
        for grp in range(ngroup):
            hg = a[grp] * hcar + b[grp]
            hs.append(hg)
            hcar = hg[SUBLANES - 1:SUBLANES]
        carries.append(hcar)

        y = jnp.concatenate(hs, axis=0) * _gelu(yb_ref[rows, cols].astype(F32))
        ss = ss + jnp.sum(y * y, axis=-1, keepdims=True)
        ys.append(y)
        between(hd)
    scale = lax.rsqrt(ss * (1.0 / D_LRU) + RMS_EPS)
    out = jnp.concatenate(ys, axis=-1) * scale * nrm
    return out.astype(BF16), jnp.concatenate(tails, axis=-1), jnp.concatenate(carries, axis=-1)


def _lru_oproj_kernel(xb_ref, yb_ref, ob_ref, h_ref, cw_ref, cb_ref, wx_ref, bx_ref, wa_ref,
                      ba_ref, lam_ref, nrm_ref, woa_ref, wob_ref, o_ref, tail_ref, carry_ref):
    t = pl.program_id(1)

    @pl.when(t == 0)
    def _():
        tail_ref[...] = jnp.zeros_like(tail_ref)
        carry_ref[...] = jnp.zeros_like(carry_ref)

    sp_lam = _softplus(-lam_ref[...])
    tail, carry = tail_ref[...], carry_ref[0:1, :]
    ts = xb_ref.shape[0]
    ncol = D_MODEL // LRU_HEADS
    pieces = {}

    def project(name, lhs, w_ref):
        def step(hd):
            cols = slice(hd * ncol, (hd + 1) * ncol)
            pieces.setdefault(name, []).append(
                jnp.dot(lhs, w_ref[:, cols], preferred_element_type=F32))
        return step

    between = project("b", ob_ref[...], wob_ref)
    for r0 in range(0, ts, LRU_ROWS):
        out_a, tail, carry = _lru_rows(
            xb_ref, yb_ref, slice(r0, r0 + LRU_ROWS), tail, carry, (t == 0) if r0 == 0 else None,
            cw_ref[...], cb_ref[...], wx_ref, bx_ref[...], wa_ref, ba_ref[...], sp_lam,
            nrm_ref[...], between)
        between = project(("a", r0), out_a, woa_ref)
    for hd in range(LRU_HEADS):
        between(hd)
    tail_ref[...] = tail
    carry_ref[0:1, :] = carry
    acc_a = jnp.concatenate(
        [jnp.concatenate(pieces["a", r0], axis=-1) for r0 in range(0, ts, LRU_ROWS)], axis=0)
    o_ref[...] = h_ref[...] + jnp.concatenate(pieces["b"], axis=-1) + acc_a


def _lru_oproj(zm, out_b, h, cw, cb, wx, bx, wa, ba, lam, nrm, wo, layer, batch, seq, ts=512):
    T = zm.shape[0]
    nt = seq // ts
    vec = pl.BlockSpec((1, D_LRU), lambda b, t: (0, 0))
    mat = pl.BlockSpec((LRU_HEADS, LRU_BLOCK, LRU_BLOCK), lambda b, t: (0, 0, 0))

    def rows(width, col):
        return pl.BlockSpec((ts, width), lambda b, t: (b * nt + t, col))

    nbytes = (2 * (2 * ts * D_LRU * 4 + ts * D_RWKV * 2 + 2 * ts * D_MODEL * 4
                   + 2 * D_LRU * D_MODEL * 2) + 16 * LRU_ROWS * D_LRU * 4 + 2 * ts * D_MODEL * 4)
    return pl.pallas_call(
        _lru_oproj_kernel,
        grid=(batch, nt),
        in_specs=[
            rows(D_LRU, 0), rows(D_LRU, 1), rows(D_RWKV, 0), rows(D_MODEL, 0),
            pl.BlockSpec((LRU_CONV, D_LRU), lambda b, t: (0, 0)),
            vec, mat, vec, mat, vec, vec, vec,
            pl.BlockSpec((None, D_LRU, D_MODEL), lambda b, t: (layer, 0, 0)),
            pl.BlockSpec((None, D_RWKV, D_MODEL), lambda b, t: (layer, 1, 0)),
        ],
        out_specs=rows(D_MODEL, 0),
        out_shape=jax.ShapeDtypeStruct((T, D_MODEL), F32),
        scratch_shapes=[pltpu.VMEM((SUBLANES, D_LRU), F32),
                        pltpu.VMEM((SUBLANES, D_LRU), F32)],
        compiler_params=pltpu.CompilerParams(
            dimension_semantics=("arbitrary", "arbitrary"),
            vmem_limit_bytes=_vmem_limit(nbytes)),
        name="lru_oproj",
    )(zm, zm, out_b, h, cw, cb, wx, bx, wa, ba, lam, nrm, wo, wo)


def _mm(a, b):
    return jnp.dot(a.astype(BF16), b.astype(BF16), preferred_element_type=F32)


def _mm_nt(a, b):
    return lax.dot_general(a.astype(BF16), b.astype(BF16), (((1,), (1,)), ((), ())),
                           preferred_element_type=F32)


def _mm_tn(a, b):
    return lax.dot_general(a.astype(BF16), b.astype(BF16), (((0,), (0,)), ((), ())),
                           preferred_element_type=F32)


def _split3(x):
    hi = x.astype(BF16)
    r1 = x - hi.astype(F32)
    mid = r1.astype(BF16)
    lo = (r1 - mid.astype(F32)).astype(BF16)
    return hi, mid, lo


def _seg_sum(x):
    lane_lo = lax.broadcasted_iota(jnp.int32, (1, PAIR), 1) < HEAD
    out = []
    for p in range(x.shape[1] // PAIR):
        t = x[:, p * PAIR:(p + 1) * PAIR]
        s0 = jnp.sum(jnp.where(lane_lo, t, 0.0), axis=-1, keepdims=True)
        s1 = jnp.sum(jnp.where(lane_lo, 0.0, t), axis=-1, keepdims=True)
        out.append(jnp.where(lane_lo, s0, s1))
    return jnp.concatenate(out, axis=-1)


def _rwkv_kernel(has_vres, *refs):
    if has_vres:
        (r_ref, k_ref, v_ref, zl_ref, vf_ref, mur_ref, muk_ref, muv_ref, mul_ref,
         w0_ref, w2_ref, a0_ref, a2_ref, g2_ref, v0_ref, v2_ref,
         kkw_ref, ka_ref, rk_ref, lnw_ref, lnb_ref, o_ref, s_ref, prev_ref, prevz_ref) = refs
        vfo_ref = None
    else:
        (r_ref, k_ref, v_ref, zl_ref, mur_ref, muk_ref, muv_ref, mul_ref,
         w0_ref, w2_ref, a0_ref, a2_ref, g2_ref,
         kkw_ref, ka_ref, rk_ref, lnw_ref, lnb_ref, o_ref, vfo_ref,
         s_ref, prev_ref, prevz_ref) = refs
        vf_ref = v0_ref = v2_ref = None

    C = CHUNK
    nchunk = r_ref.shape[0] // C
    nh = r_ref.shape[1] // HEAD

    @pl.when(pl.program_id(2) == 0)
    def _():
        s_ref[...] = jnp.zeros_like(s_ref)
        prev_ref[...] = jnp.zeros_like(prev_ref)
        prevz_ref[...] = jnp.zeros_like(prevz_ref)

    ri3 = lax.broadcasted_iota(jnp.int32, (C, 3 * C), 0)
    ci3 = lax.broadcasted_iota(jnp.int32, (C, 3 * C), 1) % C
    tri3 = (ri3 >= ci3).astype(BF16)
    ri2 = lax.broadcasted_iota(jnp.int32, (2 * C, 2 * C), 0)
    ci2 = lax.broadcasted_iota(jnp.int32, (2 * C, 2 * C), 1) % C
    keep2 = jnp.where(ri2 < C, ri2, ri2 - C + 1) > ci2
    row0 = lax.broadcasted_iota(jnp.int32, (C, 1), 0) == 0
    zeros_h = jnp.zeros((C, HEAD), BF16)

    def shift_lerp(cur, prev_row, mu):
        sh = jnp.where(row0, prev_row, pltpu.roll(cur, 1, axis=0))
        return cur + (sh - cur) * mu

    def chunk_rows(c):
        return pl.ds(pl.multiple_of(c * C, C), C)

    def prep(c, out):
        rows = chunk_rows(c)
        first = c == 0
        before = pl.ds(pl.multiple_of(jnp.maximum(c * C - ROW_GROUP, 0), ROW_GROUP), ROW_GROUP)

        def lerp(ref, carried, mu):
            prev_row = ref[before, :][ROW_GROUP - 1:ROW_GROUP].astype(F32)
            return shift_lerp(ref[rows, :].astype(F32), jnp.where(first, carried, prev_row), mu)

        zl = lerp(zl_ref, prevz_ref[0:1, :], mul_ref[...])
        z01 = zl[:, 0:LANES]
        wpre = w0_ref[...] + _mm(jnp.tanh(z01), w2_ref[...])
        apre = a0_ref[...] + _mm(z01, a2_ref[...])
        g = _mm(_sigmoid(zl[:, LANES:3 * LANES]), g2_ref[...])
        if has_vres:
            mpre = v0_ref[...] + _mm(zl[:, 2 * LANES:3 * LANES], v2_ref[...])
        yield
        w_log = -_softplus(-wpre) - 0.5
        logw = -jnp.exp(w_log)
        cum = jnp.dot(tri3, jnp.concatenate(_split3(logw), axis=0), preferred_element_type=F32)
        yield
        r = lerp(r_ref, prev_ref[0:1, :], mur_ref[...])
        k = lerp(k_ref, prev_ref[1:2, :], muk_ref[...])
        a = _sigmoid(apre)
        yield
        v = lerp(v_ref, prev_ref[2:3, :], muv_ref[...])
        if has_vres:
            v = v + (vf_ref[rows, :] - v) * _sigmoid(mpre)
        else:
            vfo_ref[rows, :] = v
        yield
        kk = k * kkw_ref[...]
        kk = kk / jnp.maximum(jnp.sqrt(_seg_sum(kk * kk)), 1e-12)
        yield
        k2 = k * (1.0 + (a - 1.0) * ka_ref[...])
        bb = kk * a
        bonus = _seg_sum(r * k2 * rk_ref[...]) * v
        yield
        p_in = jnp.exp(cum)
        p_ex = jnp.exp(cum - logw)
        p_inv = jnp.exp(-cum)
        p_end = p_in[C - 1:C, :]
        yield
        rt = r * p_in
        at = -kk * p_ex
        rt_b, at_b, v_b = rt.astype(BF16), at.astype(BF16), v.astype(BF16)
        yield
        bt = bb * p_inv
        kt = k2 * p_inv
        bt_b, kt_b = bt.astype(BF16), kt.astype(BF16)
        yield
        bhat_b = (bt * p_end).astype(BF16)
        khat_b = (kt * p_end).astype(BF16)
        out["local"] = (rt_b, at_b, bt_b, kt_b, v_b, bhat_b, khat_b, at, rt, p_end)
        out["post"] = (bonus, g)

    def heads_stage(local, out):
        rt_b, at_b, bt_b, kt_b, v_b, bhat_b, khat_b, at, rt, p_end = local
        heads = range(nh)
        sls = [slice(hh * HEAD, (hh + 1) * HEAD) for hh in heads]
        sc_b, m, vh_b, d = [], [], [], []
        for sl in sls:
            ar = jnp.concatenate([at_b[:, sl], rt_b[:, sl]], axis=0)
            bk = jnp.concatenate([bt_b[:, sl], kt_b[:, sl]], axis=0)
            sc = jnp.where(keep2, _mm_nt(ar, bk), 0.0)
            sc_b.append(sc.astype(BF16))
            m.append(sc[:C, :C])
            vh_b.append(v_b[:, sl])
        yield
        for hh in heads:
            zv = jnp.concatenate([zeros_h, vh_b[hh]], axis=0)
            x_loc = jnp.dot(sc_b[hh][:C], zv, preferred_element_type=F32)
            d.append(jnp.concatenate([at[:, sls[hh]], x_loc], axis=-1))
        yield
        nstep = int(math.log2(C))
        for i in range(nstep):
            for hh in heads:
                m_b = m[hh].astype(BF16)
                if i + 1 < nstep:
                    rhs = jnp.concatenate([d[hh].astype(BF16), m_b], axis=-1)
                    prod = jnp.dot(m_b, rhs, preferred_element_type=F32)
                    d[hh] = d[hh] + prod[:, :2 * HEAD]
                    m[hh] = prod[:, 2 * HEAD:]
                else:
                    d[hh] = d[hh] + jnp.dot(m_b, d[hh].astype(BF16), preferred_element_type=F32)
            yield
        o1, wz = [], []
        for hh in heads:
            gmat = jnp.concatenate(
                [d[hh].astype(BF16), jnp.concatenate([zeros_h, vh_b[hh]], axis=-1)], axis=0)
            o1.append(jnp.dot(sc_b[hh][C:], gmat, preferred_element_type=F32))
            bkh = jnp.concatenate([bhat_b[:, sls[hh]], khat_b[:, sls[hh]]], axis=0)
            wz.append(_mm_tn(gmat, bkh))
        yield
        ys = []
        for hh in heads:
            rbar = rt[:, sls[hh]] + o1[hh][:, :HEAD]
            st = s_ref[hh]
            ys.append(_mm_nt(rbar, st) + o1[hh][:, HEAD:])
            s_ref[hh] = st * p_end[:, sls[hh]] + _mm(st, wz[hh][:HEAD]) + wz[hh][HEAD:]
        out["y"] = jnp.concatenate(ys, axis=-1)

    def tail(c, y, post):
        bonus, g = post
        mean = _seg_sum(y) * (1.0 / HEAD)
        yc = y - mean
        yield
        var = _seg_sum(yc * yc) * (1.0 / HEAD)
        yield
        yn = yc * lax.rsqrt(var + LNX_EPS) * lnw_ref[...] + lnb_ref[...]
        o_ref[chunk_rows(c), :] = ((yn + bonus) * g).astype(o_ref.dtype)

    def run_interleaved(*gens):
        alive = list(gens)
        while alive:
            for gen in list(alive):
                if next(gen, "done") == "done":
                    alive.remove(gen)

    def body(i, carry):
        local, y_prev, post_prev, post_cur = carry
        out = {}
        run_interleaved(heads_stage(local, out),
                        tail(jnp.maximum(i - 1, 0), y_prev, post_prev),
                        prep(jnp.minimum(i + 1, nchunk - 1), out))
        return out["local"], out["y"], post_cur, out["post"]

    first = {}
    run_interleaved(prep(0, first))
    zeros_w = jnp.zeros((C, r_ref.shape[1]), F32)
    _, y_last, post_last, _ = lax.fori_loop(
        0, nchunk, body, (first["local"], zeros_w, (zeros_w, zeros_w), first["post"]))
    run_interleaved(tail(nchunk - 1, y_last, post_last))

    last = pl.ds(r_ref.shape[0] - ROW_GROUP, ROW_GROUP)
    for slot, ref in enumerate((r_ref, k_ref, v_ref)):
        prev_ref[slot:slot + 1, :] = ref[last, :][ROW_GROUP - 1:ROW_GROUP].astype(F32)
    prevz_ref[0:1, :] = zl_ref[last, :][ROW_GROUP - 1:ROW_GROUP]


def _rwkv(zm, zl, vfirst, prm, batch, seq, hg=HEAD_GROUP, ts=RWKV_ROWS):
    T = zm.shape[0]
    W = hg * HEAD
    ng = D_RWKV // W
    nt = seq // ts
    col0 = 2 * D_LRU // W
    has_vres = vfirst is not None

    def col(off):
        return pl.BlockSpec((ts, W), lambda b, g, t: (b * nt + t, off + g))

    vecg = pl.BlockSpec((1, W), lambda b, g, t: (0, g))
    vec_k = pl.BlockSpec((1, W), lambda b, g, t: (0, ng + g))
    vec_v = pl.BlockSpec((1, W), lambda b, g, t: (0, 2 * ng + g))
    vec_l = pl.BlockSpec((1, D_LORA), lambda b, g, t: (0, 0))

    def lora(rows):
        return pl.BlockSpec((rows, W), lambda b, g, t: (0, g))

    in_specs = [col(col0), col(col0 + ng), col(col0 + 2 * ng),
                pl.BlockSpec((ts, D_LORA), lambda b, g, t: (b * nt + t, 0))]
    args = [zm, zm, zm, zl]
    if has_vres:
        in_specs.append(col(0))
        args.append(vfirst)
    in_specs += [vecg, vec_k, vec_v, vec_l, vecg, lora(LANES), vecg, lora(LANES), lora(2 * LANES)]
    args += [prm["mu_rkv"], prm["mu_rkv"], prm["mu_rkv"], prm["mu_lora"],
             prm["w0"], prm["w2"], prm["a0"], prm["a2"], prm["g2"]]
    if has_vres:
        in_specs += [vecg, lora(LANES)]
        args += [prm["v0"], prm["v2"]]
    in_specs += [vecg] * 5
    args += [prm["kk"], prm["ka"], prm["rk"], prm["lnw"], prm["lnb"]]

    if has_vres:
        out_specs = col(0)
        out_shape = jax.ShapeDtypeStruct((T, D_RWKV), BF16)
    else:
        out_specs = [col(0), col(0)]
        out_shape = [jax.ShapeDtypeStruct((T, D_RWKV), BF16),
                     jax.ShapeDtypeStruct((T, D_RWKV), F32)]
    nbytes = 2 * ts * (5 * W * 4 + D_LORA * 4 + W * 2) + hg * HEAD * HEAD * 4
    res = pl.pallas_call(
        functools.partial(_rwkv_kernel, has_vres),
        grid=(batch, ng, nt),
        in_specs=in_specs,
        out_specs=out_specs,
        out_shape=out_shape,
        scratch_shapes=[pltpu.VMEM((hg, HEAD, HEAD), F32),
                        pltpu.VMEM((SUBLANES, W), F32),
                        pltpu.VMEM((SUBLANES, D_LORA), F32)],
        compiler_params=pltpu.CompilerParams(
            dimension_semantics=("arbitrary", "arbitrary", "arbitrary"),
            vmem_limit_bytes=_vmem_limit(nbytes)),
        name="rwkv7",
    )(*args)
    if has_vres:
        return res, vfirst
    return res[0], res[1]


def _ffn_kernel(tiles_per_seq, h_ref, g_ref, wg_ref, wu_ref, cw_ref, cb_ref, wd_ref,
                o_ref, u_ref, acc_ref, tail_ref):
    i = pl.program_id(0)
    j = pl.program_id(1)

    @pl.when(j == 0)
    def _():
        u_ref[...] = _rms(h_ref[...], g_ref[...]).astype(BF16)
        acc_ref[...] = jnp.zeros_like(acc_ref)

    @pl.when(i % tiles_per_seq == 0)
    def _():
        tail_ref[j] = jnp.zeros(tail_ref.shape[1:], F32)

    u = u_ref[...]
    gate = jnp.dot(u, wg_ref[...], preferred_element_type=F32)
    tm = gate.shape[0]
    tail = tail_ref[j]
    cw = cw_ref[...]
    conv = gate * cw[FFN_CONV - 1:FFN_CONV] + cb_ref[...]
    for d in range(1, FFN_CONV):
        conv = conv + _shift_rows(gate, d, tail) * cw[FFN_CONV - 1 - d:FFN_CONV - d]
    tail_ref[j] = gate[tm - SUBLANES:]
    up = jnp.dot(u, wu_ref[...], preferred_element_type=F32)
    act = (_gelu(conv) * up).astype(BF16)
    acc_ref[...] += jnp.dot(act, wd_ref[...], preferred_element_type=F32)

    @pl.when(j == pl.num_programs(1) - 1)
    def _():
        o_ref[...] = h_ref[...] + acc_ref[...]


def _ffn(h, g, wg, wu, cw, cb, wd, layer, seq, tm=512, tf=1024):
    T = h.shape[0]
    nf = D_FF // tf
    nbytes = (2 * (2 * tm * D_MODEL * 4 + 3 * D_MODEL * tf * 2) + tm * D_MODEL * 6
              + nf * SUBLANES * tf * 4 + 6 * tm * tf * 4)
    return pl.pallas_call(
        functools.partial(_ffn_kernel, seq // tm),
        grid=(T // tm, nf),
        in_specs=[
            pl.BlockSpec((tm, D_MODEL), lambda i, j: (i, 0)),
            pl.BlockSpec((1, D_MODEL), lambda i, j: (0, 0)),
            pl.BlockSpec((None, D_MODEL, tf), lambda i, j: (layer, 0, j)),
            pl.BlockSpec((None, D_MODEL, tf), lambda i, j: (layer, 0, j)),
            pl.BlockSpec((FFN_CONV, tf), lambda i, j: (0, j)),
            pl.BlockSpec((1, tf), lambda i, j: (0, j)),
            pl.BlockSpec((None, tf, D_MODEL), lambda i, j: (layer, j, 0)),
        ],
        out_specs=pl.BlockSpec((tm, D_MODEL), lambda i, j: (i, 0)),
        out_shape=jax.ShapeDtypeStruct((T, D_MODEL), F32),
        scratch_shapes=[pltpu.VMEM((tm, D_MODEL), BF16),
                        pltpu.VMEM((tm, D_MODEL), F32),
                        pltpu.VMEM((nf, SUBLANES, tf), F32)],
        compiler_params=pltpu.CompilerParams(
            dimension_semantics=("arbitrary", "arbitrary"),
            vmem_limit_bytes=_vmem_limit(nbytes)),
        name="ffn",
    )(h, g, wg, wu, cw, cb, wd)


def _ple_kernel(final, h_ref, p_ref, g_ref, wg_ref, wp_ref, gp_ref, gf_ref, o_ref):
    h = h_ref[...]
    u = _rms(h, g_ref[...]).astype(BF16)
    gate = _sigmoid(jnp.dot(u, wg_ref[...], preferred_element_type=F32))
    proj = jnp.dot(p_ref[...].astype(BF16), wp_ref[...], preferred_element_type=F32)
    out = h + _rms(gate * proj, gp_ref[...])
    if final:
        out = _rms(out, gf_ref[...])
    o_ref[...] = out


def _ple(h, p, g, wg, layer, wp, gp, gf, final, tm=512):
    T = h.shape[0]
    vec = pl.BlockSpec((1, D_MODEL), lambda i: (0, 0))
    nbytes = 2 * (2 * tm * D_MODEL * 4 + tm * D_PLE * 4 + D_MODEL * D_MODEL * 2
                  + D_PLE * D_MODEL * 2) + 4 * tm * D_MODEL * 4
    return pl.pallas_call(
        functools.partial(_ple_kernel, final),
        grid=(T // tm,),
        in_specs=[
            pl.BlockSpec((tm, D_MODEL), lambda i: (i, 0)),
            pl.BlockSpec((tm, D_PLE), lambda i: (i, 0)),
            vec,
            pl.BlockSpec((None, D_MODEL, D_MODEL), lambda i: (layer, 0, 0)),
            pl.BlockSpec((D_PLE, D_MODEL), lambda i: (0, 0)),
            vec, vec,
        ],
        out_specs=pl.BlockSpec((tm, D_MODEL), lambda i: (i, 0)),
        out_shape=jax.ShapeDtypeStruct((T, D_MODEL), F32),
        compiler_params=pltpu.CompilerParams(
            dimension_semantics=("arbitrary",),
            vmem_limit_bytes=_vmem_limit(nbytes)),
        name="ple",
    )(h, p, g, wg, wp, gp, gf)


def _row(v):
    return v.reshape(1, -1).astype(F32)


def _pad_rows(w, top, total):
    return jnp.pad(w, ((top, total - top - w.shape[0]), (0, 0)))


def kernel(x, p, ln_mix, w_in, w_in_vres, mu_shift, mu_shift_vres, conv_a_w, conv_a_b, lru_wx, lru_bx, lru_wa, lru_ba, lru_lambda, lru_norm, rwkv_w0, rwkv_w2, rwkv_a0, rwkv_a2, rwkv_v0, rwkv_v2, rwkv_g2, rwkv_kk, rwkv_ka, rwkv_rk, rwkv_lnx_w, rwkv_lnx_b, w_o, ln_ffn, w_gate, w_up, conv_f_w, conv_f_b, w_down, ln_ple, w_ple_gate, w_ple_proj, ln_ple_post, ln_final):
    batch, seq, _ = x.shape
    depth = w_in.shape[0]
    T = batch * seq
    h = x.reshape(T, D_MODEL)
    n_lora = LORA_W + LORA_A + LORA_G
    vfirst = None
    w_in_b, w_o_b, w_gate_b, w_up_b, w_down_b, w_ple_gate_b = (
        w.astype(BF16) for w in (w_in[:, :, :D_MAIN], w_o, w_gate, w_up, w_down, w_ple_gate))
    for i in range(depth):
        lora_cols = [w_in[i][:, D_MAIN:]]
        mu_l = [mu_shift[i][3 * D_RWKV:]]
        if i > 0:
            lora_cols.append(w_in_vres[i - 1])
            mu_l.append(mu_shift_vres[i - 1])
        w_lora = jnp.concatenate(lora_cols, axis=1)
        w_lora = jnp.pad(w_lora, ((0, 0), (0, D_LORA - w_lora.shape[1]))).astype(BF16)
        mu_lora = jnp.concatenate(mu_l, axis=0)
        mu_lora = jnp.pad(mu_lora, (0, D_LORA - mu_lora.shape[0]))

        zm, zl = _inproj(h, _row(ln_mix[i]), w_in_b, i, w_lora)

        prm = {
            "mu_rkv": _row(mu_shift[i][:3 * D_RWKV]),
            "mu_lora": _row(mu_lora),
            "w0": _row(rwkv_w0[i]),
            "w2": _pad_rows(rwkv_w2[i], 0, LANES),
            "a0": _row(rwkv_a0[i]),
            "a2": _pad_rows(rwkv_a2[i], LORA_W, LANES),
            "g2": _pad_rows(rwkv_g2[i], 0, 2 * LANES),
            "kk": _row(rwkv_kk[i]), "ka": _row(rwkv_ka[i]), "rk": _row(rwkv_rk[i]),
            "lnw": _row(rwkv_lnx_w[i]), "lnb": _row(rwkv_lnx_b[i]),
        }
        if i > 0:
            prm["v0"] = _row(rwkv_v0[i - 1])
            prm["v2"] = _pad_rows(rwkv_v2[i - 1], n_lora - 2 * LANES, LANES)
        out_b, vfirst = _rwkv(zm, zl, vfirst, prm, batch, seq)

        h = _lru_oproj(zm, out_b, h, conv_a_w[i], _row(conv_a_b[i]), lru_wx[i].astype(BF16),
                       _row(lru_bx[i]), lru_wa[i].astype(BF16), _row(lru_ba[i]),
                       _row(lru_lambda[i]), _row(lru_norm[i]), w_o_b, i, batch, seq)
        h = _ffn(h, _row(ln_ffn[i]), w_gate_b, w_up_b, conv_f_w[i], _row(conv_f_b[i]),
                 w_down_b, i, seq)
        h = _ple(h, p[i].reshape(T, D_PLE), _row(ln_ple[i]), w_ple_gate_b, i,
                 w_ple_proj[i].astype(BF16), _row(ln_ple_post[i]), _row(ln_final),
                 final=(i == depth - 1))
    return h.reshape(batch, seq, D_MODEL)
```

```python
import functools
import math

import jax
import jax.numpy as jnp
from jax import lax
from jax.experimental import pallas as pl
from jax.experimental.pallas import tpu as pltpu

F32 = jnp.float32
BF16 = jnp.bfloat16

D_MODEL = 2048
D_LRU = 1024
D_RWKV = 1024
LRU_HEADS = 4
LRU_BLOCK = 256
LRU_CONV = 4
LRU_C = 8.0
HEAD = 64
N_HEADS = D_RWKV // HEAD
LORA_W = 64
LORA_A = 64
LORA_V = 32
LORA_G = 160
D_MAIN = 2 * D_LRU + 3 * D_RWKV
D_LORA = 384
D_FF = 3 * D_MODEL
FFN_CONV = 3
D_PLE = 256
RMS_EPS = 1e-6
LNX_EPS = 64e-5

V7X_VMEM_BYTES = 64 * 1024 * 1024
SUBLANES = 8
LANES = 128

CHUNK = 64
PAIR = 2 * HEAD
LRU_ROWS = 256
HEAD_GROUP = 16
RWKV_ROWS = 1024


def _vmem_limit(nbytes):
    return int(min(V7X_VMEM_BYTES - 4 * 1024 * 1024, nbytes + 16 * 1024 * 1024))


def _rms(x, g):
    return x * lax.rsqrt(jnp.mean(x * x, axis=-1, keepdims=True) + RMS_EPS) * g


def _gelu(x):
    c = math.sqrt(2.0 / math.pi)
    return 0.5 * x * (1.0 + jnp.tanh(c * (x + 0.044715 * (x * x * x))))


def _sigmoid(x):
    return 1.0 / (1.0 + jnp.exp(-x))


def _softplus(x):
    return jnp.maximum(x, 0.0) + jnp.log1p(jnp.exp(-jnp.abs(x)))


def _shift_rows(x, d, prev8):
    rolled = pltpu.roll(x, d, axis=0)
    prev = pltpu.roll(prev8, d, axis=0)
    row = lax.broadcasted_iota(jnp.int32, prev8.shape, 0)
    top = jnp.where(row < d, prev, rolled[:SUBLANES])
    return jnp.concatenate([top, rolled[SUBLANES:]], axis=0)


def _inproj_kernel(x_ref, g_ref, wm_ref, wl_ref, zm_ref, zl_ref, u_ref):
    @pl.when(pl.program_id(1) == 0)
    def _():
        u_ref[...] = _rms(x_ref[...], g_ref[...]).astype(BF16)
        zl_ref[...] = jnp.dot(u_ref[...], wl_ref[...], preferred_element_type=F32)

    zm_ref[...] = jnp.dot(u_ref[...], wm_ref[...], preferred_element_type=F32)


def _inproj(h, g, w_in, layer, w_lora, tm=1024, tn=1024):
    T = h.shape[0]
    nbytes = 2 * (tm * D_MODEL * 4 + D_MODEL * tn * 2 + D_MODEL * D_LORA * 2
                  + tm * tn * 4 + tm * D_LORA * 4) + tm * D_MODEL * 2
    return pl.pallas_call(
        _inproj_kernel,
        grid=(T // tm, D_MAIN // tn),
        in_specs=[
            pl.BlockSpec((tm, D_MODEL), lambda i, j: (i, 0)),
            pl.BlockSpec((1, D_MODEL), lambda i, j: (0, 0)),
            pl.BlockSpec((None, D_MODEL, tn), lambda i, j: (layer, 0, j)),
            pl.BlockSpec((D_MODEL, D_LORA), lambda i, j: (0, 0)),
        ],
        out_specs=[
            pl.BlockSpec((tm, tn), lambda i, j: (i, j)),
            pl.BlockSpec((tm, D_LORA), lambda i, j: (i, 0)),
        ],
        out_shape=[
            jax.ShapeDtypeStruct((T, D_MAIN), F32),
            jax.ShapeDtypeStruct((T, D_LORA), F32),
        ],
        scratch_shapes=[pltpu.VMEM((tm, D_MODEL), BF16)],
        compiler_params=pltpu.CompilerParams(
            dimension_semantics=("arbitrary", "arbitrary"),
            vmem_limit_bytes=_vmem_limit(nbytes)),
        name="inproj",
    )(h, g, w_in, w_lora)


def _lru_rows(xb_ref, yb_ref, rows, tail, carry, seq_start, cw, cb, wx_ref, bx, wa_ref, ba,
              sp_lam, nrm, between):
    nrows = rows.stop - rows.start
    ngroup = nrows // SUBLANES
    sub = lax.broadcasted_iota(jnp.int32, (1, SUBLANES, 1), 1)
    ys, tails, carries, ss = [], [], [], 0.0
    for hd in range(LRU_HEADS):
        cols = slice(hd * LRU_BLOCK, (hd + 1) * LRU_BLOCK)
        x = xb_ref[rows, cols]
        xc = x * cw[LRU_CONV - 1:LRU_CONV, cols] + cb[:, cols]
        for d in range(1, LRU_CONV):
            xc = xc + _shift_rows(x, d, tail[:, cols]) * cw[LRU_CONV - 1 - d:LRU_CONV - d, cols]
        tails.append(x[nrows - SUBLANES:])

        xcb = xc.astype(BF16)
        gate_x = _sigmoid(jnp.dot(xcb, wx_ref[hd], preferred_element_type=F32) + bx[:, cols])
        gate_a = _sigmoid(jnp.dot(xcb, wa_ref[hd], preferred_element_type=F32) + ba[:, cols])
        log_a = (-LRU_C) * gate_a * sp_lam[:, cols]
        a = jnp.exp(log_a)
        mult = jnp.sqrt(1.0 - a * a)
        if seq_start is not None:
            row = lax.broadcasted_iota(jnp.int32, (nrows, 1), 0)
            mult = jnp.where(jnp.logical_and(row == 0, seq_start), 1.0, mult)
        b = xc * gate_x * mult

        a = a.reshape(ngroup, SUBLANES, LRU_BLOCK)
        b = b.reshape(ngroup, SUBLANES, LRU_BLOCK)
        d = 1
        while d < SUBLANES:
            keep = sub >= d
            a_sh = jnp.where(keep, pltpu.roll(a, d, axis=1), 1.0)
            b_sh = jnp.where(keep, pltpu.roll(b, d, axis=1), 0.0)
            b = a * b_sh + b
            a = a * a_sh
            d *= 2
        hcar = carry[:, cols]
        hs = []
        for grp in range(ngroup):
            hg = a[grp] * hcar + b[grp]
            hs.append(hg)
            hcar = hg[SUBLANES - 1:SUBLANES]
        carries.append(hcar)

        y = jnp.concatenate(hs, axis=0) * _gelu(yb_ref[rows, cols])
        ss = ss + jnp.sum(y * y, axis=-1, keepdims=True)
        ys.append(y)
        between(hd)
    scale = lax.rsqrt(ss * (1.0 / D_LRU) + RMS_EPS)
    out = jnp.concatenate(ys, axis=-1) * scale * nrm
    return out.astype(BF16), jnp.concatenate(tails, axis=-1), jnp.concatenate(carries, axis=-1)


def _lru_oproj_kernel(xb_ref, yb_ref, ob_ref, h_ref, cw_ref, cb_ref, wx_ref, bx_ref, wa_ref,
                      ba_ref, lam_ref, nrm_ref, woa_ref, wob_ref, o_ref, tail_ref, carry_ref):
    t = pl.program_id(1)

    @pl.when(t == 0)
    def _():
        tail_ref[...] = jnp.zeros_like(tail_ref)
        carry_ref[...] = jnp.zeros_like(carry_ref)

    sp_lam = _softplus(-lam_ref[...])
    tail, carry = tail_ref[...], carry_ref[0:1, :]
    ts = xb_ref.shape[0]
    ncol = D_MODEL // LRU_HEADS
    pieces = {}

    def project(name, lhs, w_ref):
        def step(hd):
            cols = slice(hd * ncol, (hd + 1) * ncol)
            pieces.setdefault(name, []).append(
                jnp.dot(lhs, w_ref[:, cols], preferred_element_type=F32))
        return step

    between = project("b", ob_ref[...], wob_ref)
    for r0 in range(0, ts, LRU_ROWS):
        out_a, tail, carry = _lru_rows(
            xb_ref, yb_ref, slice(r0, r0 + LRU_ROWS), tail, carry, (t == 0) if r0 == 0 else None,
            cw_ref[...], cb_ref[...], wx_ref, bx_ref[...], wa_ref, ba_ref[...], sp_lam,
            nrm_ref[...], between)
        between = project(("a", r0), out_a, woa_ref)
    for hd in range(LRU_HEADS):
        between(hd)
    tail_ref[...] = tail
    carry_ref[0:1, :] = carry
    acc_a = jnp.concatenate(
        [jnp.concatenate(pieces["a", r0], axis=-1) for r0 in range(0, ts, LRU_ROWS)], axis=0)
    o_ref[...] = h_ref[...] + jnp.concatenate(pieces["b"], axis=-1) + acc_a


def _lru_oproj(zm, out_b, h, cw, cb, wx, bx, wa, ba, lam, nrm, wo, layer, batch, seq, ts=512):
    T = zm.shape[0]
    nt = seq // ts
    vec = pl.BlockSpec((1, D_LRU), lambda b, t: (0, 0))
    mat = pl.BlockSpec((LRU_HEADS, LRU_BLOCK, LRU_BLOCK), lambda b, t: (0, 0, 0))

    def rows(width, col):
        return pl.BlockSpec((ts, width), lambda b, t: (b * nt + t, col))

    nbytes = (2 * (2 * ts * D_LRU * 4 + ts * D_RWKV * 2 + 2 * ts * D_MODEL * 4
                   + 2 * D_LRU * D_MODEL * 2) + 16 * LRU_ROWS * D_LRU * 4 + 2 * ts * D_MODEL * 4)
    return pl.pallas_call(
        _lru_oproj_kernel,
        grid=(batch, nt),
        in_specs=[
            rows(D_LRU, 0), rows(D_LRU, 1), rows(D_RWKV, 0), rows(D_MODEL, 0),
            pl.BlockSpec((LRU_CONV, D_LRU), lambda b, t: (0, 0)),
            vec, mat, vec, mat, vec, vec, vec,
            pl.BlockSpec((None, D_LRU, D_MODEL), lambda b, t: (layer, 0, 0)),
            pl.BlockSpec((None, D_RWKV, D_MODEL), lambda b, t: (layer, 1, 0)),
        ],
        out_specs=rows(D_MODEL, 0),
        out_shape=jax.ShapeDtypeStruct((T, D_MODEL), F32),
        scratch_shapes=[pltpu.VMEM((SUBLANES, D_LRU), F32),
                        pltpu.VMEM((SUBLANES, D_LRU), F32)],
        compiler_params=pltpu.CompilerParams(
            dimension_semantics=("arbitrary", "arbitrary"),
            vmem_limit_bytes=_vmem_limit(nbytes)),
        name="lru_oproj",
    )(zm, zm, out_b, h, cw, cb, wx, bx, wa, ba, lam, nrm, wo, wo)


def _mm(a, b):
    return jnp.dot(a.astype(BF16), b.astype(BF16), preferred_element_type=F32)


def _mm_nt(a, b):
    return lax.dot_general(a.astype(BF16), b.astype(BF16), (((1,), (1,)), ((), ())),
                           preferred_element_type=F32)


def _mm_tn(a, b):
    return lax.dot_general(a.astype(BF16), b.astype(BF16), (((0,), (0,)), ((), ())),
                           preferred_element_type=F32)


def _split3(x):
    hi = x.astype(BF16)
    r1 = x - hi.astype(F32)
    mid = r1.astype(BF16)
    lo = (r1 - mid.astype(F32)).astype(BF16)
    return hi, mid, lo


def _seg_sum(x):
    lane_lo = lax.broadcasted_iota(jnp.int32, (1, PAIR), 1) < HEAD
    out = []
    for p in range(x.shape[1] // PAIR):
        t = x[:, p * PAIR:(p + 1) * PAIR]
        s0 = jnp.sum(jnp.where(lane_lo, t, 0.0), axis=-1, keepdims=True)
        s1 = jnp.sum(jnp.where(lane_lo, 0.0, t), axis=-1, keepdims=True)
        out.append(jnp.where(lane_lo, s0, s1))
    return jnp.concatenate(out, axis=-1)


def _rwkv_kernel(has_vres, *refs):
    if has_vres:
        (r_ref, k_ref, v_ref, zl_ref, vf_ref, mur_ref, muk_ref, muv_ref, mul_ref,
         w0_ref, w2_ref, a0_ref, a2_ref, g2_ref, v0_ref, v2_ref,
         kkw_ref, ka_ref, rk_ref, lnw_ref, lnb_ref, o_ref, s_ref, prev_ref, prevz_ref) = refs
        vfo_ref = None
    else:
        (r_ref, k_ref, v_ref, zl_ref, mur_ref, muk_ref, muv_ref, mul_ref,
         w0_ref, w2_ref, a0_ref, a2_ref, g2_ref,
         kkw_ref, ka_ref, rk_ref, lnw_ref, lnb_ref, o_ref, vfo_ref,
         s_ref, prev_ref, prevz_ref) = refs
        vf_ref = v0_ref = v2_ref = None

    C = CHUNK
    nchunk = r_ref.shape[0] // C
    nh = r_ref.shape[1] // HEAD

    @pl.when(pl.program_id(2) == 0)
    def _():
        s_ref[...] = jnp.zeros_like(s_ref)
        prev_ref[...] = jnp.zeros_like(prev_ref)
        prevz_ref[...] = jnp.zeros_like(prevz_ref)

    ri3 = lax.broadcasted_iota(jnp.int32, (C, 3 * C), 0)
    ci3 = lax.broadcasted_iota(jnp.int32, (C, 3 * C), 1) % C
    tri3 = (ri3 >= ci3).astype(BF16)
    ri2 = lax.broadcasted_iota(jnp.int32, (2 * C, 2 * C), 0)
    ci2 = lax.broadcasted_iota(jnp.int32, (2 * C, 2 * C), 1) % C
    keep2 = jnp.where(ri2 < C, ri2, ri2 - C + 1) > ci2
    row0 = lax.broadcasted_iota(jnp.int32, (C, 1), 0) == 0
    zeros_h = jnp.zeros((C, HEAD), BF16)

    def shift_lerp(cur, prev_row, mu):
        sh = jnp.where(row0, prev_row, pltpu.roll(cur, 1, axis=0))
        return cur + (sh - cur) * mu

    def chunk_rows(c):
        return pl.ds(pl.multiple_of(c * C, C), C)

    def prep(c, out):
        rows = chunk_rows(c)
        first = c == 0
        before = pl.ds(jnp.maximum(c * C - 1, 0), 1)

        def lerp(ref, carried, mu):
            return shift_lerp(ref[rows, :], jnp.where(first, carried, ref[before, :]), mu)

        zl = lerp(zl_ref, prevz_ref[0:1, :], mul_ref[...])
        z01 = zl[:, 0:LANES]
        wpre = w0_ref[...] + _mm(jnp.tanh(z01), w2_ref[...])
        apre = a0_ref[...] + _mm(z01, a2_ref[...])
        g = _mm(_sigmoid(zl[:, LANES:3 * LANES]), g2_ref[...])
        if has_vres:
            mpre = v0_ref[...] + _mm(zl[:, 2 * LANES:3 * LANES], v2_ref[...])
        yield
        w_log = -_softplus(-wpre) - 0.5
        logw = -jnp.exp(w_log)
        cum = jnp.dot(tri3, jnp.concatenate(_split3(logw), axis=0), preferred_element_type=F32)
        yield
        r = lerp(r_ref, prev_ref[0:1, :], mur_ref[...])
        k = lerp(k_ref, prev_ref[1:2, :], muk_ref[...])
        a = _sigmoid(apre)
        yield
        v = lerp(v_ref, prev_ref[2:3, :], muv_ref[...])
        if has_vres:
            v = v + (vf_ref[rows, :] - v) * _sigmoid(mpre)
        else:
            vfo_ref[rows, :] = v
        yield
        kk = k * kkw_ref[...]
        kk = kk / jnp.maximum(jnp.sqrt(_seg_sum(kk * kk)), 1e-12)
        yield
        k2 = k * (1.0 + (a - 1.0) * ka_ref[...])
        bb = kk * a
        bonus = _seg_sum(r * k2 * rk_ref[...]) * v
        yield
        p_in = jnp.exp(cum)
        p_ex = jnp.exp(cum - logw)
        p_inv = jnp.exp(-cum)
        p_end = p_in[C - 1:C, :]
        yield
        rt = r * p_in
        at = -kk * p_ex
        rt_b, at_b, v_b = rt.astype(BF16), at.astype(BF16), v.astype(BF16)
        yield
        bt = bb * p_inv
        kt = k2 * p_inv
        bt_b, kt_b = bt.astype(BF16), kt.astype(BF16)
        yield
        bhat_b = (bt * p_end).astype(BF16)
        khat_b = (kt * p_end).astype(BF16)
        out["local"] = (rt_b, at_b, bt_b, kt_b, v_b, bhat_b, khat_b, at, rt, p_end)
        out["post"] = (bonus, g)

    def heads_stage(local, out):
        rt_b, at_b, bt_b, kt_b, v_b, bhat_b, khat_b, at, rt, p_end = local
        heads = range(nh)
        sls = [slice(hh * HEAD, (hh + 1) * HEAD) for hh in heads]
        sc_b, m, vh_b, d = [], [], [], []
        for sl in sls:
            ar = jnp.concatenate([at_b[:, sl], rt_b[:, sl]], axis=0)
            bk = jnp.concatenate([bt_b[:, sl], kt_b[:, sl]], axis=0)
            sc = jnp.where(keep2, _mm_nt(ar, bk), 0.0)
            sc_b.append(sc.astype(BF16))
            m.append(sc[:C, :C])
            vh_b.append(v_b[:, sl])
        yield
        for hh in heads:
            zv = jnp.concatenate([zeros_h, vh_b[hh]], axis=0)
            x_loc = jnp.dot(sc_b[hh][:C], zv, preferred_element_type=F32)
            d.append(jnp.concatenate([at[:, sls[hh]], x_loc], axis=-1))
        yield
        nstep = int(math.log2(C))
        for i in range(nstep):
            for hh in heads:
                m_b = m[hh].astype(BF16)
                if i + 1 < nstep:
                    rhs = jnp.concatenate([d[hh].astype(BF16), m_b], axis=-1)
                    prod = jnp.dot(m_b, rhs, preferred_element_type=F32)
                    d[hh] = d[hh] + prod[:, :2 * HEAD]
                    m[hh] = prod[:, 2 * HEAD:]
                else:
                    d[hh] = d[hh] + jnp.dot(m_b, d[hh].astype(BF16), preferred_element_type=F32)
            yield
        o1, wz = [], []
        for hh in heads:
            gmat = jnp.concatenate(
                [d[hh].astype(BF16), jnp.concatenate([zeros_h, vh_b[hh]], axis=-1)], axis=0)
            o1.append(jnp.dot(sc_b[hh][C:], gmat, preferred_element_type=F32))
            bkh = jnp.concatenate([bhat_b[:, sls[hh]], khat_b[:, sls[hh]]], axis=0)
            wz.append(_mm_tn(gmat, bkh))
        yield
        ys = []
        for hh in heads:
            rbar = rt[:, sls[hh]] + o1[hh][:, :HEAD]
            st = s_ref[hh]
            ys.append(_mm_nt(rbar, st) + o1[hh][:, HEAD:])
            s_ref[hh] = st * p_end[:, sls[hh]] + _mm(st, wz[hh][:HEAD]) + wz[hh][HEAD:]
        out["y"] = jnp.concatenate(ys, axis=-1)

    def tail(c, y, post):
        bonus, g = post
        mean = _seg_sum(y) * (1.0 / HEAD)
        yc = y - mean
        yield
        var = _seg_sum(yc * yc) * (1.0 / HEAD)
        yield
        yn = yc * lax.rsqrt(var + LNX_EPS) * lnw_ref[...] + lnb_ref[...]
        o_ref[chunk_rows(c), :] = ((yn + bonus) * g).astype(o_ref.dtype)

    def run_interleaved(*gens):
        alive = list(gens)
        while alive:
            for gen in list(alive):
                if next(gen, "done") == "done":
                    alive.remove(gen)

    def body(i, carry):
        local, y_prev, post_prev, post_cur = carry
        out = {}
        run_interleaved(heads_stage(local, out),
                        tail(jnp.maximum(i - 1, 0), y_prev, post_prev),
                        prep(jnp.minimum(i + 1, nchunk - 1), out))
        return out["local"], out["y"], post_cur, out["post"]

    first = {}
    run_interleaved(prep(0, first))
    zeros_w = jnp.zeros((C, r_ref.shape[1]), F32)
    _, y_last, post_last, _ = lax.fori_loop(
        0, nchunk, body, (first["local"], zeros_w, (zeros_w, zeros_w), first["post"]))
    run_interleaved(tail(nchunk - 1, y_last, post_last))

    last = pl.ds(r_ref.shape[0] - 1, 1)
    prev_ref[0:1, :] = r_ref[last, :]
    prev_ref[1:2, :] = k_ref[last, :]
    prev_ref[2:3, :] = v_ref[last, :]
    prevz_ref[0:1, :] = zl_ref[last, :]


def _rwkv(zm, zl, vfirst, prm, batch, seq, hg=HEAD_GROUP, ts=RWKV_ROWS):
    T = zm.shape[0]
    W = hg * HEAD
    ng = D_RWKV // W
    nt = seq // ts
    col0 = 2 * D_LRU // W
    has_vres = vfirst is not None

    def col(off):
        return pl.BlockSpec((ts, W), lambda b, g, t: (b * nt + t, off + g))

    vecg = pl.BlockSpec((1, W), lambda b, g, t: (0, g))
    vec_k = pl.BlockSpec((1, W), lambda b, g, t: (0, ng + g))
    vec_v = pl.BlockSpec((1, W), lambda b, g, t: (0, 2 * ng + g))
    vec_l = pl.BlockSpec((1, D_LORA), lambda b, g, t: (0, 0))

    def lora(rows):
        return pl.BlockSpec((rows, W), lambda b, g, t: (0, g))

    in_specs = [col(col0), col(col0 + ng), col(col0 + 2 * ng),
                pl.BlockSpec((ts, D_LORA), lambda b, g, t: (b * nt + t, 0))]
    args = [zm, zm, zm, zl]
    if has_vres:
        in_specs.append(col(0))
        args.append(vfirst)
    in_specs += [vecg, vec_k, vec_v, vec_l, vecg, lora(LANES), vecg, lora(LANES), lora(2 * LANES)]
    args += [prm["mu_rkv"], prm["mu_rkv"], prm["mu_rkv"], prm["mu_lora"],
             prm["w0"], prm["w2"], prm["a0"], prm["a2"], prm["g2"]]
    if has_vres:
        in_specs += [vecg, lora(LANES)]
        args += [prm["v0"], prm["v2"]]
    in_specs += [vecg] * 5
    args += [prm["kk"], prm["ka"], prm["rk"], prm["lnw"], prm["lnb"]]

    if has_vres:
        out_specs = col(0)
        out_shape = jax.ShapeDtypeStruct((T, D_RWKV), BF16)
    else:
        out_specs = [col(0), col(0)]
        out_shape = [jax.ShapeDtypeStruct((T, D_RWKV), BF16),
                     jax.ShapeDtypeStruct((T, D_RWKV), F32)]
    nbytes = 2 * ts * (5 * W * 4 + D_LORA * 4 + W * 2) + hg * HEAD * HEAD * 4
    res = pl.pallas_call(
        functools.partial(_rwkv_kernel, has_vres),
        grid=(batch, ng, nt),
        in_specs=in_specs,
        out_specs=out_specs,
        out_shape=out_shape,
        scratch_shapes=[pltpu.VMEM((hg, HEAD, HEAD), F32),
                        pltpu.VMEM((SUBLANES, W), F32),
                        pltpu.VMEM((SUBLANES, D_LORA), F32)],
        compiler_params=pltpu.CompilerParams(
            dimension_semantics=("arbitrary", "arbitrary", "arbitrary"),
            vmem_limit_bytes=_vmem_limit(nbytes)),
        name="rwkv7",
    )(*args)
    if has_vres:
        return res, vfirst
    return res[0], res[1]


def _ffn_kernel(tiles_per_seq, h_ref, g_ref, wg_ref, wu_ref, cw_ref, cb_ref, wd_ref,
                o_ref, u_ref, acc_ref, tail_ref):
    i = pl.program_id(0)
    j = pl.program_id(1)

    @pl.when(j == 0)
    def _():
        u_ref[...] = _rms(h_ref[...], g_ref[...]).astype(BF16)
        acc_ref[...] = jnp.zeros_like(acc_ref)

    @pl.when(i % tiles_per_seq == 0)
    def _():
        tail_ref[j] = jnp.zeros(tail_ref.shape[1:], F32)

    u = u_ref[...]
    gate = jnp.dot(u, wg_ref[...], preferred_element_type=F32)
    tm = gate.shape[0]
    tail = tail_ref[j]
    cw = cw_ref[...]
    conv = gate * cw[FFN_CONV - 1:FFN_CONV] + cb_ref[...]
    for d in range(1, FFN_CONV):
        conv = conv + _shift_rows(gate, d, tail) * cw[FFN_CONV - 1 - d:FFN_CONV - d]
    tail_ref[j] = gate[tm - SUBLANES:]
    up = jnp.dot(u, wu_ref[...], preferred_element_type=F32)
    act = (_gelu(conv) * up).astype(BF16)
    acc_ref[...] += jnp.dot(act, wd_ref[...], preferred_element_type=F32)

    @pl.when(j == pl.num_programs(1) - 1)
    def _():
        o_ref[...] = h_ref[...] + acc_ref[...]


def _ffn(h, g, wg, wu, cw, cb, wd, layer, seq, tm=512, tf=1024):
    T = h.shape[0]
    nf = D_FF // tf
    nbytes = (2 * (2 * tm * D_MODEL * 4 + 3 * D_MODEL * tf * 2) + tm * D_MODEL * 6
              + nf * SUBLANES * tf * 4 + 6 * tm * tf * 4)
    return pl.pallas_call(
        functools.partial(_ffn_kernel, seq // tm),
        grid=(T // tm, nf),
        in_specs=[
            pl.BlockSpec((tm, D_MODEL), lambda i, j: (i, 0)),
            pl.BlockSpec((1, D_MODEL), lambda i, j: (0, 0)),
            pl.BlockSpec((None, D_MODEL, tf), lambda i, j: (layer, 0, j)),
            pl.BlockSpec((None, D_MODEL, tf), lambda i, j: (layer, 0, j)),
            pl.BlockSpec((FFN_CONV, tf), lambda i, j: (0, j)),
            pl.BlockSpec((1, tf), lambda i, j: (0, j)),
            pl.BlockSpec((None, tf, D_MODEL), lambda i, j: (layer, j, 0)),
        ],
        out_specs=pl.BlockSpec((tm, D_MODEL), lambda i, j: (i, 0)),
        out_shape=jax.ShapeDtypeStruct((T, D_MODEL), F32),
        scratch_shapes=[pltpu.VMEM((tm, D_MODEL), BF16),
                        pltpu.VMEM((tm, D_MODEL), F32),
                        pltpu.VMEM((nf, SUBLANES, tf), F32)],
        compiler_params=pltpu.CompilerParams(
            dimension_semantics=("arbitrary", "arbitrary"),
            vmem_limit_bytes=_vmem_limit(nbytes)),
        name="ffn",
    )(h, g, wg, wu, cw, cb, wd)


def _ple_kernel(final, h_ref, p_ref, g_ref, wg_ref, wp_ref, gp_ref, gf_ref, o_ref):
    h = h_ref[...]
    u = _rms(h, g_ref[...]).astype(BF16)
    gate = _sigmoid(jnp.dot(u, wg_ref[...], preferred_element_type=F32))
    proj = jnp.dot(p_ref[...].astype(BF16), wp_ref[...], preferred_element_type=F32)
    out = h + _rms(gate * proj, gp_ref[...])
    if final:
        out = _rms(out, gf_ref[...])
    o_ref[...] = out


def _ple(h, p, g, wg, layer, wp, gp, gf, final, tm=512):
    T = h.shape[0]
    vec = pl.BlockSpec((1, D_MODEL), lambda i: (0, 0))
    nbytes = 2 * (2 * tm * D_MODEL * 4 + tm * D_PLE * 4 + D_MODEL * D_MODEL * 2
                  + D_PLE * D_MODEL * 2) + 4 * tm * D_MODEL * 4
    return pl.pallas_call(
        functools.partial(_ple_kernel, final),
        grid=(T // tm,),
        in_specs=[
            pl.BlockSpec((tm, D_MODEL), lambda i: (i, 0)),
            pl.BlockSpec((tm, D_PLE), lambda i: (i, 0)),
            vec,
            pl.BlockSpec((None, D_MODEL, D_MODEL), lambda i: (layer, 0, 0)),
            pl.BlockSpec((D_PLE, D_MODEL), lambda i: (0, 0)),
            vec, vec,
        ],
        out_specs=pl.BlockSpec((tm, D_MODEL), lambda i: (i, 0)),
        out_shape=jax.ShapeDtypeStruct((T, D_MODEL), F32),
        compiler_params=pltpu.CompilerParams(
            dimension_semantics=("arbitrary",),
            vmem_limit_bytes=_vmem_limit(nbytes)),
        name="ple",
    )(h, p, g, wg, wp, gp, gf)


def _cast_kernel(w_ref, o_ref):
    o_ref[...] = w_ref[...].astype(o_ref.dtype)


def _cast_main_columns(w_in, tr=256):
    depth, rows, _ = w_in.shape
    return pl.pallas_call(
        _cast_kernel,
        grid=(depth, rows // tr),
        in_specs=[pl.BlockSpec((None, tr, D_MAIN), lambda l, i: (l, i, 0))],
        out_specs=pl.BlockSpec((None, tr, D_MAIN), lambda l, i: (l, i, 0)),
        out_shape=jax.ShapeDtypeStruct((depth, rows, D_MAIN), BF16),
        compiler_params=pltpu.CompilerParams(
            dimension_semantics=("arbitrary", "arbitrary"),
            vmem_limit_bytes=_vmem_limit(2 * tr * D_MAIN * 6)),
        name="cast_w_in",
    )(w_in)


def _row(v):
    return v.reshape(1, -1).astype(F32)


def _pad_rows(w, top, total):
    return jnp.pad(w, ((top, total - top - w.shape[0]), (0, 0)))


def kernel(x, p, ln_mix, w_in, w_in_vres, mu_shift, mu_shift_vres, conv_a_w, conv_a_b, lru_wx, lru_bx, lru_wa, lru_ba, lru_lambda, lru_norm, rwkv_w0, rwkv_w2, rwkv_a0, rwkv_a2, rwkv_v0, rwkv_v2, rwkv_g2, rwkv_kk, rwkv_ka, rwkv_rk, rwkv_lnx_w, rwkv_lnx_b, w_o, ln_ffn, w_gate, w_up, conv_f_w, conv_f_b, w_down, ln_ple, w_ple_gate, w_ple_proj, ln_ple_post, ln_final):
    batch, seq, _ = x.shape
    depth = w_in.shape[0]
    T = batch * seq
    h = x.reshape(T, D_MODEL)
    n_lora = LORA_W + LORA_A + LORA_G
    vfirst = None
    w_in_b = _cast_main_columns(w_in)
    w_o_b, w_gate_b, w_up_b, w_down_b, w_ple_gate_b = (
        w.astype(BF16) for w in (w_o, w_gate, w_up, w_down, w_ple_gate))
    for i in range(depth):
        lora_cols = [w_in[i][:, D_MAIN:]]
        mu_l = [mu_shift[i][3 * D_RWKV:]]
        if i > 0:
            lora_cols.append(w_in_vres[i - 1])
            mu_l.append(mu_shift_vres[i - 1])
        w_lora = jnp.concatenate(lora_cols, axis=1)
        w_lora = jnp.pad(w_lora, ((0, 0), (0, D_LORA - w_lora.shape[1]))).astype(BF16)
        mu_lora = jnp.concatenate(mu_l, axis=0)
        mu_lora = jnp.pad(mu_lora, (0, D_LORA - mu_lora.shape[0]))

        zm, zl = _inproj(h, _row(ln_mix[i]), w_in_b, i, w_lora)

        prm = {
            "mu_rkv": _row(mu_shift[i][:3 * D_RWKV]),
            "mu_lora": _row(mu_lora),
            "w0": _row(rwkv_w0[i]),
            "w2": _pad_rows(rwkv_w2[i], 0, LANES),
            "a0": _row(rwkv_a0[i]),
            "a2": _pad_rows(rwkv_a2[i], LORA_W, LANES),
            "g2": _pad_rows(rwkv_g2[i], 0, 2 * LANES),
            "kk": _row(rwkv_kk[i]), "ka": _row(rwkv_ka[i]), "rk": _row(rwkv_rk[i]),
            "lnw": _row(rwkv_lnx_w[i]), "lnb": _row(rwkv_lnx_b[i]),
        }
        if i > 0:
            prm["v0"] = _row(rwkv_v0[i - 1])
            prm["v2"] = _pad_rows(rwkv_v2[i - 1], n_lora - 2 * LANES, LANES)
        out_b, vfirst = _rwkv(zm, zl, vfirst, prm, batch, seq)

        h = _lru_oproj(zm, out_b, h, conv_a_w[i], _row(conv_a_b[i]), lru_wx[i].astype(BF16),
                       _row(lru_bx[i]), lru_wa[i].astype(BF16), _row(lru_ba[i]),
                       _row(lru_lambda[i]), _row(lru_norm[i]), w_o_b, i, batch, seq)
        h = _ffn(h, _row(ln_ffn[i]), w_gate_b, w_up_b, conv_f_w[i], _row(conv_f_b[i]),
                 w_down_b, i, seq)
        h = _ple(h, p[i].reshape(T, D_PLE), _row(ln_ple[i]), w_ple_gate_b, i,
                 w_ple_proj[i].astype(BF16), _row(ln_ple_post[i]), _row(ln_final),
                 final=(i == depth - 1))
    return h.reshape(batch, seq, D_MODEL)
```

```python
import functools
import math

import jax
import jax.numpy as jnp
from jax import lax
from jax.experimental import pallas as pl
from jax.experimental.pallas import tpu as pltpu

F32 = jnp.float32
BF16 = jnp.bfloat16

D_MODEL = 2048
D_LRU = 1024
D_RWKV = 1024
LRU_HEADS = 4
LRU_BLOCK = 256
LRU_CONV = 4
LRU_C = 8.0
HEAD = 64
N_HEADS = D_RWKV // HEAD
LORA_W = 64
LORA_A = 64
LORA_V = 32
LORA_G = 160
D_MAIN = 2 * D_LRU + 3 * D_RWKV
D_LORA = 384
D_FF = 3 * D_MODEL
FFN_CONV = 3
D_PLE = 256
RMS_EPS = 1e-6
LNX_EPS = 64e-5

V7X_VMEM_BYTES = 64 * 1024 * 1024
SUBLANES = 8
LANES = 128
ROW_TILE = 16

CHUNK = 64
PAIR = 2 * HEAD
LRU_ROWS = 256
HEAD_GROUP = 16
RWKV_ROWS = 1024


def _vmem_limit(nbytes):
    return int(min(V7X_VMEM_BYTES - 4 * 1024 * 1024, nbytes + 16 * 1024 * 1024))


def _rms(x, g):
    return x * lax.rsqrt(jnp.mean(x * x, axis=-1, keepdims=True) + RMS_EPS) * g


def _gelu(x):
    c = math.sqrt(2.0 / math.pi)
    return 0.5 * x * (1.0 + jnp.tanh(c * (x + 0.044715 * (x * x * x))))


def _sigmoid(x):
    return 1.0 / (1.0 + jnp.exp(-x))


def _softplus(x):
    return jnp.maximum(x, 0.0) + jnp.log1p(jnp.exp(-jnp.abs(x)))


def _shift_rows(x, d, prev8):
    rolled = pltpu.roll(x, d, axis=0)
    prev = pltpu.roll(prev8, d, axis=0)
    row = lax.broadcasted_iota(jnp.int32, prev8.shape, 0)
    top = jnp.where(row < d, prev, rolled[:SUBLANES])
    return jnp.concatenate([top, rolled[SUBLANES:]], axis=0)


def _inproj_kernel(x_ref, g_ref, wm_ref, wl_ref, zm_ref, zl_ref, u_ref):
    @pl.when(pl.program_id(1) == 0)
    def _():
        u_ref[...] = _rms(x_ref[...], g_ref[...]).astype(BF16)
        zl_ref[...] = jnp.dot(u_ref[...], wl_ref[...], preferred_element_type=F32)

    zm_ref[...] = jnp.dot(u_ref[...], wm_ref[...], preferred_element_type=F32)


def _inproj(h, g, w_in, layer, w_lora, tm=1024, tn=1024):
    T = h.shape[0]
    nbytes = 2 * (tm * D_MODEL * 4 + D_MODEL * tn * 2 + D_MODEL * D_LORA * 2
                  + tm * tn * 4 + tm * D_LORA * 4) + tm * D_MODEL * 2
    return pl.pallas_call(
        _inproj_kernel,
        grid=(T // tm, D_MAIN // tn),
        in_specs=[
            pl.BlockSpec((tm, D_MODEL), lambda i, j: (i, 0)),
            pl.BlockSpec((1, D_MODEL), lambda i, j: (0, 0)),
            pl.BlockSpec((None, D_MODEL, tn), lambda i, j: (layer, 0, j)),
            pl.BlockSpec((D_MODEL, D_LORA), lambda i, j: (0, 0)),
        ],
        out_specs=[
            pl.BlockSpec((tm, tn), lambda i, j: (i, j)),
            pl.BlockSpec((tm, D_LORA), lambda i, j: (i, 0)),
        ],
        out_shape=[
            jax.ShapeDtypeStruct((T, D_MAIN), F32),
            jax.ShapeDtypeStruct((T, D_LORA), F32),
        ],
        scratch_shapes=[pltpu.VMEM((tm, D_MODEL), BF16)],
        compiler_params=pltpu.CompilerParams(
            dimension_semantics=("arbitrary", "arbitrary"),
            vmem_limit_bytes=_vmem_limit(nbytes)),
        name="inproj",
    )(h, g, w_in, w_lora)


def _lru_rows(xb_ref, yb_ref, rows, tail, carry, seq_start, cw, cb, wx_ref, bx, wa_ref, ba,
              sp_lam, nrm, between):
    nrows = rows.stop - rows.start
    ngroup = nrows // SUBLANES
    sub = lax.broadcasted_iota(jnp.int32, (1, SUBLANES, 1), 1)
    ys, tails, carries, ss = [], [], [], 0.0
    for hd in range(LRU_HEADS):
        cols = slice(hd * LRU_BLOCK, (hd + 1) * LRU_BLOCK)
        x = xb_ref[rows, cols]
        xc = x * cw[LRU_CONV - 1:LRU_CONV, cols] + cb[:, cols]
        for d in range(1, LRU_CONV):
            xc = xc + _shift_rows(x, d, tail[:, cols]) * cw[LRU_CONV - 1 - d:LRU_CONV - d, cols]
        tails.append(x[nrows - SUBLANES:])

        xcb = xc.astype(BF16)
        gate_x = _sigmoid(jnp.dot(xcb, wx_ref[hd], preferred_element_type=F32) + bx[:, cols])
        gate_a = _sigmoid(jnp.dot(xcb, wa_ref[hd], preferred_element_type=F32) + ba[:, cols])
        log_a = (-LRU_C) * gate_a * sp_lam[:, cols]
        a = jnp.exp(log_a)
        mult = jnp.sqrt(1.0 - a * a)
        if seq_start is not None:
            row = lax.broadcasted_iota(jnp.int32, (nrows, 1), 0)
            mult = jnp.where(jnp.logical_and(row == 0, seq_start), 1.0, mult)
        b = xc * gate_x * mult

        a = a.reshape(ngroup, SUBLANES, LRU_BLOCK)
        b = b.reshape(ngroup, SUBLANES, LRU_BLOCK)
        d = 1
        while d < SUBLANES:
            keep = sub >= d
            a_sh = jnp.where(keep, pltpu.roll(a, d, axis=1), 1.0)
            b_sh = jnp.where(keep, pltpu.roll(b, d, axis=1), 0.0)
            b = a * b_sh + b
            a = a * a_sh
            d *= 2
        hcar = carry[:, cols]
        hs = []
        for grp in range(ngroup):
            hg = a[grp] * hcar + b[grp]
            hs.append(hg)
            hcar = hg[SUBLANES - 1:SUBLANES]
        carries.append(hcar)

        y = jnp.concatenate(hs, axis=0) * _gelu(yb_ref[rows, cols])
        ss = ss + jnp.sum(y * y, axis=-1, keepdims=True)
        ys.append(y)
        between(hd)
    scale = lax.rsqrt(ss * (1.0 / D_LRU) + RMS_EPS)
    out = jnp.concatenate(ys, axis=-1) * scale * nrm
    return out.astype(BF16), jnp.concatenate(tails, axis=-1), jnp.concatenate(carries, axis=-1)


def _lru_oproj_kernel(xb_ref, yb_ref, ob_ref, h_ref, cw_ref, cb_ref, wx_ref, bx_ref, wa_ref,
                      ba_ref, lam_ref, nrm_ref, woa_ref, wob_ref, o_ref, tail_ref, carry_ref):
    t = pl.program_id(1)

    @pl.when(t == 0)
    def _():
        tail_ref[...] = jnp.zeros_like(tail_ref)
        carry_ref[...] = jnp.zeros_like(carry_ref)

    sp_lam = _softplus(-lam_ref[...])
    tail, carry = tail_ref[...], carry_ref[0:1, :]
    ts = xb_ref.shape[0]
    ncol = D_MODEL // LRU_HEADS
    pieces = {}

    def project(name, lhs, w_ref):
        def step(hd):
            cols = slice(hd * ncol, (hd + 1) * ncol)
            pieces.setdefault(name, []).append(
                jnp.dot(lhs, w_ref[:, cols], preferred_element_type=F32))
        return step

    between = project("b", ob_ref[...], wob_ref)
    for r0 in range(0, ts, LRU_ROWS):
        out_a, tail, carry = _lru_rows(
            xb_ref, yb_ref, slice(r0, r0 + LRU_ROWS), tail, carry, (t == 0) if r0 == 0 else None,
            cw_ref[...], cb_ref[...], wx_ref, bx_ref[...], wa_ref, ba_ref[...], sp_lam,
            nrm_ref[...], between)
        between = project(("a", r0), out_a, woa_ref)
    for hd in range(LRU_HEADS):
        between(hd)
    tail_ref[...] = tail
    carry_ref[0:1, :] = carry
    acc_a = jnp.concatenate(
        [jnp.concatenate(pieces["a", r0], axis=-1) for r0 in range(0, ts, LRU_ROWS)], axis=0)
    o_ref[...] = h_ref[...] + jnp.concatenate(pieces["b"], axis=-1) + acc_a


def _lru_oproj(zm, out_b, h, cw, cb, wx, bx, wa, ba, lam, nrm, wo, layer, batch, seq, ts=512):
    T = zm.shape[0]
    nt = seq // ts
    vec = pl.BlockSpec((1, D_LRU), lambda b, t: (0, 0))
    mat = pl.BlockSpec((LRU_HEADS, LRU_BLOCK, LRU_BLOCK), lambda b, t: (0, 0, 0))

    def rows(width, col):
        return pl.BlockSpec((ts, width), lambda b, t: (b * nt + t, col))

    nbytes = (2 * (2 * ts * D_LRU * 4 + ts * D_RWKV * 2 + 2 * ts * D_MODEL * 4
                   + 2 * D_LRU * D_MODEL * 2) + 16 * LRU_ROWS * D_LRU * 4 + 2 * ts * D_MODEL * 4)
    return pl.pallas_call(
        _lru_oproj_kernel,
        grid=(batch, nt),
        in_specs=[
            rows(D_LRU, 0), rows(D_LRU, 1), rows(D_RWKV, 0), rows(D_MODEL, 0),
            pl.BlockSpec((LRU_CONV, D_LRU), lambda b, t: (0, 0)),
            vec, mat, vec, mat, vec, vec, vec,
            pl.BlockSpec((None, D_LRU, D_MODEL), lambda b, t: (layer, 0, 0)),
            pl.BlockSpec((None, D_RWKV, D_MODEL), lambda b, t: (layer, 1, 0)),
        ],
        out_specs=rows(D_MODEL, 0),
        out_shape=jax.ShapeDtypeStruct((T, D_MODEL), F32),
        scratch_shapes=[pltpu.VMEM((SUBLANES, D_LRU), F32),
                        pltpu.VMEM((SUBLANES, D_LRU), F32)],
        compiler_params=pltpu.CompilerParams(
            dimension_semantics=("arbitrary", "arbitrary"),
            vmem_limit_bytes=_vmem_limit(nbytes)),
        name="lru_oproj",
    )(zm, zm, out_b, h, cw, cb, wx, bx, wa, ba, lam, nrm, wo, wo)


def _mm(a, b):
    return jnp.dot(a.astype(BF16), b.astype(BF16), preferred_element_type=F32)


def _mm_nt(a, b):
    return lax.dot_general(a.astype(BF16), b.astype(BF16), (((1,), (1,)), ((), ())),
                           preferred_element_type=F32)


def _mm_tn(a, b):
    return lax.dot_general(a.astype(BF16), b.astype(BF16), (((0,), (0,)), ((), ())),
                           preferred_element_type=F32)


def _split3(x):
    hi = x.astype(BF16)
    r1 = x - hi.astype(F32)
    mid = r1.astype(BF16)
    lo = (r1 - mid.astype(F32)).astype(BF16)
    return hi, mid, lo


def _seg_sum(x):
    lane_lo = lax.broadcasted_iota(jnp.int32, (1, PAIR), 1) < HEAD
    out = []
    for p in range(x.shape[1] // PAIR):
        t = x[:, p * PAIR:(p + 1) * PAIR]
        s0 = jnp.sum(jnp.where(lane_lo, t, 0.0), axis=-1, keepdims=True)
        s1 = jnp.sum(jnp.where(lane_lo, 0.0, t), axis=-1, keepdims=True)
        out.append(jnp.where(lane_lo, s0, s1))
    return jnp.concatenate(out, axis=-1)


def _rwkv_kernel(has_vres, *refs):
    if has_vres:
        (r_ref, k_ref, v_ref, zl_ref, vf_ref, mur_ref, muk_ref, muv_ref, mul_ref,
         w0_ref, w2_ref, a0_ref, a2_ref, g2_ref, v0_ref, v2_ref,
         kkw_ref, ka_ref, rk_ref, lnw_ref, lnb_ref, o_ref, s_ref, prev_ref, prevz_ref) = refs
        vfo_ref = None
    else:
        (r_ref, k_ref, v_ref, zl_ref, mur_ref, muk_ref, muv_ref, mul_ref,
         w0_ref, w2_ref, a0_ref, a2_ref, g2_ref,
         kkw_ref, ka_ref, rk_ref, lnw_ref, lnb_ref, o_ref, vfo_ref,
         s_ref, prev_ref, prevz_ref) = refs
        vf_ref = v0_ref = v2_ref = None

    C = CHUNK
    nchunk = r_ref.shape[0] // C
    nh = r_ref.shape[1] // HEAD

    @pl.when(pl.program_id(2) == 0)
    def _():
        s_ref[...] = jnp.zeros_like(s_ref)
        prev_ref[...] = jnp.zeros_like(prev_ref)
        prevz_ref[...] = jnp.zeros_like(prevz_ref)

    ri3 = lax.broadcasted_iota(jnp.int32, (C, 3 * C), 0)
    ci3 = lax.broadcasted_iota(jnp.int32, (C, 3 * C), 1) % C
    tri3 = (ri3 >= ci3).astype(BF16)
    ri2 = lax.broadcasted_iota(jnp.int32, (2 * C, 2 * C), 0)
    ci2 = lax.broadcasted_iota(jnp.int32, (2 * C, 2 * C), 1) % C
    keep2 = jnp.where(ri2 < C, ri2, ri2 - C + 1) > ci2
    row0 = lax.broadcasted_iota(jnp.int32, (C, 1), 0) == 0
    zeros_h = jnp.zeros((C, HEAD), BF16)

    def shift_lerp(cur, prev_row, mu):
        sh = jnp.where(row0, prev_row, pltpu.roll(cur, 1, axis=0))
        return cur + (sh - cur) * mu

    def chunk_rows(c):
        return pl.ds(pl.multiple_of(c * C, C), C)

    def prep(c, out):
        rows = chunk_rows(c)
        first = c == 0
        before = pl.ds(jnp.maximum(c * C - 1, 0), 1)

        def lerp(ref, carried, mu):
            return shift_lerp(ref[rows, :], jnp.where(first, carried, ref[before, :]), mu)

        zl = lerp(zl_ref, prevz_ref[0:1, :], mul_ref[...])
        z01 = zl[:, 0:LANES]
        wpre = w0_ref[...] + _mm(jnp.tanh(z01), w2_ref[...])
        apre = a0_ref[...] + _mm(z01, a2_ref[...])
        g = _mm(_sigmoid(zl[:, LANES:3 * LANES]), g2_ref[...])
        if has_vres:
            mpre = v0_ref[...] + _mm(zl[:, 2 * LANES:3 * LANES], v2_ref[...])
        yield
        w_log = -_softplus(-wpre) - 0.5
        logw = -jnp.exp(w_log)
        cum = jnp.dot(tri3, jnp.concatenate(_split3(logw), axis=0), preferred_element_type=F32)
        yield
        r = lerp(r_ref, prev_ref[0:1, :], mur_ref[...])
        k = lerp(k_ref, prev_ref[1:2, :], muk_ref[...])
        a = _sigmoid(apre)
        yield
        v = lerp(v_ref, prev_ref[2:3, :], muv_ref[...])
        if has_vres:
            v = v + (vf_ref[rows, :] - v) * _sigmoid(mpre)
        else:
            vfo_ref[rows, :] = v
        yield
        kk = k * kkw_ref[...]
        kk = kk / jnp.maximum(jnp.sqrt(_seg_sum(kk * kk)), 1e-12)
        yield
        k2 = k * (1.0 + (a - 1.0) * ka_ref[...])
        bb = kk * a
        bonus = _seg_sum(r * k2 * rk_ref[...]) * v
        yield
        p_in = jnp.exp(cum)
        p_ex = jnp.exp(cum - logw)
        p_inv = jnp.exp(-cum)
        p_end = p_in[C - 1:C, :]
        yield
        rt = r * p_in
        at = -kk * p_ex
        rt_b, at_b, v_b = rt.astype(BF16), at.astype(BF16), v.astype(BF16)
        yield
        bt = bb * p_inv
        kt = k2 * p_inv
        bt_b, kt_b = bt.astype(BF16), kt.astype(BF16)
        yield
        bhat_b = (bt * p_end).astype(BF16)
        khat_b = (kt * p_end).astype(BF16)
        out["local"] = (rt_b, at_b, bt_b, kt_b, v_b, bhat_b, khat_b, at, rt, p_end)
        out["post"] = (bonus, g)

    def heads_stage(local, out):
        rt_b, at_b, bt_b, kt_b, v_b, bhat_b, khat_b, at, rt, p_end = local
        heads = range(nh)
        sls = [slice(hh * HEAD, (hh + 1) * HEAD) for hh in heads]
        sc_b, m, vh_b, d = [], [], [], []
        for sl in sls:
            ar = jnp.concatenate([at_b[:, sl], rt_b[:, sl]], axis=0)
            bk = jnp.concatenate([bt_b[:, sl], kt_b[:, sl]], axis=0)
            sc = jnp.where(keep2, _mm_nt(ar, bk), 0.0)
            sc_b.append(sc.astype(BF16))
            m.append(sc[:C, :C])
            vh_b.append(v_b[:, sl])
        yield
        for hh in heads:
            zv = jnp.concatenate([zeros_h, vh_b[hh]], axis=0)
            x_loc = jnp.dot(sc_b[hh][:C], zv, preferred_element_type=F32)
            d.append(jnp.concatenate([at[:, sls[hh]], x_loc], axis=-1))
        yield
        nstep = int(math.log2(C))
        for i in range(nstep):
            lo = (2 ** i // ROW_TILE) * ROW_TILE
            for hh in heads:
                m_b = m[hh].astype(BF16)[lo:, :C - lo]
                d_b = d[hh].astype(BF16)[:C - lo]
                if i + 1 < nstep:
                    rhs = jnp.concatenate([d_b, m[hh].astype(BF16)[:C - lo]], axis=-1)
                    prod = jnp.dot(m_b, rhs, preferred_element_type=F32)
                    upd, m_new = prod[:, :2 * HEAD], prod[:, 2 * HEAD:]
                    if lo:
                        m_new = jnp.concatenate([jnp.zeros((lo, C), F32), m_new], axis=0)
                    m[hh] = m_new
                else:
                    upd = jnp.dot(m_b, d_b, preferred_element_type=F32)
                if lo:
                    upd = jnp.concatenate([jnp.zeros((lo, 2 * HEAD), F32), upd], axis=0)
                d[hh] = d[hh] + upd
            yield
        o1, wz = [], []
        for hh in heads:
            gmat = jnp.concatenate(
                [d[hh].astype(BF16), jnp.concatenate([zeros_h, vh_b[hh]], axis=-1)], axis=0)
            o1.append(jnp.dot(sc_b[hh][C:], gmat, preferred_element_type=F32))
            bkh = jnp.concatenate([bhat_b[:, sls[hh]], khat_b[:, sls[hh]]], axis=0)
            wz.append(_mm_tn(gmat, bkh))
        yield
        ys = []
        for hh in heads:
            rbar = rt[:, sls[hh]] + o1[hh][:, :HEAD]
            st = s_ref[hh]
            ys.append(_mm_nt(rbar, st) + o1[hh][:, HEAD:])
            s_ref[hh] = st * p_end[:, sls[hh]] + _mm(st, wz[hh][:HEAD]) + wz[hh][HEAD:]
        out["y"] = jnp.concatenate(ys, axis=-1)

    def tail(c, y, post):
        bonus, g = post
        mean = _seg_sum(y) * (1.0 / HEAD)
        yc = y - mean
        yield
        var = _seg_sum(yc * yc) * (1.0 / HEAD)
        yield
        yn = yc * lax.rsqrt(var + LNX_EPS) * lnw_ref[...] + lnb_ref[...]
        o_ref[chunk_rows(c), :] = ((yn + bonus) * g).astype(o_ref.dtype)

    def run_interleaved(*gens):
        alive = list(gens)
        while alive:
            for gen in list(alive):
                if next(gen, "done") == "done":
                    alive.remove(gen)

    def body(i, carry):
        local, y_prev, post_prev, post_cur = carry
        out = {}
        run_interleaved(heads_stage(local, out),
                        tail(jnp.maximum(i - 1, 0), y_prev, post_prev),
                        prep(jnp.minimum(i + 1, nchunk - 1), out))
        return out["local"], out["y"], post_cur, out["post"]

    first = {}
    run_interleaved(prep(0, first))
    zeros_w = jnp.zeros((C, r_ref.shape[1]), F32)
    _, y_last, post_last, _ = lax.fori_loop(
        0, nchunk, body, (first["local"], zeros_w, (zeros_w, zeros_w), first["post"]))
    run_interleaved(tail(nchunk - 1, y_last, post_last))

    last = pl.ds(r_ref.shape[0] - 1, 1)
    prev_ref[0:1, :] = r_ref[last, :]
    prev_ref[1:2, :] = k_ref[last, :]
    prev_ref[2:3, :] = v_ref[last, :]
    prevz_ref[0:1, :] = zl_ref[last, :]


def _rwkv(zm, zl, vfirst, prm, batch, seq, hg=HEAD_GROUP, ts=RWKV_ROWS):
    T = zm.shape[0]
    W = hg * HEAD
    ng = D_RWKV // W
    nt = seq // ts
    col0 = 2 * D_LRU // W
    has_vres = vfirst is not None

    def col(off):
        return pl.BlockSpec((ts, W), lambda b, g, t: (b * nt + t, off + g))

    vecg = pl.BlockSpec((1, W), lambda b, g, t: (0, g))
    vec_k = pl.BlockSpec((1, W), lambda b, g, t: (0, ng + g))
    vec_v = pl.BlockSpec((1, W), lambda b, g, t: (0, 2 * ng + g))
    vec_l = pl.BlockSpec((1, D_LORA), lambda b, g, t: (0, 0))

    def lora(rows):
        return pl.BlockSpec((rows, W), lambda b, g, t: (0, g))

    in_specs = [col(col0), col(col0 + ng), col(col0 + 2 * ng),
                pl.BlockSpec((ts, D_LORA), lambda b, g, t: (b * nt + t, 0))]
    args = [zm, zm, zm, zl]
    if has_vres:
        in_specs.append(col(0))
        args.append(vfirst)
    in_specs += [vecg, vec_k, vec_v, vec_l, vecg, lora(LANES), vecg, lora(LANES), lora(2 * LANES)]
    args += [prm["mu_rkv"], prm["mu_rkv"], prm["mu_rkv"], prm["mu_lora"],
             prm["w0"], prm["w2"], prm["a0"], prm["a2"], prm["g2"]]
    if has_vres:
        in_specs += [vecg, lora(LANES)]
        args += [prm["v0"], prm["v2"]]
    in_specs += [vecg] * 5
    args += [prm["kk"], prm["ka"], prm["rk"], prm["lnw"], prm["lnb"]]

    if has_vres:
        out_specs = col(0)
        out_shape = jax.ShapeDtypeStruct((T, D_RWKV), BF16)
    else:
        out_specs = [col(0), col(0)]
        out_shape = [jax.ShapeDtypeStruct((T, D_RWKV), BF16),
                     jax.ShapeDtypeStruct((T, D_RWKV), F32)]
    nbytes = 2 * ts * (5 * W * 4 + D_LORA * 4 + W * 2) + hg * HEAD * HEAD * 4
    res = pl.pallas_call(
        functools.partial(_rwkv_kernel, has_vres),
        grid=(batch, ng, nt),
        in_specs=in_specs,
        out_specs=out_specs,
        out_shape=out_shape,
        scratch_shapes=[pltpu.VMEM((hg, HEAD, HEAD), F32),
                        pltpu.VMEM((SUBLANES, W), F32),
                        pltpu.VMEM((SUBLANES, D_LORA), F32)],
        compiler_params=pltpu.CompilerParams(
            dimension_semantics=("arbitrary", "arbitrary", "arbitrary"),
            vmem_limit_bytes=_vmem_limit(nbytes)),
        name="rwkv7",
    )(*args)
    if has_vres:
        return res, vfirst
    return res[0], res[1]


def _ffn_kernel(tiles_per_seq, h_ref, g_ref, wg_ref, wu_ref, cw_ref, cb_ref, wd_ref,
                o_ref, u_ref, tail_ref):
    i = pl.program_id(0)
    j = pl.program_id(1)

    @pl.when(j == 0)
    def _():
        h = h_ref[...]
        u_ref[...] = _rms(h, g_ref[...]).astype(BF16)
        o_ref[...] = h

    @pl.when(i % tiles_per_seq == 0)
    def _():
        tail_ref[j] = jnp.zeros(tail_ref.shape[1:], F32)

    u = u_ref[...]
    gate = jnp.dot(u, wg_ref[...], preferred_element_type=F32)
    tm = gate.shape[0]
    tail = tail_ref[j]
    cw = cw_ref[...]
    conv = gate * cw[FFN_CONV - 1:FFN_CONV] + cb_ref[...]
    for d in range(1, FFN_CONV):
        conv = conv + _shift_rows(gate, d, tail) * cw[FFN_CONV - 1 - d:FFN_CONV - d]
    tail_ref[j] = gate[tm - SUBLANES:]
    up = jnp.dot(u, wu_ref[...], preferred_element_type=F32)
    act = (_gelu(conv) * up).astype(BF16)
    o_ref[...] += jnp.dot(act, wd_ref[...], preferred_element_type=F32)


def _ffn(h, g, wg, wu, cw, cb, wd, layer, seq, tm=512, tf=1024):
    T = h.shape[0]
    nf = D_FF // tf
    nbytes = (2 * (2 * tm * D_MODEL * 4 + 3 * D_MODEL * tf * 2) + tm * D_MODEL * 2
              + nf * SUBLANES * tf * 4 + 6 * tm * tf * 4)
    return pl.pallas_call(
        functools.partial(_ffn_kernel, seq // tm),
        grid=(T // tm, nf),
        in_specs=[
            pl.BlockSpec((tm, D_MODEL), lambda i, j: (i, 0)),
            pl.BlockSpec((1, D_MODEL), lambda i, j: (0, 0)),
            pl.BlockSpec((None, D_MODEL, tf), lambda i, j: (layer, 0, j)),
            pl.BlockSpec((None, D_MODEL, tf), lambda i, j: (layer, 0, j)),
            pl.BlockSpec((FFN_CONV, tf), lambda i, j: (0, j)),
            pl.BlockSpec((1, tf), lambda i, j: (0, j)),
            pl.BlockSpec((None, tf, D_MODEL), lambda i, j: (layer, j, 0)),
        ],
        out_specs=pl.BlockSpec((tm, D_MODEL), lambda i, j: (i, 0)),
        out_shape=jax.ShapeDtypeStruct((T, D_MODEL), F32),
        scratch_shapes=[pltpu.VMEM((tm, D_MODEL), BF16),
                        pltpu.VMEM((nf, SUBLANES, tf), F32)],
        compiler_params=pltpu.CompilerParams(
            dimension_semantics=("arbitrary", "arbitrary"),
            vmem_limit_bytes=_vmem_limit(nbytes)),
        name="ffn",
    )(h, g, wg, wu, cw, cb, wd)


def _ple_kernel(final, h_ref, p_ref, g_ref, wg_ref, wp_ref, gp_ref, gf_ref, o_ref):
    h = h_ref[...]
    u = _rms(h, g_ref[...]).astype(BF16)
    gate = _sigmoid(jnp.dot(u, wg_ref[...], preferred_element_type=F32))
    proj = jnp.dot(p_ref[...].astype(BF16), wp_ref[...], preferred_element_type=F32)
    out = h + _rms(gate * proj, gp_ref[...])
    if final:
        out = _rms(out, gf_ref[...])
    o_ref[...] = out


def _ple(h, p, g, wg, layer, wp, gp, gf, final, tm=512):
    T = h.shape[0]
    vec = pl.BlockSpec((1, D_MODEL), lambda i: (0, 0))
    nbytes = 2 * (2 * tm * D_MODEL * 4 + tm * D_PLE * 4 + D_MODEL * D_MODEL * 2
                  + D_PLE * D_MODEL * 2) + 4 * tm * D_MODEL * 4
    return pl.pallas_call(
        functools.partial(_ple_kernel, final),
        grid=(T // tm,),
        in_specs=[
            pl.BlockSpec((tm, D_MODEL), lambda i: (i, 0)),
            pl.BlockSpec((tm, D_PLE), lambda i: (i, 0)),
            vec,
            pl.BlockSpec((None, D_MODEL, D_MODEL), lambda i: (layer, 0, 0)),
            pl.BlockSpec((D_PLE, D_MODEL), lambda i: (0, 0)),
            vec, vec,
        ],
        out_specs=pl.BlockSpec((tm, D_MODEL), lambda i: (i, 0)),
        out_shape=jax.ShapeDtypeStruct((T, D_MODEL), F32),
        compiler_params=pltpu.CompilerParams(
            dimension_semantics=("arbitrary",),
            vmem_limit_bytes=_vmem_limit(nbytes)),
        name="ple",
    )(h, p, g, wg, wp, gp, gf)


def _row(v):
    return v.reshape(1, -1).astype(F32)


def _pad_rows(w, top, total):
    return jnp.pad(w, ((top, total - top - w.shape[0]), (0, 0)))


def kernel(x, p, ln_mix, w_in, w_in_vres, mu_shift, mu_shift_vres, conv_a_w, conv_a_b, lru_wx, lru_bx, lru_wa, lru_ba, lru_lambda, lru_norm, rwkv_w0, rwkv_w2, rwkv_a0, rwkv_a2, rwkv_v0, rwkv_v2, rwkv_g2, rwkv_kk, rwkv_ka, rwkv_rk, rwkv_lnx_w, rwkv_lnx_b, w_o, ln_ffn, w_gate, w_up, conv_f_w, conv_f_b, w_down, ln_ple, w_ple_gate, w_ple_proj, ln_ple_post, ln_final):
    batch, seq, _ = x.shape
    depth = w_in.shape[0]
    T = batch * seq
    h = x.reshape(T, D_MODEL)
    n_lora = LORA_W + LORA_A + LORA_G
    vfirst = None
    w_in_b, w_o_b, w_gate_b, w_up_b, w_down_b, w_ple_gate_b = (
        w.astype(BF16) for w in (w_in[:, :, :D_MAIN], w_o, w_gate, w_up, w_down, w_ple_gate))
    for i in range(depth):
        lora_cols = [w_in[i][:, D_MAIN:]]
        mu_l = [mu_shift[i][3 * D_RWKV:]]
        if i > 0:
            lora_cols.append(w_in_vres[i - 1])
            mu_l.append(mu_shift_vres[i - 1])
        w_lora = jnp.concatenate(lora_cols, axis=1)
        w_lora = jnp.pad(w_lora, ((0, 0), (0, D_LORA - w_lora.shape[1]))).astype(BF16)
        mu_lora = jnp.concatenate(mu_l, axis=0)
        mu_lora = jnp.pad(mu_lora, (0, D_LORA - mu_lora.shape[0]))

        zm, zl = _inproj(h, _row(ln_mix[i]), w_in_b, i, w_lora)

        prm = {
            "mu_rkv": _row(mu_shift[i][:3 * D_RWKV]),
            "mu_lora": _row(mu_lora),
            "w0": _row(rwkv_w0[i]),
            "w2": _pad_rows(rwkv_w2[i], 0, LANES),
            "a0": _row(rwkv_a0[i]),
            "a2": _pad_rows(rwkv_a2[i], LORA_W, LANES),
            "g2": _pad_rows(rwkv_g2[i], 0, 2 * LANES),
            "kk": _row(rwkv_kk[i]), "ka": _row(rwkv_ka[i]), "rk": _row(rwkv_rk[i]),
            "lnw": _row(rwkv_lnx_w[i]), "lnb": _row(rwkv_lnx_b[i]),
        }
        if i > 0:
            prm["v0"] = _row(rwkv_v0[i - 1])
            prm["v2"] = _pad_rows(rwkv_v2[i - 1], n_lora - 2 * LANES, LANES)
        out_b, vfirst = _rwkv(zm, zl, vfirst, prm, batch, seq)

        h = _lru_oproj(zm, out_b, h, conv_a_w[i], _row(conv_a_b[i]), lru_wx[i].astype(BF16),
                       _row(lru_bx[i]), lru_wa[i].astype(BF16), _row(lru_ba[i]),
                       _row(lru_lambda[i]), _row(lru_norm[i]), w_o_b, i, batch, seq)
        h = _ffn(h, _row(ln_ffn[i]), w_gate_b, w_up_b, conv_f_w[i], _row(conv_f_b[i]),
                 w_down_b, i, seq)
        h = _ple(h, p[i].reshape(T, D_PLE), _row(ln_ple[i]), w_ple_gate_b, i,
                 w_ple_proj[i].astype(BF16), _row(ln_ple_post[i]), _row(ln_final),
                 final=(i == depth - 1))
    return h.reshape(batch, seq, D_MODEL)
```

```python
import functools
import math

import jax
import jax.numpy as jnp
from jax import lax
from jax.experimental import pallas as pl
from jax.experimental.pallas import tpu as pltpu

F32 = jnp.float32
BF16 = jnp.bfloat16

D_MODEL = 2048
D_LRU = 1024
D_RWKV = 1024
LRU_HEADS = 4
LRU_BLOCK = 256
LRU_CONV = 4
LRU_C = 8.0
HEAD = 64
LORA_W = 64
LORA_A = 64
LORA_G = 160
D_MAIN = 2 * D_LRU + 3 * D_RWKV
D_LORA = 384
D_FF = 3 * D_MODEL
FFN_CONV = 3
D_PLE = 256
RMS_EPS = 1e-6
LNX_EPS = 64e-5

V7X_VMEM_BYTES = 64 * 1024 * 1024
SUBLANES = 8
LANES = 128
ROW_TILE = 16

CHUNK = 64
PAIR = 2 * HEAD
LRU_ROWS = 256
HEAD_GROUP = 16
RWKV_ROWS = 1024

VMEM_TEMPORARIES_BYTES = 16 * 1024 * 1024
VMEM_UNSCOPED_BYTES = 4 * 1024 * 1024


def _vmem_limit(nbytes):
    return int(min(V7X_VMEM_BYTES - VMEM_UNSCOPED_BYTES, nbytes + VMEM_TEMPORARIES_BYTES))


def _rms(x, g):
    return x * lax.rsqrt(jnp.mean(x * x, axis=-1, keepdims=True) + RMS_EPS) * g


def _gelu(x):
    c = math.sqrt(2.0 / math.pi)
    return 0.5 * x * (1.0 + jnp.tanh(c * (x + 0.044715 * (x * x * x))))


def _sigmoid(x):
    return 1.0 / (1.0 + jnp.exp(-x))


def _softplus(x):
    return jnp.maximum(x, 0.0) + jnp.log1p(jnp.exp(-jnp.abs(x)))


def _shift_rows(x, d, prev8):
    rolled = pltpu.roll(x, d, axis=0)
    prev = pltpu.roll(prev8, d, axis=0)
    row = lax.broadcasted_iota(jnp.int32, prev8.shape, 0)
    top = jnp.where(row < d, prev, rolled[:SUBLANES])
    return jnp.concatenate([top, rolled[SUBLANES:]], axis=0)


def _inproj_kernel(x_ref, g_ref, wm_ref, wl_ref, zm_ref, zl_ref, u_ref):
    @pl.when(pl.program_id(1) == 0)
    def _():
        u_ref[...] = _rms(x_ref[...], g_ref[...]).astype(BF16)
        zl_ref[...] = jnp.dot(u_ref[...], wl_ref[...], preferred_element_type=F32)

    zm_ref[...] = jnp.dot(u_ref[...], wm_ref[...], preferred_element_type=F32)


def _inproj(h, g, w_in, layer, w_lora, tm=1024, tn=1024):
    T = h.shape[0]
    nbytes = 2 * (tm * D_MODEL * 4 + D_MODEL * tn * 2 + D_MODEL * D_LORA * 2
                  + tm * tn * 4 + tm * D_LORA * 4) + tm * D_MODEL * 2
    return pl.pallas_call(
        _inproj_kernel,
        grid=(T // tm, D_MAIN // tn),
        in_specs=[
            pl.BlockSpec((tm, D_MODEL), lambda i, j: (i, 0)),
            pl.BlockSpec((1, D_MODEL), lambda i, j: (0, 0)),
            pl.BlockSpec((None, D_MODEL, tn), lambda i, j: (layer, 0, j)),
            pl.BlockSpec((D_MODEL, D_LORA), lambda i, j: (0, 0)),
        ],
        out_specs=[
            pl.BlockSpec((tm, tn), lambda i, j: (i, j)),
            pl.BlockSpec((tm, D_LORA), lambda i, j: (i, 0)),
        ],
        out_shape=[
            jax.ShapeDtypeStruct((T, D_MAIN), F32),
            jax.ShapeDtypeStruct((T, D_LORA), F32),
        ],
        scratch_shapes=[pltpu.VMEM((tm, D_MODEL), BF16)],
        compiler_params=pltpu.CompilerParams(
            dimension_semantics=("arbitrary", "arbitrary"),
            vmem_limit_bytes=_vmem_limit(nbytes)),
        name="inproj",
    )(h, g, w_in, w_lora)


def _lru_rows(xb_ref, yb_ref, rows, tail, carry, seq_start, cw, cb, wx_ref, bx, wa_ref, ba,
              sp_lam, nrm, between):
    nrows = rows.stop - rows.start
    ngroup = nrows // SUBLANES
    sub = lax.broadcasted_iota(jnp.int32, (1, SUBLANES, 1), 1)
    ys, tails, carries, ss = [], [], [], 0.0
    for hd in range(LRU_HEADS):
        cols = slice(hd * LRU_BLOCK, (hd + 1) * LRU_BLOCK)
        x = xb_ref[rows, cols]
        xc = x * cw[LRU_CONV - 1:LRU_CONV, cols] + cb[:, cols]
        for d in range(1, LRU_CONV):
            xc = xc + _shift_rows(x, d, tail[:, cols]) * cw[LRU_CONV - 1 - d:LRU_CONV - d, cols]
        tails.append(x[nrows - SUBLANES:])

        xcb = xc.astype(BF16)
        gate_x = _sigmoid(jnp.dot(xcb, wx_ref[hd], preferred_element_type=F32) + bx[:, cols])
        gate_a = _sigmoid(jnp.dot(xcb, wa_ref[hd], preferred_element_type=F32) + ba[:, cols])
        log_a = (-LRU_C) * gate_a * sp_lam[:, cols]
        a = jnp.exp(log_a)
        mult = jnp.sqrt(1.0 - a * a)
        if seq_start is not None:
            row = lax.broadcasted_iota(jnp.int32, (nrows, 1), 0)
            mult = jnp.where(jnp.logical_and(row == 0, seq_start), 1.0, mult)
        b = xc * gate_x * mult

        a = a.reshape(ngroup, SUBLANES, LRU_BLOCK)
        b = b.reshape(ngroup, SUBLANES, LRU_BLOCK)
        d = 1
        while d < SUBLANES:
            keep = sub >= d
            a_sh = jnp.where(keep, pltpu.roll(a, d, axis=1), 1.0)
            b_sh = jnp.where(keep, pltpu.roll(b, d, axis=1), 0.0)
            b = a * b_sh + b
            a = a * a_sh
            d *= 2
        hcar = carry[:, cols]
        hs = []
        for grp in range(ngroup):
            hg = a[grp] * hcar + b[grp]
            hs.append(hg)
            hcar = hg[SUBLANES - 1:SUBLANES]
        carries.append(hcar)

        y = jnp.concatenate(hs, axis=0) * _gelu(yb_ref[rows, cols])
        ss = ss + jnp.sum(y * y, axis=-1, keepdims=True)
        ys.append(y)
        between(hd)
    scale = lax.rsqrt(ss * (1.0 / D_LRU) + RMS_EPS)
    out = jnp.concatenate(ys, axis=-1) * scale * nrm
    return out.astype(BF16), jnp.concatenate(tails, axis=-1), jnp.concatenate(carries, axis=-1)


def _lru_oproj_kernel(xb_ref, yb_ref, ob_ref, h_ref, cw_ref, cb_ref, wx_ref, bx_ref, wa_ref,
                      ba_ref, lam_ref, nrm_ref, woa_ref, wob_ref, o_ref, tail_ref, carry_ref):
    t = pl.program_id(1)

    @pl.when(t == 0)
    def _():
        tail_ref[...] = jnp.zeros_like(tail_ref)
        carry_ref[...] = jnp.zeros_like(carry_ref)

    sp_lam = _softplus(-lam_ref[...])
    tail, carry = tail_ref[...], carry_ref[0:1, :]
    ts = xb_ref.shape[0]
    ncol = D_MODEL // LRU_HEADS
    pieces = {}

    def project(name, lhs, w_ref):
        def step(hd):
            cols = slice(hd * ncol, (hd + 1) * ncol)
            pieces.setdefault(name, []).append(
                jnp.dot(lhs, w_ref[:, cols], preferred_element_type=F32))
        return step

    between = project("b", ob_ref[...], wob_ref)
    for r0 in range(0, ts, LRU_ROWS):
        out_a, tail, carry = _lru_rows(
            xb_ref, yb_ref, slice(r0, r0 + LRU_ROWS), tail, carry, (t == 0) if r0 == 0 else None,
            cw_ref[...], cb_ref[...], wx_ref, bx_ref[...], wa_ref, ba_ref[...], sp_lam,
            nrm_ref[...], between)
        between = project(("a", r0), out_a, woa_ref)
    for hd in range(LRU_HEADS):
        between(hd)
    tail_ref[...] = tail
    carry_ref[0:1, :] = carry
    acc_a = jnp.concatenate(
        [jnp.concatenate(pieces["a", r0], axis=-1) for r0 in range(0, ts, LRU_ROWS)], axis=0)
    o_ref[...] = h_ref[...] + jnp.concatenate(pieces["b"], axis=-1) + acc_a


def _lru_oproj(zm, out_b, h, cw, cb, wx, bx, wa, ba, lam, nrm, wo, layer, batch, seq, ts=512):
    T = zm.shape[0]
    nt = seq // ts
    vec = pl.BlockSpec((1, D_LRU), lambda b, t: (0, 0))
    mat = pl.BlockSpec((LRU_HEADS, LRU_BLOCK, LRU_BLOCK), lambda b, t: (0, 0, 0))

    def rows(width, col):
        return pl.BlockSpec((ts, width), lambda b, t: (b * nt + t, col))

    nbytes = (2 * (2 * ts * D_LRU * 4 + ts * D_RWKV * 2 + 2 * ts * D_MODEL * 4
                   + 2 * D_LRU * D_MODEL * 2) + 16 * LRU_ROWS * D_LRU * 4 + 2 * ts * D_MODEL * 4)
    return pl.pallas_call(
        _lru_oproj_kernel,
        grid=(batch, nt),
        in_specs=[
            rows(D_LRU, 0), rows(D_LRU, 1), rows(D_RWKV, 0), rows(D_MODEL, 0),
            pl.BlockSpec((LRU_CONV, D_LRU), lambda b, t: (0, 0)),
            vec, mat, vec, mat, vec, vec, vec,
            pl.BlockSpec((None, D_LRU, D_MODEL), lambda b, t: (layer, 0, 0)),
            pl.BlockSpec((None, D_RWKV, D_MODEL), lambda b, t: (layer, 1, 0)),
        ],
        out_specs=rows(D_MODEL, 0),
        out_shape=jax.ShapeDtypeStruct((T, D_MODEL), F32),
        scratch_shapes=[pltpu.VMEM((SUBLANES, D_LRU), F32),
                        pltpu.VMEM((SUBLANES, D_LRU), F32)],
        compiler_params=pltpu.CompilerParams(
            dimension_semantics=("arbitrary", "arbitrary"),
            vmem_limit_bytes=_vmem_limit(nbytes)),
        name="lru_oproj",
    )(zm, zm, out_b, h, cw, cb, wx, bx, wa, ba, lam, nrm, wo, wo)


def _mm(a, b):
    return jnp.dot(a.astype(BF16), b.astype(BF16), preferred_element_type=F32)


def _mm_nt(a, b):
    return lax.dot_general(a.astype(BF16), b.astype(BF16), (((1,), (1,)), ((), ())),
                           preferred_element_type=F32)


def _mm_tn(a, b):
    return lax.dot_general(a.astype(BF16), b.astype(BF16), (((0,), (0,)), ((), ())),
                           preferred_element_type=F32)


def _split3(x):
    hi = x.astype(BF16)
    r1 = x - hi.astype(F32)
    mid = r1.astype(BF16)
    lo = (r1 - mid.astype(F32)).astype(BF16)
    return hi, mid, lo


def _seg_sum(x):
    lane_lo = lax.broadcasted_iota(jnp.int32, (1, PAIR), 1) < HEAD
    out = []
    for p in range(x.shape[1] // PAIR):
        t = x[:, p * PAIR:(p + 1) * PAIR]
        s0 = jnp.sum(jnp.where(lane_lo, t, 0.0), axis=-1, keepdims=True)
        s1 = jnp.sum(jnp.where(lane_lo, 0.0, t), axis=-1, keepdims=True)
        out.append(jnp.where(lane_lo, s0, s1))
    return jnp.concatenate(out, axis=-1)


def _rwkv_kernel(has_vres, *refs):
    if has_vres:
        (r_ref, k_ref, v_ref, zl_ref, vf_ref, mur_ref, muk_ref, muv_ref, mul_ref,
         w0_ref, w2_ref, a0_ref, a2_ref, g2_ref, v0_ref, v2_ref,
         kkw_ref, ka_ref, rk_ref, lnw_ref, lnb_ref, o_ref, s_ref, prev_ref, prevz_ref) = refs
        vfo_ref = None
    else:
        (r_ref, k_ref, v_ref, zl_ref, mur_ref, muk_ref, muv_ref, mul_ref,
         w0_ref, w2_ref, a0_ref, a2_ref, g2_ref,
         kkw_ref, ka_ref, rk_ref, lnw_ref, lnb_ref, o_ref, vfo_ref,
         s_ref, prev_ref, prevz_ref) = refs
        vf_ref = v0_ref = v2_ref = None

    C = CHUNK
    nchunk = r_ref.shape[0] // C
    nh = r_ref.shape[1] // HEAD

    @pl.when(pl.program_id(2) == 0)
    def _():
        s_ref[...] = jnp.zeros_like(s_ref)
        prev_ref[...] = jnp.zeros_like(prev_ref)
        prevz_ref[...] = jnp.zeros_like(prevz_ref)

    ri3 = lax.broadcasted_iota(jnp.int32, (C, 3 * C), 0)
    ci3 = lax.broadcasted_iota(jnp.int32, (C, 3 * C), 1) % C
    tri3 = (ri3 >= ci3).astype(BF16)
    ri2 = lax.broadcasted_iota(jnp.int32, (2 * C, 2 * C), 0)
    ci2 = lax.broadcasted_iota(jnp.int32, (2 * C, 2 * C), 1) % C
    keep2 = jnp.where(ri2 < C, ri2, ri2 - C + 1) > ci2
    row0 = lax.broadcasted_iota(jnp.int32, (C, 1), 0) == 0
    zeros_h = jnp.zeros((C, HEAD), BF16)

    def shift_lerp(cur, prev_row, mu):
        sh = jnp.where(row0, prev_row, pltpu.roll(cur, 1, axis=0))
        return cur + (sh - cur) * mu

    def chunk_rows(c):
        return pl.ds(pl.multiple_of(c * C, C), C)

    def prep(c, out):
        rows = chunk_rows(c)
        first = c == 0
        before = pl.ds(jnp.maximum(c * C - 1, 0), 1)

        def lerp(ref, carried, mu):
            return shift_lerp(ref[rows, :], jnp.where(first, carried, ref[before, :]), mu)

        zl = lerp(zl_ref, prevz_ref[0:1, :], mul_ref[...])
        z01 = zl[:, 0:LANES]
        wpre = w0_ref[...] + _mm(jnp.tanh(z01), w2_ref[...])
        apre = a0_ref[...] + _mm(z01, a2_ref[...])
        g = _mm(_sigmoid(zl[:, LANES:3 * LANES]), g2_ref[...])
        if has_vres:
            mpre = v0_ref[...] + _mm(zl[:, 2 * LANES:3 * LANES], v2_ref[...])
        yield
        w_log = -_softplus(-wpre) - 0.5
        logw = -jnp.exp(w_log)
        cum = jnp.dot(tri3, jnp.concatenate(_split3(logw), axis=0), preferred_element_type=F32)
        yield
        r = lerp(r_ref, prev_ref[0:1, :], mur_ref[...])
        k = lerp(k_ref, prev_ref[1:2, :], muk_ref[...])
        a = _sigmoid(apre)
        yield
        v = lerp(v_ref, prev_ref[2:3, :], muv_ref[...])
        if has_vres:
            v = v + (vf_ref[rows, :] - v) * _sigmoid(mpre)
        else:
            vfo_ref[rows, :] = v
        yield
        kk = k * kkw_ref[...]
        kk = kk / jnp.maximum(jnp.sqrt(_seg_sum(kk * kk)), 1e-12)
        yield
        k2 = k * (1.0 + (a - 1.0) * ka_ref[...])
        bb = kk * a
        bonus = _seg_sum(r * k2 * rk_ref[...]) * v
        yield
        p_in = jnp.exp(cum)
        p_ex = jnp.exp(cum - logw)
        p_inv = jnp.exp(-cum)
        p_end = p_in[C - 1:C, :]
        yield
        rt = r * p_in
        at = -kk * p_ex
        rt_b, at_b, v_b = rt.astype(BF16), at.astype(BF16), v.astype(BF16)
        yield
        bt = bb * p_inv
        kt = k2 * p_inv
        bk_t = jnp.transpose(jnp.concatenate([bt, kt], axis=0)).astype(BF16)
        yield
        bhat_b = (bt * p_end).astype(BF16)
        khat_b = (kt * p_end).astype(BF16)
        out["local"] = (rt_b, at_b, bk_t, v_b, bhat_b, khat_b, at, rt, p_end)
        out["post"] = (bonus, g)

    def heads_stage(local, out):
        rt_b, at_b, bk_t, v_b, bhat_b, khat_b, at, rt, p_end = local
        heads = range(nh)
        sls = [slice(hh * HEAD, (hh + 1) * HEAD) for hh in heads]
        sc_b, m, vh_b, d = [], [], [], []
        for sl in sls:
            ar = jnp.concatenate([at_b[:, sl], rt_b[:, sl]], axis=0)
            sc = jnp.dot(ar, bk_t[sl, :], preferred_element_type=F32)
            sc = jnp.where(keep2, sc, 0.0)
            sc_b.append(sc.astype(BF16))
            m.append(sc[:C, :C])
            vh_b.append(v_b[:, sl])
        yield
        for hh in heads:
            zv = jnp.concatenate([zeros_h, vh_b[hh]], axis=0)
            x_loc = jnp.dot(sc_b[hh][:C], zv, preferred_element_type=F32)
            d.append(jnp.concatenate([at[:, sls[hh]], x_loc], axis=-1))
        yield
        nstep = int(math.log2(C))
        for i in range(nstep):
            lo = (2 ** i // ROW_TILE) * ROW_TILE
            for hh in heads:
                m_b = m[hh].astype(BF16)[lo:, :C - lo]
                d_b = d[hh].astype(BF16)[:C - lo]
                if i + 1 < nstep:
                    rhs = jnp.concatenate([d_b, m[hh].astype(BF16)[:C - lo]], axis=-1)
                    prod = jnp.dot(m_b, rhs, preferred_element_type=F32)
                    upd, m_new = prod[:, :2 * HEAD], prod[:, 2 * HEAD:]
                    if lo:
                        m_new = jnp.concatenate([jnp.zeros((lo, C), F32), m_new], axis=0)
                    m[hh] = m_new
                else:
                    upd = jnp.dot(m_b, d_b, preferred_element_type=F32)
                if lo:
                    upd = jnp.concatenate([jnp.zeros((lo, 2 * HEAD), F32), upd], axis=0)
                d[hh] = d[hh] + upd
            yield
        o1, wz = [], []
        for hh in heads:
            gmat = jnp.concatenate(
                [d[hh].astype(BF16), jnp.concatenate([zeros_h, vh_b[hh]], axis=-1)], axis=0)
            o1.append(jnp.dot(sc_b[hh][C:], gmat, preferred_element_type=F32))
            bkh = jnp.concatenate([bhat_b[:, sls[hh]], khat_b[:, sls[hh]]], axis=0)
            wz.append(_mm_tn(gmat, bkh))
        yield
        ys = []
        for hh in heads:
            rbar = rt[:, sls[hh]] + o1[hh][:, :HEAD]
            st = s_ref[hh]
            ys.append(_mm_nt(rbar, st) + o1[hh][:, HEAD:])
            s_ref[hh] = st * p_end[:, sls[hh]] + _mm(st, wz[hh][:HEAD]) + wz[hh][HEAD:]
        out["y"] = jnp.concatenate(ys, axis=-1)

    def tail(c, y, post):
        bonus, g = post
        mean = _seg_sum(y) * (1.0 / HEAD)
        yc = y - mean
        yield
        var = _seg_sum(yc * yc) * (1.0 / HEAD)
        yield
        yn = yc * lax.rsqrt(var + LNX_EPS) * lnw_ref[...] + lnb_ref[...]
        o_ref[chunk_rows(c), :] = ((yn + bonus) * g).astype(o_ref.dtype)

    def run_interleaved(*gens):
        alive = list(gens)
        while alive:
            for gen in list(alive):
                if next(gen, "done") == "done":
                    alive.remove(gen)

    def body(i, carry):
        local, y_prev, post_prev, post_cur = carry
        out = {}
        run_interleaved(heads_stage(local, out),
                        tail(jnp.maximum(i - 1, 0), y_prev, post_prev),
                        prep(jnp.minimum(i + 1, nchunk - 1), out))
        return out["local"], out["y"], post_cur, out["post"]

    first = {}
    run_interleaved(prep(0, first))
    zeros_w = jnp.zeros((C, r_ref.shape[1]), F32)
    _, y_last, post_last, _ = lax.fori_loop(
        0, nchunk, body, (first["local"], zeros_w, (zeros_w, zeros_w), first["post"]))
    run_interleaved(tail(nchunk - 1, y_last, post_last))

    last = pl.ds(r_ref.shape[0] - 1, 1)
    prev_ref[0:1, :] = r_ref[last, :]
    prev_ref[1:2, :] = k_ref[last, :]
    prev_ref[2:3, :] = v_ref[last, :]
    prevz_ref[0:1, :] = zl_ref[last, :]


def _rwkv(zm, zl, vfirst, prm, batch, seq, hg=HEAD_GROUP, ts=RWKV_ROWS):
    T = zm.shape[0]
    W = hg * HEAD
    ng = D_RWKV // W
    nt = seq // ts
    col0 = 2 * D_LRU // W
    has_vres = vfirst is not None

    def col(off):
        return pl.BlockSpec((ts, W), lambda b, g, t: (b * nt + t, off + g))

    vecg = pl.BlockSpec((1, W), lambda b, g, t: (0, g))
    vec_k = pl.BlockSpec((1, W), lambda b, g, t: (0, ng + g))
    vec_v = pl.BlockSpec((1, W), lambda b, g, t: (0, 2 * ng + g))
    vec_l = pl.BlockSpec((1, D_LORA), lambda b, g, t: (0, 0))

    def lora(rows):
        return pl.BlockSpec((rows, W), lambda b, g, t: (0, g))

    in_specs = [col(col0), col(col0 + ng), col(col0 + 2 * ng),
                pl.BlockSpec((ts, D_LORA), lambda b, g, t: (b * nt + t, 0))]
    args = [zm, zm, zm, zl]
    if has_vres:
        in_specs.append(col(0))
        args.append(vfirst)
    in_specs += [vecg, vec_k, vec_v, vec_l, vecg, lora(LANES), vecg, lora(LANES), lora(2 * LANES)]
    args += [prm["mu_rkv"], prm["mu_rkv"], prm["mu_rkv"], prm["mu_lora"],
             prm["w0"], prm["w2"], prm["a0"], prm["a2"], prm["g2"]]
    if has_vres:
        in_specs += [vecg, lora(LANES)]
        args += [prm["v0"], prm["v2"]]
    in_specs += [vecg] * 5
    args += [prm["kk"], prm["ka"], prm["rk"], prm["lnw"], prm["lnb"]]

    if has_vres:
        out_specs = col(0)
        out_shape = jax.ShapeDtypeStruct((T, D_RWKV), BF16)
    else:
        out_specs = [col(0), col(0)]
        out_shape = [jax.ShapeDtypeStruct((T, D_RWKV), BF16),
                     jax.ShapeDtypeStruct((T, D_RWKV), F32)]
    nbytes = 2 * ts * (5 * W * 4 + D_LORA * 4 + W * 2) + hg * HEAD * HEAD * 4
    res = pl.pallas_call(
        functools.partial(_rwkv_kernel, has_vres),
        grid=(batch, ng, nt),
        in_specs=in_specs,
        out_specs=out_specs,
        out_shape=out_shape,
        scratch_shapes=[pltpu.VMEM((hg, HEAD, HEAD), F32),
                        pltpu.VMEM((SUBLANES, W), F32),
                        pltpu.VMEM((SUBLANES, D_LORA), F32)],
        compiler_params=pltpu.CompilerParams(
            dimension_semantics=("arbitrary", "arbitrary", "arbitrary"),
            vmem_limit_bytes=_vmem_limit(nbytes)),
        name="rwkv7",
    )(*args)
    if has_vres:
        return res, vfirst
    return res[0], res[1]


def _ffn_kernel(tiles_per_seq, h_ref, g_ref, wg_ref, wu_ref, cw_ref, cb_ref, wd_ref,
                o_ref, u_ref, tail_ref):
    i = pl.program_id(0)
    j = pl.program_id(1)

    @pl.when(j == 0)
    def _():
        h = h_ref[...]
        u_ref[...] = _rms(h, g_ref[...]).astype(BF16)
        o_ref[...] = h

    @pl.when(i % tiles_per_seq == 0)
    def _():
        tail_ref[j] = jnp.zeros(tail_ref.shape[1:], F32)

    u = u_ref[...]
    gate = jnp.dot(u, wg_ref[...], preferred_element_type=F32)
    tm = gate.shape[0]
    tail = tail_ref[j]
    cw = cw_ref[...]
    conv = gate * cw[FFN_CONV - 1:FFN_CONV] + cb_ref[...]
    for d in range(1, FFN_CONV):
        conv = conv + _shift_rows(gate, d, tail) * cw[FFN_CONV - 1 - d:FFN_CONV - d]
    tail_ref[j] = gate[tm - SUBLANES:]
    up = jnp.dot(u, wu_ref[...], preferred_element_type=F32)
    act = (_gelu(conv) * up).astype(BF16)
    o_ref[...] += jnp.dot(act, wd_ref[...], preferred_element_type=F32)


def _ffn(h, g, wg, wu, cw, cb, wd, layer, seq, tm=512, tf=1024):
    T = h.shape[0]
    nf = D_FF // tf
    nbytes = (2 * (2 * tm * D_MODEL * 4 + 3 * D_MODEL * tf * 2) + tm * D_MODEL * 2
              + nf * SUBLANES * tf * 4 + 6 * tm * tf * 4)
    return pl.pallas_call(
        functools.partial(_ffn_kernel, seq // tm),
        grid=(T // tm, nf),
        in_specs=[
            pl.BlockSpec((tm, D_MODEL), lambda i, j: (i, 0)),
            pl.BlockSpec((1, D_MODEL), lambda i, j: (0, 0)),
            pl.BlockSpec((None, D_MODEL, tf), lambda i, j: (layer, 0, j)),
            pl.BlockSpec((None, D_MODEL, tf), lambda i, j: (layer, 0, j)),
            pl.BlockSpec((FFN_CONV, tf), lambda i, j: (0, j)),
            pl.BlockSpec((1, tf), lambda i, j: (0, j)),
            pl.BlockSpec((None, tf, D_MODEL), lambda i, j: (layer, j, 0)),
        ],
        out_specs=pl.BlockSpec((tm, D_MODEL), lambda i, j: (i, 0)),
        out_shape=jax.ShapeDtypeStruct((T, D_MODEL), F32),
        scratch_shapes=[pltpu.VMEM((tm, D_MODEL), BF16),
                        pltpu.VMEM((nf, SUBLANES, tf), F32)],
        compiler_params=pltpu.CompilerParams(
            dimension_semantics=("arbitrary", "arbitrary"),
            vmem_limit_bytes=_vmem_limit(nbytes)),
        name="ffn",
    )(h, g, wg, wu, cw, cb, wd)


def _ple_kernel(final, h_ref, p_ref, g_ref, wg_ref, wp_ref, gp_ref, gf_ref, o_ref):
    h = h_ref[...]
    u = _rms(h, g_ref[...]).astype(BF16)
    gate = _sigmoid(jnp.dot(u, wg_ref[...], preferred_element_type=F32))
    proj = jnp.dot(p_ref[...].astype(BF16), wp_ref[...], preferred_element_type=F32)
    out = h + _rms(gate * proj, gp_ref[...])
    if final:
        out = _rms(out, gf_ref[...])
    o_ref[...] = out


def _ple(h, p, g, wg, layer, wp, gp, gf, final, tm=512):
    T = h.shape[0]
    vec = pl.BlockSpec((1, D_MODEL), lambda i: (0, 0))
    nbytes = 2 * (2 * tm * D_MODEL * 4 + tm * D_PLE * 4 + D_MODEL * D_MODEL * 2
                  + D_PLE * D_MODEL * 2) + 4 * tm * D_MODEL * 4
    return pl.pallas_call(
        functools.partial(_ple_kernel, final),
        grid=(T // tm,),
        in_specs=[
            pl.BlockSpec((tm, D_MODEL), lambda i: (i, 0)),
            pl.BlockSpec((tm, D_PLE), lambda i: (i, 0)),
            vec,
            pl.BlockSpec((None, D_MODEL, D_MODEL), lambda i: (layer, 0, 0)),
            pl.BlockSpec((D_PLE, D_MODEL), lambda i: (0, 0)),
            vec, vec,
        ],
        out_specs=pl.BlockSpec((tm, D_MODEL), lambda i: (i, 0)),
        out_shape=jax.ShapeDtypeStruct((T, D_MODEL), F32),
        compiler_params=pltpu.CompilerParams(
            dimension_semantics=("arbitrary",),
            vmem_limit_bytes=_vmem_limit(nbytes)),
        name="ple",
    )(h, p, g, wg, wp, gp, gf)


def _row(v):
    return v.reshape(1, -1).astype(F32)


def _pad_rows(w, top, total):
    return jnp.pad(w, ((top, total - top - w.shape[0]), (0, 0)))


def kernel(x, p, ln_mix, w_in, w_in_vres, mu_shift, mu_shift_vres, conv_a_w, conv_a_b, lru_wx, lru_bx, lru_wa, lru_ba, lru_lambda, lru_norm, rwkv_w0, rwkv_w2, rwkv_a0, rwkv_a2, rwkv_v0, rwkv_v2, rwkv_g2, rwkv_kk, rwkv_ka, rwkv_rk, rwkv_lnx_w, rwkv_lnx_b, w_o, ln_ffn, w_gate, w_up, conv_f_w, conv_f_b, w_down, ln_ple, w_ple_gate, w_ple_proj, ln_ple_post, ln_final):
    batch, seq, _ = x.shape
    depth = w_in.shape[0]
    T = batch * seq
    h = x.reshape(T, D_MODEL)
    n_lora = LORA_W + LORA_A + LORA_G
    vfirst = None
    w_in_b, w_o_b, w_gate_b, w_up_b, w_down_b, w_ple_gate_b = (
        w.astype(BF16) for w in (w_in[:, :, :D_MAIN], w_o, w_gate, w_up, w_down, w_ple_gate))
    for i in range(depth):
        lora_cols = [w_in[i][:, D_MAIN:]]
        mu_l = [mu_shift[i][3 * D_RWKV:]]
        if i > 0:
            lora_cols.append(w_in_vres[i - 1])
            mu_l.append(mu_shift_vres[i - 1])
        w_lora = jnp.concatenate(lora_cols, axis=1)
        w_lora = jnp.pad(w_lora, ((0, 0), (0, D_LORA - w_lora.shape[1]))).astype(BF16)
        mu_lora = jnp.concatenate(mu_l, axis=0)
        mu_lora = jnp.pad(mu_lora, (0, D_LORA - mu_lora.shape[0]))

        zm, zl = _inproj(h, _row(ln_mix[i]), w_in_b, i, w_lora)

        prm = {
            "mu_rkv": _row(mu_shift[i][:3 * D_RWKV]),
            "mu_lora": _row(mu_lora),
            "w0": _row(rwkv_w0[i]),
            "w2": _pad_rows(rwkv_w2[i], 0, LANES),
            "a0": _row(rwkv_a0[i]),
            "a2": _pad_rows(rwkv_a2[i], LORA_W, LANES),
            "g2": _pad_rows(rwkv_g2[i], 0, 2 * LANES),
            "kk": _row(rwkv_kk[i]), "ka": _row(rwkv_ka[i]), "rk": _row(rwkv_rk[i]),
            "lnw": _row(rwkv_lnx_w[i]), "lnb": _row(rwkv_lnx_b[i]),
        }
        if i > 0:
            prm["v0"] = _row(rwkv_v0[i - 1])
            prm["v2"] = _pad_rows(rwkv_v2[i - 1], n_lora - 2 * LANES, LANES)
        out_b, vfirst = _rwkv(zm, zl, vfirst, prm, batch, seq)

        h = _lru_oproj(zm, out_b, h, conv_a_w[i], _row(conv_a_b[i]), lru_wx[i].astype(BF16),
                       _row(lru_bx[i]), lru_wa[i].astype(BF16), _row(lru_ba[i]),
                       _row(lru_lambda[i]), _row(lru_norm[i]), w_o_b, i, batch, seq)
        h = _ffn(h, _row(ln_ffn[i]), w_gate_b, w_up_b, conv_f_w[i], _row(conv_f_b[i]),
                 w_down_b, i, seq)
        h = _ple(h, p[i].reshape(T, D_PLE), _row(ln_ple[i]), w_ple_gate_b, i,
                 w_ple_proj[i].astype(BF16), _row(ln_ple_post[i]), _row(ln_final),
                 final=(i == depth - 1))
    return h.reshape(batch, seq, D_MODEL)
```

```python
import functools
import math

import jax
import jax.numpy as jnp
from jax import lax
from jax.experimental import pallas as pl
from jax.experimental.pallas import tpu as pltpu

F32 = jnp.float32
BF16 = jnp.bfloat16

D_MODEL = 2048
D_LRU = 1024
D_RWKV = 1024
LRU_HEADS = 4
LRU_BLOCK = 256
LRU_CONV = 4
LRU_C = 8.0
HEAD = 64
LORA_W = 64
LORA_A = 64
LORA_G = 160
D_MAIN = 2 * D_LRU + 3 * D_RWKV
D_LORA = 384
D_FF = 3 * D_MODEL
FFN_CONV = 3
D_PLE = 256
RMS_EPS = 1e-6
LNX_EPS = 64e-5

V7X_VMEM_BYTES = 64 * 1024 * 1024
SUBLANES = 8
LANES = 128
ROW_TILE = 16

CHUNK = 64
PAIR = 2 * HEAD
LRU_ROWS = 256
HEAD_GROUP = 16
RWKV_ROWS = 1024

VMEM_TEMPORARIES_BYTES = 16 * 1024 * 1024
VMEM_UNSCOPED_BYTES = 4 * 1024 * 1024


def _vmem_limit(nbytes):
    return int(min(V7X_VMEM_BYTES - VMEM_UNSCOPED_BYTES, nbytes + VMEM_TEMPORARIES_BYTES))


def _rms(x, g):
    return x * lax.rsqrt(jnp.mean(x * x, axis=-1, keepdims=True) + RMS_EPS) * g


def _gelu(x):
    c = math.sqrt(2.0 / math.pi)
    return 0.5 * x * (1.0 + jnp.tanh(c * (x + 0.044715 * (x * x * x))))


def _sigmoid(x):
    return 1.0 / (1.0 + jnp.exp(-x))


def _softplus(x):
    return jnp.maximum(x, 0.0) + jnp.log1p(jnp.exp(-jnp.abs(x)))


def _shift_rows(x, d, prev8):
    rolled = pltpu.roll(x, d, axis=0)
    prev = pltpu.roll(prev8, d, axis=0)
    row = lax.broadcasted_iota(jnp.int32, prev8.shape, 0)
    top = jnp.where(row < d, prev, rolled[:SUBLANES])
    return jnp.concatenate([top, rolled[SUBLANES:]], axis=0)


def _inproj_kernel(x_ref, g_ref, wm_ref, wl_ref, zm_ref, zl_ref, u_ref):
    @pl.when(pl.program_id(1) == 0)
    def _():
        u_ref[...] = _rms(x_ref[...], g_ref[...]).astype(BF16)
        zl_ref[...] = jnp.dot(u_ref[...], wl_ref[...], preferred_element_type=F32)

    zm_ref[...] = jnp.dot(u_ref[...], wm_ref[...], preferred_element_type=F32)


def _inproj(h, g, w_in, layer, w_lora, tm=1024, tn=1024):
    T = h.shape[0]
    nbytes = 2 * (tm * D_MODEL * 4 + D_MODEL * tn * 2 + D_MODEL * D_LORA * 2
                  + tm * tn * 4 + tm * D_LORA * 4) + tm * D_MODEL * 2
    return pl.pallas_call(
        _inproj_kernel,
        grid=(T // tm, D_MAIN // tn),
        in_specs=[
            pl.BlockSpec((tm, D_MODEL), lambda i, j: (i, 0)),
            pl.BlockSpec((1, D_MODEL), lambda i, j: (0, 0)),
            pl.BlockSpec((None, D_MODEL, tn), lambda i, j: (layer, 0, j)),
            pl.BlockSpec((D_MODEL, D_LORA), lambda i, j: (0, 0)),
        ],
        out_specs=[
            pl.BlockSpec((tm, tn), lambda i, j: (i, j)),
            pl.BlockSpec((tm, D_LORA), lambda i, j: (i, 0)),
        ],
        out_shape=[
            jax.ShapeDtypeStruct((T, D_MAIN), F32),
            jax.ShapeDtypeStruct((T, D_LORA), F32),
        ],
        scratch_shapes=[pltpu.VMEM((tm, D_MODEL), BF16)],
        compiler_params=pltpu.CompilerParams(
            dimension_semantics=("arbitrary", "arbitrary"),
            vmem_limit_bytes=_vmem_limit(nbytes)),
        name="inproj",
    )(h, g, w_in, w_lora)


def _lru_rows(xb_ref, yb_ref, rows, tail, carry, seq_start, cw, cb, wx_ref, bx, wa_ref, ba,
              sp_lam, nrm, between):
    nrows = rows.stop - rows.start
    ngroup = nrows // SUBLANES
    sub = lax.broadcasted_iota(jnp.int32, (1, SUBLANES, 1), 1)
    ys, tails, carries, ss = [], [], [], 0.0
    for hd in range(LRU_HEADS):
        cols = slice(hd * LRU_BLOCK, (hd + 1) * LRU_BLOCK)
        x = xb_ref[rows, cols]
        xc = x * cw[LRU_CONV - 1:LRU_CONV, cols] + cb[:, cols]
        for d in range(1, LRU_CONV):
            xc = xc + _shift_rows(x, d, tail[:, cols]) * cw[LRU_CONV - 1 - d:LRU_CONV - d, cols]
        tails.append(x[nrows - SUBLANES:])

        xcb = xc.astype(BF16)
        gate_x = _sigmoid(jnp.dot(xcb, wx_ref[hd], preferred_element_type=F32) + bx[:, cols])
        gate_a = _sigmoid(jnp.dot(xcb, wa_ref[hd], preferred_element_type=F32) + ba[:, cols])
        log_a = (-LRU_C) * gate_a * sp_lam[:, cols]
        a = jnp.exp(log_a)
        mult = jnp.sqrt(1.0 - a * a)
        if seq_start is not None:
            row = lax.broadcasted_iota(jnp.int32, (nrows, 1), 0)
            mult = jnp.where(jnp.logical_and(row == 0, seq_start), 1.0, mult)
        b = xc * gate_x * mult

        a = a.reshape(ngroup, SUBLANES, LRU_BLOCK)
        b = b.reshape(ngroup, SUBLANES, LRU_BLOCK)
        d = 1
        while d < SUBLANES:
            keep = sub >= d
            a_sh = jnp.where(keep, pltpu.roll(a, d, axis=1), 1.0)
            b_sh = jnp.where(keep, pltpu.roll(b, d, axis=1), 0.0)
            b = a * b_sh + b
            a = a * a_sh
            d *= 2
        hcar = carry[:, cols]
        hs = []
        for grp in range(ngroup):
            hg = a[grp] * hcar + b[grp]
            hs.append(hg)
            hcar = hg[SUBLANES - 1:SUBLANES]
        carries.append(hcar)

        y = jnp.concatenate(hs, axis=0) * _gelu(yb_ref[rows, cols])
        ss = ss + jnp.sum(y * y, axis=-1, keepdims=True)
        ys.append(y)
        between(hd)
    scale = lax.rsqrt(ss * (1.0 / D_LRU) + RMS_EPS)
    out = jnp.concatenate(ys, axis=-1) * scale * nrm
    return out.astype(BF16), jnp.concatenate(tails, axis=-1), jnp.concatenate(carries, axis=-1)


def _lru_oproj_kernel(xb_ref, yb_ref, ob_ref, h_ref, cw_ref, cb_ref, wx_ref, bx_ref, wa_ref,
                      ba_ref, lam_ref, nrm_ref, woa_ref, wob_ref, o_ref, tail_ref, carry_ref):
    t = pl.program_id(1)

    @pl.when(t == 0)
    def _():
        tail_ref[...] = jnp.zeros_like(tail_ref)
        carry_ref[...] = jnp.zeros_like(carry_ref)

    sp_lam = _softplus(-lam_ref[...])
    tail, carry = tail_ref[...], carry_ref[0:1, :]
    ts = xb_ref.shape[0]
    ncol = D_MODEL // LRU_HEADS
    pieces = {}

    def project(name, lhs, w_ref):
        def step(hd):
            cols = slice(hd * ncol, (hd + 1) * ncol)
            pieces.setdefault(name, []).append(
                jnp.dot(lhs, w_ref[:, cols], preferred_element_type=F32))
        return step

    between = project("b", ob_ref[...], wob_ref)
    for r0 in range(0, ts, LRU_ROWS):
        out_a, tail, carry = _lru_rows(
            xb_ref, yb_ref, slice(r0, r0 + LRU_ROWS), tail, carry, (t == 0) if r0 == 0 else None,
            cw_ref[...], cb_ref[...], wx_ref, bx_ref[...], wa_ref, ba_ref[...], sp_lam,
            nrm_ref[...], between)
        between = project(("a", r0), out_a, woa_ref)
    for hd in range(LRU_HEADS):
        between(hd)
    tail_ref[...] = tail
    carry_ref[0:1, :] = carry
    acc_a = jnp.concatenate(
        [jnp.concatenate(pieces["a", r0], axis=-1) for r0 in range(0, ts, LRU_ROWS)], axis=0)
    o_ref[...] = h_ref[...] + jnp.concatenate(pieces["b"], axis=-1) + acc_a


def _lru_oproj(zm, out_b, h, cw, cb, wx, bx, wa, ba, lam, nrm, wo, layer, batch, seq, ts=512):
    T = zm.shape[0]
    nt = seq // ts
    vec = pl.BlockSpec((1, D_LRU), lambda b, t: (0, 0))
    mat = pl.BlockSpec((LRU_HEADS, LRU_BLOCK, LRU_BLOCK), lambda b, t: (0, 0, 0))

    def rows(width, col):
        return pl.BlockSpec((ts, width), lambda b, t: (b * nt + t, col))

    nbytes = (2 * (2 * ts * D_LRU * 4 + ts * D_RWKV * 2 + 2 * ts * D_MODEL * 4
                   + 2 * D_LRU * D_MODEL * 2) + 16 * LRU_ROWS * D_LRU * 4 + 2 * ts * D_MODEL * 4)
    return pl.pallas_call(
        _lru_oproj_kernel,
        grid=(batch, nt),
        in_specs=[
            rows(D_LRU, 0), rows(D_LRU, 1), rows(D_RWKV, 0), rows(D_MODEL, 0),
            pl.BlockSpec((LRU_CONV, D_LRU), lambda b, t: (0, 0)),
            vec, mat, vec, mat, vec, vec, vec,
            pl.BlockSpec((None, D_LRU, D_MODEL), lambda b, t: (layer, 0, 0)),
            pl.BlockSpec((None, D_RWKV, D_MODEL), lambda b, t: (layer, 1, 0)),
        ],
        out_specs=rows(D_MODEL, 0),
        out_shape=jax.ShapeDtypeStruct((T, D_MODEL), F32),
        scratch_shapes=[pltpu.VMEM((SUBLANES, D_LRU), F32),
                        pltpu.VMEM((SUBLANES, D_LRU), F32)],
        compiler_params=pltpu.CompilerParams(
            dimension_semantics=("arbitrary", "arbitrary"),
            vmem_limit_bytes=_vmem_limit(nbytes)),
        name="lru_oproj",
    )(zm, zm, out_b, h, cw, cb, wx, bx, wa, ba, lam, nrm, wo, wo)


def _mm(a, b):
    return jnp.dot(a.astype(BF16), b.astype(BF16), preferred_element_type=F32)


def _mm_nt(a, b):
    return lax.dot_general(a.astype(BF16), b.astype(BF16), (((1,), (1,)), ((), ())),
                           preferred_element_type=F32)


def _mm_tn(a, b):
    return lax.dot_general(a.astype(BF16), b.astype(BF16), (((0,), (0,)), ((), ())),
                           preferred_element_type=F32)


def _split3(x):
    hi = x.astype(BF16)
    r1 = x - hi.astype(F32)
    mid = r1.astype(BF16)
    lo = (r1 - mid.astype(F32)).astype(BF16)
    return hi, mid, lo


def _seg_sum(x):
    lane_lo = lax.broadcasted_iota(jnp.int32, (1, PAIR), 1) < HEAD
    out = []
    for p in range(x.shape[1] // PAIR):
        t = x[:, p * PAIR:(p + 1) * PAIR]
        s0 = jnp.sum(jnp.where(lane_lo, t, 0.0), axis=-1, keepdims=True)
        s1 = jnp.sum(jnp.where(lane_lo, 0.0, t), axis=-1, keepdims=True)
        out.append(jnp.where(lane_lo, s0, s1))
    return jnp.concatenate(out, axis=-1)


def _rwkv_kernel(has_vres, *refs):
    if has_vres:
        (r_ref, k_ref, v_ref, zl_ref, vf_ref, mur_ref, muk_ref, muv_ref, mul_ref,
         w0_ref, w2_ref, a0_ref, a2_ref, g2_ref, v0_ref, v2_ref,
         kkw_ref, ka_ref, rk_ref, lnw_ref, lnb_ref, o_ref, s_ref, prev_ref, prevz_ref) = refs
        vfo_ref = None
    else:
        (r_ref, k_ref, v_ref, zl_ref, mur_ref, muk_ref, muv_ref, mul_ref,
         w0_ref, w2_ref, a0_ref, a2_ref, g2_ref,
         kkw_ref, ka_ref, rk_ref, lnw_ref, lnb_ref, o_ref, vfo_ref,
         s_ref, prev_ref, prevz_ref) = refs
        vf_ref = v0_ref = v2_ref = None

    C = CHUNK
    nchunk = r_ref.shape[0] // C
    nh = r_ref.shape[1] // HEAD

    @pl.when(pl.program_id(2) == 0)
    def _():
        s_ref[...] = jnp.zeros_like(s_ref)
        prev_ref[...] = jnp.zeros_like(prev_ref)
        prevz_ref[...] = jnp.zeros_like(prevz_ref)

    ri3 = lax.broadcasted_iota(jnp.int32, (C, 3 * C), 0)
    ci3 = lax.broadcasted_iota(jnp.int32, (C, 3 * C), 1) % C
    tri3 = (ri3 >= ci3).astype(BF16)
    ri2 = lax.broadcasted_iota(jnp.int32, (2 * C, 2 * C), 0)
    ci2 = lax.broadcasted_iota(jnp.int32, (2 * C, 2 * C), 1) % C
    keep2 = jnp.where(ri2 < C, ri2, ri2 - C + 1) > ci2
    row0 = lax.broadcasted_iota(jnp.int32, (C, 1), 0) == 0
    zeros_h = jnp.zeros((C, HEAD), BF16)

    def shift_lerp(cur, prev_row, mu):
        sh = jnp.where(row0, prev_row, pltpu.roll(cur, 1, axis=0))
        return cur + (sh - cur) * mu

    def chunk_rows(c):
        return pl.ds(pl.multiple_of(c * C, C), C)

    def prep(c, out):
        rows = chunk_rows(c)
        first = c == 0
        before = pl.ds(jnp.maximum(c * C - 1, 0), 1)

        def lerp(ref, carried, mu):
            return shift_lerp(ref[rows, :], jnp.where(first, carried, ref[before, :]), mu)

        zl = lerp(zl_ref, prevz_ref[0:1, :], mul_ref[...])
        z01 = zl[:, 0:LANES]
        wpre = w0_ref[...] + _mm(jnp.tanh(z01), w2_ref[...])
        apre = a0_ref[...] + _mm(z01, a2_ref[...])
        g = _mm(_sigmoid(zl[:, LANES:3 * LANES]), g2_ref[...])
        if has_vres:
            mpre = v0_ref[...] + _mm(zl[:, 2 * LANES:3 * LANES], v2_ref[...])
        yield
        w_log = -_softplus(-wpre) - 0.5
        logw = -jnp.exp(w_log)
        cum = jnp.dot(tri3, jnp.concatenate(_split3(logw), axis=0), preferred_element_type=F32)
        yield
        r = lerp(r_ref, prev_ref[0:1, :], mur_ref[...])
        k = lerp(k_ref, prev_ref[1:2, :], muk_ref[...])
        a = _sigmoid(apre)
        yield
        v = lerp(v_ref, prev_ref[2:3, :], muv_ref[...])
        if has_vres:
            v = v + (vf_ref[rows, :] - v) * _sigmoid(mpre)
        else:
            vfo_ref[rows, :] = v
        yield
        kk = k * kkw_ref[...]
        kk = kk / jnp.maximum(jnp.sqrt(_seg_sum(kk * kk)), 1e-12)
        yield
        k2 = k * (1.0 + (a - 1.0) * ka_ref[...])
        bb = kk * a
        bonus = _seg_sum(r * k2 * rk_ref[...]) * v
        yield
        p_in = jnp.exp(cum)
        p_ex = jnp.exp(cum - logw)
        p_inv = jnp.exp(-cum)
        p_end = p_in[C - 1:C, :]
        yield
        rt = r * p_in
        at = -kk * p_ex
        rt_b, at_b, v_b = rt.astype(BF16), at.astype(BF16), v.astype(BF16)
        yield
        bt = bb * p_inv
        kt = k2 * p_inv
        bk_t = jnp.transpose(jnp.concatenate([bt, kt], axis=0)).astype(BF16)
        yield
        bkh_t = jnp.transpose(jnp.concatenate([bt * p_end, kt * p_end], axis=0)).astype(BF16)
        out["local"] = (rt_b, at_b, bk_t, v_b, bkh_t, at, rt, p_end)
        out["post"] = (bonus, g)

    def heads_stage(local, out):
        rt_b, at_b, bk_t, v_b, bkh_t, at, rt, p_end = local
        heads = range(nh)
        sls = [slice(hh * HEAD, (hh + 1) * HEAD) for hh in heads]
        sc_b, m, vh_b, d = [], [], [], []
        for sl in sls:
            ar = jnp.concatenate([at_b[:, sl], rt_b[:, sl]], axis=0)
            sc = jnp.dot(ar, bk_t[sl, :], preferred_element_type=F32)
            sc = jnp.where(keep2, sc, 0.0)
            sc_b.append(sc.astype(BF16))
            m.append(sc[:C, :C])
            vh_b.append(v_b[:, sl])
        yield
        for hh in heads:
            zv = jnp.concatenate([zeros_h, vh_b[hh]], axis=0)
            x_loc = jnp.dot(sc_b[hh][:C], zv, preferred_element_type=F32)
            d.append(jnp.concatenate([at[:, sls[hh]], x_loc], axis=-1))
        yield
        nstep = int(math.log2(C))
        for i in range(nstep):
            lo = (2 ** i // ROW_TILE) * ROW_TILE
            for hh in heads:
                m_b = m[hh].astype(BF16)[lo:, :C - lo]
                d_b = d[hh].astype(BF16)[:C - lo]
                if i + 1 < nstep:
                    rhs = jnp.concatenate([d_b, m[hh].astype(BF16)[:C - lo]], axis=-1)
                    prod = jnp.dot(m_b, rhs, preferred_element_type=F32)
                    upd, m_new = prod[:, :2 * HEAD], prod[:, 2 * HEAD:]
                    if lo:
                        m_new = jnp.concatenate([jnp.zeros((lo, C), F32), m_new], axis=0)
                    m[hh] = m_new
                else:
                    upd = jnp.dot(m_b, d_b, preferred_element_type=F32)
                if lo:
                    upd = jnp.concatenate([jnp.zeros((lo, 2 * HEAD), F32), upd], axis=0)
                d[hh] = d[hh] + upd
            yield
        mixed = []
        for hh in heads:
            gmat = jnp.concatenate(
                [d[hh].astype(BF16), jnp.concatenate([zeros_h, vh_b[hh]], axis=-1)], axis=0)
            lhs = jnp.concatenate([sc_b[hh][C:], bkh_t[sls[hh], :]], axis=0)
            mixed.append(jnp.dot(lhs, gmat, preferred_element_type=F32))
        yield
        p_cols = [jnp.transpose(jnp.broadcast_to(p_end[:, q * PAIR:(q + 1) * PAIR], (PAIR, PAIR)))
                  for q in range(nh // 2)]
        ys = []
        for hh in heads:
            rbar = rt[:, sls[hh]] + mixed[hh][:C, :HEAD]
            st = s_ref[hh]
            both = _mm(jnp.concatenate([rbar, mixed[hh][C:, :HEAD]], axis=0), st)
            ys.append(both[:C] + mixed[hh][:C, HEAD:])
            p_col = p_cols[hh // 2][(hh % 2) * HEAD:(hh % 2 + 1) * HEAD, :HEAD]
            s_ref[hh] = st * p_col + both[C:] + mixed[hh][C:, HEAD:]
        out["y"] = jnp.concatenate(ys, axis=-1)

    def tail(c, y, post):
        bonus, g = post
        mean = _seg_sum(y) * (1.0 / HEAD)
        yc = y - mean
        yield
        var = _seg_sum(yc * yc) * (1.0 / HEAD)
        yield
        yn = yc * lax.rsqrt(var + LNX_EPS) * lnw_ref[...] + lnb_ref[...]
        o_ref[chunk_rows(c), :] = ((yn + bonus) * g).astype(o_ref.dtype)

    def run_interleaved(*gens):
        alive = list(gens)
        while alive:
            for gen in list(alive):
                if next(gen, "done") == "done":
                    alive.remove(gen)

    def body(i, carry):
        local, y_prev, post_prev, post_cur = carry
        out = {}
        run_interleaved(heads_stage(local, out),
                        tail(jnp.maximum(i - 1, 0), y_prev, post_prev),
                        prep(jnp.minimum(i + 1, nchunk - 1), out))
        return out["local"], out["y"], post_cur, out["post"]

    first = {}
    run_interleaved(prep(0, first))
    zeros_w = jnp.zeros((C, r_ref.shape[1]), F32)
    _, y_last, post_last, _ = lax.fori_loop(
        0, nchunk, body, (first["local"], zeros_w, (zeros_w, zeros_w), first["post"]))
    run_interleaved(tail(nchunk - 1, y_last, post_last))

    last = pl.ds(r_ref.shape[0] - 1, 1)
    prev_ref[0:1, :] = r_ref[last, :]
    prev_ref[1:2, :] = k_ref[last, :]
    prev_ref[2:3, :] = v_ref[last, :]
    prevz_ref[0:1, :] = zl_ref[last, :]


def _rwkv(zm, zl, vfirst, prm, batch, seq, hg=HEAD_GROUP, ts=RWKV_ROWS):
    T = zm.shape[0]
    W = hg * HEAD
    ng = D_RWKV // W
    nt = seq // ts
    col0 = 2 * D_LRU // W
    has_vres = vfirst is not None

    def col(off):
        return pl.BlockSpec((ts, W), lambda b, g, t: (b * nt + t, off + g))

    vecg = pl.BlockSpec((1, W), lambda b, g, t: (0, g))
    vec_k = pl.BlockSpec((1, W), lambda b, g, t: (0, ng + g))
    vec_v = pl.BlockSpec((1, W), lambda b, g, t: (0, 2 * ng + g))
    vec_l = pl.BlockSpec((1, D_LORA), lambda b, g, t: (0, 0))

    def lora(rows):
        return pl.BlockSpec((rows, W), lambda b, g, t: (0, g))

    in_specs = [col(col0), col(col0 + ng), col(col0 + 2 * ng),
                pl.BlockSpec((ts, D_LORA), lambda b, g, t: (b * nt + t, 0))]
    args = [zm, zm, zm, zl]
    if has_vres:
        in_specs.append(col(0))
        args.append(vfirst)
    in_specs += [vecg, vec_k, vec_v, vec_l, vecg, lora(LANES), vecg, lora(LANES), lora(2 * LANES)]
    args += [prm["mu_rkv"], prm["mu_rkv"], prm["mu_rkv"], prm["mu_lora"],
             prm["w0"], prm["w2"], prm["a0"], prm["a2"], prm["g2"]]
    if has_vres:
        in_specs += [vecg, lora(LANES)]
        args += [prm["v0"], prm["v2"]]
    in_specs += [vecg] * 5
    args += [prm["kk"], prm["ka"], prm["rk"], prm["lnw"], prm["lnb"]]

    if has_vres:
        out_specs = col(0)
        out_shape = jax.ShapeDtypeStruct((T, D_RWKV), BF16)
    else:
        out_specs = [col(0), col(0)]
        out_shape = [jax.ShapeDtypeStruct((T, D_RWKV), BF16),
                     jax.ShapeDtypeStruct((T, D_RWKV), F32)]
    nbytes = 2 * ts * (5 * W * 4 + D_LORA * 4 + W * 2) + hg * HEAD * HEAD * 4
    res = pl.pallas_call(
        functools.partial(_rwkv_kernel, has_vres),
        grid=(batch, ng, nt),
        in_specs=in_specs,
        out_specs=out_specs,
        out_shape=out_shape,
        scratch_shapes=[pltpu.VMEM((hg, HEAD, HEAD), F32),
                        pltpu.VMEM((SUBLANES, W), F32),
                        pltpu.VMEM((SUBLANES, D_LORA), F32)],
        compiler_params=pltpu.CompilerParams(
            dimension_semantics=("arbitrary", "arbitrary", "arbitrary"),
            vmem_limit_bytes=_vmem_limit(nbytes)),
        name="rwkv7",
    )(*args)
    if has_vres:
        return res, vfirst
    return res[0], res[1]


def _ffn_kernel(tiles_per_seq, h_ref, g_ref, wg_ref, wu_ref, cw_ref, cb_ref, wd_ref,
                o_ref, u_ref, tail_ref):
    i = pl.program_id(0)
    j = pl.program_id(1)

    @pl.when(j == 0)
    def _():
        h = h_ref[...]
        u_ref[...] = _rms(h, g_ref[...]).astype(BF16)
        o_ref[...] = h

    @pl.when(i % tiles_per_seq == 0)
    def _():
        tail_ref[j] = jnp.zeros(tail_ref.shape[1:], F32)

    u = u_ref[...]
    gate = jnp.dot(u, wg_ref[...], preferred_element_type=F32)
    tm = gate.shape[0]
    tail = tail_ref[j]
    cw = cw_ref[...]
    conv = gate * cw[FFN_CONV - 1:FFN_CONV] + cb_ref[...]
    for d in range(1, FFN_CONV):
        conv = conv + _shift_rows(gate, d, tail) * cw[FFN_CONV - 1 - d:FFN_CONV - d]
    tail_ref[j] = gate[tm - SUBLANES:]
    up = jnp.dot(u, wu_ref[...], preferred_element_type=F32)
    act = (_gelu(conv) * up).astype(BF16)
    o_ref[...] += jnp.dot(act, wd_ref[...], preferred_element_type=F32)


def _ffn(h, g, wg, wu, cw, cb, wd, layer, seq, tm=512, tf=1024):
    T = h.shape[0]
    nf = D_FF // tf
    nbytes = (2 * (2 * tm * D_MODEL * 4 + 3 * D_MODEL * tf * 2) + tm * D_MODEL * 2
              + nf * SUBLANES * tf * 4 + 6 * tm * tf * 4)
    return pl.pallas_call(
        functools.partial(_ffn_kernel, seq // tm),
        grid=(T // tm, nf),
        in_specs=[
            pl.BlockSpec((tm, D_MODEL), lambda i, j: (i, 0)),
            pl.BlockSpec((1, D_MODEL), lambda i, j: (0, 0)),
            pl.BlockSpec((None, D_MODEL, tf), lambda i, j: (layer, 0, j)),
            pl.BlockSpec((None, D_MODEL, tf), lambda i, j: (layer, 0, j)),
            pl.BlockSpec((FFN_CONV, tf), lambda i, j: (0, j)),
            pl.BlockSpec((1, tf), lambda i, j: (0, j)),
            pl.BlockSpec((None, tf, D_MODEL), lambda i, j: (layer, j, 0)),
        ],
        out_specs=pl.BlockSpec((tm, D_MODEL), lambda i, j: (i, 0)),
        out_shape=jax.ShapeDtypeStruct((T, D_MODEL), F32),
        scratch_shapes=[pltpu.VMEM((tm, D_MODEL), BF16),
                        pltpu.VMEM((nf, SUBLANES, tf), F32)],
        compiler_params=pltpu.CompilerParams(
            dimension_semantics=("arbitrary", "arbitrary"),
            vmem_limit_bytes=_vmem_limit(nbytes)),
        name="ffn",
    )(h, g, wg, wu, cw, cb, wd)


def _ple_kernel(final, h_ref, p_ref, g_ref, wg_ref, wp_ref, gp_ref, gf_ref, o_ref):
    h = h_ref[...]
    u = _rms(h, g_ref[...]).astype(BF16)
    gate = _sigmoid(jnp.dot(u, wg_ref[...], preferred_element_type=F32))
    proj = jnp.dot(p_ref[...].astype(BF16), wp_ref[...], preferred_element_type=F32)
    out = h + _rms(gate * proj, gp_ref[...])
    if final:
        out = _rms(out, gf_ref[...])
    o_ref[...] = out


def _ple(h, p, g, wg, layer, wp, gp, gf, final, tm=512):
    T = h.shape[0]
    vec = pl.BlockSpec((1, D_MODEL), lambda i: (0, 0))
    nbytes = 2 * (2 * tm * D_MODEL * 4 + tm * D_PLE * 4 + D_MODEL * D_MODEL * 2
                  + D_PLE * D_MODEL * 2) + 4 * tm * D_MODEL * 4
    return pl.pallas_call(
        functools.partial(_ple_kernel, final),
        grid=(T // tm,),
        in_specs=[
            pl.BlockSpec((tm, D_MODEL), lambda i: (i, 0)),
            pl.BlockSpec((tm, D_PLE), lambda i: (i, 0)),
            vec,
            pl.BlockSpec((None, D_MODEL, D_MODEL), lambda i: (layer, 0, 0)),
            pl.BlockSpec((D_PLE, D_MODEL), lambda i: (0, 0)),
            vec, vec,
        ],
        out_specs=pl.BlockSpec((tm, D_MODEL), lambda i: (i, 0)),
        out_shape=jax.ShapeDtypeStruct((T, D_MODEL), F32),
        compiler_params=pltpu.CompilerParams(
            dimension_semantics=("arbitrary",),
            vmem_limit_bytes=_vmem_limit(nbytes)),
        name="ple",
    )(h, p, g, wg, wp, gp, gf)


def _row(v):
    return v.reshape(1, -1).astype(F32)


def _pad_rows(w, top, total):
    return jnp.pad(w, ((top, total - top - w.shape[0]), (0, 0)))


def kernel(x, p, ln_mix, w_in, w_in_vres, mu_shift, mu_shift_vres, conv_a_w, conv_a_b, lru_wx, lru_bx, lru_wa, lru_ba, lru_lambda, lru_norm, rwkv_w0, rwkv_w2, rwkv_a0, rwkv_a2, rwkv_v0, rwkv_v2, rwkv_g2, rwkv_kk, rwkv_ka, rwkv_rk, rwkv_lnx_w, rwkv_lnx_b, w_o, ln_ffn, w_gate, w_up, conv_f_w, conv_f_b, w_down, ln_ple, w_ple_gate, w_ple_proj, ln_ple_post, ln_final):
    batch, seq, _ = x.shape
    depth = w_in.shape[0]
    T = batch * seq
    h = x.reshape(T, D_MODEL)
    n_lora = LORA_W + LORA_A + LORA_G
    vfirst = None
    w_in_b, w_o_b, w_gate_b, w_up_b, w_down_b, w_ple_gate_b = (
        w.astype(BF16) for w in (w_in[:, :, :D_MAIN], w_o, w_gate, w_up, w_down, w_ple_gate))
    for i in range(depth):
        lora_cols = [w_in[i][:, D_MAIN:]]
        mu_l = [mu_shift[i][3 * D_RWKV:]]
        if i > 0:
            lora_cols.append(w_in_vres[i - 1])
            mu_l.append(mu_shift_vres[i - 1])
        w_lora = jnp.concatenate(lora_cols, axis=1)
        w_lora = jnp.pad(w_lora, ((0, 0), (0, D_LORA - w_lora.shape[1]))).astype(BF16)
        mu_lora = jnp.concatenate(mu_l, axis=0)
        mu_lora = jnp.pad(mu_lora, (0, D_LORA - mu_lora.shape[0]))

        zm, zl = _inproj(h, _row(ln_mix[i]), w_in_b, i, w_lora)

        prm = {
            "mu_rkv": _row(mu_shift[i][:3 * D_RWKV]),
            "mu_lora": _row(mu_lora),
            "w0": _row(rwkv_w0[i]),
            "w2": _pad_rows(rwkv_w2[i], 0, LANES),
            "a0": _row(rwkv_a0[i]),
            "a2": _pad_rows(rwkv_a2[i], LORA_W, LANES),
            "g2": _pad_rows(rwkv_g2[i], 0, 2 * LANES),
            "kk": _row(rwkv_kk[i]), "ka": _row(rwkv_ka[i]), "rk": _row(rwkv_rk[i]),
            "lnw": _row(rwkv_lnx_w[i]), "lnb": _row(rwkv_lnx_b[i]),
        }
        if i > 0:
            prm["v0"] = _row(rwkv_v0[i - 1])
            prm["v2"] = _pad_rows(rwkv_v2[i - 1], n_lora - 2 * LANES, LANES)
        out_b, vfirst = _rwkv(zm, zl, vfirst, prm, batch, seq)

        h = _lru_oproj(zm, out_b, h, conv_a_w[i], _row(conv_a_b[i]), lru_wx[i].astype(BF16),
                       _row(lru_bx[i]), lru_wa[i].astype(BF16), _row(lru_ba[i]),
                       _row(lru_lambda[i]), _row(lru_norm[i]), w_o_b, i, batch, seq)
        h = _ffn(h, _row(ln_ffn[i]), w_gate_b, w_up_b, conv_f_w[i], _row(conv_f_b[i]),
                 w_down_b, i, seq)
        h = _ple(h, p[i].reshape(T, D_PLE), _row(ln_ple[i]), w_ple_gate_b, i,
                 w_ple_proj[i].astype(BF16), _row(ln_ple_post[i]), _row(ln_final),
                 final=(i == depth - 1))
    return h.reshape(batch, seq, D_MODEL)
```

```python
import functools
import math

import jax
import jax.numpy as jnp
from jax import lax
from jax.experimental import pallas as pl
from jax.experimental.pallas import tpu as pltpu

F32 = jnp.float32
BF16 = jnp.bfloat16

D_MODEL = 2048
D_LRU = 1024
D_RWKV = 1024
LRU_HEADS = 4
LRU_BLOCK = 256
LRU_CONV = 4
LRU_C = 8.0
HEAD = 64
LORA_W = 64
LORA_A = 64
LORA_G = 160
D_MAIN = 2 * D_LRU + 3 * D_RWKV
D_LORA = 384
D_FF = 3 * D_MODEL
FFN_CONV = 3
D_PLE = 256
RMS_EPS = 1e-6
LNX_EPS = 64e-5

V7X_VMEM_BYTES = 64 * 1024 * 1024
SUBLANES = 8
LANES = 128
ROW_TILE = 16

CHUNK = 64
PAIR = 2 * HEAD
LRU_ROWS = 256
HEAD_GROUP = 16
RWKV_ROWS = 1024

VMEM_TEMPORARIES_BYTES = 16 * 1024 * 1024
VMEM_UNSCOPED_BYTES = 4 * 1024 * 1024


def _vmem_limit(nbytes):
    return int(min(V7X_VMEM_BYTES - VMEM_UNSCOPED_BYTES, nbytes + VMEM_TEMPORARIES_BYTES))


def _rms(x, g):
    return x * lax.rsqrt(jnp.mean(x * x, axis=-1, keepdims=True) + RMS_EPS) * g


def _gelu(x):
    c = math.sqrt(2.0 / math.pi)
    return 0.5 * x * (1.0 + jnp.tanh(c * (x + 0.044715 * (x * x * x))))


def _sigmoid(x):
    return 1.0 / (1.0 + jnp.exp(-x))


def _softplus(x):
    return jnp.maximum(x, 0.0) + jnp.log1p(jnp.exp(-jnp.abs(x)))


def _shift_rows(x, d, prev8):
    rolled = pltpu.roll(x, d, axis=0)
    prev = pltpu.roll(prev8, d, axis=0)
    row = lax.broadcasted_iota(jnp.int32, prev8.shape, 0)
    top = jnp.where(row < d, prev, rolled[:SUBLANES])
    return jnp.concatenate([top, rolled[SUBLANES:]], axis=0)


def _inproj_kernel(x_ref, g_ref, wm_ref, wl_ref, zm_ref, zl_ref, u_ref):
    @pl.when(pl.program_id(1) == 0)
    def _():
        u_ref[...] = _rms(x_ref[...], g_ref[...]).astype(BF16)
        zl_ref[...] = jnp.dot(u_ref[...], wl_ref[...], preferred_element_type=F32)

    zm_ref[...] = jnp.dot(u_ref[...], wm_ref[...], preferred_element_type=F32)


def _inproj(h, g, w_in, layer, w_lora, tm=1024, tn=1024):
    T = h.shape[0]
    nbytes = 2 * (tm * D_MODEL * 4 + D_MODEL * tn * 2 + D_MODEL * D_LORA * 2
                  + tm * tn * 4 + tm * D_LORA * 4) + tm * D_MODEL * 2
    return pl.pallas_call(
        _inproj_kernel,
        grid=(T // tm, D_MAIN // tn),
        in_specs=[
            pl.BlockSpec((tm, D_MODEL), lambda i, j: (i, 0)),
            pl.BlockSpec((1, D_MODEL), lambda i, j: (0, 0)),
            pl.BlockSpec((None, D_MODEL, tn), lambda i, j: (layer, 0, j)),
            pl.BlockSpec((D_MODEL, D_LORA), lambda i, j: (0, 0)),
        ],
        out_specs=[
            pl.BlockSpec((tm, tn), lambda i, j: (i, j)),
            pl.BlockSpec((tm, D_LORA), lambda i, j: (i, 0)),
        ],
        out_shape=[
            jax.ShapeDtypeStruct((T, D_MAIN), F32),
            jax.ShapeDtypeStruct((T, D_LORA), F32),
        ],
        scratch_shapes=[pltpu.VMEM((tm, D_MODEL), BF16)],
        compiler_params=pltpu.CompilerParams(
            dimension_semantics=("arbitrary", "arbitrary"),
            vmem_limit_bytes=_vmem_limit(nbytes)),
        name="inproj",
    )(h, g, w_in, w_lora)


def _lru_rows(xb_ref, yb_ref, rows, tail, carry, seq_start, cw, cb, wx_ref, bx, wa_ref, ba,
              sp_lam, nrm, between):
    nrows = rows.stop - rows.start
    ngroup = nrows // SUBLANES
    sub = lax.broadcasted_iota(jnp.int32, (1, SUBLANES, 1), 1)
    ys, tails, carries, ss = [], [], [], 0.0
    for hd in range(LRU_HEADS):
        cols = slice(hd * LRU_BLOCK, (hd + 1) * LRU_BLOCK)
        x = xb_ref[rows, cols]
        xc = x * cw[LRU_CONV - 1:LRU_CONV, cols] + cb[:, cols]
        for d in range(1, LRU_CONV):
            xc = xc + _shift_rows(x, d, tail[:, cols]) * cw[LRU_CONV - 1 - d:LRU_CONV - d, cols]
        tails.append(x[nrows - SUBLANES:])

        xcb = xc.astype(BF16)
        gate_x = _sigmoid(jnp.dot(xcb, wx_ref[hd], preferred_element_type=F32) + bx[:, cols])
        gate_a = _sigmoid(jnp.dot(xcb, wa_ref[hd], preferred_element_type=F32) + ba[:, cols])
        log_a = (-LRU_C) * gate_a * sp_lam[:, cols]
        a = jnp.exp(log_a)
        mult = jnp.sqrt(1.0 - a * a)
        if seq_start is not None:
            row = lax.broadcasted_iota(jnp.int32, (nrows, 1), 0)
            mult = jnp.where(jnp.logical_and(row == 0, seq_start), 1.0, mult)
        b = xc * gate_x * mult

        a = a.reshape(ngroup, SUBLANES, LRU_BLOCK)
        b = b.reshape(ngroup, SUBLANES, LRU_BLOCK)
        d = 1
        while d < SUBLANES:
            keep = sub >= d
            a_sh = jnp.where(keep, pltpu.roll(a, d, axis=1), 1.0)
            b_sh = jnp.where(keep, pltpu.roll(b, d, axis=1), 0.0)
            b = a * b_sh + b
            a = a * a_sh
            d *= 2
        hcar = carry[:, cols]
        hs = []
        for grp in range(ngroup):
            hg = a[grp] * hcar + b[grp]
            hs.append(hg)
            hcar = hg[SUBLANES - 1:SUBLANES]
        carries.append(hcar)

        y = jnp.concatenate(hs, axis=0) * _gelu(yb_ref[rows, cols])
        ss = ss + jnp.sum(y * y, axis=-1, keepdims=True)
        ys.append(y)
        between(hd)
    scale = lax.rsqrt(ss * (1.0 / D_LRU) + RMS_EPS)
    out = jnp.concatenate(ys, axis=-1) * scale * nrm
    return out.astype(BF16), jnp.concatenate(tails, axis=-1), jnp.concatenate(carries, axis=-1)


def _lru_oproj_kernel(xb_ref, yb_ref, ob_ref, h_ref, cw_ref, cb_ref, wx_ref, bx_ref, wa_ref,
                      ba_ref, lam_ref, nrm_ref, woa_ref, wob_ref, o_ref, tail_ref, carry_ref):
    t = pl.program_id(1)

    @pl.when(t == 0)
    def _():
        tail_ref[...] = jnp.zeros_like(tail_ref)
        carry_ref[...] = jnp.zeros_like(carry_ref)

    sp_lam = _softplus(-lam_ref[...])
    tail, carry = tail_ref[...], carry_ref[0:1, :]
    ts = xb_ref.shape[0]
    ncol = D_MODEL // LRU_HEADS
    pieces = {}

    def project(name, lhs, w_ref):
        def step(hd):
            cols = slice(hd * ncol, (hd + 1) * ncol)
            pieces.setdefault(name, []).append(
                jnp.dot(lhs, w_ref[:, cols], preferred_element_type=F32))
        return step

    between = project("b", ob_ref[...], wob_ref)
    for r0 in range(0, ts, LRU_ROWS):
        out_a, tail, carry = _lru_rows(
            xb_ref, yb_ref, slice(r0, r0 + LRU_ROWS), tail, carry, (t == 0) if r0 == 0 else None,
            cw_ref[...], cb_ref[...], wx_ref, bx_ref[...], wa_ref, ba_ref[...], sp_lam,
            nrm_ref[...], between)
        between = project(("a", r0), out_a, woa_ref)
    for hd in range(LRU_HEADS):
        between(hd)
    tail_ref[...] = tail
    carry_ref[0:1, :] = carry
    acc_a = jnp.concatenate(
        [jnp.concatenate(pieces["a", r0], axis=-1) for r0 in range(0, ts, LRU_ROWS)], axis=0)
    o_ref[...] = h_ref[...] + jnp.concatenate(pieces["b"], axis=-1) + acc_a


def _lru_oproj(zm, out_b, h, cw, cb, wx, bx, wa, ba, lam, nrm, wo, layer, batch, seq, ts=512):
    T = zm.shape[0]
    nt = seq // ts
    vec = pl.BlockSpec((1, D_LRU), lambda b, t: (0, 0))
    mat = pl.BlockSpec((LRU_HEADS, LRU_BLOCK, LRU_BLOCK), lambda b, t: (0, 0, 0))

    def rows(width, col):
        return pl.BlockSpec((ts, width), lambda b, t: (b * nt + t, col))

    nbytes = (2 * (2 * ts * D_LRU * 4 + ts * D_RWKV * 2 + 2 * ts * D_MODEL * 4
                   + 2 * D_LRU * D_MODEL * 2) + 16 * LRU_ROWS * D_LRU * 4 + 2 * ts * D_MODEL * 4)
    return pl.pallas_call(
        _lru_oproj_kernel,
        grid=(batch, nt),
        in_specs=[
            rows(D_LRU, 0), rows(D_LRU, 1), rows(D_RWKV, 0), rows(D_MODEL, 0),
            pl.BlockSpec((LRU_CONV, D_LRU), lambda b, t: (0, 0)),
            vec, mat, vec, mat, vec, vec, vec,
            pl.BlockSpec((None, D_LRU, D_MODEL), lambda b, t: (layer, 0, 0)),
            pl.BlockSpec((None, D_RWKV, D_MODEL), lambda b, t: (layer, 1, 0)),
        ],
        out_specs=rows(D_MODEL, 0),
        out_shape=jax.ShapeDtypeStruct((T, D_MODEL), F32),
        scratch_shapes=[pltpu.VMEM((SUBLANES, D_LRU), F32),
                        pltpu.VMEM((SUBLANES, D_LRU), F32)],
        compiler_params=pltpu.CompilerParams(
            dimension_semantics=("arbitrary", "arbitrary"),
            vmem_limit_bytes=_vmem_limit(nbytes)),
        name="lru_oproj",
    )(zm, zm, out_b, h, cw, cb, wx, bx, wa, ba, lam, nrm, wo, wo)


def _mm(a, b):
    return jnp.dot(a.astype(BF16), b.astype(BF16), preferred_element_type=F32)


def _mm_nt(a, b):
    return lax.dot_general(a.astype(BF16), b.astype(BF16), (((1,), (1,)), ((), ())),
                           preferred_element_type=F32)


def _mm_tn(a, b):
    return lax.dot_general(a.astype(BF16), b.astype(BF16), (((0,), (0,)), ((), ())),
                           preferred_element_type=F32)


def _split3(x):
    hi = x.astype(BF16)
    r1 = x - hi.astype(F32)
    mid = r1.astype(BF16)
    lo = (r1 - mid.astype(F32)).astype(BF16)
    return hi, mid, lo


def _seg_sum(x):
    lane_lo = lax.broadcasted_iota(jnp.int32, (1, PAIR), 1) < HEAD
    out = []
    for p in range(x.shape[1] // PAIR):
        t = x[:, p * PAIR:(p + 1) * PAIR]
        s0 = jnp.sum(jnp.where(lane_lo, t, 0.0), axis=-1, keepdims=True)
        s1 = jnp.sum(jnp.where(lane_lo, 0.0, t), axis=-1, keepdims=True)
        out.append(jnp.where(lane_lo, s0, s1))
    return jnp.concatenate(out, axis=-1)


def _rwkv_kernel(has_vres, *refs):
    if has_vres:
        (r_ref, k_ref, v_ref, zl_ref, vf_ref, mur_ref, muk_ref, muv_ref, mul_ref,
         w0_ref, w2_ref, a0_ref, a2_ref, g2_ref, v0_ref, v2_ref,
         kkw_ref, ka_ref, rk_ref, lnw_ref, lnb_ref, o_ref, s_ref, prev_ref, prevz_ref) = refs
        vfo_ref = None
    else:
        (r_ref, k_ref, v_ref, zl_ref, mur_ref, muk_ref, muv_ref, mul_ref,
         w0_ref, w2_ref, a0_ref, a2_ref, g2_ref,
         kkw_ref, ka_ref, rk_ref, lnw_ref, lnb_ref, o_ref, vfo_ref,
         s_ref, prev_ref, prevz_ref) = refs
        vf_ref = v0_ref = v2_ref = None

    C = CHUNK
    nchunk = r_ref.shape[0] // C
    nh = r_ref.shape[1] // HEAD

    @pl.when(pl.program_id(2) == 0)
    def _():
        s_ref[...] = jnp.zeros_like(s_ref)
        prev_ref[...] = jnp.zeros_like(prev_ref)
        prevz_ref[...] = jnp.zeros_like(prevz_ref)

    ri3 = lax.broadcasted_iota(jnp.int32, (C, 3 * C), 0)
    ci3 = lax.broadcasted_iota(jnp.int32, (C, 3 * C), 1) % C
    tri3 = (ri3 >= ci3).astype(BF16)
    ri2 = lax.broadcasted_iota(jnp.int32, (2 * C, 2 * C), 0)
    ci2 = lax.broadcasted_iota(jnp.int32, (2 * C, 2 * C), 1) % C
    keep2 = jnp.where(ri2 < C, ri2, ri2 - C + 1) > ci2
    row0 = lax.broadcasted_iota(jnp.int32, (C, 1), 0) == 0
    zeros_h = jnp.zeros((C, HEAD), BF16)

    def shift_lerp(cur, prev_row, mu):
        sh = jnp.where(row0, prev_row, pltpu.roll(cur, 1, axis=0))
        return cur + (sh - cur) * mu

    def chunk_rows(c):
        return pl.ds(pl.multiple_of(c * C, C), C)

    def prep(c, out):
        rows = chunk_rows(c)
        first = c == 0
        before = pl.ds(jnp.maximum(c * C - 1, 0), 1)

        def lerp(ref, carried, mu):
            return shift_lerp(ref[rows, :], jnp.where(first, carried, ref[before, :]), mu)

        zl = lerp(zl_ref, prevz_ref[0:1, :], mul_ref[...])
        z01 = zl[:, 0:LANES]
        wpre = w0_ref[...] + _mm(jnp.tanh(z01), w2_ref[...])
        apre = a0_ref[...] + _mm(z01, a2_ref[...])
        g = _mm(_sigmoid(zl[:, LANES:3 * LANES]), g2_ref[...])
        if has_vres:
            mpre = v0_ref[...] + _mm(zl[:, 2 * LANES:3 * LANES], v2_ref[...])
        yield
        w_log = -_softplus(-wpre) - 0.5
        logw = -jnp.exp(w_log)
        cum = jnp.dot(tri3, jnp.concatenate(_split3(logw), axis=0), preferred_element_type=F32)
        yield
        r = lerp(r_ref, prev_ref[0:1, :], mur_ref[...])
        k = lerp(k_ref, prev_ref[1:2, :], muk_ref[...])
        a = _sigmoid(apre)
        yield
        v = lerp(v_ref, prev_ref[2:3, :], muv_ref[...])
        if has_vres:
            v = v + (vf_ref[rows, :] - v) * _sigmoid(mpre)
        else:
            vfo_ref[rows, :] = v
        yield
        kk = k * kkw_ref[...]
        kk = kk / jnp.maximum(jnp.sqrt(_seg_sum(kk * kk)), 1e-12)
        yield
        k2 = k * (1.0 + (a - 1.0) * ka_ref[...])
        bb = kk * a
        bonus = _seg_sum(r * k2 * rk_ref[...]) * v
        yield
        p_in = jnp.exp(cum)
        p_ex = jnp.exp(cum - logw)
        p_inv = jnp.exp(-cum)
        p_end = p_in[C - 1:C, :]
        yield
        rt = r * p_in
        at = -kk * p_ex
        rt_b, at_b, v_b = rt.astype(BF16), at.astype(BF16), v.astype(BF16)
        yield
        bt = bb * p_inv
        kt = k2 * p_inv
        bk_t = jnp.transpose(jnp.concatenate([bt, kt], axis=0)).astype(BF16)
        yield
        bhat_b = (bt * p_end).astype(BF16)
        khat_b = (kt * p_end).astype(BF16)
        out["local"] = (rt_b, at_b, bk_t, v_b, bhat_b, khat_b, at, rt, p_end)
        out["post"] = (bonus, g)

    def heads_stage(local, out):
        rt_b, at_b, bk_t, v_b, bhat_b, khat_b, at, rt, p_end = local
        heads = range(nh)
        sls = [slice(hh * HEAD, (hh + 1) * HEAD) for hh in heads]
        sc_b, m, vh_b, d = [], [], [], []
        for sl in sls:
            ar = jnp.concatenate([at_b[:, sl], rt_b[:, sl]], axis=0)
            sc = jnp.dot(ar, bk_t[sl, :], preferred_element_type=F32)
            sc = jnp.where(keep2, sc, 0.0)
            sc_b.append(sc.astype(BF16))
            m.append(sc[:C, :C])
            vh_b.append(v_b[:, sl])
        yield
        for hh in heads:
            zv = jnp.concatenate([zeros_h, vh_b[hh]], axis=0)
            x_loc = jnp.dot(sc_b[hh][:C], zv, preferred_element_type=F32)
            d.append(jnp.concatenate([at[:, sls[hh]], x_loc], axis=-1))
        yield
        nstep = int(math.log2(C))
        for i in range(nstep):
            lo = (2 ** i // ROW_TILE) * ROW_TILE
            for hh in heads:
                m_b = m[hh].astype(BF16)[lo:, :C - lo]
                d_b = d[hh].astype(BF16)[:C - lo]
                if i + 1 < nstep:
                    rhs = jnp.concatenate([d_b, m[hh].astype(BF16)[:C - lo]], axis=-1)
                    prod = jnp.dot(m_b, rhs, preferred_element_type=F32)
                    upd, m_new = prod[:, :2 * HEAD], prod[:, 2 * HEAD:]
                    if lo:
                        m_new = jnp.concatenate([jnp.zeros((lo, C), F32), m_new], axis=0)
                    m[hh] = m_new
                else:
                    upd = jnp.dot(m_b, d_b, preferred_element_type=F32)
                if lo:
                    upd = jnp.concatenate([jnp.zeros((lo, 2 * HEAD), F32), upd], axis=0)
                d[hh] = d[hh] + upd
            yield
        o1, wz = [], []
        for hh in heads:
            gmat = jnp.concatenate(
                [d[hh].astype(BF16), jnp.concatenate([zeros_h, vh_b[hh]], axis=-1)], axis=0)
            o1.append(jnp.dot(sc_b[hh][C:], gmat, preferred_element_type=F32))
            bkh = jnp.concatenate([bhat_b[:, sls[hh]], khat_b[:, sls[hh]]], axis=0)
            wz.append(_mm_tn(gmat, bkh))
        yield
        ys = []
        for hh in heads:
            rbar = rt[:, sls[hh]] + o1[hh][:, :HEAD]
            st = s_ref[hh]
            ys.append(_mm_nt(rbar, st) + o1[hh][:, HEAD:])
            s_ref[hh] = st * p_end[:, sls[hh]] + _mm(st, wz[hh][:HEAD]) + wz[hh][HEAD:]
        out["y"] = jnp.concatenate(ys, axis=-1)

    def tail(c, y, post):
        bonus, g = post
        mean = _seg_sum(y) * (1.0 / HEAD)
        yc = y - mean
        yield
        var = _seg_sum(yc * yc) * (1.0 / HEAD)
        yield
        yn = yc * lax.rsqrt(var + LNX_EPS) * lnw_ref[...] + lnb_ref[...]
        o_ref[chunk_rows(c), :] = ((yn + bonus) * g).astype(o_ref.dtype)

    def run_interleaved(*gens):
        alive = list(gens)
        while alive:
            for gen in list(alive):
                if next(gen, "done") == "done":
                    alive.remove(gen)

    def body(i, carry):
        local, y_prev, post_prev, post_cur = carry
        out = {}
        run_interleaved(heads_stage(local, out),
                        tail(jnp.maximum(i - 1, 0), y_prev, post_prev),
                        prep(jnp.minimum(i + 1, nchunk - 1), out))
        return out["local"], out["y"], post_cur, out["post"]

    first = {}
    run_interleaved(prep(0, first))
    zeros_w = jnp.zeros((C, r_ref.shape[1]), F32)
    _, y_last, post_last, _ = lax.fori_loop(
        0, nchunk, body, (first["local"], zeros_w, (zeros_w, zeros_w), first["post"]))
    run_interleaved(tail(nchunk - 1, y_last, post_last))

    last = pl.ds(r_ref.shape[0] - 1, 1)
    prev_ref[0:1, :] = r_ref[last, :]
    prev_ref[1:2, :] = k_ref[last, :]
    prev_ref[2:3, :] = v_ref[last, :]
    prevz_ref[0:1, :] = zl_ref[last, :]


def _rwkv(zm, zl, vfirst, prm, batch, seq, hg=HEAD_GROUP, ts=RWKV_ROWS):
    T = zm.shape[0]
    W = hg * HEAD
    ng = D_RWKV // W
    nt = seq // ts
    col0 = 2 * D_LRU // W
    has_vres = vfirst is not None

    def col(off):
        return pl.BlockSpec((ts, W), lambda b, g, t: (b * nt + t, off + g))

    vecg = pl.BlockSpec((1, W), lambda b, g, t: (0, g))
    vec_k = pl.BlockSpec((1, W), lambda b, g, t: (0, ng + g))
    vec_v = pl.BlockSpec((1, W), lambda b, g, t: (0, 2 * ng + g))
    vec_l = pl.BlockSpec((1, D_LORA), lambda b, g, t: (0, 0))

    def lora(rows):
        return pl.BlockSpec((rows, W), lambda b, g, t: (0, g))

    in_specs = [col(col0), col(col0 + ng), col(col0 + 2 * ng),
                pl.BlockSpec((ts, D_LORA), lambda b, g, t: (b * nt + t, 0))]
    args = [zm, zm, zm, zl]
    if has_vres:
        in_specs.append(col(0))
        args.append(vfirst)
    in_specs += [vecg, vec_k, vec_v, vec_l, vecg, lora(LANES), vecg, lora(LANES), lora(2 * LANES)]
    args += [prm["mu_rkv"], prm["mu_rkv"], prm["mu_rkv"], prm["mu_lora"],
             prm["w0"], prm["w2"], prm["a0"], prm["a2"], prm["g2"]]
    if has_vres:
        in_specs += [vecg, lora(LANES)]
        args += [prm["v0"], prm["v2"]]
    in_specs += [vecg] * 5
    args += [prm["kk"], prm["ka"], prm["rk"], prm["lnw"], prm["lnb"]]

    if has_vres:
        out_specs = col(0)
        out_shape = jax.ShapeDtypeStruct((T, D_RWKV), BF16)
    else:
        out_specs = [col(0), col(0)]
        out_shape = [jax.ShapeDtypeStruct((T, D_RWKV), BF16),
                     jax.ShapeDtypeStruct((T, D_RWKV), F32)]
    nbytes = 2 * ts * (5 * W * 4 + D_LORA * 4 + W * 2) + hg * HEAD * HEAD * 4
    res = pl.pallas_call(
        functools.partial(_rwkv_kernel, has_vres),
        grid=(batch, ng, nt),
        in_specs=in_specs,
        out_specs=out_specs,
        out_shape=out_shape,
        scratch_shapes=[pltpu.VMEM((hg, HEAD, HEAD), F32),
                        pltpu.VMEM((SUBLANES, W), F32),
                        pltpu.VMEM((SUBLANES, D_LORA), F32)],
        compiler_params=pltpu.CompilerParams(
            dimension_semantics=("arbitrary", "arbitrary", "arbitrary"),
            vmem_limit_bytes=_vmem_limit(nbytes)),
        name="rwkv7",
    )(*args)
    if has_vres:
        return res, vfirst
    return res[0], res[1]


def _ffn_kernel(tiles_per_seq, h_ref, g_ref, wg_ref, wu_ref, cw_ref, cb_ref, wd_ref,
                o_ref, u_ref, tail_ref):
    i = pl.program_id(0)
    j = pl.program_id(1)

    @pl.when(j == 0)
    def _():
        h = h_ref[...]
        u_ref[...] = _rms(h, g_ref[...]).astype(BF16)
        o_ref[...] = h

    @pl.when(i % tiles_per_seq == 0)
    def _():
        tail_ref[j] = jnp.zeros(tail_ref.shape[1:], F32)

    u = u_ref[...]
    gate = jnp.dot(u, wg_ref[...], preferred_element_type=F32)
    tm = gate.shape[0]
    tail = tail_ref[j]
    cw = cw_ref[...]
    conv = gate * cw[FFN_CONV - 1:FFN_CONV] + cb_ref[...]
    for d in range(1, FFN_CONV):
        conv = conv + _shift_rows(gate, d, tail) * cw[FFN_CONV - 1 - d:FFN_CONV - d]
    tail_ref[j] = gate[tm - SUBLANES:]
    up = jnp.dot(u, wu_ref[...], preferred_element_type=F32)
    act = (_gelu(conv) * up).astype(BF16)
    o_ref[...] += jnp.dot(act, wd_ref[...], preferred_element_type=F32)


def _ffn(h, g, wg, wu, cw, cb, wd, layer, seq, tm=1024, tf=512):
    T = h.shape[0]
    nf = D_FF // tf
    nbytes = (2 * (2 * tm * D_MODEL * 4 + 3 * D_MODEL * tf * 2) + tm * D_MODEL * 2
              + nf * SUBLANES * tf * 4 + 6 * tm * tf * 4)
    return pl.pallas_call(
        functools.partial(_ffn_kernel, seq // tm),
        grid=(T // tm, nf),
        in_specs=[
            pl.BlockSpec((tm, D_MODEL), lambda i, j: (i, 0)),
            pl.BlockSpec((1, D_MODEL), lambda i, j: (0, 0)),
            pl.BlockSpec((None, D_MODEL, tf), lambda i, j: (layer, 0, j)),
            pl.BlockSpec((None, D_MODEL, tf), lambda i, j: (layer, 0, j)),
            pl.BlockSpec((FFN_CONV, tf), lambda i, j: (0, j)),
            pl.BlockSpec((1, tf), lambda i, j: (0, j)),
            pl.BlockSpec((None, tf, D_MODEL), lambda i, j: (layer, j, 0)),
        ],
        out_specs=pl.BlockSpec((tm, D_MODEL), lambda i, j: (i, 0)),
        out_shape=jax.ShapeDtypeStruct((T, D_MODEL), F32),
        scratch_shapes=[pltpu.VMEM((tm, D_MODEL), BF16),
                        pltpu.VMEM((nf, SUBLANES, tf), F32)],
        compiler_params=pltpu.CompilerParams(
            dimension_semantics=("arbitrary", "arbitrary"),
            vmem_limit_bytes=_vmem_limit(nbytes)),
        name="ffn",
    )(h, g, wg, wu, cw, cb, wd)


def _ple_kernel(final, h_ref, p_ref, g_ref, wg_ref, wp_ref, gp_ref, gf_ref, o_ref):
    h = h_ref[...]
    u = _rms(h, g_ref[...]).astype(BF16)
    gate = _sigmoid(jnp.dot(u, wg_ref[...], preferred_element_type=F32))
    proj = jnp.dot(p_ref[...].astype(BF16), wp_ref[...], preferred_element_type=F32)
    out = h + _rms(gate * proj, gp_ref[...])
    if final:
        out = _rms(out, gf_ref[...])
    o_ref[...] = out


def _ple(h, p, g, wg, layer, wp, gp, gf, final, tm=512):
    T = h.shape[0]
    vec = pl.BlockSpec((1, D_MODEL), lambda i: (0, 0))
    nbytes = 2 * (2 * tm * D_MODEL * 4 + tm * D_PLE * 4 + D_MODEL * D_MODEL * 2
                  + D_PLE * D_MODEL * 2) + 4 * tm * D_MODEL * 4
    return pl.pallas_call(
        functools.partial(_ple_kernel, final),
        grid=(T // tm,),
        in_specs=[
            pl.BlockSpec((tm, D_MODEL), lambda i: (i, 0)),
            pl.BlockSpec((tm, D_PLE), lambda i: (i, 0)),
            vec,
            pl.BlockSpec((None, D_MODEL, D_MODEL), lambda i: (layer, 0, 0)),
            pl.BlockSpec((D_PLE, D_MODEL), lambda i: (0, 0)),
            vec, vec,
        ],
        out_specs=pl.BlockSpec((tm, D_MODEL), lambda i: (i, 0)),
        out_shape=jax.ShapeDtypeStruct((T, D_MODEL), F32),
        compiler_params=pltpu.CompilerParams(
            dimension_semantics=("arbitrary",),
            vmem_limit_bytes=_vmem_limit(nbytes)),
        name="ple",
    )(h, p, g, wg, wp, gp, gf)


def _row(v):
    return v.reshape(1, -1).astype(F32)


def _pad_rows(w, top, total):
    return jnp.pad(w, ((top, total - top - w.shape[0]), (0, 0)))


def kernel(x, p, ln_mix, w_in, w_in_vres, mu_shift, mu_shift_vres, conv_a_w, conv_a_b, lru_wx, lru_bx, lru_wa, lru_ba, lru_lambda, lru_norm, rwkv_w0, rwkv_w2, rwkv_a0, rwkv_a2, rwkv_v0, rwkv_v2, rwkv_g2, rwkv_kk, rwkv_ka, rwkv_rk, rwkv_lnx_w, rwkv_lnx_b, w_o, ln_ffn, w_gate, w_up, conv_f_w, conv_f_b, w_down, ln_ple, w_ple_gate, w_ple_proj, ln_ple_post, ln_final):
    batch, seq, _ = x.shape
    depth = w_in.shape[0]
    T = batch * seq
    h = x.reshape(T, D_MODEL)
    n_lora = LORA_W + LORA_A + LORA_G
    vfirst = None
    w_in_b, w_o_b, w_gate_b, w_up_b, w_down_b, w_ple_gate_b = (
        w.astype(BF16) for w in (w_in[:, :, :D_MAIN], w_o, w_gate, w_up, w_down, w_ple_gate))
    for i in range(depth):
        lora_cols = [w_in[i][:, D_MAIN:]]
        mu_l = [mu_shift[i][3 * D_RWKV:]]
        if i > 0:
            lora_cols.append(w_in_vres[i - 1])
            mu_l.append(mu_shift_vres[i - 1])
        w_lora = jnp.concatenate(lora_cols, axis=1)
        w_lora = jnp.pad(w_lora, ((0, 0), (0, D_LORA - w_lora.shape[1]))).astype(BF16)
        mu_lora = jnp.concatenate(mu_l, axis=0)
        mu_lora = jnp.pad(mu_lora, (0, D_LORA - mu_lora.shape[0]))

        zm, zl = _inproj(h, _row(ln_mix[i]), w_in_b, i, w_lora)

        prm = {
            "mu_rkv": _row(mu_shift[i][:3 * D_RWKV]),
            "mu_lora": _row(mu_lora),
            "w0": _row(rwkv_w0[i]),
            "w2": _pad_rows(rwkv_w2[i], 0, LANES),
            "a0": _row(rwkv_a0[i]),
            "a2": _pad_rows(rwkv_a2[i], LORA_W, LANES),
            "g2": _pad_rows(rwkv_g2[i], 0, 2 * LANES),
            "kk": _row(rwkv_kk[i]), "ka": _row(rwkv_ka[i]), "rk": _row(rwkv_rk[i]),
            "lnw": _row(rwkv_lnx_w[i]), "lnb": _row(rwkv_lnx_b[i]),
        }
        if i > 0:
            prm["v0"] = _row(rwkv_v0[i - 1])
            prm["v2"] = _pad_rows(rwkv_v2[i - 1], n_lora - 2 * LANES, LANES)
        out_b, vfirst = _rwkv(zm, zl, vfirst, prm, batch, seq)

        h = _lru_oproj(zm, out_b, h, conv_a_w[i], _row(conv_a_b[i]), lru_wx[i].astype(BF16),
                       _row(lru_bx[i]), lru_wa[i].astype(BF16), _row(lru_ba[i]),
                       _row(lru_lambda[i]), _row(lru_norm[i]), w_o_b, i, batch, seq)
        h = _ffn(h, _row(ln_ffn[i]), w_gate_b, w_up_b, conv_f_w[i], _row(conv_f_b[i]),
                 w_down_b, i, seq)
        h = _ple(h, p[i].reshape(T, D_PLE), _row(ln_ple[i]), w_ple_gate_b, i,
                 w_ple_proj[i].astype(BF16), _row(ln_ple_post[i]), _row(ln_final),
                 final=(i == depth - 1))
    return h.reshape(batch, seq, D_MODEL)
```

```python
import functools
import math

import jax
import jax.numpy as jnp
from jax import lax
from jax.experimental import pallas as pl
from jax.experimental.pallas import tpu as pltpu

F32 = jnp.float32
BF16 = jnp.bfloat16

D_MODEL = 2048
D_LRU = 1024
D_RWKV = 1024
LRU_HEADS = 4
LRU_BLOCK = 256
LRU_CONV = 4
LRU_C = 8.0
HEAD = 64
LORA_W = 64
LORA_A = 64
LORA_G = 160
D_MAIN = 2 * D_LRU + 3 * D_RWKV
D_LORA = 384
D_FF = 3 * D_MODEL
FFN_CONV = 3
D_PLE = 256
RMS_EPS = 1e-6
LNX_EPS = 64e-5

V7X_VMEM_BYTES = 64 * 1024 * 1024
SUBLANES = 8
LANES = 128
ROW_TILE = 16

CHUNK = 64
PAIR = 2 * HEAD
LRU_ROWS = 256
HEAD_GROUP = 16
RWKV_ROWS = 1024

VMEM_TEMPORARIES_BYTES = 16 * 1024 * 1024
VMEM_UNSCOPED_BYTES = 4 * 1024 * 1024


def _vmem_limit(nbytes):
    return int(min(V7X_VMEM_BYTES - VMEM_UNSCOPED_BYTES, nbytes + VMEM_TEMPORARIES_BYTES))


def _rms(x, g):
    return x * lax.rsqrt(jnp.mean(x * x, axis=-1, keepdims=True) + RMS_EPS) * g


def _gelu(x):
    c = math.sqrt(2.0 / math.pi)
    return 0.5 * x * (1.0 + jnp.tanh(c * (x + 0.044715 * (x * x * x))))


def _sigmoid(x):
    return 1.0 / (1.0 + jnp.exp(-x))


def _softplus(x):
    return jnp.maximum(x, 0.0) + jnp.log1p(jnp.exp(-jnp.abs(x)))


def _shift_rows(x, d, prev8):
    rolled = pltpu.roll(x, d, axis=0)
    prev = pltpu.roll(prev8, d, axis=0)
    row = lax.broadcasted_iota(jnp.int32, prev8.shape, 0)
    top = jnp.where(row < d, prev, rolled[:SUBLANES])
    return jnp.concatenate([top, rolled[SUBLANES:]], axis=0)


def _inproj_kernel(x_ref, g_ref, wm_ref, wl_ref, zm_ref, zl_ref, u_ref):
    @pl.when(pl.program_id(1) == 0)
    def _():
        u_ref[...] = _rms(x_ref[...], g_ref[...]).astype(BF16)
        zl_ref[...] = jnp.dot(u_ref[...], wl_ref[...], preferred_element_type=F32)

    zm_ref[...] = jnp.dot(u_ref[...], wm_ref[...], preferred_element_type=F32)


def _inproj(h, g, w_in, layer, w_lora, tm=1024, tn=1280):
    T = h.shape[0]
    nbytes = 2 * (tm * D_MODEL * 4 + D_MODEL * tn * 2 + D_MODEL * D_LORA * 2
                  + tm * tn * 4 + tm * D_LORA * 4) + tm * D_MODEL * 2
    return pl.pallas_call(
        _inproj_kernel,
        grid=(T // tm, D_MAIN // tn),
        in_specs=[
            pl.BlockSpec((tm, D_MODEL), lambda i, j: (i, 0)),
            pl.BlockSpec((1, D_MODEL), lambda i, j: (0, 0)),
            pl.BlockSpec((None, D_MODEL, tn), lambda i, j: (layer, 0, j)),
            pl.BlockSpec((D_MODEL, D_LORA), lambda i, j: (0, 0)),
        ],
        out_specs=[
            pl.BlockSpec((tm, tn), lambda i, j: (i, j)),
            pl.BlockSpec((tm, D_LORA), lambda i, j: (i, 0)),
        ],
        out_shape=[
            jax.ShapeDtypeStruct((T, D_MAIN), F32),
            jax.ShapeDtypeStruct((T, D_LORA), F32),
        ],
        scratch_shapes=[pltpu.VMEM((tm, D_MODEL), BF16)],
        compiler_params=pltpu.CompilerParams(
            dimension_semantics=("arbitrary", "arbitrary"),
            vmem_limit_bytes=_vmem_limit(nbytes)),
        name="inproj",
    )(h, g, w_in, w_lora)


def _lru_rows(xb_ref, yb_ref, rows, tail, carry, seq_start, cw, cb, wx_ref, bx, wa_ref, ba,
              sp_lam, nrm, between):
    nrows = rows.stop - rows.start
    ngroup = nrows // SUBLANES
    sub = lax.broadcasted_iota(jnp.int32, (1, SUBLANES, 1), 1)
    ys, tails, carries, ss = [], [], [], 0.0
    for hd in range(LRU_HEADS):
        cols = slice(hd * LRU_BLOCK, (hd + 1) * LRU_BLOCK)
        x = xb_ref[rows, cols]
        xc = x * cw[LRU_CONV - 1:LRU_CONV, cols] + cb[:, cols]
        for d in range(1, LRU_CONV):
            xc = xc + _shift_rows(x, d, tail[:, cols]) * cw[LRU_CONV - 1 - d:LRU_CONV - d, cols]
        tails.append(x[nrows - SUBLANES:])

        xcb = xc.astype(BF16)
        gate_x = _sigmoid(jnp.dot(xcb, wx_ref[hd], preferred_element_type=F32) + bx[:, cols])
        gate_a = _sigmoid(jnp.dot(xcb, wa_ref[hd], preferred_element_type=F32) + ba[:, cols])
        log_a = (-LRU_C) * gate_a * sp_lam[:, cols]
        a = jnp.exp(log_a)
        mult = jnp.sqrt(1.0 - a * a)
        if seq_start is not None:
            row = lax.broadcasted_iota(jnp.int32, (nrows, 1), 0)
            mult = jnp.where(jnp.logical_and(row == 0, seq_start), 1.0, mult)
        b = xc * gate_x * mult

        a = a.reshape(ngroup, SUBLANES, LRU_BLOCK)
        b = b.reshape(ngroup, SUBLANES, LRU_BLOCK)
        d = 1
        while d < SUBLANES:
            keep = sub >= d
            a_sh = jnp.where(keep, pltpu.roll(a, d, axis=1), 1.0)
            b_sh = jnp.where(keep, pltpu.roll(b, d, axis=1), 0.0)
            b = a * b_sh + b
            a = a * a_sh
            d *= 2
        hcar = carry[:, cols]
        hs = []
        for grp in range(ngroup):
            hg = a[grp] * hcar + b[grp]
            hs.append(hg)
            hcar = hg[SUBLANES - 1:SUBLANES]
        carries.append(hcar)

        y = jnp.concatenate(hs, axis=0) * _gelu(yb_ref[rows, cols])
        ss = ss + jnp.sum(y * y, axis=-1, keepdims=True)
        ys.append(y)
        between(hd)
    scale = lax.rsqrt(ss * (1.0 / D_LRU) + RMS_EPS)
    out = jnp.concatenate(ys, axis=-1) * scale * nrm
    return out.astype(BF16), jnp.concatenate(tails, axis=-1), jnp.concatenate(carries, axis=-1)


def _lru_oproj_kernel(xb_ref, yb_ref, ob_ref, h_ref, cw_ref, cb_ref, wx_ref, bx_ref, wa_ref,
                      ba_ref, lam_ref, nrm_ref, woa_ref, wob_ref, o_ref, tail_ref, carry_ref):
    t = pl.program_id(1)

    @pl.when(t == 0)
    def _():
        tail_ref[...] = jnp.zeros_like(tail_ref)
        carry_ref[...] = jnp.zeros_like(carry_ref)

    sp_lam = _softplus(-lam_ref[...])
    tail, carry = tail_ref[...], carry_ref[0:1, :]
    ts = xb_ref.shape[0]
    ncol = D_MODEL // LRU_HEADS
    pieces = {}

    def project(name, lhs, w_ref):
        def step(hd):
            cols = slice(hd * ncol, (hd + 1) * ncol)
            pieces.setdefault(name, []).append(
                jnp.dot(lhs, w_ref[:, cols], preferred_element_type=F32))
        return step

    between = project("b", ob_ref[...], wob_ref)
    for r0 in range(0, ts, LRU_ROWS):
        out_a, tail, carry = _lru_rows(
            xb_ref, yb_ref, slice(r0, r0 + LRU_ROWS), tail, carry, (t == 0) if r0 == 0 else None,
            cw_ref[...], cb_ref[...], wx_ref, bx_ref[...], wa_ref, ba_ref[...], sp_lam,
            nrm_ref[...], between)
        between = project(("a", r0), out_a, woa_ref)
    for hd in range(LRU_HEADS):
        between(hd)
    tail_ref[...] = tail
    carry_ref[0:1, :] = carry
    acc_a = jnp.concatenate(
        [jnp.concatenate(pieces["a", r0], axis=-1) for r0 in range(0, ts, LRU_ROWS)], axis=0)
    o_ref[...] = h_ref[...] + jnp.concatenate(pieces["b"], axis=-1) + acc_a


def _lru_oproj(zm, out_b, h, cw, cb, wx, bx, wa, ba, lam, nrm, wo, layer, batch, seq, ts=512):
    T = zm.shape[0]
    nt = seq // ts
    vec = pl.BlockSpec((1, D_LRU), lambda b, t: (0, 0))
    mat = pl.BlockSpec((LRU_HEADS, LRU_BLOCK, LRU_BLOCK), lambda b, t: (0, 0, 0))

    def rows(width, col):
        return pl.BlockSpec((ts, width), lambda b, t: (b * nt + t, col))

    nbytes = (2 * (2 * ts * D_LRU * 4 + ts * D_RWKV * 2 + 2 * ts * D_MODEL * 4
                   + 2 * D_LRU * D_MODEL * 2) + 16 * LRU_ROWS * D_LRU * 4 + 2 * ts * D_MODEL * 4)
    return pl.pallas_call(
        _lru_oproj_kernel,
        grid=(batch, nt),
        in_specs=[
            rows(D_LRU, 0), rows(D_LRU, 1), rows(D_RWKV, 0), rows(D_MODEL, 0),
            pl.BlockSpec((LRU_CONV, D_LRU), lambda b, t: (0, 0)),
            vec, mat, vec, mat, vec, vec, vec,
            pl.BlockSpec((None, D_LRU, D_MODEL), lambda b, t: (layer, 0, 0)),
            pl.BlockSpec((None, D_RWKV, D_MODEL), lambda b, t: (layer, 1, 0)),
        ],
        out_specs=rows(D_MODEL, 0),
        out_shape=jax.ShapeDtypeStruct((T, D_MODEL), F32),
        scratch_shapes=[pltpu.VMEM((SUBLANES, D_LRU), F32),
                        pltpu.VMEM((SUBLANES, D_LRU), F32)],
        compiler_params=pltpu.CompilerParams(
            dimension_semantics=("arbitrary", "arbitrary"),
            vmem_limit_bytes=_vmem_limit(nbytes)),
        name="lru_oproj",
    )(zm, zm, out_b, h, cw, cb, wx, bx, wa, ba, lam, nrm, wo, wo)


def _mm(a, b):
    return jnp.dot(a.astype(BF16), b.astype(BF16), preferred_element_type=F32)


def _mm_nt(a, b):
    return lax.dot_general(a.astype(BF16), b.astype(BF16), (((1,), (1,)), ((), ())),
                           preferred_element_type=F32)


def _mm_tn(a, b):
    return lax.dot_general(a.astype(BF16), b.astype(BF16), (((0,), (0,)), ((), ())),
                           preferred_element_type=F32)


def _split3(x):
    hi = x.astype(BF16)
    r1 = x - hi.astype(F32)
    mid = r1.astype(BF16)
    lo = (r1 - mid.astype(F32)).astype(BF16)
    return hi, mid, lo


def _seg_sum(x):
    lane_lo = lax.broadcasted_iota(jnp.int32, (1, PAIR), 1) < HEAD
    out = []
    for p in range(x.shape[1] // PAIR):
        t = x[:, p * PAIR:(p + 1) * PAIR]
        s0 = jnp.sum(jnp.where(lane_lo, t, 0.0), axis=-1, keepdims=True)
        s1 = jnp.sum(jnp.where(lane_lo, 0.0, t), axis=-1, keepdims=True)
        out.append(jnp.where(lane_lo, s0, s1))
    return jnp.concatenate(out, axis=-1)


def _rwkv_kernel(has_vres, *refs):
    if has_vres:
        (r_ref, k_ref, v_ref, zl_ref, vf_ref, mur_ref, muk_ref, muv_ref, mul_ref,
         w0_ref, w2_ref, a0_ref, a2_ref, g2_ref, v0_ref, v2_ref,
         kkw_ref, ka_ref, rk_ref, lnw_ref, lnb_ref, o_ref, s_ref, prev_ref, prevz_ref) = refs
        vfo_ref = None
    else:
        (r_ref, k_ref, v_ref, zl_ref, mur_ref, muk_ref, muv_ref, mul_ref,
         w0_ref, w2_ref, a0_ref, a2_ref, g2_ref,
         kkw_ref, ka_ref, rk_ref, lnw_ref, lnb_ref, o_ref, vfo_ref,
         s_ref, prev_ref, prevz_ref) = refs
        vf_ref = v0_ref = v2_ref = None

    C = CHUNK
    nchunk = r_ref.shape[0] // C
    nh = r_ref.shape[1] // HEAD

    @pl.when(pl.program_id(2) == 0)
    def _():
        s_ref[...] = jnp.zeros_like(s_ref)
        prev_ref[...] = jnp.zeros_like(prev_ref)
        prevz_ref[...] = jnp.zeros_like(prevz_ref)

    ri3 = lax.broadcasted_iota(jnp.int32, (C, 3 * C), 0)
    ci3 = lax.broadcasted_iota(jnp.int32, (C, 3 * C), 1) % C
    tri3 = (ri3 >= ci3).astype(BF16)
    ri2 = lax.broadcasted_iota(jnp.int32, (2 * C, 2 * C), 0)
    ci2 = lax.broadcasted_iota(jnp.int32, (2 * C, 2 * C), 1) % C
    keep2 = jnp.where(ri2 < C, ri2, ri2 - C + 1) > ci2
    row0 = lax.broadcasted_iota(jnp.int32, (C, 1), 0) == 0
    zeros_h = jnp.zeros((C, HEAD), BF16)

    def shift_lerp(cur, prev_row, mu):
        sh = jnp.where(row0, prev_row, pltpu.roll(cur, 1, axis=0))
        return cur + (sh - cur) * mu

    def chunk_rows(c):
        return pl.ds(pl.multiple_of(c * C, C), C)

    def prep(c, out):
        rows = chunk_rows(c)
        first = c == 0
        before = pl.ds(jnp.maximum(c * C - 1, 0), 1)

        def lerp(ref, carried, mu):
            return shift_lerp(ref[rows, :], jnp.where(first, carried, ref[before, :]), mu)

        zl = lerp(zl_ref, prevz_ref[0:1, :], mul_ref[...])
        z01 = zl[:, 0:LANES]
        wpre = w0_ref[...] + _mm(jnp.tanh(z01), w2_ref[...])
        apre = a0_ref[...] + _mm(z01, a2_ref[...])
        g = _mm(_sigmoid(zl[:, LANES:3 * LANES]), g2_ref[...])
        if has_vres:
            mpre = v0_ref[...] + _mm(zl[:, 2 * LANES:3 * LANES], v2_ref[...])
        yield
        w_log = -_softplus(-wpre) - 0.5
        logw = -jnp.exp(w_log)
        cum = jnp.dot(tri3, jnp.concatenate(_split3(logw), axis=0), preferred_element_type=F32)
        yield
        r = lerp(r_ref, prev_ref[0:1, :], mur_ref[...])
        k = lerp(k_ref, prev_ref[1:2, :], muk_ref[...])
        a = _sigmoid(apre)
        yield
        v = lerp(v_ref, prev_ref[2:3, :], muv_ref[...])
        if has_vres:
            v = v + (vf_ref[rows, :] - v) * _sigmoid(mpre)
        else:
            vfo_ref[rows, :] = v
        yield
        kk = k * kkw_ref[...]
        kk = kk / jnp.maximum(jnp.sqrt(_seg_sum(kk * kk)), 1e-12)
        yield
        k2 = k * (1.0 + (a - 1.0) * ka_ref[...])
        bb = kk * a
        bonus = _seg_sum(r * k2 * rk_ref[...]) * v
        yield
        p_in = jnp.exp(cum)
        p_ex = jnp.exp(cum - logw)
        p_inv = jnp.exp(-cum)
        p_end = p_in[C - 1:C, :]
        yield
        rt = r * p_in
        at = -kk * p_ex
        rt_b, at_b, v_b = rt.astype(BF16), at.astype(BF16), v.astype(BF16)
        yield
        bt = bb * p_inv
        kt = k2 * p_inv
        bk_t = jnp.transpose(jnp.concatenate([bt, kt], axis=0)).astype(BF16)
        yield
        bhat_b = (bt * p_end).astype(BF16)
        khat_b = (kt * p_end).astype(BF16)
        out["local"] = (rt_b, at_b, bk_t, v_b, bhat_b, khat_b, at, rt, p_end)
        out["post"] = (bonus, g)

    def heads_stage(local, out):
        rt_b, at_b, bk_t, v_b, bhat_b, khat_b, at, rt, p_end = local
        heads = range(nh)
        sls = [slice(hh * HEAD, (hh + 1) * HEAD) for hh in heads]
        sc_b, m, vh_b, d = [], [], [], []
        for sl in sls:
            ar = jnp.concatenate([at_b[:, sl], rt_b[:, sl]], axis=0)
            sc = jnp.dot(ar, bk_t[sl, :], preferred_element_type=F32)
            sc = jnp.where(keep2, sc, 0.0)
            sc_b.append(sc.astype(BF16))
            m.append(sc[:C, :C])
            vh_b.append(v_b[:, sl])
        yield
        for hh in heads:
            zv = jnp.concatenate([zeros_h, vh_b[hh]], axis=0)
            x_loc = jnp.dot(sc_b[hh][:C], zv, preferred_element_type=F32)
            d.append(jnp.concatenate([at[:, sls[hh]], x_loc], axis=-1))
        yield
        nstep = int(math.log2(C))
        for i in range(nstep):
            lo = (2 ** i // ROW_TILE) * ROW_TILE
            for hh in heads:
                m_b = m[hh].astype(BF16)[lo:, :C - lo]
                d_b = d[hh].astype(BF16)[:C - lo]
                if i + 1 < nstep:
                    rhs = jnp.concatenate([d_b, m[hh].astype(BF16)[:C - lo]], axis=-1)
                    prod = jnp.dot(m_b, rhs, preferred_element_type=F32)
                    upd, m_new = prod[:, :2 * HEAD], prod[:, 2 * HEAD:]
                    if lo:
                        m_new = jnp.concatenate([jnp.zeros((lo, C), F32), m_new], axis=0)
                    m[hh] = m_new
                else:
                    upd = jnp.dot(m_b, d_b, preferred_element_type=F32)
                if lo:
                    upd = jnp.concatenate([jnp.zeros((lo, 2 * HEAD), F32), upd], axis=0)
                d[hh] = d[hh] + upd
            yield
        o1, wz = [], []
        for hh in heads:
            gmat = jnp.concatenate(
                [d[hh].astype(BF16), jnp.concatenate([zeros_h, vh_b[hh]], axis=-1)], axis=0)
            o1.append(jnp.dot(sc_b[hh][C:], gmat, preferred_element_type=F32))
            bkh = jnp.concatenate([bhat_b[:, sls[hh]], khat_b[:, sls[hh]]], axis=0)
            wz.append(_mm_tn(gmat, bkh))
        yield
        ys = []
        for hh in heads:
            rbar = rt[:, sls[hh]] + o1[hh][:, :HEAD]
            st = s_ref[hh]
            ys.append(_mm_nt(rbar, st) + o1[hh][:, HEAD:])
            s_ref[hh] = st * p_end[:, sls[hh]] + _mm(st, wz[hh][:HEAD]) + wz[hh][HEAD:]
        out["y"] = jnp.concatenate(ys, axis=-1)

    def tail(c, y, post):
        bonus, g = post
        mean = _seg_sum(y) * (1.0 / HEAD)
        yc = y - mean
        yield
        var = _seg_sum(yc * yc) * (1.0 / HEAD)
        yield
        yn = yc * lax.rsqrt(var + LNX_EPS) * lnw_ref[...] + lnb_ref[...]
        o_ref[chunk_rows(c), :] = ((yn + bonus) * g).astype(o_ref.dtype)

    def run_interleaved(*gens):
        alive = list(gens)
        while alive:
            for gen in list(alive):
                if next(gen, "done") == "done":
                    alive.remove(gen)

    def body(i, carry):
        local, y_prev, post_prev, post_cur = carry
        out = {}
        run_interleaved(heads_stage(local, out),
                        tail(jnp.maximum(i - 1, 0), y_prev, post_prev),
                        prep(jnp.minimum(i + 1, nchunk - 1), out))
        return out["local"], out["y"], post_cur, out["post"]

    first = {}
    run_interleaved(prep(0, first))
    zeros_w = jnp.zeros((C, r_ref.shape[1]), F32)
    _, y_last, post_last, _ = lax.fori_loop(
        0, nchunk, body, (first["local"], zeros_w, (zeros_w, zeros_w), first["post"]))
    run_interleaved(tail(nchunk - 1, y_last, post_last))

    last = pl.ds(r_ref.shape[0] - 1, 1)
    prev_ref[0:1, :] = r_ref[last, :]
    prev_ref[1:2, :] = k_ref[last, :]
    prev_ref[2:3, :] = v_ref[last, :]
    prevz_ref[0:1, :] = zl_ref[last, :]


def _rwkv(zm, zl, vfirst, prm, batch, seq, hg=HEAD_GROUP, ts=RWKV_ROWS):
    T = zm.shape[0]
    W = hg * HEAD
    ng = D_RWKV // W
    nt = seq // ts
    col0 = 2 * D_LRU // W
    has_vres = vfirst is not None

    def col(off):
        return pl.BlockSpec((ts, W), lambda b, g, t: (b * nt + t, off + g))

    vecg = pl.BlockSpec((1, W), lambda b, g, t: (0, g))
    vec_k = pl.BlockSpec((1, W), lambda b, g, t: (0, ng + g))
    vec_v = pl.BlockSpec((1, W), lambda b, g, t: (0, 2 * ng + g))
    vec_l = pl.BlockSpec((1, D_LORA), lambda b, g, t: (0, 0))

    def lora(rows):
        return pl.BlockSpec((rows, W), lambda b, g, t: (0, g))

    in_specs = [col(col0), col(col0 + ng), col(col0 + 2 * ng),
                pl.BlockSpec((ts, D_LORA), lambda b, g, t: (b * nt + t, 0))]
    args = [zm, zm, zm, zl]
    if has_vres:
        in_specs.append(col(0))
        args.append(vfirst)
    in_specs += [vecg, vec_k, vec_v, vec_l, vecg, lora(LANES), vecg, lora(LANES), lora(2 * LANES)]
    args += [prm["mu_rkv"], prm["mu_rkv"], prm["mu_rkv"], prm["mu_lora"],
             prm["w0"], prm["w2"], prm["a0"], prm["a2"], prm["g2"]]
    if has_vres:
        in_specs += [vecg, lora(LANES)]
        args += [prm["v0"], prm["v2"]]
    in_specs += [vecg] * 5
    args += [prm["kk"], prm["ka"], prm["rk"], prm["lnw"], prm["lnb"]]

    if has_vres:
        out_specs = col(0)
        out_shape = jax.ShapeDtypeStruct((T, D_RWKV), BF16)
    else:
        out_specs = [col(0), col(0)]
        out_shape = [jax.ShapeDtypeStruct((T, D_RWKV), BF16),
                     jax.ShapeDtypeStruct((T, D_RWKV), F32)]
    nbytes = 2 * ts * (5 * W * 4 + D_LORA * 4 + W * 2) + hg * HEAD * HEAD * 4
    res = pl.pallas_call(
        functools.partial(_rwkv_kernel, has_vres),
        grid=(batch, ng, nt),
        in_specs=in_specs,
        out_specs=out_specs,
        out_shape=out_shape,
        scratch_shapes=[pltpu.VMEM((hg, HEAD, HEAD), F32),
                        pltpu.VMEM((SUBLANES, W), F32),
                        pltpu.VMEM((SUBLANES, D_LORA), F32)],
        compiler_params=pltpu.CompilerParams(
            dimension_semantics=("arbitrary", "arbitrary", "arbitrary"),
            vmem_limit_bytes=_vmem_limit(nbytes)),
        name="rwkv7",
    )(*args)
    if has_vres:
        return res, vfirst
    return res[0], res[1]


def _ffn_kernel(tiles_per_seq, h_ref, g_ref, wg_ref, wu_ref, cw_ref, cb_ref, wd_ref,
                o_ref, u_ref, tail_ref):
    i = pl.program_id(0)
    j = pl.program_id(1)

    @pl.when(j == 0)
    def _():
        h = h_ref[...]
        u_ref[...] = _rms(h, g_ref[...]).astype(BF16)
        o_ref[...] = h

    @pl.when(i % tiles_per_seq == 0)
    def _():
        tail_ref[j] = jnp.zeros(tail_ref.shape[1:], F32)

    u = u_ref[...]
    gate = jnp.dot(u, wg_ref[...], preferred_element_type=F32)
    tm = gate.shape[0]
    tail = tail_ref[j]
    cw = cw_ref[...]
    conv = gate * cw[FFN_CONV - 1:FFN_CONV] + cb_ref[...]
    for d in range(1, FFN_CONV):
        conv = conv + _shift_rows(gate, d, tail) * cw[FFN_CONV - 1 - d:FFN_CONV - d]
    tail_ref[j] = gate[tm - SUBLANES:]
    up = jnp.dot(u, wu_ref[...], preferred_element_type=F32)
    act = (_gelu(conv) * up).astype(BF16)
    o_ref[...] += jnp.dot(act, wd_ref[...], preferred_element_type=F32)


def _ffn(h, g, wg, wu, cw, cb, wd, layer, seq, tm=512, tf=1024):
    T = h.shape[0]
    nf = D_FF // tf
    nbytes = (2 * (2 * tm * D_MODEL * 4 + 3 * D_MODEL * tf * 2) + tm * D_MODEL * 2
              + nf * SUBLANES * tf * 4 + 6 * tm * tf * 4)
    return pl.pallas_call(
        functools.partial(_ffn_kernel, seq // tm),
        grid=(T // tm, nf),
        in_specs=[
            pl.BlockSpec((tm, D_MODEL), lambda i, j: (i, 0)),
            pl.BlockSpec((1, D_MODEL), lambda i, j: (0, 0)),
            pl.BlockSpec((None, D_MODEL, tf), lambda i, j: (layer, 0, j)),
            pl.BlockSpec((None, D_MODEL, tf), lambda i, j: (layer, 0, j)),
            pl.BlockSpec((FFN_CONV, tf), lambda i, j: (0, j)),
            pl.BlockSpec((1, tf), lambda i, j: (0, j)),
            pl.BlockSpec((None, tf, D_MODEL), lambda i, j: (layer, j, 0)),
        ],
        out_specs=pl.BlockSpec((tm, D_MODEL), lambda i, j: (i, 0)),
        out_shape=jax.ShapeDtypeStruct((T, D_MODEL), F32),
        scratch_shapes=[pltpu.VMEM((tm, D_MODEL), BF16),
                        pltpu.VMEM((nf, SUBLANES, tf), F32)],
        compiler_params=pltpu.CompilerParams(
            dimension_semantics=("arbitrary", "arbitrary"),
            vmem_limit_bytes=_vmem_limit(nbytes)),
        name="ffn",
    )(h, g, wg, wu, cw, cb, wd)


def _ple_kernel(final, h_ref, p_ref, g_ref, wg_ref, wp_ref, gp_ref, gf_ref, o_ref):
    h = h_ref[...]
    u = _rms(h, g_ref[...]).astype(BF16)
    gate = _sigmoid(jnp.dot(u, wg_ref[...], preferred_element_type=F32))
    proj = jnp.dot(p_ref[...].astype(BF16), wp_ref[...], preferred_element_type=F32)
    out = h + _rms(gate * proj, gp_ref[...])
    if final:
        out = _rms(out, gf_ref[...])
    o_ref[...] = out


def _ple(h, p, g, wg, layer, wp, gp, gf, final, tm=512):
    T = h.shape[0]
    vec = pl.BlockSpec((1, D_MODEL), lambda i: (0, 0))
    nbytes = 2 * (2 * tm * D_MODEL * 4 + tm * D_PLE * 4 + D_MODEL * D_MODEL * 2
                  + D_PLE * D_MODEL * 2) + 4 * tm * D_MODEL * 4
    return pl.pallas_call(
        functools.partial(_ple_kernel, final),
        grid=(T // tm,),
        in_specs=[
            pl.BlockSpec((tm, D_MODEL), lambda i: (i, 0)),
            pl.BlockSpec((tm, D_PLE), lambda i: (i, 0)),
            vec,
            pl.BlockSpec((None, D_MODEL, D_MODEL), lambda i: (layer, 0, 0)),
            pl.BlockSpec((D_PLE, D_MODEL), lambda i: (0, 0)),
            vec, vec,
        ],
        out_specs=pl.BlockSpec((tm, D_MODEL), lambda i: (i, 0)),
        out_shape=jax.ShapeDtypeStruct((T, D_MODEL), F32),
        compiler_params=pltpu.CompilerParams(
            dimension_semantics=("arbitrary",),
            vmem_limit_bytes=_vmem_limit(nbytes)),
        name="ple",
    )(h, p, g, wg, wp, gp, gf)


def _row(v):
    return v.reshape(1, -1).astype(F32)


def _pad_rows(w, top, total):
    return jnp.pad(w, ((top, total - top - w.shape[0]), (0, 0)))


def kernel(x, p, ln_mix, w_in, w_in_vres, mu_shift, mu_shift_vres, conv_a_w, conv_a_b, lru_wx, lru_bx, lru_wa, lru_ba, lru_lambda, lru_norm, rwkv_w0, rwkv_w2, rwkv_a0, rwkv_a2, rwkv_v0, rwkv_v2, rwkv_g2, rwkv_kk, rwkv_ka, rwkv_rk, rwkv_lnx_w, rwkv_lnx_b, w_o, ln_ffn, w_gate, w_up, conv_f_w, conv_f_b, w_down, ln_ple, w_ple_gate, w_ple_proj, ln_ple_post, ln_final):
    batch, seq, _ = x.shape
    depth = w_in.shape[0]
    T = batch * seq
    h = x.reshape(T, D_MODEL)
    n_lora = LORA_W + LORA_A + LORA_G
    vfirst = None
    w_in_b, w_o_b, w_gate_b, w_up_b, w_down_b, w_ple_gate_b = (
        w.astype(BF16) for w in (w_in[:, :, :D_MAIN], w_o, w_gate, w_up, w_down, w_ple_gate))
    for i in range(depth):
        lora_cols = [w_in[i][:, D_MAIN:]]
        mu_l = [mu_shift[i][3 * D_RWKV:]]
        if i > 0:
            lora_cols.append(w_in_vres[i - 1])
            mu_l.append(mu_shift_vres[i - 1])
        w_lora = jnp.concatenate(lora_cols, axis=1)
        w_lora = jnp.pad(w_lora, ((0, 0), (0, D_LORA - w_lora.shape[1]))).astype(BF16)
        mu_lora = jnp.concatenate(mu_l, axis=0)
        mu_lora = jnp.pad(mu_lora, (0, D_LORA - mu_lora.shape[0]))

        zm, zl = _inproj(h, _row(ln_mix[i]), w_in_b, i, w_lora)

        prm = {
            "mu_rkv": _row(mu_shift[i][:3 * D_RWKV]),
            "mu_lora": _row(mu_lora),
            "w0": _row(rwkv_w0[i]),
            "w2": _pad_rows(rwkv_w2[i], 0, LANES),
            "a0": _row(rwkv_a0[i]),
            "a2": _pad_rows(rwkv_a2[i], LORA_W, LANES),
            "g2": _pad_rows(rwkv_g2[i], 0, 2 * LANES),
            "kk": _row(rwkv_kk[i]), "ka": _row(rwkv_ka[i]), "rk": _row(rwkv_rk[i]),
            "lnw": _row(rwkv_lnx_w[i]), "lnb": _row(rwkv_lnx_b[i]),
        }
        if i > 0:
            prm["v0"] = _row(rwkv_v0[i - 1])
            prm["v2"] = _pad_rows(rwkv_v2[i - 1], n_lora - 2 * LANES, LANES)
        out_b, vfirst = _rwkv(zm, zl, vfirst, prm, batch, seq)

        h = _lru_oproj(zm, out_b, h, conv_a_w[i], _row(conv_a_b[i]), lru_wx[i].astype(BF16),
                       _row(lru_bx[i]), lru_wa[i].astype(BF16), _row(lru_ba[i]),
                       _row(lru_lambda[i]), _row(lru_norm[i]), w_o_b, i, batch, seq)
        h = _ffn(h, _row(ln_ffn[i]), w_gate_b, w_up_b, conv_f_w[i], _row(conv_f_b[i]),
                 w_down_b, i, seq)
        h = _ple(h, p[i].reshape(T, D_PLE), _row(ln_ple[i]), w_ple_gate_b, i,
                 w_ple_proj[i].astype(BF16), _row(ln_ple_post[i]), _row(ln_final),
                 final=(i == depth - 1))
    return h.reshape(batch, seq, D_MODEL)
```

```python
import functools
import math

import jax
import jax.numpy as jnp
from jax import lax
from jax.experimental import pallas as pl
from jax.experimental.pallas import tpu as pltpu

F32 = jnp.float32
BF16 = jnp.bfloat16

D_MODEL = 2048
D_LRU = 1024
D_RWKV = 1024
LRU_HEADS = 4
LRU_BLOCK = 256
LRU_CONV = 4
LRU_C = 8.0
HEAD = 64
LORA_W = 64
LORA_A = 64
LORA_G = 160
D_MAIN = 2 * D_LRU + 3 * D_RWKV
D_LORA = 384
D_FF = 3 * D_MODEL
FFN_CONV = 3
D_PLE = 256
RMS_EPS = 1e-6
LNX_EPS = 64e-5

V7X_VMEM_BYTES = 64 * 1024 * 1024
SUBLANES = 8
LANES = 128
ROW_TILE = 16

CHUNK = 64
PAIR = 2 * HEAD
LRU_ROWS = 256
HEAD_GROUP = 16
RWKV_ROWS = 1024

VMEM_TEMPORARIES_BYTES = 16 * 1024 * 1024
VMEM_UNSCOPED_BYTES = 4 * 1024 * 1024


def _vmem_limit(nbytes):
    return int(min(V7X_VMEM_BYTES - VMEM_UNSCOPED_BYTES, nbytes + VMEM_TEMPORARIES_BYTES))


def _rms(x, g):
    return x * lax.rsqrt(jnp.mean(x * x, axis=-1, keepdims=True) + RMS_EPS) * g


def _gelu(x):
    c = math.sqrt(2.0 / math.pi)
    return 0.5 * x * (1.0 + jnp.tanh(c * (x + 0.044715 * (x * x * x))))


def _sigmoid(x):
    return 1.0 / (1.0 + jnp.exp(-x))


def _softplus(x):
    return jnp.maximum(x, 0.0) + jnp.log1p(jnp.exp(-jnp.abs(x)))


def _shift_rows(x, d, prev8):
    rolled = pltpu.roll(x, d, axis=0)
    prev = pltpu.roll(prev8, d, axis=0)
    row = lax.broadcasted_iota(jnp.int32, prev8.shape, 0)
    top = jnp.where(row < d, prev, rolled[:SUBLANES])
    return jnp.concatenate([top, rolled[SUBLANES:]], axis=0)


def _inproj_kernel(x_ref, g_ref, wm_ref, wl_ref, zm_ref, zl_ref, u_ref):
    @pl.when(pl.program_id(1) == 0)
    def _():
        u_ref[...] = _rms(x_ref[...], g_ref[...]).astype(BF16)
        zl_ref[...] = jnp.dot(u_ref[...], wl_ref[...], preferred_element_type=F32)

    zm_ref[...] = jnp.dot(u_ref[...], wm_ref[...], preferred_element_type=F32)


def _inproj(h, g, w_in, layer, w_lora, tm=1024, tn=1280):
    T = h.shape[0]
    nbytes = 2 * (tm * D_MODEL * 4 + D_MODEL * tn * 2 + D_MODEL * D_LORA * 2
                  + tm * tn * 4 + tm * D_LORA * 4) + tm * D_MODEL * 2
    return pl.pallas_call(
        _inproj_kernel,
        grid=(T // tm, D_MAIN // tn),
        in_specs=[
            pl.BlockSpec((tm, D_MODEL), lambda i, j: (i, 0)),
            pl.BlockSpec((1, D_MODEL), lambda i, j: (0, 0)),
            pl.BlockSpec((None, D_MODEL, tn), lambda i, j: (layer, 0, j)),
            pl.BlockSpec((D_MODEL, D_LORA), lambda i, j: (0, 0)),
        ],
        out_specs=[
            pl.BlockSpec((tm, tn), lambda i, j: (i, j)),
            pl.BlockSpec((tm, D_LORA), lambda i, j: (i, 0)),
        ],
        out_shape=[
            jax.ShapeDtypeStruct((T, D_MAIN), F32),
            jax.ShapeDtypeStruct((T, D_LORA), F32),
        ],
        scratch_shapes=[pltpu.VMEM((tm, D_MODEL), BF16)],
        compiler_params=pltpu.CompilerParams(
            dimension_semantics=("arbitrary", "arbitrary"),
            vmem_limit_bytes=_vmem_limit(nbytes)),
        name="inproj",
    )(h, g, w_in, w_lora)


def _lru_rows(xb_ref, yb_ref, rows, tail, carry, seq_start, cw, cb, wx_ref, bx, wa_ref, ba,
              sp_lam, nrm, between):
    nrows = rows.stop - rows.start
    ngroup = nrows // SUBLANES
    sub = lax.broadcasted_iota(jnp.int32, (1, SUBLANES, 1), 1)
    ys, tails, carries, ss = [], [], [], 0.0
    for hd in range(LRU_HEADS):
        cols = slice(hd * LRU_BLOCK, (hd + 1) * LRU_BLOCK)
        x = xb_ref[rows, cols]
        xc = x * cw[LRU_CONV - 1:LRU_CONV, cols] + cb[:, cols]
        for d in range(1, LRU_CONV):
            xc = xc + _shift_rows(x, d, tail[:, cols]) * cw[LRU_CONV - 1 - d:LRU_CONV - d, cols]
        tails.append(x[nrows - SUBLANES:])

        xcb = xc.astype(BF16)
        gate_x = _sigmoid(jnp.dot(xcb, wx_ref[hd], preferred_element_type=F32) + bx[:, cols])
        gate_a = _sigmoid(jnp.dot(xcb, wa_ref[hd], preferred_element_type=F32) + ba[:, cols])
        log_a = (-LRU_C) * gate_a * sp_lam[:, cols]
        a = jnp.exp(log_a)
        mult = jnp.sqrt(1.0 - a * a)
        if seq_start is not None:
            row = lax.broadcasted_iota(jnp.int32, (nrows, 1), 0)
            mult = jnp.where(jnp.logical_and(row == 0, seq_start), 1.0, mult)
        b = xc * gate_x * mult

        a = a.reshape(ngroup, SUBLANES, LRU_BLOCK)
        b = b.reshape(ngroup, SUBLANES, LRU_BLOCK)
        d = 1
        while d < SUBLANES:
            keep = sub >= d
            a_sh = jnp.where(keep, pltpu.roll(a, d, axis=1), 1.0)
            b_sh = jnp.where(keep, pltpu.roll(b, d, axis=1), 0.0)
            b = a * b_sh + b
            a = a * a_sh
            d *= 2
        hcar = carry[:, cols]
        hs = []
        for grp in range(ngroup):
            hg = a[grp] * hcar + b[grp]
            hs.append(hg)
            hcar = hg[SUBLANES - 1:SUBLANES]
        carries.append(hcar)

        y = jnp.concatenate(hs, axis=0) * _gelu(yb_ref[rows, cols])
        ss = ss + jnp.sum(y * y, axis=-1, keepdims=True)
        ys.append(y)
        between(hd)
    scale = lax.rsqrt(ss * (1.0 / D_LRU) + RMS_EPS)
    out = jnp.concatenate(ys, axis=-1) * scale * nrm
    return out.astype(BF16), jnp.concatenate(tails, axis=-1), jnp.concatenate(carries, axis=-1)


def _lru_oproj_kernel(xb_ref, yb_ref, ob_ref, h_ref, cw_ref, cb_ref, wx_ref, bx_ref, wa_ref,
                      ba_ref, lam_ref, nrm_ref, woa_ref, wob_ref, o_ref, tail_ref, carry_ref):
    t = pl.program_id(1)

    @pl.when(t == 0)
    def _():
        tail_ref[...] = jnp.zeros_like(tail_ref)
        carry_ref[...] = jnp.zeros_like(carry_ref)

    sp_lam = _softplus(-lam_ref[...])
    tail, carry = tail_ref[...], carry_ref[0:1, :]
    ts = xb_ref.shape[0]
    ncol = D_MODEL // LRU_HEADS
    pieces = {}

    def project(name, lhs, w_ref):
        def step(hd):
            cols = slice(hd * ncol, (hd + 1) * ncol)
            pieces.setdefault(name, []).append(
                jnp.dot(lhs, w_ref[:, cols], preferred_element_type=F32))
        return step

    between = project("b", ob_ref[...], wob_ref)
    for r0 in range(0, ts, LRU_ROWS):
        out_a, tail, carry = _lru_rows(
            xb_ref, yb_ref, slice(r0, r0 + LRU_ROWS), tail, carry, (t == 0) if r0 == 0 else None,
            cw_ref[...], cb_ref[...], wx_ref, bx_ref[...], wa_ref, ba_ref[...], sp_lam,
            nrm_ref[...], between)
        between = project(("a", r0), out_a, woa_ref)
    for hd in range(LRU_HEADS):
        between(hd)
    tail_ref[...] = tail
    carry_ref[0:1, :] = carry
    acc_a = jnp.concatenate(
        [jnp.concatenate(pieces["a", r0], axis=-1) for r0 in range(0, ts, LRU_ROWS)], axis=0)
    o_ref[...] = h_ref[...] + jnp.concatenate(pieces["b"], axis=-1) + acc_a


def _lru_oproj(zm, out_b, h, cw, cb, wx, bx, wa, ba, lam, nrm, wo, layer, batch, seq, ts=512):
    T = zm.shape[0]
    nt = seq // ts
    vec = pl.BlockSpec((1, D_LRU), lambda b, t: (0, 0))
    mat = pl.BlockSpec((LRU_HEADS, LRU_BLOCK, LRU_BLOCK), lambda b, t: (0, 0, 0))

    def rows(width, col):
        return pl.BlockSpec((ts, width), lambda b, t: (b * nt + t, col))

    nbytes = (2 * (2 * ts * D_LRU * 4 + ts * D_RWKV * 2 + 2 * ts * D_MODEL * 4
                   + 2 * D_LRU * D_MODEL * 2) + 16 * LRU_ROWS * D_LRU * 4 + 2 * ts * D_MODEL * 4)
    return pl.pallas_call(
        _lru_oproj_kernel,
        grid=(batch, nt),
        in_specs=[
            rows(D_LRU, 0), rows(D_LRU, 1), rows(D_RWKV, 0), rows(D_MODEL, 0),
            pl.BlockSpec((LRU_CONV, D_LRU), lambda b, t: (0, 0)),
            vec, mat, vec, mat, vec, vec, vec,
            pl.BlockSpec((None, D_LRU, D_MODEL), lambda b, t: (layer, 0, 0)),
            pl.BlockSpec((None, D_RWKV, D_MODEL), lambda b, t: (layer, 1, 0)),
        ],
        out_specs=rows(D_MODEL, 0),
        out_shape=jax.ShapeDtypeStruct((T, D_MODEL), F32),
        scratch_shapes=[pltpu.VMEM((SUBLANES, D_LRU), F32),
                        pltpu.VMEM((SUBLANES, D_LRU), F32)],
        compiler_params=pltpu.CompilerParams(
            dimension_semantics=("arbitrary", "arbitrary"),
            vmem_limit_bytes=_vmem_limit(nbytes)),
        name="lru_oproj",
    )(zm, zm, out_b, h, cw, cb, wx, bx, wa, ba, lam, nrm, wo, wo)


def _mm(a, b):
    return jnp.dot(a.astype(BF16), b.astype(BF16), preferred_element_type=F32)


def _mm_nt(a, b):
    return lax.dot_general(a.astype(BF16), b.astype(BF16), (((1,), (1,)), ((), ())),
                           preferred_element_type=F32)


def _mm_tn(a, b):
    return lax.dot_general(a.astype(BF16), b.astype(BF16), (((0,), (0,)), ((), ())),
                           preferred_element_type=F32)


def _split3(x):
    hi = x.astype(BF16)
    r1 = x - hi.astype(F32)
    mid = r1.astype(BF16)
    lo = (r1 - mid.astype(F32)).astype(BF16)
    return hi, mid, lo


def _seg_sum(x):
    lane_lo = lax.broadcasted_iota(jnp.int32, (1, PAIR), 1) < HEAD
    out = []
    for p in range(x.shape[1] // PAIR):
        t = x[:, p * PAIR:(p + 1) * PAIR]
        s0 = jnp.sum(jnp.where(lane_lo, t, 0.0), axis=-1, keepdims=True)
        s1 = jnp.sum(jnp.where(lane_lo, 0.0, t), axis=-1, keepdims=True)
        out.append(jnp.where(lane_lo, s0, s1))
    return jnp.concatenate(out, axis=-1)


def _rwkv_kernel(has_vres, *refs):
    if has_vres:
        (r_ref, k_ref, v_ref, zl_ref, vf_ref, mur_ref, muk_ref, muv_ref, mul_ref,
         w0_ref, w2_ref, a0_ref, a2_ref, g2_ref, v0_ref, v2_ref,
         kkw_ref, ka_ref, rk_ref, lnw_ref, lnb_ref, o_ref, s_ref, prev_ref, prevz_ref) = refs
        vfo_ref = None
    else:
        (r_ref, k_ref, v_ref, zl_ref, mur_ref, muk_ref, muv_ref, mul_ref,
         w0_ref, w2_ref, a0_ref, a2_ref, g2_ref,
         kkw_ref, ka_ref, rk_ref, lnw_ref, lnb_ref, o_ref, vfo_ref,
         s_ref, prev_ref, prevz_ref) = refs
        vf_ref = v0_ref = v2_ref = None

    C = CHUNK
    nchunk = r_ref.shape[0] // C
    nh = r_ref.shape[1] // HEAD

    @pl.when(pl.program_id(2) == 0)
    def _():
        s_ref[...] = jnp.zeros_like(s_ref)
        prev_ref[...] = jnp.zeros_like(prev_ref)
        prevz_ref[...] = jnp.zeros_like(prevz_ref)

    ri3 = lax.broadcasted_iota(jnp.int32, (C, 3 * C), 0)
    ci3 = lax.broadcasted_iota(jnp.int32, (C, 3 * C), 1) % C
    tri3 = (ri3 >= ci3).astype(BF16)
    ri2 = lax.broadcasted_iota(jnp.int32, (2 * C, 2 * C), 0)
    ci2 = lax.broadcasted_iota(jnp.int32, (2 * C, 2 * C), 1) % C
    keep2 = jnp.where(ri2 < C, ri2, ri2 - C + 1) > ci2
    row0 = lax.broadcasted_iota(jnp.int32, (C, 1), 0) == 0
    zeros_h = jnp.zeros((C, HEAD), BF16)

    def shift_lerp(cur, prev_row, mu):
        sh = jnp.where(row0, prev_row, pltpu.roll(cur, 1, axis=0))
        return cur + (sh - cur) * mu

    def chunk_rows(c):
        return pl.ds(pl.multiple_of(c * C, C), C)

    def prep(c, out):
        rows = chunk_rows(c)
        first = c == 0
        before = pl.ds(jnp.maximum(c * C - 1, 0), 1)

        def lerp(ref, carried, mu):
            return shift_lerp(ref[rows, :], jnp.where(first, carried, ref[before, :]), mu)

        zl = lerp(zl_ref, prevz_ref[0:1, :], mul_ref[...])
        z01 = zl[:, 0:LANES]
        wpre = w0_ref[...] + _mm(jnp.tanh(z01), w2_ref[...])
        apre = a0_ref[...] + _mm(z01, a2_ref[...])
        g = _mm(_sigmoid(zl[:, LANES:3 * LANES]), g2_ref[...])
        if has_vres:
            mpre = v0_ref[...] + _mm(zl[:, 2 * LANES:3 * LANES], v2_ref[...])
        yield
        w_log = -_softplus(-wpre) - 0.5
        logw = -jnp.exp(w_log)
        cum = jnp.dot(tri3, jnp.concatenate(_split3(logw), axis=0), preferred_element_type=F32)
        yield
        r = lerp(r_ref, prev_ref[0:1, :], mur_ref[...])
        k = lerp(k_ref, prev_ref[1:2, :], muk_ref[...])
        a = _sigmoid(apre)
        yield
        v = lerp(v_ref, prev_ref[2:3, :], muv_ref[...])
        if has_vres:
            v = v + (vf_ref[rows, :] - v) * _sigmoid(mpre)
        else:
            vfo_ref[rows, :] = v
        yield
        kk = k * kkw_ref[...]
        kk = kk / jnp.maximum(jnp.sqrt(_seg_sum(kk * kk)), 1e-12)
        yield
        k2 = k * (1.0 + (a - 1.0) * ka_ref[...])
        bb = kk * a
        bonus = _seg_sum(r * k2 * rk_ref[...]) * v
        yield
        p_in = jnp.exp(cum)
        p_ex = jnp.exp(cum - logw)
        p_inv = jnp.exp(-cum)
        p_end = p_in[C - 1:C, :]
        yield
        rt = r * p_in
        at = -kk * p_ex
        rt_b, at_b, v_b = rt.astype(BF16), at.astype(BF16), v.astype(BF16)
        yield
        bt = bb * p_inv
        kt = k2 * p_inv
        bk_t = jnp.transpose(jnp.concatenate([bt, kt], axis=0)).astype(BF16)
        yield
        bhat_b = (bt * p_end).astype(BF16)
        khat_b = (kt * p_end).astype(BF16)
        out["local"] = (rt_b, at_b, bk_t, v_b, bhat_b, khat_b, at, rt, p_end)
        out["post"] = (bonus, g)

    def heads_stage(local, out):
        rt_b, at_b, bk_t, v_b, bhat_b, khat_b, at, rt, p_end = local
        heads = range(nh)
        sls = [slice(hh * HEAD, (hh + 1) * HEAD) for hh in heads]
        sc_b, m, vh_b, d = [], [], [], []
        for sl in sls:
            ar = jnp.concatenate([at_b[:, sl], rt_b[:, sl]], axis=0)
            sc = jnp.dot(ar, bk_t[sl, :], preferred_element_type=F32)
            sc = jnp.where(keep2, sc, 0.0)
            sc_b.append(sc.astype(BF16))
            m.append(sc[:C, :C])
            vh_b.append(v_b[:, sl])
        yield
        for hh in heads:
            zv = jnp.concatenate([zeros_h, vh_b[hh]], axis=0)
            x_loc = jnp.dot(sc_b[hh][:C], zv, preferred_element_type=F32)
            d.append(jnp.concatenate([at[:, sls[hh]], x_loc], axis=-1))
        yield
        nstep = int(math.log2(C))
        for i in range(nstep):
            lo = (2 ** i // ROW_TILE) * ROW_TILE
            for hh in heads:
                m_b = m[hh].astype(BF16)[lo:, :C - lo]
                d_b = d[hh].astype(BF16)[:C - lo]
                if i + 1 < nstep:
                    rhs = jnp.concatenate([d_b, m[hh].astype(BF16)[:C - lo]], axis=-1)
                    prod = jnp.dot(m_b, rhs, preferred_element_type=F32)
                    upd, m_new = prod[:, :2 * HEAD], prod[:, 2 * HEAD:]
                    if lo:
                        m_new = jnp.concatenate([jnp.zeros((lo, C), F32), m_new], axis=0)
                    m[hh] = m_new
                else:
                    upd = jnp.dot(m_b, d_b, preferred_element_type=F32)
                if lo:
                    upd = jnp.concatenate([jnp.zeros((lo, 2 * HEAD), F32), upd], axis=0)
                d[hh] = d[hh] + upd
            yield
        o1, wz = [], []
        for hh in heads:
            gmat = jnp.concatenate(
                [d[hh].astype(BF16), jnp.concatenate([zeros_h, vh_b[hh]], axis=-1)], axis=0)
            o1.append(jnp.dot(sc_b[hh][C:], gmat, preferred_element_type=F32))
            bkh = jnp.concatenate([bhat_b[:, sls[hh]], khat_b[:, sls[hh]]], axis=0)
            wz.append(_mm_tn(gmat, bkh))
        yield
        ys = []
        for hh in heads:
            rbar = rt[:, sls[hh]] + o1[hh][:, :HEAD]
            st = s_ref[hh]
            ys.append(_mm_nt(rbar, st) + o1[hh][:, HEAD:])
            s_ref[hh] = st * p_end[:, sls[hh]] + _mm(st, wz[hh][:HEAD]) + wz[hh][HEAD:]
        out["y"] = jnp.concatenate(ys, axis=-1)

    def tail(c, y, post):
        bonus, g = post
        mean = _seg_sum(y) * (1.0 / HEAD)
        yc = y - mean
        yield
        var = _seg_sum(yc * yc) * (1.0 / HEAD)
        yield
        yn = yc * lax.rsqrt(var + LNX_EPS) * lnw_ref[...] + lnb_ref[...]
        o_ref[chunk_rows(c), :] = ((yn + bonus) * g).astype(o_ref.dtype)

    def run_interleaved(*gens):
        alive = list(gens)
        while alive:
            for gen in list(alive):
                if next(gen, "done") == "done":
                    alive.remove(gen)

    def body(i, carry):
        local, y_prev, post_prev, post_cur = carry
        out = {}
        run_interleaved(heads_stage(local, out),
                        tail(jnp.maximum(i - 1, 0), y_prev, post_prev),
                        prep(jnp.minimum(i + 1, nchunk - 1), out))
        return out["local"], out["y"], post_cur, out["post"]

    first = {}
    run_interleaved(prep(0, first))
    zeros_w = jnp.zeros((C, r_ref.shape[1]), F32)
    _, y_last, post_last, _ = lax.fori_loop(
        0, nchunk, body, (first["local"], zeros_w, (zeros_w, zeros_w), first["post"]))
    run_interleaved(tail(nchunk - 1, y_last, post_last))

    last = pl.ds(r_ref.shape[0] - 1, 1)
    prev_ref[0:1, :] = r_ref[last, :]
    prev_ref[1:2, :] = k_ref[last, :]
    prev_ref[2:3, :] = v_ref[last, :]
    prevz_ref[0:1, :] = zl_ref[last, :]


def _rwkv(zm, zl, vfirst, prm, batch, seq, hg=HEAD_GROUP, ts=RWKV_ROWS):
    T = zm.shape[0]
    W = hg * HEAD
    ng = D_RWKV // W
    nt = seq // ts
    col0 = 2 * D_LRU // W
    has_vres = vfirst is not None

    def col(off):
        return pl.BlockSpec((ts, W), lambda b, g, t: (b * nt + t, off + g))

    vecg = pl.BlockSpec((1, W), lambda b, g, t: (0, g))
    vec_k = pl.BlockSpec((1, W), lambda b, g, t: (0, ng + g))
    vec_v = pl.BlockSpec((1, W), lambda b, g, t: (0, 2 * ng + g))
    vec_l = pl.BlockSpec((1, D_LORA), lambda b, g, t: (0, 0))

    def lora(rows):
        return pl.BlockSpec((rows, W), lambda b, g, t: (0, g))

    in_specs = [col(col0), col(col0 + ng), col(col0 + 2 * ng),
                pl.BlockSpec((ts, D_LORA), lambda b, g, t: (b * nt + t, 0))]
    args = [zm, zm, zm, zl]
    if has_vres:
        in_specs.append(col(0))
        args.append(vfirst)
    in_specs += [vecg, vec_k, vec_v, vec_l, vecg, lora(LANES), vecg, lora(LANES), lora(2 * LANES)]
    args += [prm["mu_rkv"], prm["mu_rkv"], prm["mu_rkv"], prm["mu_lora"],
             prm["w0"], prm["w2"], prm["a0"], prm["a2"], prm["g2"]]
    if has_vres:
        in_specs += [vecg, lora(LANES)]
        args += [prm["v0"], prm["v2"]]
    in_specs += [vecg] * 5
    args += [prm["kk"], prm["ka"], prm["rk"], prm["lnw"], prm["lnb"]]

    if has_vres:
        out_specs = col(0)
        out_shape = jax.ShapeDtypeStruct((T, D_RWKV), BF16)
    else:
        out_specs = [col(0), col(0)]
        out_shape = [jax.ShapeDtypeStruct((T, D_RWKV), BF16),
                     jax.ShapeDtypeStruct((T, D_RWKV), F32)]
    nbytes = 2 * ts * (5 * W * 4 + D_LORA * 4 + W * 2) + hg * HEAD * HEAD * 4
    res = pl.pallas_call(
        functools.partial(_rwkv_kernel, has_vres),
        grid=(batch, ng, nt),
        in_specs=in_specs,
        out_specs=out_specs,
        out_shape=out_shape,
        scratch_shapes=[pltpu.VMEM((hg, HEAD, HEAD), F32),
                        pltpu.VMEM((SUBLANES, W), F32),
                        pltpu.VMEM((SUBLANES, D_LORA), F32)],
        compiler_params=pltpu.CompilerParams(
            dimension_semantics=("arbitrary", "arbitrary", "arbitrary"),
            vmem_limit_bytes=_vmem_limit(nbytes)),
        name="rwkv7",
    )(*args)
    if has_vres:
        return res, vfirst
    return res[0], res[1]


def _ffn_kernel(tiles_per_seq, h_ref, g_ref, wg_ref, wu_ref, cw_ref, cb_ref, wd_ref,
                o_ref, u_ref, tail_ref):
    i = pl.program_id(0)
    j = pl.program_id(1)

    @pl.when(j == 0)
    def _():
        h = h_ref[...]
        u_ref[...] = _rms(h, g_ref[...]).astype(BF16)
        o_ref[...] = h

    @pl.when(i % tiles_per_seq == 0)
    def _():
        tail_ref[j] = jnp.zeros(tail_ref.shape[1:], F32)

    u = u_ref[...]
    gate = jnp.dot(u, wg_ref[...], preferred_element_type=F32)
    tm = gate.shape[0]
    tail = tail_ref[j]
    cw = cw_ref[...]
    conv = gate * cw[FFN_CONV - 1:FFN_CONV] + cb_ref[...]
    for d in range(1, FFN_CONV):
        conv = conv + _shift_rows(gate, d, tail) * cw[FFN_CONV - 1 - d:FFN_CONV - d]
    tail_ref[j] = gate[tm - SUBLANES:]
    up = jnp.dot(u, wu_ref[...], preferred_element_type=F32)
    act = (_gelu(conv) * up).astype(BF16)
    o_ref[...] += jnp.dot(act, wd_ref[...], preferred_element_type=F32)


def _ffn(h, g, wg, wu, cw, cb, wd, layer, seq, tm=512, tf=1024):
    T = h.shape[0]
    nf = D_FF // tf
    nbytes = (2 * (2 * tm * D_MODEL * 4 + 3 * D_MODEL * tf * 2) + tm * D_MODEL * 2
              + nf * SUBLANES * tf * 4 + 6 * tm * tf * 4)
    return pl.pallas_call(
        functools.partial(_ffn_kernel, seq // tm),
        grid=(T // tm, nf),
        in_specs=[
            pl.BlockSpec((tm, D_MODEL), lambda i, j: (i, 0)),
            pl.BlockSpec((1, D_MODEL), lambda i, j: (0, 0)),
            pl.BlockSpec((None, D_MODEL, tf), lambda i, j: (layer, 0, j)),
            pl.BlockSpec((None, D_MODEL, tf), lambda i, j: (layer, 0, j)),
            pl.BlockSpec((FFN_CONV, tf), lambda i, j: (0, j)),
            pl.BlockSpec((1, tf), lambda i, j: (0, j)),
            pl.BlockSpec((None, tf, D_MODEL), lambda i, j: (layer, j, 0)),
        ],
        out_specs=pl.BlockSpec((tm, D_MODEL), lambda i, j: (i, 0)),
        out_shape=jax.ShapeDtypeStruct((T, D_MODEL), F32),
        scratch_shapes=[pltpu.VMEM((tm, D_MODEL), BF16),
                        pltpu.VMEM((nf, SUBLANES, tf), F32)],
        compiler_params=pltpu.CompilerParams(
            dimension_semantics=("arbitrary", "arbitrary"),
            vmem_limit_bytes=_vmem_limit(nbytes)),
        name="ffn",
    )(h, g, wg, wu, cw, cb, wd)


def _ple_kernel(final, h_ref, p_ref, g_ref, wg_ref, wp_ref, gp_ref, gf_ref, o_ref):
    h = h_ref[...]
    u = _rms(h, g_ref[...]).astype(BF16)
    gate = _sigmoid(jnp.dot(u, wg_ref[...], preferred_element_type=F32))
    proj = jnp.dot(p_ref[...].astype(BF16), wp_ref[...], preferred_element_type=F32)
    out = h + _rms(gate * proj, gp_ref[...])
    if final:
        out = _rms(out, gf_ref[...])
    o_ref[...] = out


def _ple(h, p, g, wg, layer, wp, gp, gf, final, tm=512):
    T = h.shape[0]
    vec = pl.BlockSpec((1, D_MODEL), lambda i: (0, 0))
    nbytes = 2 * (2 * tm * D_MODEL * 4 + tm * D_PLE * 4 + D_MODEL * D_MODEL * 2
                  + D_PLE * D_MODEL * 2) + 4 * tm * D_MODEL * 4
    return pl.pallas_call(
        functools.partial(_ple_kernel, final),
        grid=(T // tm,),
        in_specs=[
            pl.BlockSpec((tm, D_MODEL), lambda i: (i, 0)),
            pl.BlockSpec((tm, D_PLE), lambda i: (i, 0)),
            vec,
            pl.BlockSpec((None, D_MODEL, D_MODEL), lambda i: (layer, 0, 0)),
            pl.BlockSpec((D_PLE, D_MODEL), lambda i: (0, 0)),
            vec, vec,
        ],
        out_specs=pl.BlockSpec((tm, D_MODEL), lambda i: (i, 0)),
        out_shape=jax.ShapeDtypeStruct((T, D_MODEL), F32),
        compiler_params=pltpu.CompilerParams(
            dimension_semantics=("arbitrary",),
            vmem_limit_bytes=_vmem_limit(nbytes)),
        name="ple",
    )(h, p, g, wg, wp, gp, gf)


def _row(v):
    return v.reshape(1, -1).astype(F32)


def _pad_rows(w, top, total):
    return jnp.pad(w, ((top, total - top - w.shape[0]), (0, 0)))


def kernel(x, p, ln_mix, w_in, w_in_vres, mu_shift, mu_shift_vres, conv_a_w, conv_a_b, lru_wx, lru_bx, lru_wa, lru_ba, lru_lambda, lru_norm, rwkv_w0, rwkv_w2, rwkv_a0, rwkv_a2, rwkv_v0, rwkv_v2, rwkv_g2, rwkv_kk, rwkv_ka, rwkv_rk, rwkv_lnx_w, rwkv_lnx_b, w_o, ln_ffn, w_gate, w_up, conv_f_w, conv_f_b, w_down, ln_ple, w_ple_gate, w_ple_proj, ln_ple_post, ln_final):
    batch, seq, _ = x.shape
    depth = w_in.shape[0]
    T = batch * seq
    h = x.reshape(T, D_MODEL)
    n_lora = LORA_W + LORA_A + LORA_G
    vfirst = None
    w_in_b, w_o_b, w_gate_b, w_up_b, w_down_b, w_ple_gate_b = (
        w.astype(BF16) for w in (w_in, w_o, w_gate, w_up, w_down, w_ple_gate))
    for i in range(depth):
        lora_cols = [w_in_b[i][:, D_MAIN:]]
        mu_l = [mu_shift[i][3 * D_RWKV:]]
        if i > 0:
            lora_cols.append(w_in_vres[i - 1].astype(BF16))
            mu_l.append(mu_shift_vres[i - 1])
        w_lora = jnp.concatenate(lora_cols, axis=1)
        w_lora = jnp.pad(w_lora, ((0, 0), (0, D_LORA - w_lora.shape[1])))
        mu_lora = jnp.concatenate(mu_l, axis=0)
        mu_lora = jnp.pad(mu_lora, (0, D_LORA - mu_lora.shape[0]))

        zm, zl = _inproj(h, _row(ln_mix[i]), w_in_b, i, w_lora)

        prm = {
            "mu_rkv": _row(mu_shift[i][:3 * D_RWKV]),
            "mu_lora": _row(mu_lora),
            "w0": _row(rwkv_w0[i]),
            "w2": _pad_rows(rwkv_w2[i], 0, LANES),
            "a0": _row(rwkv_a0[i]),
            "a2": _pad_rows(rwkv_a2[i], LORA_W, LANES),
            "g2": _pad_rows(rwkv_g2[i], 0, 2 * LANES),
            "kk": _row(rwkv_kk[i]), "ka": _row(rwkv_ka[i]), "rk": _row(rwkv_rk[i]),
            "lnw": _row(rwkv_lnx_w[i]), "lnb": _row(rwkv_lnx_b[i]),
        }
        if i > 0:
            prm["v0"] = _row(rwkv_v0[i - 1])
            prm["v2"] = _pad_rows(rwkv_v2[i - 1], n_lora - 2 * LANES, LANES)
        out_b, vfirst = _rwkv(zm, zl, vfirst, prm, batch, seq)

        h = _lru_oproj(zm, out_b, h, conv_a_w[i], _row(conv_a_b[i]), lru_wx[i].astype(BF16),
                       _row(lru_bx[i]), lru_wa[i].astype(BF16), _row(lru_ba[i]),
                       _row(lru_lambda[i]), _row(lru_norm[i]), w_o_b, i, batch, seq)
        h = _ffn(h, _row(ln_ffn[i]), w_gate_b, w_up_b, conv_f_w[i], _row(conv_f_b[i]),
                 w_down_b, i, seq)
        h = _ple(h, p[i].reshape(T, D_PLE), _row(ln_ple[i]), w_ple_gate_b, i,
                 w_ple_proj[i].astype(BF16), _row(ln_ple_post[i]), _row(ln_final),
                 final=(i == depth - 1))
    return h.reshape(batch, seq, D_MODEL)
```

```python
import functools
import math

import jax
import jax.numpy as jnp
from jax import lax
from jax.experimental import pallas as pl
from jax.experimental.pallas import tpu as pltpu

F32 = jnp.float32
BF16 = jnp.bfloat16

D_MODEL = 2048
D_LRU = 1024
D_RWKV = 1024
LRU_HEADS = 4
LRU_BLOCK = 256
LRU_CONV = 4
LRU_C = 8.0
HEAD = 64
LORA_W = 64
LORA_A = 64
LORA_G = 160
D_MAIN = 2 * D_LRU + 3 * D_RWKV
D_LORA = 384
D_FF = 3 * D_MODEL
FFN_CONV = 3
D_PLE = 256
RMS_EPS = 1e-6
LNX_EPS = 64e-5

V7X_VMEM_BYTES = 64 * 1024 * 1024
SUBLANES = 8
LANES = 128
ROW_TILE = 16

CHUNK = 64
PAIR = 2 * HEAD
LRU_ROWS = 256
HEAD_GROUP = 16
RWKV_ROWS = 1024

VMEM_TEMPORARIES_BYTES = 16 * 1024 * 1024
VMEM_UNSCOPED_BYTES = 4 * 1024 * 1024


def _vmem_limit(nbytes):
    return int(min(V7X_VMEM_BYTES - VMEM_UNSCOPED_BYTES, nbytes + VMEM_TEMPORARIES_BYTES))


def _rms(x, g):
    return x * lax.rsqrt(jnp.mean(x * x, axis=-1, keepdims=True) + RMS_EPS) * g


def _gelu(x):
    c = math.sqrt(2.0 / math.pi)
    return 0.5 * x * (1.0 + jnp.tanh(c * (x + 0.044715 * (x * x * x))))


def _sigmoid(x):
    return 1.0 / (1.0 + jnp.exp(-x))


def _softplus(x):
    return jnp.maximum(x, 0.0) + jnp.log1p(jnp.exp(-jnp.abs(x)))


def _shift_rows(x, d, prev8):
    rolled = pltpu.roll(x, d, axis=0)
    prev = pltpu.roll(prev8, d, axis=0)
    row = lax.broadcasted_iota(jnp.int32, prev8.shape, 0)
    top = jnp.where(row < d, prev, rolled[:SUBLANES])
    return jnp.concatenate([top, rolled[SUBLANES:]], axis=0)


def _inproj_kernel(x_ref, g_ref, wm_ref, wl_ref, zm_ref, zl_ref, u_ref):
    @pl.when(pl.program_id(1) == 0)
    def _():
        u_ref[...] = _rms(x_ref[...], g_ref[...]).astype(BF16)
        zl_ref[...] = jnp.dot(u_ref[...], wl_ref[...], preferred_element_type=F32)

    zm_ref[...] = jnp.dot(u_ref[...], wm_ref[...], preferred_element_type=F32)


def _inproj(h, g, w_in, layer, w_lora, tm=1024, tn=1280):
    T = h.shape[0]
    nbytes = 2 * (tm * D_MODEL * 4 + D_MODEL * tn * 2 + D_MODEL * D_LORA * 2
                  + tm * tn * 4 + tm * D_LORA * 4) + tm * D_MODEL * 2
    return pl.pallas_call(
        _inproj_kernel,
        grid=(T // tm, D_MAIN // tn),
        in_specs=[
            pl.BlockSpec((tm, D_MODEL), lambda i, j: (i, 0)),
            pl.BlockSpec((1, D_MODEL), lambda i, j: (0, 0)),
            pl.BlockSpec((None, D_MODEL, tn), lambda i, j: (layer, 0, j)),
            pl.BlockSpec((D_MODEL, D_LORA), lambda i, j: (0, 0)),
        ],
        out_specs=[
            pl.BlockSpec((tm, tn), lambda i, j: (i, j)),
            pl.BlockSpec((tm, D_LORA), lambda i, j: (i, 0)),
        ],
        out_shape=[
            jax.ShapeDtypeStruct((T, D_MAIN), F32),
            jax.ShapeDtypeStruct((T, D_LORA), F32),
        ],
        scratch_shapes=[pltpu.VMEM((tm, D_MODEL), BF16)],
        compiler_params=pltpu.CompilerParams(
            dimension_semantics=("arbitrary", "arbitrary"),
            vmem_limit_bytes=_vmem_limit(nbytes)),
        name="inproj",
    )(h, g, w_in, w_lora)


def _lru_rows(xb_ref, yb_ref, rows, tail, carry, seq_start, cw, cb, wx_ref, bx, wa_ref, ba,
              sp_lam, nrm, between):
    nrows = rows.stop - rows.start
    ngroup = nrows // SUBLANES
    sub = lax.broadcasted_iota(jnp.int32, (1, SUBLANES, 1), 1)
    ys, tails, carries, ss = [], [], [], 0.0
    for hd in range(LRU_HEADS):
        cols = slice(hd * LRU_BLOCK, (hd + 1) * LRU_BLOCK)
        x = xb_ref[rows, cols]
        xc = x * cw[LRU_CONV - 1:LRU_CONV, cols] + cb[:, cols]
        for d in range(1, LRU_CONV):
            xc = xc + _shift_rows(x, d, tail[:, cols]) * cw[LRU_CONV - 1 - d:LRU_CONV - d, cols]
        tails.append(x[nrows - SUBLANES:])

        xcb = xc.astype(BF16)
        gate_x = _sigmoid(jnp.dot(xcb, wx_ref[hd], preferred_element_type=F32) + bx[:, cols])
        gate_a = _sigmoid(jnp.dot(xcb, wa_ref[hd], preferred_element_type=F32) + ba[:, cols])
        log_a = (-LRU_C) * gate_a * sp_lam[:, cols]
        a = jnp.exp(log_a)
        mult = jnp.sqrt(1.0 - a * a)
        if seq_start is not None:
            row = lax.broadcasted_iota(jnp.int32, (nrows, 1), 0)
            mult = jnp.where(jnp.logical_and(row == 0, seq_start), 1.0, mult)
        b = xc * gate_x * mult

        a = a.reshape(ngroup, SUBLANES, LRU_BLOCK)
        b = b.reshape(ngroup, SUBLANES, LRU_BLOCK)
        d = 1
        while d < SUBLANES:
            keep = sub >= d
            a_sh = jnp.where(keep, pltpu.roll(a, d, axis=1), 1.0)
            b_sh = jnp.where(keep, pltpu.roll(b, d, axis=1), 0.0)
            b = a * b_sh + b
            a = a * a_sh
            d *= 2
        hcar = carry[:, cols]
        hs = []
        for grp in range(ngroup):
            hg = a[grp] * hcar + b[grp]
            hs.append(hg)
            hcar = hg[SUBLANES - 1:SUBLANES]
        carries.append(hcar)

        y = jnp.concatenate(hs, axis=0) * _gelu(yb_ref[rows, cols])
        ss = ss + jnp.sum(y * y, axis=-1, keepdims=True)
        ys.append(y)
        between(hd)
    scale = lax.rsqrt(ss * (1.0 / D_LRU) + RMS_EPS)
    out = jnp.concatenate(ys, axis=-1) * scale * nrm
    return out.astype(BF16), jnp.concatenate(tails, axis=-1), jnp.concatenate(carries, axis=-1)


def _lru_oproj_kernel(xb_ref, yb_ref, ob_ref, h_ref, cw_ref, cb_ref, wx_ref, bx_ref, wa_ref,
                      ba_ref, lam_ref, nrm_ref, woa_ref, wob_ref, o_ref, tail_ref, carry_ref):
    t = pl.program_id(1)

    @pl.when(t == 0)
    def _():
        tail_ref[...] = jnp.zeros_like(tail_ref)
        carry_ref[...] = jnp.zeros_like(carry_ref)

    sp_lam = _softplus(-lam_ref[...])
    tail, carry = tail_ref[...], carry_ref[0:1, :]
    ts = xb_ref.shape[0]
    ncol = D_MODEL // LRU_HEADS
    pieces = {}

    def project(name, lhs, w_ref):
        def step(hd):
            cols = slice(hd * ncol, (hd + 1) * ncol)
            pieces.setdefault(name, []).append(
                jnp.dot(lhs, w_ref[:, cols], preferred_element_type=F32))
        return step

    between = project("b", ob_ref[...], wob_ref)
    for r0 in range(0, ts, LRU_ROWS):
        out_a, tail, carry = _lru_rows(
            xb_ref, yb_ref, slice(r0, r0 + LRU_ROWS), tail, carry, (t == 0) if r0 == 0 else None,
            cw_ref[...], cb_ref[...], wx_ref, bx_ref[...], wa_ref, ba_ref[...], sp_lam,
            nrm_ref[...], between)
        between = project(("a", r0), out_a, woa_ref)
    for hd in range(LRU_HEADS):
        between(hd)
    tail_ref[...] = tail
    carry_ref[0:1, :] = carry
    acc_a = jnp.concatenate(
        [jnp.concatenate(pieces["a", r0], axis=-1) for r0 in range(0, ts, LRU_ROWS)], axis=0)
    o_ref[...] = h_ref[...] + jnp.concatenate(pieces["b"], axis=-1) + acc_a


def _lru_oproj(zm, out_b, h, cw, cb, wx, bx, wa, ba, lam, nrm, wo, layer, batch, seq, ts=512):
    T = zm.shape[0]
    nt = seq // ts
    vec = pl.BlockSpec((1, D_LRU), lambda b, t: (0, 0))
    mat = pl.BlockSpec((LRU_HEADS, LRU_BLOCK, LRU_BLOCK), lambda b, t: (0, 0, 0))

    def rows(width, col):
        return pl.BlockSpec((ts, width), lambda b, t: (b * nt + t, col))

    nbytes = (2 * (2 * ts * D_LRU * 4 + ts * D_RWKV * 2 + 2 * ts * D_MODEL * 4
                   + 2 * D_LRU * D_MODEL * 2) + 16 * LRU_ROWS * D_LRU * 4 + 2 * ts * D_MODEL * 4)
    return pl.pallas_call(
        _lru_oproj_kernel,
        grid=(batch, nt),
        in_specs=[
            rows(D_LRU, 0), rows(D_LRU, 1), rows(D_RWKV, 0), rows(D_MODEL, 0),
            pl.BlockSpec((LRU_CONV, D_LRU), lambda b, t: (0, 0)),
            vec, mat, vec, mat, vec, vec, vec,
            pl.BlockSpec((None, D_LRU, D_MODEL), lambda b, t: (layer, 0, 0)),
            pl.BlockSpec((None, D_RWKV, D_MODEL), lambda b, t: (layer, 1, 0)),
        ],
        out_specs=rows(D_MODEL, 0),
        out_shape=jax.ShapeDtypeStruct((T, D_MODEL), F32),
        scratch_shapes=[pltpu.VMEM((SUBLANES, D_LRU), F32),
                        pltpu.VMEM((SUBLANES, D_LRU), F32)],
        compiler_params=pltpu.CompilerParams(
            dimension_semantics=("arbitrary", "arbitrary"),
            vmem_limit_bytes=_vmem_limit(nbytes)),
        name="lru_oproj",
    )(zm, zm, out_b, h, cw, cb, wx, bx, wa, ba, lam, nrm, wo, wo)


def _mm(a, b):
    return jnp.dot(a.astype(BF16), b.astype(BF16), preferred_element_type=F32)


def _mm_nt(a, b):
    return lax.dot_general(a.astype(BF16), b.astype(BF16), (((1,), (1,)), ((), ())),
                           preferred_element_type=F32)


def _mm_tn(a, b):
    return lax.dot_general(a.astype(BF16), b.astype(BF16), (((0,), (0,)), ((), ())),
                           preferred_element_type=F32)


def _split3(x):
    hi = x.astype(BF16)
    r1 = x - hi.astype(F32)
    mid = r1.astype(BF16)
    lo = (r1 - mid.astype(F32)).astype(BF16)
    return hi, mid, lo


def _seg_sum(x):
    lane_lo = lax.broadcasted_iota(jnp.int32, (1, PAIR), 1) < HEAD
    out = []
    for p in range(x.shape[1] // PAIR):
        t = x[:, p * PAIR:(p + 1) * PAIR]
        s0 = jnp.sum(jnp.where(lane_lo, t, 0.0), axis=-1, keepdims=True)
        s1 = jnp.sum(jnp.where(lane_lo, 0.0, t), axis=-1, keepdims=True)
        out.append(jnp.where(lane_lo, s0, s1))
    return jnp.concatenate(out, axis=-1)


def _rwkv_kernel(has_vres, *refs):
    if has_vres:
        (r_ref, k_ref, v_ref, zl_ref, vf_ref, mur_ref, muk_ref, muv_ref, mul_ref,
         w0_ref, w2_ref, a0_ref, a2_ref, g2_ref, v0_ref, v2_ref,
         kkw_ref, ka_ref, rk_ref, lnw_ref, lnb_ref, o_ref, s_ref, prev_ref, prevz_ref) = refs
        vfo_ref = None
    else:
        (r_ref, k_ref, v_ref, zl_ref, mur_ref, muk_ref, muv_ref, mul_ref,
         w0_ref, w2_ref, a0_ref, a2_ref, g2_ref,
         kkw_ref, ka_ref, rk_ref, lnw_ref, lnb_ref, o_ref, vfo_ref,
         s_ref, prev_ref, prevz_ref) = refs
        vf_ref = v0_ref = v2_ref = None

    C = CHUNK
    nchunk = r_ref.shape[0] // C
    nh = r_ref.shape[1] // HEAD

    @pl.when(pl.program_id(2) == 0)
    def _():
        s_ref[...] = jnp.zeros_like(s_ref)
        prev_ref[...] = jnp.zeros_like(prev_ref)
        prevz_ref[...] = jnp.zeros_like(prevz_ref)

    ri3 = lax.broadcasted_iota(jnp.int32, (C, 3 * C), 0)
    ci3 = lax.broadcasted_iota(jnp.int32, (C, 3 * C), 1) % C
    tri3 = (ri3 >= ci3).astype(BF16)
    ri2 = lax.broadcasted_iota(jnp.int32, (2 * C, 2 * C), 0)
    ci2 = lax.broadcasted_iota(jnp.int32, (2 * C, 2 * C), 1) % C
    keep2 = jnp.where(ri2 < C, ri2, ri2 - C + 1) > ci2
    row0 = lax.broadcasted_iota(jnp.int32, (C, 1), 0) == 0
    zeros_h = jnp.zeros((C, HEAD), BF16)

    def shift_lerp(cur, prev_row, mu):
        sh = jnp.where(row0, prev_row, pltpu.roll(cur, 1, axis=0))
        return cur + (sh - cur) * mu

    def chunk_rows(c):
        return pl.ds(pl.multiple_of(c * C, C), C)

    def prep(c, out):
        rows = chunk_rows(c)
        first = c == 0
        before = pl.ds(jnp.maximum(c * C - 1, 0), 1)

        def lerp(ref, carried, mu):
            return shift_lerp(ref[rows, :], jnp.where(first, carried, ref[before, :]), mu)

        zl = lerp(zl_ref, prevz_ref[0:1, :], mul_ref[...])
        z01 = zl[:, 0:LANES]
        wpre = w0_ref[...] + _mm(jnp.tanh(z01), w2_ref[...])
        apre = a0_ref[...] + _mm(z01, a2_ref[...])
        g = _mm(_sigmoid(zl[:, LANES:3 * LANES]), g2_ref[...])
        if has_vres:
            mpre = v0_ref[...] + _mm(zl[:, 2 * LANES:3 * LANES], v2_ref[...])
        yield
        w_log = -_softplus(-wpre) - 0.5
        logw = -jnp.exp(w_log)
        cum = jnp.dot(tri3, jnp.concatenate(_split3(logw), axis=0), preferred_element_type=F32)
        yield
        r = lerp(r_ref, prev_ref[0:1, :], mur_ref[...])
        k = lerp(k_ref, prev_ref[1:2, :], muk_ref[...])
        a = _sigmoid(apre)
        yield
        v = lerp(v_ref, prev_ref[2:3, :], muv_ref[...])
        if has_vres:
            v = v + (vf_ref[rows, :] - v) * _sigmoid(mpre)
        else:
            vfo_ref[rows, :] = v
        yield
        kk = k * kkw_ref[...]
        kk = kk / jnp.maximum(jnp.sqrt(_seg_sum(kk * kk)), 1e-12)
        yield
        k2 = k * (1.0 + (a - 1.0) * ka_ref[...])
        bb = kk * a
        bonus = _seg_sum(r * k2 * rk_ref[...]) * v
        yield
        p_in = jnp.exp(cum)
        p_ex = jnp.exp(cum - logw)
        p_inv = jnp.exp(-cum)
        p_end = p_in[C - 1:C, :]
        yield
        rt = r * p_in
        at = -kk * p_ex
        rt_b, at_b, v_b = rt.astype(BF16), at.astype(BF16), v.astype(BF16)
        yield
        bt = bb * p_inv
        kt = k2 * p_inv
        bk_t = jnp.transpose(jnp.concatenate([bt, kt], axis=0)).astype(BF16)
        yield
        bhat_b = (bt * p_end).astype(BF16)
        khat_b = (kt * p_end).astype(BF16)
        out["local"] = (rt_b, at_b, bk_t, v_b, bhat_b, khat_b, at, rt, p_end)
        out["post"] = (bonus, g)

    def heads_stage(local, out):
        rt_b, at_b, bk_t, v_b, bhat_b, khat_b, at, rt, p_end = local
        heads = range(nh)
        sls = [slice(hh * HEAD, (hh + 1) * HEAD) for hh in heads]
        sc_b, m, vh_b, d = [], [], [], []
        for sl in sls:
            ar = jnp.concatenate([at_b[:, sl], rt_b[:, sl]], axis=0)
            sc = jnp.dot(ar, bk_t[sl, :], preferred_element_type=F32)
            sc = jnp.where(keep2, sc, 0.0)
            sc_b.append(sc.astype(BF16))
            m.append(sc[:C, :C])
            vh_b.append(v_b[:, sl])
        yield
        for hh in heads:
            zv = jnp.concatenate([zeros_h, vh_b[hh]], axis=0)
            x_loc = jnp.dot(sc_b[hh][:C], zv, preferred_element_type=F32)
            d.append(jnp.concatenate([at[:, sls[hh]], x_loc], axis=-1))
        yield
        nstep = int(math.log2(C))
        for i in range(nstep):
            lo = (2 ** i // ROW_TILE) * ROW_TILE
            for hh in heads:
                m_b = m[hh].astype(BF16)[lo:, :C - lo]
                d_b = d[hh].astype(BF16)[:C - lo]
                if i + 1 < nstep:
                    rhs = jnp.concatenate([d_b, m[hh].astype(BF16)[:C - lo]], axis=-1)
                    prod = jnp.dot(m_b, rhs, preferred_element_type=F32)
                    upd, m_new = prod[:, :2 * HEAD], prod[:, 2 * HEAD:]
                    if lo:
                        m_new = jnp.concatenate([jnp.zeros((lo, C), F32), m_new], axis=0)
                    m[hh] = m_new
                else:
                    upd = jnp.dot(m_b, d_b, preferred_element_type=F32)
                if lo:
                    upd = jnp.concatenate([jnp.zeros((lo, 2 * HEAD), F32), upd], axis=0)
                d[hh] = d[hh] + upd
            yield
        o1, wz = [], []
        for hh in heads:
            gmat = jnp.concatenate(
                [d[hh].astype(BF16), jnp.concatenate([zeros_h, vh_b[hh]], axis=-1)], axis=0)
            o1.append(jnp.dot(sc_b[hh][C:], gmat, preferred_element_type=F32))
            bkh = jnp.concatenate([bhat_b[:, sls[hh]], khat_b[:, sls[hh]]], axis=0)
            wz.append(_mm_tn(gmat, bkh))
        yield
        ys = []
        for hh in heads:
            rbar = rt[:, sls[hh]] + o1[hh][:, :HEAD]
            st = s_ref[hh]
            ys.append(_mm_nt(rbar, st) + o1[hh][:, HEAD:])
            s_ref[hh] = st * p_end[:, sls[hh]] + _mm(st, wz[hh][:HEAD]) + wz[hh][HEAD:]
        out["y"] = jnp.concatenate(ys, axis=-1)

    def tail(c, y, post):
        bonus, g = post
        mean = _seg_sum(y) * (1.0 / HEAD)
        yc = y - mean
        yield
        var = _seg_sum(yc * yc) * (1.0 / HEAD)
        yield
        yn = yc * lax.rsqrt(var + LNX_EPS) * lnw_ref[...] + lnb_ref[...]
        o_ref[chunk_rows(c), :] = ((yn + bonus) * g).astype(o_ref.dtype)

    def run_interleaved(*gens):
        alive = list(gens)
        while alive:
            for gen in list(alive):
                if next(gen, "done") == "done":
                    alive.remove(gen)

    def body(i, carry):
        local, y_prev, post_prev, post_cur = carry
        out = {}
        run_interleaved(heads_stage(local, out),
                        tail(jnp.maximum(i - 1, 0), y_prev, post_prev),
                        prep(jnp.minimum(i + 1, nchunk - 1), out))
        return out["local"], out["y"], post_cur, out["post"]

    first = {}
    run_interleaved(prep(0, first))
    zeros_w = jnp.zeros((C, r_ref.shape[1]), F32)
    _, y_last, post_last, _ = lax.fori_loop(
        0, nchunk, body, (first["local"], zeros_w, (zeros_w, zeros_w), first["post"]))
    run_interleaved(tail(nchunk - 1, y_last, post_last))

    last = pl.ds(r_ref.shape[0] - 1, 1)
    prev_ref[0:1, :] = r_ref[last, :]
    prev_ref[1:2, :] = k_ref[last, :]
    prev_ref[2:3, :] = v_ref[last, :]
    prevz_ref[0:1, :] = zl_ref[last, :]


def _rwkv(zm, zl, vfirst, prm, batch, seq, hg=HEAD_GROUP, ts=RWKV_ROWS):
    T = zm.shape[0]
    W = hg * HEAD
    ng = D_RWKV // W
    nt = seq // ts
    col0 = 2 * D_LRU // W
    has_vres = vfirst is not None

    def col(off):
        return pl.BlockSpec((ts, W), lambda b, g, t: (b * nt + t, off + g))

    vecg = pl.BlockSpec((1, W), lambda b, g, t: (0, g))
    vec_k = pl.BlockSpec((1, W), lambda b, g, t: (0, ng + g))
    vec_v = pl.BlockSpec((1, W), lambda b, g, t: (0, 2 * ng + g))
    vec_l = pl.BlockSpec((1, D_LORA), lambda b, g, t: (0, 0))

    def lora(rows):
        return pl.BlockSpec((rows, W), lambda b, g, t: (0, g))

    in_specs = [col(col0), col(col0 + ng), col(col0 + 2 * ng),
                pl.BlockSpec((ts, D_LORA), lambda b, g, t: (b * nt + t, 0))]
    args = [zm, zm, zm, zl]
    if has_vres:
        in_specs.append(col(0))
        args.append(vfirst)
    in_specs += [vecg, vec_k, vec_v, vec_l, vecg, lora(LANES), vecg, lora(LANES), lora(2 * LANES)]
    args += [prm["mu_rkv"], prm["mu_rkv"], prm["mu_rkv"], prm["mu_lora"],
             prm["w0"], prm["w2"], prm["a0"], prm["a2"], prm["g2"]]
    if has_vres:
        in_specs += [vecg, lora(LANES)]
        args += [prm["v0"], prm["v2"]]
    in_specs += [vecg] * 5
    args += [prm["kk"], prm["ka"], prm["rk"], prm["lnw"], prm["lnb"]]

    if has_vres:
        out_specs = col(0)
        out_shape = jax.ShapeDtypeStruct((T, D_RWKV), BF16)
    else:
        out_specs = [col(0), col(0)]
        out_shape = [jax.ShapeDtypeStruct((T, D_RWKV), BF16),
                     jax.ShapeDtypeStruct((T, D_RWKV), F32)]
    nbytes = 2 * ts * (5 * W * 4 + D_LORA * 4 + W * 2) + hg * HEAD * HEAD * 4
    res = pl.pallas_call(
        functools.partial(_rwkv_kernel, has_vres),
        grid=(batch, ng, nt),
        in_specs=in_specs,
        out_specs=out_specs,
        out_shape=out_shape,
        scratch_shapes=[pltpu.VMEM((hg, HEAD, HEAD), F32),
                        pltpu.VMEM((SUBLANES, W), F32),
                        pltpu.VMEM((SUBLANES, D_LORA), F32)],
        compiler_params=pltpu.CompilerParams(
            dimension_semantics=("arbitrary", "arbitrary", "arbitrary"),
            vmem_limit_bytes=_vmem_limit(nbytes)),
        name="rwkv7",
    )(*args)
    if has_vres:
        return res, vfirst
    return res[0], res[1]


def _ffn_kernel(tiles_per_seq, h_ref, g_ref, wg_ref, wu_ref, cw_ref, cb_ref, wd_ref,
                o_ref, u_ref, tail_ref):
    i = pl.program_id(0)
    j = pl.program_id(1)

    @pl.when(j == 0)
    def _():
        h = h_ref[...]
        u_ref[...] = _rms(h, g_ref[...]).astype(BF16)
        o_ref[...] = h

    @pl.when(i % tiles_per_seq == 0)
    def _():
        tail_ref[j] = jnp.zeros(tail_ref.shape[1:], F32)

    u = u_ref[...]
    gate = jnp.dot(u, wg_ref[...], preferred_element_type=F32)
    tm = gate.shape[0]
    tail = tail_ref[j]
    cw = cw_ref[...]
    conv = gate * cw[FFN_CONV - 1:FFN_CONV] + cb_ref[...]
    for d in range(1, FFN_CONV):
        conv = conv + _shift_rows(gate, d, tail) * cw[FFN_CONV - 1 - d:FFN_CONV - d]
    tail_ref[j] = gate[tm - SUBLANES:]
    up = jnp.dot(u, wu_ref[...], preferred_element_type=F32)
    act = (_gelu(conv) * up).astype(BF16)
    o_ref[...] += jnp.dot(act, wd_ref[...], preferred_element_type=F32)


def _ffn(h, g, wg, wu, cw, cb, wd, layer, seq, tm=512, tf=1024):
    T = h.shape[0]
    nf = D_FF // tf
    nbytes = (2 * (2 * tm * D_MODEL * 4 + 3 * D_MODEL * tf * 2) + tm * D_MODEL * 2
              + nf * SUBLANES * tf * 4 + 6 * tm * tf * 4)
    return pl.pallas_call(
        functools.partial(_ffn_kernel, seq // tm),
        grid=(T // tm, nf),
        in_specs=[
            pl.BlockSpec((tm, D_MODEL), lambda i, j: (i, 0)),
            pl.BlockSpec((1, D_MODEL), lambda i, j: (0, 0)),
            pl.BlockSpec((None, D_MODEL, tf), lambda i, j: (layer, 0, j)),
            pl.BlockSpec((None, D_MODEL, tf), lambda i, j: (layer, 0, j)),
            pl.BlockSpec((FFN_CONV, tf), lambda i, j: (0, j)),
            pl.BlockSpec((1, tf), lambda i, j: (0, j)),
            pl.BlockSpec((None, tf, D_MODEL), lambda i, j: (layer, j, 0)),
        ],
        out_specs=pl.BlockSpec((tm, D_MODEL), lambda i, j: (i, 0)),
        out_shape=jax.ShapeDtypeStruct((T, D_MODEL), F32),
        scratch_shapes=[pltpu.VMEM((tm, D_MODEL), BF16),
                        pltpu.VMEM((nf, SUBLANES, tf), F32)],
        compiler_params=pltpu.CompilerParams(
            dimension_semantics=("arbitrary", "arbitrary"),
            vmem_limit_bytes=_vmem_limit(nbytes)),
        name="ffn",
    )(h, g, wg, wu, cw, cb, wd)


def _ple_kernel(final, h_ref, p_ref, g_ref, wg_ref, wp_ref, gp_ref, gf_ref, o_ref):
    h = h_ref[...]
    u = _rms(h, g_ref[...]).astype(BF16)
    gate = _sigmoid(jnp.dot(u, wg_ref[...], preferred_element_type=F32))
    proj = jnp.dot(p_ref[...].astype(BF16), wp_ref[...], preferred_element_type=F32)
    out = h + _rms(gate * proj, gp_ref[...])
    if final:
        out = _rms(out, gf_ref[...])
    o_ref[...] = out


def _ple(h, p, g, wg, layer, wp, gp, gf, final, tm=512):
    T = h.shape[0]
    vec = pl.BlockSpec((1, D_MODEL), lambda i: (0, 0))
    nbytes = 2 * (2 * tm * D_MODEL * 4 + tm * D_PLE * 4 + D_MODEL * D_MODEL * 2
                  + D_PLE * D_MODEL * 2) + 4 * tm * D_MODEL * 4
    return pl.pallas_call(
        functools.partial(_ple_kernel, final),
        grid=(T // tm,),
        in_specs=[
            pl.BlockSpec((tm, D_MODEL), lambda i: (i, 0)),
            pl.BlockSpec((None, tm, D_PLE), lambda i: (layer, i, 0)),
            vec,
            pl.BlockSpec((None, D_MODEL, D_MODEL), lambda i: (layer, 0, 0)),
            pl.BlockSpec((D_PLE, D_MODEL), lambda i: (0, 0)),
            vec, vec,
        ],
        out_specs=pl.BlockSpec((tm, D_MODEL), lambda i: (i, 0)),
        out_shape=jax.ShapeDtypeStruct((T, D_MODEL), F32),
        compiler_params=pltpu.CompilerParams(
            dimension_semantics=("arbitrary",),
            vmem_limit_bytes=_vmem_limit(nbytes)),
        name="ple",
    )(h, p, g, wg, wp, gp, gf)


def _row(v):
    return v.reshape(1, -1).astype(F32)


def _pad_rows(w, top, total):
    return jnp.pad(w, ((top, total - top - w.shape[0]), (0, 0)))


def kernel(x, p, ln_mix, w_in, w_in_vres, mu_shift, mu_shift_vres, conv_a_w, conv_a_b, lru_wx, lru_bx, lru_wa, lru_ba, lru_lambda, lru_norm, rwkv_w0, rwkv_w2, rwkv_a0, rwkv_a2, rwkv_v0, rwkv_v2, rwkv_g2, rwkv_kk, rwkv_ka, rwkv_rk, rwkv_lnx_w, rwkv_lnx_b, w_o, ln_ffn, w_gate, w_up, conv_f_w, conv_f_b, w_down, ln_ple, w_ple_gate, w_ple_proj, ln_ple_post, ln_final):
    batch, seq, _ = x.shape
    depth = w_in.shape[0]
    T = batch * seq
    h = x.reshape(T, D_MODEL)
    n_lora = LORA_W + LORA_A + LORA_G
    vfirst = None
    w_in_b, w_o_b, w_gate_b, w_up_b, w_down_b, w_ple_gate_b = (
        w.astype(BF16) for w in (w_in, w_o, w_gate, w_up, w_down, w_ple_gate))
    for i in range(depth):
        lora_cols = [w_in_b[i, :, D_MAIN:]]
        mu_l = [mu_shift[i][3 * D_RWKV:]]
        if i > 0:
            lora_cols.append(w_in_vres[i - 1].astype(BF16))
            mu_l.append(mu_shift_vres[i - 1])
        w_lora = jnp.concatenate(lora_cols, axis=1)
        w_lora = jnp.pad(w_lora, ((0, 0), (0, D_LORA - w_lora.shape[1])))
        mu_lora = jnp.concatenate(mu_l, axis=0)
        mu_lora = jnp.pad(mu_lora, (0, D_LORA - mu_lora.shape[0]))

        zm, zl = _inproj(h, _row(ln_mix[i]), w_in_b, i, w_lora)

        prm = {
            "mu_rkv": _row(mu_shift[i][:3 * D_RWKV]),
            "mu_lora": _row(mu_lora),
            "w0": _row(rwkv_w0[i]),
            "w2": _pad_rows(rwkv_w2[i], 0, LANES),
            "a0": _row(rwkv_a0[i]),
            "a2": _pad_rows(rwkv_a2[i], LORA_W, LANES),
            "g2": _pad_rows(rwkv_g2[i], 0, 2 * LANES),
            "kk": _row(rwkv_kk[i]), "ka": _row(rwkv_ka[i]), "rk": _row(rwkv_rk[i]),
            "lnw": _row(rwkv_lnx_w[i]), "lnb": _row(rwkv_lnx_b[i]),
        }
        if i > 0:
            prm["v0"] = _row(rwkv_v0[i - 1])
            prm["v2"] = _pad_rows(rwkv_v2[i - 1], n_lora - 2 * LANES, LANES)
        out_b, vfirst = _rwkv(zm, zl, vfirst, prm, batch, seq)

        h = _lru_oproj(zm, out_b, h, conv_a_w[i], _row(conv_a_b[i]), lru_wx[i].astype(BF16),
                       _row(lru_bx[i]), lru_wa[i].astype(BF16), _row(lru_ba[i]),
                       _row(lru_lambda[i]), _row(lru_norm[i]), w_o_b, i, batch, seq)
        h = _ffn(h, _row(ln_ffn[i]), w_gate_b, w_up_b, conv_f_w[i], _row(conv_f_b[i]),
                 w_down_b, i, seq)
        h = _ple(h, p.reshape(depth, T, D_PLE), _row(ln_ple[i]), w_ple_gate_b, i,
                 w_ple_proj[i].astype(BF16), _row(ln_ple_post[i]), _row(ln_final),
                 final=(i == depth - 1))
    return h.reshape(batch, seq, D_MODEL)
```

```python
import functools
import math

import jax
import jax.numpy as jnp
from jax import lax
from jax.experimental import pallas as pl
from jax.experimental.pallas import tpu as pltpu

F32 = jnp.float32
BF16 = jnp.bfloat16

D_MODEL = 2048
D_LRU = 1024
D_RWKV = 1024
LRU_HEADS = 4
LRU_BLOCK = 256
LRU_CONV = 4
LRU_C = 8.0
HEAD = 64
LORA_W = 64
LORA_A = 64
LORA_G = 160
D_MAIN = 2 * D_LRU + 3 * D_RWKV
D_LORA = 384
D_FF = 3 * D_MODEL
FFN_CONV = 3
D_PLE = 256
RMS_EPS = 1e-6
LNX_EPS = 64e-5

V7X_VMEM_BYTES = 64 * 1024 * 1024
SUBLANES = 8
LANES = 128
ROW_TILE = 16

CHUNK = 64
PAIR = 2 * HEAD
LRU_ROWS = 256
HEAD_GROUP = 16
RWKV_ROWS = 1024

VMEM_TEMPORARIES_BYTES = 16 * 1024 * 1024
VMEM_UNSCOPED_BYTES = 4 * 1024 * 1024


def _vmem_limit(nbytes):
    return int(min(V7X_VMEM_BYTES - VMEM_UNSCOPED_BYTES, nbytes + VMEM_TEMPORARIES_BYTES))


def _rms(x, g):
    return x * lax.rsqrt(jnp.mean(x * x, axis=-1, keepdims=True) + RMS_EPS) * g


def _gelu(x):
    c = math.sqrt(2.0 / math.pi)
    return 0.5 * x * (1.0 + jnp.tanh(c * (x + 0.044715 * (x * x * x))))


def _sigmoid(x):
    return 1.0 / (1.0 + jnp.exp(-x))


def _softplus(x):
    return jnp.maximum(x, 0.0) + jnp.log1p(jnp.exp(-jnp.abs(x)))


def _shift_rows(x, d, prev8):
    rolled = pltpu.roll(x, d, axis=0)
    prev = pltpu.roll(prev8, d, axis=0)
    row = lax.broadcasted_iota(jnp.int32, prev8.shape, 0)
    top = jnp.where(row < d, prev, rolled[:SUBLANES])
    return jnp.concatenate([top, rolled[SUBLANES:]], axis=0)


def _inproj_kernel(x_ref, g_ref, wm_ref, wl_ref, zm_ref, zl_ref, u_ref):
    @pl.when(pl.program_id(1) == 0)
    def _():
        u_ref[...] = _rms(x_ref[...], g_ref[...]).astype(BF16)
        zl_ref[...] = jnp.dot(u_ref[...], wl_ref[...], preferred_element_type=F32)

    zm_ref[...] = jnp.dot(u_ref[...], wm_ref[...], preferred_element_type=F32)


def _inproj(h, g, w_in, layer, w_lora, tm=1024, tn=1280):
    T = h.shape[0]
    nbytes = 2 * (tm * D_MODEL * 4 + D_MODEL * tn * 2 + D_MODEL * D_LORA * 2
                  + tm * tn * 4 + tm * D_LORA * 4) + tm * D_MODEL * 2
    return pl.pallas_call(
        _inproj_kernel,
        grid=(T // tm, D_MAIN // tn),
        in_specs=[
            pl.BlockSpec((tm, D_MODEL), lambda i, j: (i, 0)),
            pl.BlockSpec((1, D_MODEL), lambda i, j: (0, 0)),
            pl.BlockSpec((None, D_MODEL, tn), lambda i, j: (layer, 0, j)),
            pl.BlockSpec((D_MODEL, D_LORA), lambda i, j: (0, 0)),
        ],
        out_specs=[
            pl.BlockSpec((tm, tn), lambda i, j: (i, j)),
            pl.BlockSpec((tm, D_LORA), lambda i, j: (i, 0)),
        ],
        out_shape=[
            jax.ShapeDtypeStruct((T, D_MAIN), F32),
            jax.ShapeDtypeStruct((T, D_LORA), F32),
        ],
        scratch_shapes=[pltpu.VMEM((tm, D_MODEL), BF16)],
        compiler_params=pltpu.CompilerParams(
            dimension_semantics=("arbitrary", "arbitrary"),
            vmem_limit_bytes=_vmem_limit(nbytes)),
        name="inproj",
    )(h, g, w_in, w_lora)


def _lru_rows(xb_ref, yb_ref, rows, tail, carry, seq_start, cw, cb, wx_ref, bx, wa_ref, ba,
              sp_lam, nrm, between):
    nrows = rows.stop - rows.start
    ngroup = nrows // SUBLANES
    sub = lax.broadcasted_iota(jnp.int32, (1, SUBLANES, 1), 1)
    ys, tails, carries, ss = [], [], [], 0.0
    for hd in range(LRU_HEADS):
        cols = slice(hd * LRU_BLOCK, (hd + 1) * LRU_BLOCK)
        x = xb_ref[rows, cols]
        xc = x * cw[LRU_CONV - 1:LRU_CONV, cols] + cb[:, cols]
        for d in range(1, LRU_CONV):
            xc = xc + _shift_rows(x, d, tail[:, cols]) * cw[LRU_CONV - 1 - d:LRU_CONV - d, cols]
        tails.append(x[nrows - SUBLANES:])

        xcb = xc.astype(BF16)
        gate_x = _sigmoid(jnp.dot(xcb, wx_ref[hd], preferred_element_type=F32) + bx[:, cols])
        gate_a = _sigmoid(jnp.dot(xcb, wa_ref[hd], preferred_element_type=F32) + ba[:, cols])
        log_a = (-LRU_C) * gate_a * sp_lam[:, cols]
        a = jnp.exp(log_a)
        mult = jnp.sqrt(1.0 - a * a)
        if seq_start is not None:
            row = lax.broadcasted_iota(jnp.int32, (nrows, 1), 0)
            mult = jnp.where(jnp.logical_and(row == 0, seq_start), 1.0, mult)
        b = xc * gate_x * mult

        a = a.reshape(ngroup, SUBLANES, LRU_BLOCK)
        b = b.reshape(ngroup, SUBLANES, LRU_BLOCK)
        d = 1
        while d < SUBLANES:
            keep = sub >= d
            a_sh = jnp.where(keep, pltpu.roll(a, d, axis=1), 1.0)
            b_sh = jnp.where(keep, pltpu.roll(b, d, axis=1), 0.0)
            b = a * b_sh + b
            a = a * a_sh
            d *= 2
        hcar = carry[:, cols]
        hs = []
        for grp in range(ngroup):
            hg = a[grp] * hcar + b[grp]
            hs.append(hg)
            hcar = hg[SUBLANES - 1:SUBLANES]
        carries.append(hcar)

        y = jnp.concatenate(hs, axis=0) * _gelu(yb_ref[rows, cols])
        ss = ss + jnp.sum(y * y, axis=-1, keepdims=True)
        ys.append(y)
        between(hd)
    scale = lax.rsqrt(ss * (1.0 / D_LRU) + RMS_EPS)
    out = jnp.concatenate(ys, axis=-1) * scale * nrm
    return out.astype(BF16), jnp.concatenate(tails, axis=-1), jnp.concatenate(carries, axis=-1)


def _lru_oproj_kernel(xb_ref, yb_ref, ob_ref, h_ref, cw_ref, cb_ref, wx_ref, bx_ref, wa_ref,
                      ba_ref, lam_ref, nrm_ref, woa_ref, wob_ref, o_ref, tail_ref, carry_ref):
    t = pl.program_id(1)

    @pl.when(t == 0)
    def _():
        tail_ref[...] = jnp.zeros_like(tail_ref)
        carry_ref[...] = jnp.zeros_like(carry_ref)

    sp_lam = _softplus(-lam_ref[...])
    tail, carry = tail_ref[...], carry_ref[0:1, :]
    ts = xb_ref.shape[0]
    ncol = D_MODEL // LRU_HEADS
    pieces = {}

    def project(name, lhs, w_ref):
        def step(hd):
            cols = slice(hd * ncol, (hd + 1) * ncol)
            pieces.setdefault(name, []).append(
                jnp.dot(lhs, w_ref[:, cols], preferred_element_type=F32))
        return step

    between = project("b", ob_ref[...], wob_ref)
    for r0 in range(0, ts, LRU_ROWS):
        out_a, tail, carry = _lru_rows(
            xb_ref, yb_ref, slice(r0, r0 + LRU_ROWS), tail, carry, (t == 0) if r0 == 0 else None,
            cw_ref[...], cb_ref[...], wx_ref, bx_ref[...], wa_ref, ba_ref[...], sp_lam,
            nrm_ref[...], between)
        between = project(("a", r0), out_a, woa_ref)
    for hd in range(LRU_HEADS):
        between(hd)
    tail_ref[...] = tail
    carry_ref[0:1, :] = carry
    acc_a = jnp.concatenate(
        [jnp.concatenate(pieces["a", r0], axis=-1) for r0 in range(0, ts, LRU_ROWS)], axis=0)
    o_ref[...] = h_ref[...] + jnp.concatenate(pieces["b"], axis=-1) + acc_a


def _lru_oproj(zm, out_b, h, cw, cb, wx, bx, wa, ba, lam, nrm, wo, layer, batch, seq, ts=512):
    T = zm.shape[0]
    nt = seq // ts
    vec = pl.BlockSpec((1, D_LRU), lambda b, t: (0, 0))
    mat = pl.BlockSpec((LRU_HEADS, LRU_BLOCK, LRU_BLOCK), lambda b, t: (0, 0, 0))

    def rows(width, col):
        return pl.BlockSpec((ts, width), lambda b, t: (b * nt + t, col))

    nbytes = (2 * (2 * ts * D_LRU * 4 + ts * D_RWKV * 2 + 2 * ts * D_MODEL * 4
                   + 2 * D_LRU * D_MODEL * 2) + 16 * LRU_ROWS * D_LRU * 4 + 2 * ts * D_MODEL * 4)
    return pl.pallas_call(
        _lru_oproj_kernel,
        grid=(batch, nt),
        in_specs=[
            rows(D_LRU, 0), rows(D_LRU, 1), rows(D_RWKV, 0), rows(D_MODEL, 0),
            pl.BlockSpec((LRU_CONV, D_LRU), lambda b, t: (0, 0)),
            vec, mat, vec, mat, vec, vec, vec,
            pl.BlockSpec((None, D_LRU, D_MODEL), lambda b, t: (layer, 0, 0)),
            pl.BlockSpec((None, D_RWKV, D_MODEL), lambda b, t: (layer, 1, 0)),
        ],
        out_specs=rows(D_MODEL, 0),
        out_shape=jax.ShapeDtypeStruct((T, D_MODEL), F32),
        scratch_shapes=[pltpu.VMEM((SUBLANES, D_LRU), F32),
                        pltpu.VMEM((SUBLANES, D_LRU), F32)],
        compiler_params=pltpu.CompilerParams(
            dimension_semantics=("arbitrary", "arbitrary"),
            vmem_limit_bytes=_vmem_limit(nbytes)),
        name="lru_oproj",
    )(zm, zm, out_b, h, cw, cb, wx, bx, wa, ba, lam, nrm, wo, wo)


def _mm(a, b):
    return jnp.dot(a.astype(BF16), b.astype(BF16), preferred_element_type=F32)


def _mm_nt(a, b):
    return lax.dot_general(a.astype(BF16), b.astype(BF16), (((1,), (1,)), ((), ())),
                           preferred_element_type=F32)


def _mm_tn(a, b):
    return lax.dot_general(a.astype(BF16), b.astype(BF16), (((0,), (0,)), ((), ())),
                           preferred_element_type=F32)


def _split3(x):
    hi = x.astype(BF16)
    r1 = x - hi.astype(F32)
    mid = r1.astype(BF16)
    lo = (r1 - mid.astype(F32)).astype(BF16)
    return hi, mid, lo


def _seg_sum(x):
    lane_lo = lax.broadcasted_iota(jnp.int32, (1, PAIR), 1) < HEAD
    out = []
    for p in range(x.shape[1] // PAIR):
        t = x[:, p * PAIR:(p + 1) * PAIR]
        s0 = jnp.sum(jnp.where(lane_lo, t, 0.0), axis=-1, keepdims=True)
        s1 = jnp.sum(jnp.where(lane_lo, 0.0, t), axis=-1, keepdims=True)
        out.append(jnp.where(lane_lo, s0, s1))
    return jnp.concatenate(out, axis=-1)


def _rwkv_kernel(has_vres, *refs):
    if has_vres:
        (r_ref, k_ref, v_ref, zl_ref, vf_ref, mur_ref, muk_ref, muv_ref, mul_ref,
         w0_ref, w2_ref, a0_ref, a2_ref, g2_ref, v0_ref, v2_ref,
         kkw_ref, ka_ref, rk_ref, lnw_ref, lnb_ref, o_ref, s_ref, prev_ref, prevz_ref) = refs
        vfo_ref = None
    else:
        (r_ref, k_ref, v_ref, zl_ref, mur_ref, muk_ref, muv_ref, mul_ref,
         w0_ref, w2_ref, a0_ref, a2_ref, g2_ref,
         kkw_ref, ka_ref, rk_ref, lnw_ref, lnb_ref, o_ref, vfo_ref,
         s_ref, prev_ref, prevz_ref) = refs
        vf_ref = v0_ref = v2_ref = None

    C = CHUNK
    nchunk = r_ref.shape[0] // C
    nh = r_ref.shape[1] // HEAD

    @pl.when(pl.program_id(2) == 0)
    def _():
        s_ref[...] = jnp.zeros_like(s_ref)
        prev_ref[...] = jnp.zeros_like(prev_ref)
        prevz_ref[...] = jnp.zeros_like(prevz_ref)

    ri3 = lax.broadcasted_iota(jnp.int32, (C, 3 * C), 0)
    ci3 = lax.broadcasted_iota(jnp.int32, (C, 3 * C), 1) % C
    tri3 = (ri3 >= ci3).astype(BF16)
    ri2 = lax.broadcasted_iota(jnp.int32, (2 * C, 2 * C), 0)
    ci2 = lax.broadcasted_iota(jnp.int32, (2 * C, 2 * C), 1) % C
    keep2 = jnp.where(ri2 < C, ri2, ri2 - C + 1) > ci2
    row0 = lax.broadcasted_iota(jnp.int32, (C, 1), 0) == 0
    zeros_h = jnp.zeros((C, HEAD), BF16)

    def shift_lerp(cur, prev_row, mu):
        sh = jnp.where(row0, prev_row, pltpu.roll(cur, 1, axis=0))
        return cur + (sh - cur) * mu

    def chunk_rows(c):
        return pl.ds(pl.multiple_of(c * C, C), C)

    def prep(c, out):
        rows = chunk_rows(c)
        first = c == 0
        before = pl.ds(jnp.maximum(c * C - 1, 0), 1)

        def lerp(ref, carried, mu):
            return shift_lerp(ref[rows, :], jnp.where(first, carried, ref[before, :]), mu)

        zl = lerp(zl_ref, prevz_ref[0:1, :], mul_ref[...])
        z01 = zl[:, 0:LANES]
        wpre = w0_ref[...] + _mm(jnp.tanh(z01), w2_ref[...])
        apre = a0_ref[...] + _mm(z01, a2_ref[...])
        g = _mm(_sigmoid(zl[:, LANES:3 * LANES]), g2_ref[...])
        if has_vres:
            mpre = v0_ref[...] + _mm(zl[:, 2 * LANES:3 * LANES], v2_ref[...])
        yield
        w_log = -_softplus(-wpre) - 0.5
        logw = -jnp.exp(w_log)
        cum = jnp.dot(tri3, jnp.concatenate(_split3(logw), axis=0), preferred_element_type=F32)
        yield
        r = lerp(r_ref, prev_ref[0:1, :], mur_ref[...])
        k = lerp(k_ref, prev_ref[1:2, :], muk_ref[...])
        a = _sigmoid(apre)
        yield
        v = lerp(v_ref, prev_ref[2:3, :], muv_ref[...])
        if has_vres:
            v = v + (vf_ref[rows, :] - v) * _sigmoid(mpre)
        else:
            vfo_ref[rows, :] = v
        yield
        kk = k * kkw_ref[...]
        kk = kk / jnp.maximum(jnp.sqrt(_seg_sum(kk * kk)), 1e-12)
        yield
        k2 = k * (1.0 + (a - 1.0) * ka_ref[...])
        bb = kk * a
        bonus = _seg_sum(r * k2 * rk_ref[...]) * v
        yield
        p_in = jnp.exp(cum)
        p_ex = jnp.exp(cum - logw)
        p_inv = jnp.exp(-cum)
        p_end = p_in[C - 1:C, :]
        yield
        rt = r * p_in
        at = -kk * p_ex
        rt_b, at_b, v_b = rt.astype(BF16), at.astype(BF16), v.astype(BF16)
        yield
        bt = bb * p_inv
        kt = k2 * p_inv
        bk_t = jnp.transpose(jnp.concatenate([bt, kt], axis=0)).astype(BF16)
        yield
        bhat_b = (bt * p_end).astype(BF16)
        khat_b = (kt * p_end).astype(BF16)
        out["local"] = (rt_b, at_b, bk_t, v_b, bhat_b, khat_b, at, rt, p_end)
        out["post"] = (bonus, g)

    def heads_stage(local, out):
        rt_b, at_b, bk_t, v_b, bhat_b, khat_b, at, rt, p_end = local
        heads = range(nh)
        sls = [slice(hh * HEAD, (hh + 1) * HEAD) for hh in heads]
        sc_b, m, vh_b, d = [], [], [], []
        for sl in sls:
            ar = jnp.concatenate([at_b[:, sl], rt_b[:, sl]], axis=0)
            sc = jnp.dot(ar, bk_t[sl, :], preferred_element_type=F32)
            sc = jnp.where(keep2, sc, 0.0)
            sc_b.append(sc.astype(BF16))
            m.append(sc[:C, :C])
            vh_b.append(v_b[:, sl])
        yield
        for hh in heads:
            x_loc = jnp.dot(sc_b[hh][:C, C:], vh_b[hh],
                            preferred_element_type=F32)
            d.append(jnp.concatenate([at[:, sls[hh]], x_loc], axis=-1))
        yield
        nstep = int(math.log2(C))
        for i in range(nstep):
            lo = (2 ** i // ROW_TILE) * ROW_TILE
            for hh in heads:
                m_b = m[hh].astype(BF16)[lo:, :C - lo]
                d_b = d[hh].astype(BF16)[:C - lo]
                if i + 1 < nstep:
                    rhs = jnp.concatenate([d_b, m[hh].astype(BF16)[:C - lo]], axis=-1)
                    prod = jnp.dot(m_b, rhs, preferred_element_type=F32)
                    upd, m_new = prod[:, :2 * HEAD], prod[:, 2 * HEAD:]
                    if lo:
                        m_new = jnp.concatenate([jnp.zeros((lo, C), F32), m_new], axis=0)
                    m[hh] = m_new
                else:
                    upd = jnp.dot(m_b, d_b, preferred_element_type=F32)
                if lo:
                    upd = jnp.concatenate([jnp.zeros((lo, 2 * HEAD), F32), upd], axis=0)
                d[hh] = d[hh] + upd
            yield
        o1, wz = [], []
        for hh in heads:
            gmat = jnp.concatenate(
                [d[hh].astype(BF16), jnp.concatenate([zeros_h, vh_b[hh]], axis=-1)], axis=0)
            o1.append(jnp.dot(sc_b[hh][C:], gmat, preferred_element_type=F32))
            bkh = jnp.concatenate([bhat_b[:, sls[hh]], khat_b[:, sls[hh]]], axis=0)
            wz.append(_mm_tn(gmat, bkh))
        yield
        ys = []
        for hh in heads:
            rbar = rt[:, sls[hh]] + o1[hh][:, :HEAD]
            st = s_ref[hh]
            ys.append(_mm_nt(rbar, st) + o1[hh][:, HEAD:])
            s_ref[hh] = st * p_end[:, sls[hh]] + _mm(st, wz[hh][:HEAD]) + wz[hh][HEAD:]
        out["y"] = jnp.concatenate(ys, axis=-1)

    def tail(c, y, post):
        bonus, g = post
        mean = _seg_sum(y) * (1.0 / HEAD)
        yc = y - mean
        yield
        var = _seg_sum(yc * yc) * (1.0 / HEAD)
        yield
        yn = yc * lax.rsqrt(var + LNX_EPS) * lnw_ref[...] + lnb_ref[...]
        o_ref[chunk_rows(c), :] = ((yn + bonus) * g).astype(o_ref.dtype)

    def run_interleaved(*gens):
        alive = list(gens)
        while alive:
            for gen in list(alive):
                if next(gen, "done") == "done":
                    alive.remove(gen)

    def body(i, carry):
        local, y_prev, post_prev, post_cur = carry
        out = {}
        run_interleaved(heads_stage(local, out),
                        tail(jnp.maximum(i - 1, 0), y_prev, post_prev),
                        prep(jnp.minimum(i + 1, nchunk - 1), out))
        return out["local"], out["y"], post_cur, out["post"]

    first = {}
    run_interleaved(prep(0, first))
    zeros_w = jnp.zeros((C, r_ref.shape[1]), F32)
    _, y_last, post_last, _ = lax.fori_loop(
        0, nchunk, body, (first["local"], zeros_w, (zeros_w, zeros_w), first["post"]))
    run_interleaved(tail(nchunk - 1, y_last, post_last))

    last = pl.ds(r_ref.shape[0] - 1, 1)
    prev_ref[0:1, :] = r_ref[last, :]
    prev_ref[1:2, :] = k_ref[last, :]
    prev_ref[2:3, :] = v_ref[last, :]
    prevz_ref[0:1, :] = zl_ref[last, :]


def _rwkv(zm, zl, vfirst, prm, batch, seq, hg=HEAD_GROUP, ts=RWKV_ROWS):
    T = zm.shape[0]
    W = hg * HEAD
    ng = D_RWKV // W
    nt = seq // ts
    col0 = 2 * D_LRU // W
    has_vres = vfirst is not None

    def col(off):
        return pl.BlockSpec((ts, W), lambda b, g, t: (b * nt + t, off + g))

    vecg = pl.BlockSpec((1, W), lambda b, g, t: (0, g))
    vec_k = pl.BlockSpec((1, W), lambda b, g, t: (0, ng + g))
    vec_v = pl.BlockSpec((1, W), lambda b, g, t: (0, 2 * ng + g))
    vec_l = pl.BlockSpec((1, D_LORA), lambda b, g, t: (0, 0))

    def lora(rows):
        return pl.BlockSpec((rows, W), lambda b, g, t: (0, g))

    in_specs = [col(col0), col(col0 + ng), col(col0 + 2 * ng),
                pl.BlockSpec((ts, D_LORA), lambda b, g, t: (b * nt + t, 0))]
    args = [zm, zm, zm, zl]
    if has_vres:
        in_specs.append(col(0))
        args.append(vfirst)
    in_specs += [vecg, vec_k, vec_v, vec_l, vecg, lora(LANES), vecg, lora(LANES), lora(2 * LANES)]
    args += [prm["mu_rkv"], prm["mu_rkv"], prm["mu_rkv"], prm["mu_lora"],
             prm["w0"], prm["w2"], prm["a0"], prm["a2"], prm["g2"]]
    if has_vres:
        in_specs += [vecg, lora(LANES)]
        args += [prm["v0"], prm["v2"]]
    in_specs += [vecg] * 5
    args += [prm["kk"], prm["ka"], prm["rk"], prm["lnw"], prm["lnb"]]

    if has_vres:
        out_specs = col(0)
        out_shape = jax.ShapeDtypeStruct((T, D_RWKV), BF16)
    else:
        out_specs = [col(0), col(0)]
        out_shape = [jax.ShapeDtypeStruct((T, D_RWKV), BF16),
                     jax.ShapeDtypeStruct((T, D_RWKV), F32)]
    nbytes = 2 * ts * (5 * W * 4 + D_LORA * 4 + W * 2) + hg * HEAD * HEAD * 4
    res = pl.pallas_call(
        functools.partial(_rwkv_kernel, has_vres),
        grid=(batch, ng, nt),
        in_specs=in_specs,
        out_specs=out_specs,
        out_shape=out_shape,
        scratch_shapes=[pltpu.VMEM((hg, HEAD, HEAD), F32),
                        pltpu.VMEM((SUBLANES, W), F32),
                        pltpu.VMEM((SUBLANES, D_LORA), F32)],
        compiler_params=pltpu.CompilerParams(
            dimension_semantics=("arbitrary", "arbitrary", "arbitrary"),
            vmem_limit_bytes=_vmem_limit(nbytes)),
        name="rwkv7",
    )(*args)
    if has_vres:
        return res, vfirst
    return res[0], res[1]


def _ffn_kernel(tiles_per_seq, h_ref, g_ref, wg_ref, wu_ref, cw_ref, cb_ref, wd_ref,
                o_ref, u_ref, tail_ref):
    i = pl.program_id(0)
    j = pl.program_id(1)

    @pl.when(j == 0)
    def _():
        h = h_ref[...]
        u_ref[...] = _rms(h, g_ref[...]).astype(BF16)
        o_ref[...] = h

    @pl.when(i % tiles_per_seq == 0)
    def _():
        tail_ref[j] = jnp.zeros(tail_ref.shape[1:], F32)

    u = u_ref[...]
    gate = jnp.dot(u, wg_ref[...], preferred_element_type=F32)
    tm = gate.shape[0]
    tail = tail_ref[j]
    cw = cw_ref[...]
    conv = gate * cw[FFN_CONV - 1:FFN_CONV] + cb_ref[...]
    for d in range(1, FFN_CONV):
        conv = conv + _shift_rows(gate, d, tail) * cw[FFN_CONV - 1 - d:FFN_CONV - d]
    tail_ref[j] = gate[tm - SUBLANES:]
    up = jnp.dot(u, wu_ref[...], preferred_element_type=F32)
    act = (_gelu(conv) * up).astype(BF16)
    o_ref[...] += jnp.dot(act, wd_ref[...], preferred_element_type=F32)


def _ffn(h, g, wg, wu, cw, cb, wd, layer, seq, tm=512, tf=1024):
    T = h.shape[0]
    nf = D_FF // tf
    nbytes = (2 * (2 * tm * D_MODEL * 4 + 3 * D_MODEL * tf * 2) + tm * D_MODEL * 2
              + nf * SUBLANES * tf * 4 + 6 * tm * tf * 4)
    return pl.pallas_call(
        functools.partial(_ffn_kernel, seq // tm),
        grid=(T // tm, nf),
        in_specs=[
            pl.BlockSpec((tm, D_MODEL), lambda i, j: (i, 0)),
            pl.BlockSpec((1, D_MODEL), lambda i, j: (0, 0)),
            pl.BlockSpec((None, D_MODEL, tf), lambda i, j: (layer, 0, j)),
            pl.BlockSpec((None, D_MODEL, tf), lambda i, j: (layer, 0, j)),
            pl.BlockSpec((FFN_CONV, tf), lambda i, j: (0, j)),
            pl.BlockSpec((1, tf), lambda i, j: (0, j)),
            pl.BlockSpec((None, tf, D_MODEL), lambda i, j: (layer, j, 0)),
        ],
        out_specs=pl.BlockSpec((tm, D_MODEL), lambda i, j: (i, 0)),
        out_shape=jax.ShapeDtypeStruct((T, D_MODEL), F32),
        scratch_shapes=[pltpu.VMEM((tm, D_MODEL), BF16),
                        pltpu.VMEM((nf, SUBLANES, tf), F32)],
        compiler_params=pltpu.CompilerParams(
            dimension_semantics=("arbitrary", "arbitrary"),
            vmem_limit_bytes=_vmem_limit(nbytes)),
        name="ffn",
    )(h, g, wg, wu, cw, cb, wd)


def _ple_kernel(final, h_ref, p_ref, g_ref, wg_ref, wp_ref, gp_ref, gf_ref, o_ref):
    h = h_ref[...]
    u = _rms(h, g_ref[...]).astype(BF16)
    gate = _sigmoid(jnp.dot(u, wg_ref[...], preferred_element_type=F32))
    proj = jnp.dot(p_ref[...].astype(BF16), wp_ref[...], preferred_element_type=F32)
    out = h + _rms(gate * proj, gp_ref[...])
    if final:
        out = _rms(out, gf_ref[...])
    o_ref[...] = out


def _ple(h, p, g, wg, layer, wp, gp, gf, final, tm=512):
    T = h.shape[0]
    vec = pl.BlockSpec((1, D_MODEL), lambda i: (0, 0))
    nbytes = 2 * (2 * tm * D_MODEL * 4 + tm * D_PLE * 4 + D_MODEL * D_MODEL * 2
                  + D_PLE * D_MODEL * 2) + 4 * tm * D_MODEL * 4
    return pl.pallas_call(
        functools.partial(_ple_kernel, final),
        grid=(T // tm,),
        in_specs=[
            pl.BlockSpec((tm, D_MODEL), lambda i: (i, 0)),
            pl.BlockSpec((None, tm, D_PLE), lambda i: (layer, i, 0)),
            vec,
            pl.BlockSpec((None, D_MODEL, D_MODEL), lambda i: (layer, 0, 0)),
            pl.BlockSpec((D_PLE, D_MODEL), lambda i: (0, 0)),
            vec, vec,
        ],
        out_specs=pl.BlockSpec((tm, D_MODEL), lambda i: (i, 0)),
        out_shape=jax.ShapeDtypeStruct((T, D_MODEL), F32),
        compiler_params=pltpu.CompilerParams(
            dimension_semantics=("arbitrary",),
            vmem_limit_bytes=_vmem_limit(nbytes)),
        name="ple",
    )(h, p, g, wg, wp, gp, gf)


def _row(v):
    return v.reshape(1, -1).astype(F32)


def _pad_rows(w, top, total):
    return jnp.pad(w, ((top, total - top - w.shape[0]), (0, 0)))


def kernel(x, p, ln_mix, w_in, w_in_vres, mu_shift, mu_shift_vres, conv_a_w, conv_a_b, lru_wx, lru_bx, lru_wa, lru_ba, lru_lambda, lru_norm, rwkv_w0, rwkv_w2, rwkv_a0, rwkv_a2, rwkv_v0, rwkv_v2, rwkv_g2, rwkv_kk, rwkv_ka, rwkv_rk, rwkv_lnx_w, rwkv_lnx_b, w_o, ln_ffn, w_gate, w_up, conv_f_w, conv_f_b, w_down, ln_ple, w_ple_gate, w_ple_proj, ln_ple_post, ln_final):
    batch, seq, _ = x.shape
    depth = w_in.shape[0]
    T = batch * seq
    h = x.reshape(T, D_MODEL)
    n_lora = LORA_W + LORA_A + LORA_G
    vfirst = None
    w_in_b, w_o_b, w_gate_b, w_up_b, w_down_b, w_ple_gate_b = (
        w.astype(BF16) for w in (w_in, w_o, w_gate, w_up, w_down, w_ple_gate))
    for i in range(depth):
        lora_cols = [w_in_b[i, :, D_MAIN:]]
        mu_l = [mu_shift[i][3 * D_RWKV:]]
        if i > 0:
            lora_cols.append(w_in_vres[i - 1].astype(BF16))
            mu_l.append(mu_shift_vres[i - 1])
        w_lora = jnp.concatenate(lora_cols, axis=1)
        w_lora = jnp.pad(w_lora, ((0, 0), (0, D_LORA - w_lora.shape[1])))
        mu_lora = jnp.concatenate(mu_l, axis=0)
        mu_lora = jnp.pad(mu_lora, (0, D_LORA - mu_lora.shape[0]))

        zm, zl = _inproj(h, _row(ln_mix[i]), w_in_b, i, w_lora)

        prm = {
            "mu_rkv": _row(mu_shift[i][:3 * D_RWKV]),
            "mu_lora": _row(mu_lora),
            "w0": _row(rwkv_w0[i]),
            "w2": _pad_rows(rwkv_w2[i], 0, LANES),
            "a0": _row(rwkv_a0[i]),
            "a2": _pad_rows(rwkv_a2[i], LORA_W, LANES),
            "g2": _pad_rows(rwkv_g2[i], 0, 2 * LANES),
            "kk": _row(rwkv_kk[i]), "ka": _row(rwkv_ka[i]), "rk": _row(rwkv_rk[i]),
            "lnw": _row(rwkv_lnx_w[i]), "lnb": _row(rwkv_lnx_b[i]),
        }
        if i > 0:
            prm["v0"] = _row(rwkv_v0[i - 1])
            prm["v2"] = _pad_rows(rwkv_v2[i - 1], n_lora - 2 * LANES, LANES)
        out_b, vfirst = _rwkv(zm, zl, vfirst, prm, batch, seq)

        h = _lru_oproj(zm, out_b, h, conv_a_w[i], _row(conv_a_b[i]), lru_wx[i].astype(BF16),
                       _row(lru_bx[i]), lru_wa[i].astype(BF16), _row(lru_ba[i]),
                       _row(lru_lambda[i]), _row(lru_norm[i]), w_o_b, i, batch, seq)
        h = _ffn(h, _row(ln_ffn[i]), w_gate_b, w_up_b, conv_f_w[i], _row(conv_f_b[i]),
                 w_down_b, i, seq)
        h = _ple(h, p.reshape(depth, T, D_PLE), _row(ln_ple[i]), w_ple_gate_b, i,
                 w_ple_proj[i].astype(BF16), _row(ln_ple_post[i]), _row(ln_final),
                 final=(i == depth - 1))
    return h.reshape(batch, seq, D_MODEL)
```

```python
import functools
import math

import jax
import jax.numpy as jnp
from jax import lax
from jax.experimental import pallas as pl
from jax.experimental.pallas import tpu as pltpu

F32 = jnp.float32
BF16 = jnp.bfloat16

D_MODEL = 2048
D_LRU = 1024
D_RWKV = 1024
LRU_HEADS = 4
LRU_BLOCK = 256
LRU_CONV = 4
LRU_C = 8.0
HEAD = 64
LORA_W = 64
LORA_A = 64
LORA_G = 160
D_MAIN = 2 * D_LRU + 3 * D_RWKV
D_LORA = 384
D_FF = 3 * D_MODEL
FFN_CONV = 3
D_PLE = 256
RMS_EPS = 1e-6
LNX_EPS = 64e-5

V7X_VMEM_BYTES = 64 * 1024 * 1024
SUBLANES = 8
LANES = 128
ROW_TILE = 16

CHUNK = 64
PAIR = 2 * HEAD
LRU_ROWS = 256
HEAD_GROUP = 16
RWKV_ROWS = 1024

VMEM_TEMPORARIES_BYTES = 16 * 1024 * 1024
VMEM_UNSCOPED_BYTES = 4 * 1024 * 1024


def _vmem_limit(nbytes):
    return int(min(V7X_VMEM_BYTES - VMEM_UNSCOPED_BYTES, nbytes + VMEM_TEMPORARIES_BYTES))


def _rms(x, g):
    return x * lax.rsqrt(jnp.mean(x * x, axis=-1, keepdims=True) + RMS_EPS) * g


def _gelu(x):
    c = math.sqrt(2.0 / math.pi)
    return 0.5 * x * (1.0 + jnp.tanh(c * (x + 0.044715 * (x * x * x))))


def _sigmoid(x):
    return 1.0 / (1.0 + jnp.exp(-x))


def _softplus(x):
    return jnp.maximum(x, 0.0) + jnp.log1p(jnp.exp(-jnp.abs(x)))


def _shift_rows(x, d, prev8):
    rolled = pltpu.roll(x, d, axis=0)
    prev = pltpu.roll(prev8, d, axis=0)
    row = lax.broadcasted_iota(jnp.int32, prev8.shape, 0)
    top = jnp.where(row < d, prev, rolled[:SUBLANES])
    return jnp.concatenate([top, rolled[SUBLANES:]], axis=0)


def _inproj_kernel(x_ref, g_ref, wm_ref, wl_ref, zm_ref, zl_ref, u_ref):
    @pl.when(pl.program_id(1) == 0)
    def _():
        u_ref[...] = _rms(x_ref[...], g_ref[...]).astype(BF16)
        zl_ref[...] = jnp.dot(u_ref[...], wl_ref[...], preferred_element_type=F32)

    zm_ref[...] = jnp.dot(u_ref[...], wm_ref[...], preferred_element_type=F32)


def _inproj(h, g, w_in, layer, w_lora, tm=1024, tn=1280):
    T = h.shape[0]
    nbytes = 2 * (tm * D_MODEL * 4 + D_MODEL * tn * 2 + D_MODEL * D_LORA * 2
                  + tm * tn * 4 + tm * D_LORA * 4) + tm * D_MODEL * 2
    return pl.pallas_call(
        _inproj_kernel,
        grid=(T // tm, D_MAIN // tn),
        in_specs=[
            pl.BlockSpec((tm, D_MODEL), lambda i, j: (i, 0)),
            pl.BlockSpec((1, D_MODEL), lambda i, j: (0, 0)),
            pl.BlockSpec((None, D_MODEL, tn), lambda i, j: (layer, 0, j)),
            pl.BlockSpec((D_MODEL, D_LORA), lambda i, j: (0, 0)),
        ],
        out_specs=[
            pl.BlockSpec((tm, tn), lambda i, j: (i, j)),
            pl.BlockSpec((tm, D_LORA), lambda i, j: (i, 0)),
        ],
        out_shape=[
            jax.ShapeDtypeStruct((T, D_MAIN), F32),
            jax.ShapeDtypeStruct((T, D_LORA), F32),
        ],
        scratch_shapes=[pltpu.VMEM((tm, D_MODEL), BF16)],
        compiler_params=pltpu.CompilerParams(
            dimension_semantics=("arbitrary", "arbitrary"),
            vmem_limit_bytes=_vmem_limit(nbytes)),
        name="inproj",
    )(h, g, w_in, w_lora)


def _lru_rows(xb_ref, yb_ref, rows, tail, carry, seq_start, cw, cb, wx_ref, bx, wa_ref, ba,
              sp_lam, nrm, between):
    nrows = rows.stop - rows.start
    ngroup = nrows // SUBLANES
    sub = lax.broadcasted_iota(jnp.int32, (1, SUBLANES, 1), 1)
    ys, tails, carries, ss = [], [], [], 0.0
    for hd in range(LRU_HEADS):
        cols = slice(hd * LRU_BLOCK, (hd + 1) * LRU_BLOCK)
        x = xb_ref[rows, cols]
        xc = x * cw[LRU_CONV - 1:LRU_CONV, cols] + cb[:, cols]
        for d in range(1, LRU_CONV):
            xc = xc + _shift_rows(x, d, tail[:, cols]) * cw[LRU_CONV - 1 - d:LRU_CONV - d, cols]
        tails.append(x[nrows - SUBLANES:])

        xcb = xc.astype(BF16)
        gate_x = _sigmoid(jnp.dot(xcb, wx_ref[hd], preferred_element_type=F32) + bx[:, cols])
        gate_a = _sigmoid(jnp.dot(xcb, wa_ref[hd], preferred_element_type=F32) + ba[:, cols])
        log_a = (-LRU_C) * gate_a * sp_lam[:, cols]
        a = jnp.exp(log_a)
        mult = jnp.sqrt(1.0 - a * a)
        if seq_start is not None:
            row = lax.broadcasted_iota(jnp.int32, (nrows, 1), 0)
            mult = jnp.where(jnp.logical_and(row == 0, seq_start), 1.0, mult)
        b = xc * gate_x * mult

        a = a.reshape(ngroup, SUBLANES, LRU_BLOCK)
        b = b.reshape(ngroup, SUBLANES, LRU_BLOCK)
        d = 1
        while d < SUBLANES:
            keep = sub >= d
            a_sh = jnp.where(keep, pltpu.roll(a, d, axis=1), 1.0)
            b_sh = jnp.where(keep, pltpu.roll(b, d, axis=1), 0.0)
            b = a * b_sh + b
            a = a * a_sh
            d *= 2
        hcar = carry[:, cols]
        hs = []
        for grp in range(ngroup):
            hg = a[grp] * hcar + b[grp]
            hs.append(hg)
            hcar = hg[SUBLANES - 1:SUBLANES]
        carries.append(hcar)

        y = jnp.concatenate(hs, axis=0) * _gelu(yb_ref[rows, cols])
        ss = ss + jnp.sum(y * y, axis=-1, keepdims=True)
        ys.append(y)
        between(hd)
    scale = lax.rsqrt(ss * (1.0 / D_LRU) + RMS_EPS)
    out = jnp.concatenate(ys, axis=-1) * scale * nrm
    return out.astype(BF16), jnp.concatenate(tails, axis=-1), jnp.concatenate(carries, axis=-1)


def _lru_oproj_kernel(xb_ref, yb_ref, ob_ref, h_ref, cw_ref, cb_ref, wx_ref, bx_ref, wa_ref,
                      ba_ref, lam_ref, nrm_ref, woa_ref, wob_ref, o_ref, tail_ref, carry_ref):
    t = pl.program_id(1)

    @pl.when(t == 0)
    def _():
        tail_ref[...] = jnp.zeros_like(tail_ref)
        carry_ref[...] = jnp.zeros_like(carry_ref)

    sp_lam = _softplus(-lam_ref[...])
    tail, carry = tail_ref[...], carry_ref[0:1, :]
    ts = xb_ref.shape[0]
    ncol = D_MODEL // LRU_HEADS
    pieces = {}

    def project(name, lhs, w_ref):
        def step(hd):
            cols = slice(hd * ncol, (hd + 1) * ncol)
            pieces.setdefault(name, []).append(
                jnp.dot(lhs, w_ref[:, cols], preferred_element_type=F32))
        return step

    between = project("b", ob_ref[...], wob_ref)
    for r0 in range(0, ts, LRU_ROWS):
        out_a, tail, carry = _lru_rows(
            xb_ref, yb_ref, slice(r0, r0 + LRU_ROWS), tail, carry, (t == 0) if r0 == 0 else None,
            cw_ref[...], cb_ref[...], wx_ref, bx_ref[...], wa_ref, ba_ref[...], sp_lam,
            nrm_ref[...], between)
        between = project(("a", r0), out_a, woa_ref)
    for hd in range(LRU_HEADS):
        between(hd)
    tail_ref[...] = tail
    carry_ref[0:1, :] = carry
    acc_a = jnp.concatenate(
        [jnp.concatenate(pieces["a", r0], axis=-1) for r0 in range(0, ts, LRU_ROWS)], axis=0)
    o_ref[...] = h_ref[...] + jnp.concatenate(pieces["b"], axis=-1) + acc_a


def _lru_oproj(zm, out_b, h, cw, cb, wx, bx, wa, ba, lam, nrm, wo, layer, batch, seq, ts=512):
    T = zm.shape[0]
    nt = seq // ts
    vec = pl.BlockSpec((1, D_LRU), lambda b, t: (0, 0))
    mat = pl.BlockSpec((LRU_HEADS, LRU_BLOCK, LRU_BLOCK), lambda b, t: (0, 0, 0))

    def rows(width, col):
        return pl.BlockSpec((ts, width), lambda b, t: (b * nt + t, col))

    nbytes = (2 * (2 * ts * D_LRU * 4 + ts * D_RWKV * 2 + 2 * ts * D_MODEL * 4
                   + 2 * D_LRU * D_MODEL * 2) + 16 * LRU_ROWS * D_LRU * 4 + 2 * ts * D_MODEL * 4)
    return pl.pallas_call(
        _lru_oproj_kernel,
        grid=(batch, nt),
        in_specs=[
            rows(D_LRU, 0), rows(D_LRU, 1), rows(D_RWKV, 0), rows(D_MODEL, 0),
            pl.BlockSpec((LRU_CONV, D_LRU), lambda b, t: (0, 0)),
            vec, mat, vec, mat, vec, vec, vec,
            pl.BlockSpec((None, D_LRU, D_MODEL), lambda b, t: (layer, 0, 0)),
            pl.BlockSpec((None, D_RWKV, D_MODEL), lambda b, t: (layer, 1, 0)),
        ],
        out_specs=rows(D_MODEL, 0),
        out_shape=jax.ShapeDtypeStruct((T, D_MODEL), F32),
        scratch_shapes=[pltpu.VMEM((SUBLANES, D_LRU), F32),
                        pltpu.VMEM((SUBLANES, D_LRU), F32)],
        compiler_params=pltpu.CompilerParams(
            dimension_semantics=("arbitrary", "arbitrary"),
            vmem_limit_bytes=_vmem_limit(nbytes)),
        name="lru_oproj",
    )(zm, zm, out_b, h, cw, cb, wx, bx, wa, ba, lam, nrm, wo, wo)


def _mm(a, b):
    return jnp.dot(a.astype(BF16), b.astype(BF16), preferred_element_type=F32)


def _mm_nt(a, b):
    return lax.dot_general(a.astype(BF16), b.astype(BF16), (((1,), (1,)), ((), ())),
                           preferred_element_type=F32)


def _mm_tn(a, b):
    return lax.dot_general(a.astype(BF16), b.astype(BF16), (((0,), (0,)), ((), ())),
                           preferred_element_type=F32)


def _split3(x):
    hi = x.astype(BF16)
    r1 = x - hi.astype(F32)
    mid = r1.astype(BF16)
    lo = (r1 - mid.astype(F32)).astype(BF16)
    return hi, mid, lo


def _seg_sum(x):
    lane_lo = lax.broadcasted_iota(jnp.int32, (1, PAIR), 1) < HEAD
    out = []
    for p in range(x.shape[1] // PAIR):
        t = x[:, p * PAIR:(p + 1) * PAIR]
        s0 = jnp.sum(jnp.where(lane_lo, t, 0.0), axis=-1, keepdims=True)
        s1 = jnp.sum(jnp.where(lane_lo, 0.0, t), axis=-1, keepdims=True)
        out.append(jnp.where(lane_lo, s0, s1))
    return jnp.concatenate(out, axis=-1)


def _rwkv_kernel(has_vres, *refs):
    if has_vres:
        (r_ref, k_ref, v_ref, zl_ref, vf_ref, mur_ref, muk_ref, muv_ref, mul_ref,
         w0_ref, w2_ref, a0_ref, a2_ref, g2_ref, v0_ref, v2_ref,
         kkw_ref, ka_ref, rk_ref, lnw_ref, lnb_ref, o_ref, s_ref, prev_ref, prevz_ref) = refs
        vfo_ref = None
    else:
        (r_ref, k_ref, v_ref, zl_ref, mur_ref, muk_ref, muv_ref, mul_ref,
         w0_ref, w2_ref, a0_ref, a2_ref, g2_ref,
         kkw_ref, ka_ref, rk_ref, lnw_ref, lnb_ref, o_ref, vfo_ref,
         s_ref, prev_ref, prevz_ref) = refs
        vf_ref = v0_ref = v2_ref = None

    C = CHUNK
    nchunk = r_ref.shape[0] // C
    nh = r_ref.shape[1] // HEAD

    @pl.when(pl.program_id(2) == 0)
    def _():
        s_ref[...] = jnp.zeros_like(s_ref)
        prev_ref[...] = jnp.zeros_like(prev_ref)
        prevz_ref[...] = jnp.zeros_like(prevz_ref)

    ri3 = lax.broadcasted_iota(jnp.int32, (C, 3 * C), 0)
    ci3 = lax.broadcasted_iota(jnp.int32, (C, 3 * C), 1) % C
    tri3 = (ri3 >= ci3).astype(BF16)
    ri2 = lax.broadcasted_iota(jnp.int32, (2 * C, 2 * C), 0)
    ci2 = lax.broadcasted_iota(jnp.int32, (2 * C, 2 * C), 1) % C
    keep2 = jnp.where(ri2 < C, ri2, ri2 - C + 1) > ci2
    row0 = lax.broadcasted_iota(jnp.int32, (C, 1), 0) == 0
    zeros_h = jnp.zeros((C, HEAD), BF16)

    def shift_lerp(cur, prev_row, mu):
        sh = jnp.where(row0, prev_row, pltpu.roll(cur, 1, axis=0))
        return cur + (sh - cur) * mu

    def chunk_rows(c):
        return pl.ds(pl.multiple_of(c * C, C), C)

    def prep(c, out):
        rows = chunk_rows(c)
        first = c == 0
        before = pl.ds(jnp.maximum(c * C - 1, 0), 1)

        def lerp(ref, carried, mu):
            return shift_lerp(ref[rows, :], jnp.where(first, carried, ref[before, :]), mu)

        zl = lerp(zl_ref, prevz_ref[0:1, :], mul_ref[...])
        z01 = zl[:, 0:LANES]
        wpre = w0_ref[...] + _mm(jnp.tanh(z01), w2_ref[...])
        apre = a0_ref[...] + _mm(z01, a2_ref[...])
        g = _mm(_sigmoid(zl[:, LANES:3 * LANES]), g2_ref[...])
        if has_vres:
            mpre = v0_ref[...] + _mm(zl[:, 2 * LANES:3 * LANES], v2_ref[...])
        yield
        w_log = -_softplus(-wpre) - 0.5
        logw = -jnp.exp(w_log)
        cum = jnp.dot(tri3, jnp.concatenate(_split3(logw), axis=0), preferred_element_type=F32)
        yield
        r = lerp(r_ref, prev_ref[0:1, :], mur_ref[...])
        k = lerp(k_ref, prev_ref[1:2, :], muk_ref[...])
        a = _sigmoid(apre)
        yield
        v = lerp(v_ref, prev_ref[2:3, :], muv_ref[...])
        if has_vres:
            v = v + (vf_ref[rows, :] - v) * _sigmoid(mpre)
        else:
            vfo_ref[rows, :] = v
        yield
        kk = k * kkw_ref[...]
        kk = kk / jnp.maximum(jnp.sqrt(_seg_sum(kk * kk)), 1e-12)
        yield
        k2 = k * (1.0 + (a - 1.0) * ka_ref[...])
        bb = kk * a
        bonus = _seg_sum(r * k2 * rk_ref[...]) * v
        yield
        p_in = jnp.exp(cum)
        p_ex = jnp.exp(cum - logw)
        p_inv = jnp.exp(-cum)
        p_end = p_in[C - 1:C, :]
        yield
        rt = r * p_in
        at = -kk * p_ex
        rt_b, at_b, v_b = rt.astype(BF16), at.astype(BF16), v.astype(BF16)
        yield
        bt = bb * p_inv
        kt = k2 * p_inv
        bk_t = jnp.transpose(jnp.concatenate([bt, kt], axis=0)).astype(BF16)
        yield
        bhat_b = (bt * p_end).astype(BF16)
        khat_b = (kt * p_end).astype(BF16)
        out["local"] = (rt_b, at_b, bk_t, v_b, bhat_b, khat_b, at, rt, p_end)
        out["post"] = (bonus, g)

    def heads_stage(local, out):
        rt_b, at_b, bk_t, v_b, bhat_b, khat_b, at, rt, p_end = local
        heads = range(nh)
        sls = [slice(hh * HEAD, (hh + 1) * HEAD) for hh in heads]
        sc_b, m, vh_b, d = [], [], [], []
        for sl in sls:
            ar = jnp.concatenate([at_b[:, sl], rt_b[:, sl]], axis=0)
            sc = jnp.dot(ar, bk_t[sl, :], preferred_element_type=F32)
            sc = jnp.where(keep2, sc, 0.0)
            sc_b.append(sc.astype(BF16))
            m.append(sc[:C, :C])
            vh_b.append(v_b[:, sl])
        yield
        for hh in heads:
            zv = jnp.concatenate([zeros_h, vh_b[hh]], axis=0)
            x_loc = jnp.dot(sc_b[hh][:C], zv, preferred_element_type=F32)
            d.append(jnp.concatenate([at[:, sls[hh]], x_loc], axis=-1))
        yield
        nstep = int(math.log2(C))
        for i in range(nstep):
            lo = (2 ** i // ROW_TILE) * ROW_TILE
            for hh in heads:
                m_b = m[hh].astype(BF16)[lo:, :C - lo]
                d_b = d[hh].astype(BF16)[:C - lo]
                if i + 1 < nstep:
                    rhs = jnp.concatenate([d_b, m[hh].astype(BF16)[:C - lo]], axis=-1)
                    prod = jnp.dot(m_b, rhs, preferred_element_type=F32)
                    upd, m_new = prod[:, :2 * HEAD], prod[:, 2 * HEAD:]
                    if lo:
                        m_new = jnp.concatenate([jnp.zeros((lo, C), F32), m_new], axis=0)
                    m[hh] = m_new
                else:
                    upd = jnp.dot(m_b, d_b, preferred_element_type=F32)
                if lo:
                    upd = jnp.concatenate([jnp.zeros((lo, 2 * HEAD), F32), upd], axis=0)
                d[hh] = d[hh] + upd
            yield
        o1, wz = [], []
        for hh in heads:
            gmat = jnp.concatenate(
                [d[hh].astype(BF16), jnp.concatenate([zeros_h, vh_b[hh]], axis=-1)], axis=0)
            o1.append(jnp.dot(sc_b[hh][C:], gmat, preferred_element_type=F32))
            bkh = jnp.concatenate([bhat_b[:, sls[hh]], khat_b[:, sls[hh]]], axis=0)
            wz.append(_mm_tn(gmat, bkh))
        yield
        ys = []
        for hh in heads:
            rbar = rt[:, sls[hh]] + o1[hh][:, :HEAD]
            st = s_ref[hh]
            ys.append(_mm_nt(rbar, st) + o1[hh][:, HEAD:])
            s_ref[hh] = st * p_end[:, sls[hh]] + _mm(st, wz[hh][:HEAD]) + wz[hh][HEAD:]
        out["y"] = jnp.concatenate(ys, axis=-1)

    def tail(c, y, post):
        bonus, g = post
        mean = _seg_sum(y) * (1.0 / HEAD)
        yc = y - mean
        yield
        var = _seg_sum(yc * yc) * (1.0 / HEAD)
        yield
        yn = yc * lax.rsqrt(var + LNX_EPS) * lnw_ref[...] + lnb_ref[...]
        o_ref[chunk_rows(c), :] = ((yn + bonus) * g).astype(o_ref.dtype)

    def run_interleaved(*gens):
        alive = list(gens)
        while alive:
            for gen in list(alive):
                if next(gen, "done") == "done":
                    alive.remove(gen)

    def body(i, carry):
        local, y_prev, post_prev, post_cur = carry
        out = {}
        run_interleaved(heads_stage(local, out),
                        tail(jnp.maximum(i - 1, 0), y_prev, post_prev),
                        prep(jnp.minimum(i + 1, nchunk - 1), out))
        return out["local"], out["y"], post_cur, out["post"]

    first = {}
    run_interleaved(prep(0, first))
    zeros_w = jnp.zeros((C, r_ref.shape[1]), F32)
    _, y_last, post_last, _ = lax.fori_loop(
        0, nchunk, body, (first["local"], zeros_w, (zeros_w, zeros_w), first["post"]))
    run_interleaved(tail(nchunk - 1, y_last, post_last))

    last = pl.ds(r_ref.shape[0] - 1, 1)
    prev_ref[0:1, :] = r_ref[last, :]
    prev_ref[1:2, :] = k_ref[last, :]
    prev_ref[2:3, :] = v_ref[last, :]
    prevz_ref[0:1, :] = zl_ref[last, :]


def _rwkv(zm, zl, vfirst, prm, batch, seq, hg=HEAD_GROUP, ts=RWKV_ROWS):
    T = zm.shape[0]
    W = hg * HEAD
    ng = D_RWKV // W
    nt = seq // ts
    col0 = 2 * D_LRU // W
    has_vres = vfirst is not None

    def col(off):
        return pl.BlockSpec((ts, W), lambda b, g, t: (b * nt + t, off + g))

    vecg = pl.BlockSpec((1, W), lambda b, g, t: (0, g))
    vec_k = pl.BlockSpec((1, W), lambda b, g, t: (0, ng + g))
    vec_v = pl.BlockSpec((1, W), lambda b, g, t: (0, 2 * ng + g))
    vec_l = pl.BlockSpec((1, D_LORA), lambda b, g, t: (0, 0))

    def lora(rows):
        return pl.BlockSpec((rows, W), lambda b, g, t: (0, g))

    in_specs = [col(col0), col(col0 + ng), col(col0 + 2 * ng),
                pl.BlockSpec((ts, D_LORA), lambda b, g, t: (b * nt + t, 0))]
    args = [zm, zm, zm, zl]
    if has_vres:
        in_specs.append(col(0))
        args.append(vfirst)
    in_specs += [vecg, vec_k, vec_v, vec_l, vecg, lora(LANES), vecg, lora(LANES), lora(2 * LANES)]
    args += [prm["mu_rkv"], prm["mu_rkv"], prm["mu_rkv"], prm["mu_lora"],
             prm["w0"], prm["w2"], prm["a0"], prm["a2"], prm["g2"]]
    if has_vres:
        in_specs += [vecg, lora(LANES)]
        args += [prm["v0"], prm["v2"]]
    in_specs += [vecg] * 5
    args += [prm["kk"], prm["ka"], prm["rk"], prm["lnw"], prm["lnb"]]

    if has_vres:
        out_specs = col(0)
        out_shape = jax.ShapeDtypeStruct((T, D_RWKV), BF16)
    else:
        out_specs = [col(0), col(0)]
        out_shape = [jax.ShapeDtypeStruct((T, D_RWKV), BF16),
                     jax.ShapeDtypeStruct((T, D_RWKV), F32)]
    nbytes = 2 * ts * (5 * W * 4 + D_LORA * 4 + W * 2) + hg * HEAD * HEAD * 4
    res = pl.pallas_call(
        functools.partial(_rwkv_kernel, has_vres),
        grid=(batch, ng, nt),
        in_specs=in_specs,
        out_specs=out_specs,
        out_shape=out_shape,
        scratch_shapes=[pltpu.VMEM((hg, HEAD, HEAD), F32),
                        pltpu.VMEM((SUBLANES, W), F32),
                        pltpu.VMEM((SUBLANES, D_LORA), F32)],
        compiler_params=pltpu.CompilerParams(
            dimension_semantics=("arbitrary", "arbitrary", "arbitrary"),
            vmem_limit_bytes=_vmem_limit(nbytes)),
        name="rwkv7",
    )(*args)
    if has_vres:
        return res, vfirst
    return res[0], res[1]


def _ffn_kernel(tiles_per_seq, h_ref, g_ref, wg_ref, wu_ref, cw_ref, cb_ref, wd_ref,
                o_ref, u_ref, tail_ref):
    i = pl.program_id(0)
    j = pl.program_id(1)

    @pl.when(j == 0)
    def _():
        h = h_ref[...]
        u_ref[...] = _rms(h, g_ref[...]).astype(BF16)
        o_ref[...] = h

    @pl.when(i % tiles_per_seq == 0)
    def _():
        tail_ref[j] = jnp.zeros(tail_ref.shape[1:], F32)

    u = u_ref[...]
    gate = jnp.dot(u, wg_ref[...], preferred_element_type=F32)
    tm = gate.shape[0]
    tail = tail_ref[j]
    cw = cw_ref[...]
    conv = gate * cw[FFN_CONV - 1:FFN_CONV] + cb_ref[...]
    for d in range(1, FFN_CONV):
        conv = conv + _shift_rows(gate, d, tail) * cw[FFN_CONV - 1 - d:FFN_CONV - d]
    tail_ref[j] = gate[tm - SUBLANES:]
    up = jnp.dot(u, wu_ref[...], preferred_element_type=F32)
    act = (_gelu(conv) * up).astype(BF16)
    o_ref[...] += jnp.dot(act, wd_ref[...], preferred_element_type=F32)


def _ffn(h, g, wg, wu, cw, cb, wd, layer, seq, tm=512, tf=1024):
    T = h.shape[0]
    nf = D_FF // tf
    nbytes = (2 * (2 * tm * D_MODEL * 4 + 3 * D_MODEL * tf * 2) + tm * D_MODEL * 2
              + nf * SUBLANES * tf * 4 + 6 * tm * tf * 4)
    return pl.pallas_call(
        functools.partial(_ffn_kernel, seq // tm),
        grid=(T // tm, nf),
        in_specs=[
            pl.BlockSpec((tm, D_MODEL), lambda i, j: (i, 0)),
            pl.BlockSpec((1, D_MODEL), lambda i, j: (0, 0)),
            pl.BlockSpec((None, D_MODEL, tf), lambda i, j: (layer, 0, j)),
            pl.BlockSpec((None, D_MODEL, tf), lambda i, j: (layer, 0, j)),
            pl.BlockSpec((FFN_CONV, tf), lambda i, j: (0, j)),
            pl.BlockSpec((1, tf), lambda i, j: (0, j)),
            pl.BlockSpec((None, tf, D_MODEL), lambda i, j: (layer, j, 0)),
        ],
        out_specs=pl.BlockSpec((tm, D_MODEL), lambda i, j: (i, 0)),
        out_shape=jax.ShapeDtypeStruct((T, D_MODEL), F32),
        scratch_shapes=[pltpu.VMEM((tm, D_MODEL), BF16),
                        pltpu.VMEM((nf, SUBLANES, tf), F32)],
        compiler_params=pltpu.CompilerParams(
            dimension_semantics=("arbitrary", "arbitrary"),
            vmem_limit_bytes=_vmem_limit(nbytes)),
        name="ffn",
    )(h, g, wg, wu, cw, cb, wd)


def _ple_kernel(final, h_ref, p_ref, g_ref, wg_ref, wp_ref, gp_ref, gf_ref, o_ref):
    h = h_ref[...]
    u = _rms(h, g_ref[...]).astype(BF16)
    gate = _sigmoid(jnp.dot(u, wg_ref[...], preferred_element_type=F32))
    proj = jnp.dot(p_ref[...].astype(BF16), wp_ref[...], preferred_element_type=F32)
    out = h + _rms(gate * proj, gp_ref[...])
    if final:
        out = _rms(out, gf_ref[...])
    o_ref[...] = out


def _ple(h, p, g, wg, layer, wp, gp, gf, final, tm=512):
    T = h.shape[0]
    vec = pl.BlockSpec((1, D_MODEL), lambda i: (0, 0))
    nbytes = 2 * (2 * tm * D_MODEL * 4 + tm * D_PLE * 4 + D_MODEL * D_MODEL * 2
                  + D_PLE * D_MODEL * 2) + 4 * tm * D_MODEL * 4
    return pl.pallas_call(
        functools.partial(_ple_kernel, final),
        grid=(T // tm,),
        in_specs=[
            pl.BlockSpec((tm, D_MODEL), lambda i: (i, 0)),
            pl.BlockSpec((None, tm, D_PLE), lambda i: (layer, i, 0)),
            vec,
            pl.BlockSpec((None, D_MODEL, D_MODEL), lambda i: (layer, 0, 0)),
            pl.BlockSpec((D_PLE, D_MODEL), lambda i: (0, 0)),
            vec, vec,
        ],
        out_specs=pl.BlockSpec((tm, D_MODEL), lambda i: (i, 0)),
        out_shape=jax.ShapeDtypeStruct((T, D_MODEL), F32),
        compiler_params=pltpu.CompilerParams(
            dimension_semantics=("arbitrary",),
            vmem_limit_bytes=_vmem_limit(nbytes)),
        name="ple",
    )(h, p, g, wg, wp, gp, gf)


def _row(v):
    return v.reshape(1, -1).astype(F32)


def _pad_rows(w, top, total):
    return jnp.pad(w, ((top, total - top - w.shape[0]), (0, 0)))


def kernel(x, p, ln_mix, w_in, w_in_vres, mu_shift, mu_shift_vres, conv_a_w, conv_a_b, lru_wx, lru_bx, lru_wa, lru_ba, lru_lambda, lru_norm, rwkv_w0, rwkv_w2, rwkv_a0, rwkv_a2, rwkv_v0, rwkv_v2, rwkv_g2, rwkv_kk, rwkv_ka, rwkv_rk, rwkv_lnx_w, rwkv_lnx_b, w_o, ln_ffn, w_gate, w_up, conv_f_w, conv_f_b, w_down, ln_ple, w_ple_gate, w_ple_proj, ln_ple_post, ln_final):
    batch, seq, _ = x.shape
    depth = w_in.shape[0]
    T = batch * seq
    h = x.reshape(T, D_MODEL)
    n_lora = LORA_W + LORA_A + LORA_G
    vfirst = None
    w_in_b, w_o_b, w_gate_b, w_up_b, w_down_b, w_ple_gate_b = (
        w.astype(BF16) for w in (w_in, w_o, w_gate, w_up, w_down, w_ple_gate))
    for i in range(depth):
        lora_cols = [w_in_b[i, :, D_MAIN:]]
        mu_l = [mu_shift[i][3 * D_RWKV:]]
        if i > 0:
            lora_cols.append(w_in_vres[i - 1].astype(BF16))
            mu_l.append(mu_shift_vres[i - 1])
        w_lora = jnp.concatenate(lora_cols, axis=1)
        w_lora = jnp.pad(w_lora, ((0, 0), (0, D_LORA - w_lora.shape[1])))
        mu_lora = jnp.concatenate(mu_l, axis=0)
        mu_lora = jnp.pad(mu_lora, (0, D_LORA - mu_lora.shape[0]))

        zm, zl = _inproj(h, _row(ln_mix[i]), w_in_b, i, w_lora)

        prm = {
            "mu_rkv": _row(mu_shift[i][:3 * D_RWKV]),
            "mu_lora": _row(mu_lora),
            "w0": _row(rwkv_w0[i]),
            "w2": _pad_rows(rwkv_w2[i], 0, LANES),
            "a0": _row(rwkv_a0[i]),
            "a2": _pad_rows(rwkv_a2[i], LORA_W, LANES),
            "g2": _pad_rows(rwkv_g2[i], 0, 2 * LANES),
            "kk": _row(rwkv_kk[i]), "ka": _row(rwkv_ka[i]), "rk": _row(rwkv_rk[i]),
            "lnw": _row(rwkv_lnx_w[i]), "lnb": _row(rwkv_lnx_b[i]),
        }
        if i > 0:
            prm["v0"] = _row(rwkv_v0[i - 1])
            prm["v2"] = _pad_rows(rwkv_v2[i - 1], n_lora - 2 * LANES, LANES)
        out_b, vfirst = _rwkv(zm, zl, vfirst, prm, batch, seq)

        h = _lru_oproj(zm, out_b, h, conv_a_w[i], _row(conv_a_b[i]), lru_wx[i].astype(BF16),
                       _row(lru_bx[i]), lru_wa[i].astype(BF16), _row(lru_ba[i]),
                       _row(lru_lambda[i]), _row(lru_norm[i]), w_o_b, i, batch, seq)
        h = _ffn(h, _row(ln_ffn[i]), w_gate_b, w_up_b, conv_f_w[i], _row(conv_f_b[i]),
                 w_down_b, i, seq)
        h = _ple(h, p.reshape(depth, T, D_PLE), _row(ln_ple[i]), w_ple_gate_b, i,
                 w_ple_proj[i].astype(BF16), _row(ln_ple_post[i]), _row(ln_final),
                 final=(i == depth - 1))
    return h.reshape(batch, seq, D_MODEL)
```

```python
import functools
import math

import jax
import jax.numpy as jnp
from jax import lax
from jax.experimental import pallas as pl
from jax.experimental.pallas import tpu as pltpu

F32 = jnp.float32
BF16 = jnp.bfloat16

D_MODEL = 2048
D_LRU = 1024
D_RWKV = 1024
LRU_HEADS = 4
LRU_BLOCK = 256
LRU_CONV = 4
LRU_C = 8.0
HEAD = 64
LORA_W = 64
LORA_A = 64
LORA_G = 160
D_MAIN = 2 * D_LRU + 3 * D_RWKV
D_LORA = 384
D_FF = 3 * D_MODEL
FFN_CONV = 3
D_PLE = 256
RMS_EPS = 1e-6
LNX_EPS = 64e-5

V7X_VMEM_BYTES = 64 * 1024 * 1024
SUBLANES = 8
LANES = 128
ROW_TILE = 16

CHUNK = 64
PAIR = 2 * HEAD
LRU_ROWS = 256
HEAD_GROUP = 16
RWKV_ROWS = 1024

VMEM_TEMPORARIES_BYTES = 16 * 1024 * 1024
VMEM_UNSCOPED_BYTES = 4 * 1024 * 1024


def _vmem_limit(nbytes):
    return int(min(V7X_VMEM_BYTES - VMEM_UNSCOPED_BYTES, nbytes + VMEM_TEMPORARIES_BYTES))


def _rms(x, g):
    return x * lax.rsqrt(jnp.mean(x * x, axis=-1, keepdims=True) + RMS_EPS) * g


def _gelu(x):
    c = math.sqrt(2.0 / math.pi)
    return 0.5 * x * (1.0 + jnp.tanh(c * (x + 0.044715 * (x * x * x))))


def _sigmoid(x):
    return 1.0 / (1.0 + jnp.exp(-x))


def _softplus(x):
    return jnp.maximum(x, 0.0) + jnp.log1p(jnp.exp(-jnp.abs(x)))


def _shift_rows(x, d, prev8):
    rolled = pltpu.roll(x, d, axis=0)
    prev = pltpu.roll(prev8, d, axis=0)
    row = lax.broadcasted_iota(jnp.int32, prev8.shape, 0)
    top = jnp.where(row < d, prev, rolled[:SUBLANES])
    return jnp.concatenate([top, rolled[SUBLANES:]], axis=0)


def _inproj_kernel(x_ref, g_ref, wm_ref, wl_ref, zm_ref, zl_ref, u_ref):
    @pl.when(pl.program_id(1) == 0)
    def _():
        u_ref[...] = _rms(x_ref[...], g_ref[...]).astype(BF16)
        zl_ref[...] = jnp.dot(u_ref[...], wl_ref[...], preferred_element_type=F32)

    zm_ref[...] = jnp.dot(u_ref[...], wm_ref[...], preferred_element_type=F32)


def _inproj(h, g, w_in, layer, w_lora, tm=1024, tn=1280):
    T = h.shape[0]
    nbytes = 2 * (tm * D_MODEL * 4 + D_MODEL * tn * 2 + D_MODEL * D_LORA * 2
                  + tm * tn * 4 + tm * D_LORA * 4) + tm * D_MODEL * 2
    return pl.pallas_call(
        _inproj_kernel,
        grid=(T // tm, D_MAIN // tn),
        in_specs=[
            pl.BlockSpec((tm, D_MODEL), lambda i, j: (i, 0)),
            pl.BlockSpec((1, D_MODEL), lambda i, j: (0, 0)),
            pl.BlockSpec((None, D_MODEL, tn), lambda i, j: (layer, 0, j)),
            pl.BlockSpec((D_MODEL, D_LORA), lambda i, j: (0, 0)),
        ],
        out_specs=[
            pl.BlockSpec((tm, tn), lambda i, j: (i, j)),
            pl.BlockSpec((tm, D_LORA), lambda i, j: (i, 0)),
        ],
        out_shape=[
            jax.ShapeDtypeStruct((T, D_MAIN), F32),
            jax.ShapeDtypeStruct((T, D_LORA), F32),
        ],
        scratch_shapes=[pltpu.VMEM((tm, D_MODEL), BF16)],
        compiler_params=pltpu.CompilerParams(
            dimension_semantics=("arbitrary", "arbitrary"),
            vmem_limit_bytes=_vmem_limit(nbytes)),
        name="inproj",
    )(h, g, w_in, w_lora)


def _lru_rows(xb_ref, yb_ref, rows, tail, carry, seq_start, cw, cb, wx_ref, bx, wa_ref, ba,
              sp_lam, nrm, between):
    nrows = rows.stop - rows.start
    ngroup = nrows // SUBLANES
    sub = lax.broadcasted_iota(jnp.int32, (1, SUBLANES, 1), 1)
    ys, tails, carries, ss = [], [], [], 0.0
    for hd in range(LRU_HEADS):
        cols = slice(hd * LRU_BLOCK, (hd + 1) * LRU_BLOCK)
        x = xb_ref[rows, cols]
        xc = x * cw[LRU_CONV - 1:LRU_CONV, cols] + cb[:, cols]
        for d in range(1, LRU_CONV):
            xc = xc + _shift_rows(x, d, tail[:, cols]) * cw[LRU_CONV - 1 - d:LRU_CONV - d, cols]
        tails.append(x[nrows - SUBLANES:])

        xcb = xc.astype(BF16)
        gate_x = _sigmoid(jnp.dot(xcb, wx_ref[hd], preferred_element_type=F32) + bx[:, cols])
        gate_a = _sigmoid(jnp.dot(xcb, wa_ref[hd], preferred_element_type=F32) + ba[:, cols])
        log_a = (-LRU_C) * gate_a * sp_lam[:, cols]
        a = jnp.exp(log_a)
        mult = jnp.sqrt(1.0 - a * a)
        if seq_start is not None:
            row = lax.broadcasted_iota(jnp.int32, (nrows, 1), 0)
            mult = jnp.where(jnp.logical_and(row == 0, seq_start), 1.0, mult)
        b = xc * gate_x * mult

        a = a.reshape(ngroup, SUBLANES, LRU_BLOCK)
        b = b.reshape(ngroup, SUBLANES, LRU_BLOCK)
        d = 1
        while d < SUBLANES:
            keep = sub >= d
            a_sh = jnp.where(keep, pltpu.roll(a, d, axis=1), 1.0)
            b_sh = jnp.where(keep, pltpu.roll(b, d, axis=1), 0.0)
            b = a * b_sh + b
            a = a * a_sh
            d *= 2
        hcar = carry[:, cols]
        hs = []
        for grp in range(ngroup):
            hg = a[grp] * hcar + b[grp]
            hs.append(hg)
            hcar = hg[SUBLANES - 1:SUBLANES]
        carries.append(hcar)

        y = jnp.concatenate(hs, axis=0) * _gelu(yb_ref[rows, cols])
        ss = ss + jnp.sum(y * y, axis=-1, keepdims=True)
        ys.append(y)
        between(hd)
    scale = lax.rsqrt(ss * (1.0 / D_LRU) + RMS_EPS)
    out = jnp.concatenate(ys, axis=-1) * scale * nrm
    return out.astype(BF16), jnp.concatenate(tails, axis=-1), jnp.concatenate(carries, axis=-1)


def _lru_oproj_kernel(xb_ref, yb_ref, ob_ref, h_ref, cw_ref, cb_ref, wx_ref, bx_ref, wa_ref,
                      ba_ref, lam_ref, nrm_ref, woa_ref, wob_ref, o_ref, tail_ref, carry_ref):
    t = pl.program_id(1)

    @pl.when(t == 0)
    def _():
        tail_ref[...] = jnp.zeros_like(tail_ref)
        carry_ref[...] = jnp.zeros_like(carry_ref)

    sp_lam = _softplus(-lam_ref[...])
    tail, carry = tail_ref[...], carry_ref[0:1, :]
    ts = xb_ref.shape[0]
    ncol = D_MODEL // LRU_HEADS
    pieces = {}

    def project(name, lhs, w_ref):
        def step(hd):
            cols = slice(hd * ncol, (hd + 1) * ncol)
            pieces.setdefault(name, []).append(
                jnp.dot(lhs, w_ref[:, cols], preferred_element_type=F32))
        return step

    between = project("b", ob_ref[...], wob_ref)
    for r0 in range(0, ts, LRU_ROWS):
        out_a, tail, carry = _lru_rows(
            xb_ref, yb_ref, slice(r0, r0 + LRU_ROWS), tail, carry, (t == 0) if r0 == 0 else None,
            cw_ref[...], cb_ref[...], wx_ref, bx_ref[...], wa_ref, ba_ref[...], sp_lam,
            nrm_ref[...], between)
        between = project(("a", r0), out_a, woa_ref)
    for hd in range(LRU_HEADS):
        between(hd)
    tail_ref[...] = tail
    carry_ref[0:1, :] = carry
    acc_a = jnp.concatenate(
        [jnp.concatenate(pieces["a", r0], axis=-1) for r0 in range(0, ts, LRU_ROWS)], axis=0)
    o_ref[...] = h_ref[...] + jnp.concatenate(pieces["b"], axis=-1) + acc_a


def _lru_oproj(zm, out_b, h, cw, cb, wx, bx, wa, ba, lam, nrm, wo, layer, batch, seq, ts=512):
    T = zm.shape[0]
    nt = seq // ts
    vec = pl.BlockSpec((1, D_LRU), lambda b, t: (0, 0))
    mat = pl.BlockSpec((LRU_HEADS, LRU_BLOCK, LRU_BLOCK), lambda b, t: (0, 0, 0))

    def rows(width, col):
        return pl.BlockSpec((ts, width), lambda b, t: (b * nt + t, col))

    nbytes = (2 * (2 * ts * D_LRU * 4 + ts * D_RWKV * 2 + 2 * ts * D_MODEL * 4
                   + 2 * D_LRU * D_MODEL * 2) + 16 * LRU_ROWS * D_LRU * 4 + 2 * ts * D_MODEL * 4)
    return pl.pallas_call(
        _lru_oproj_kernel,
        grid=(batch, nt),
        in_specs=[
            rows(D_LRU, 0), rows(D_LRU, 1), rows(D_RWKV, 0), rows(D_MODEL, 0),
            pl.BlockSpec((LRU_CONV, D_LRU), lambda b, t: (0, 0)),
            vec, mat, vec, mat, vec, vec, vec,
            pl.BlockSpec((None, D_LRU, D_MODEL), lambda b, t: (layer, 0, 0)),
            pl.BlockSpec((None, D_RWKV, D_MODEL), lambda b, t: (layer, 1, 0)),
        ],
        out_specs=rows(D_MODEL, 0),
        out_shape=jax.ShapeDtypeStruct((T, D_MODEL), F32),
        scratch_shapes=[pltpu.VMEM((SUBLANES, D_LRU), F32),
                        pltpu.VMEM((SUBLANES, D_LRU), F32)],
        compiler_params=pltpu.CompilerParams(
            dimension_semantics=("arbitrary", "arbitrary"),
            vmem_limit_bytes=_vmem_limit(nbytes)),
        name="lru_oproj",
    )(zm, zm, out_b, h, cw, cb, wx, bx, wa, ba, lam, nrm, wo, wo)


def _mm(a, b):
    return jnp.dot(a.astype(BF16), b.astype(BF16), preferred_element_type=F32)


def _mm_nt(a, b):
    return lax.dot_general(a.astype(BF16), b.astype(BF16), (((1,), (1,)), ((), ())),
                           preferred_element_type=F32)


def _mm_tn(a, b):
    return lax.dot_general(a.astype(BF16), b.astype(BF16), (((0,), (0,)), ((), ())),
                           preferred_element_type=F32)


def _split3(x):
    hi = x.astype(BF16)
    r1 = x - hi.astype(F32)
    mid = r1.astype(BF16)
    lo = (r1 - mid.astype(F32)).astype(BF16)
    return hi, mid, lo


def _seg_sum(x):
    lane_lo = lax.broadcasted_iota(jnp.int32, (1, PAIR), 1) < HEAD
    out = []
    for p in range(x.shape[1] // PAIR):
        t = x[:, p * PAIR:(p + 1) * PAIR]
        s0 = jnp.sum(jnp.where(lane_lo, t, 0.0), axis=-1, keepdims=True)
        s1 = jnp.sum(jnp.where(lane_lo, 0.0, t), axis=-1, keepdims=True)
        out.append(jnp.where(lane_lo, s0, s1))
    return jnp.concatenate(out, axis=-1)


def _rwkv_kernel(has_vres, *refs):
    if has_vres:
        (r_ref, k_ref, v_ref, zl_ref, vf_ref, mur_ref, muk_ref, muv_ref, mul_ref,
         w0_ref, w2_ref, a0_ref, a2_ref, g2_ref, v0_ref, v2_ref,
         kkw_ref, ka_ref, rk_ref, lnw_ref, lnb_ref, o_ref, s_ref, prev_ref, prevz_ref) = refs
        vfo_ref = None
    else:
        (r_ref, k_ref, v_ref, zl_ref, mur_ref, muk_ref, muv_ref, mul_ref,
         w0_ref, w2_ref, a0_ref, a2_ref, g2_ref,
         kkw_ref, ka_ref, rk_ref, lnw_ref, lnb_ref, o_ref, vfo_ref,
         s_ref, prev_ref, prevz_ref) = refs
        vf_ref = v0_ref = v2_ref = None

    C = CHUNK
    nchunk = r_ref.shape[0] // C
    nh = r_ref.shape[1] // HEAD

    @pl.when(pl.program_id(2) == 0)
    def _():
        s_ref[...] = jnp.zeros_like(s_ref)
        prev_ref[...] = jnp.zeros_like(prev_ref)
        prevz_ref[...] = jnp.zeros_like(prevz_ref)

    ri3 = lax.broadcasted_iota(jnp.int32, (C, 3 * C), 0)
    ci3 = lax.broadcasted_iota(jnp.int32, (C, 3 * C), 1) % C
    tri3 = (ri3 >= ci3).astype(BF16)
    ri2 = lax.broadcasted_iota(jnp.int32, (2 * C, 2 * C), 0)
    ci2 = lax.broadcasted_iota(jnp.int32, (2 * C, 2 * C), 1) % C
    keep2 = jnp.where(ri2 < C, ri2, ri2 - C + 1) > ci2
    row0 = lax.broadcasted_iota(jnp.int32, (C, 1), 0) == 0
    zeros_h = jnp.zeros((C, HEAD), BF16)

    def shift_lerp(cur, prev_row, mu):
        sh = jnp.where(row0, prev_row, pltpu.roll(cur, 1, axis=0))
        return cur + (sh - cur) * mu

    def chunk_rows(c):
        return pl.ds(pl.multiple_of(c * C, C), C)

    def prep(c, out):
        rows = chunk_rows(c)
        first = c == 0
        before = pl.ds(jnp.maximum(c * C - 1, 0), 1)

        def lerp(ref, carried, mu):
            return shift_lerp(ref[rows, :], jnp.where(first, carried, ref[before, :]), mu)

        zl = lerp(zl_ref, prevz_ref[0:1, :], mul_ref[...])
        z01 = zl[:, 0:LANES]
        wpre = w0_ref[...] + _mm(jnp.tanh(z01), w2_ref[...])
        apre = a0_ref[...] + _mm(z01, a2_ref[...])
        g = _mm(_sigmoid(zl[:, LANES:3 * LANES]), g2_ref[...])
        if has_vres:
            mpre = v0_ref[...] + _mm(zl[:, 2 * LANES:3 * LANES], v2_ref[...])
        yield
        w_log = -_softplus(-wpre) - 0.5
        logw = -jnp.exp(w_log)
        cum = jnp.dot(tri3, jnp.concatenate(_split3(logw), axis=0), preferred_element_type=F32)
        yield
        r = lerp(r_ref, prev_ref[0:1, :], mur_ref[...])
        k = lerp(k_ref, prev_ref[1:2, :], muk_ref[...])
        a = _sigmoid(apre)
        yield
        v = lerp(v_ref, prev_ref[2:3, :], muv_ref[...])
        if has_vres:
            v = v + (vf_ref[rows, :] - v) * _sigmoid(mpre)
        else:
            vfo_ref[rows, :] = v
        yield
        kk = k * kkw_ref[...]
        kk = kk / jnp.maximum(jnp.sqrt(_seg_sum(kk * kk)), 1e-12)
        yield
        k2 = k * (1.0 + (a - 1.0) * ka_ref[...])
        bb = kk * a
        bonus = _seg_sum(r * k2 * rk_ref[...]) * v
        yield
        p_in = jnp.exp(cum)
        p_ex = jnp.exp(cum - logw)
        p_inv = jnp.exp(-cum)
        p_end = p_in[C - 1:C, :]
        yield
        rt = r * p_in
        at = -kk * p_ex
        rt_b, at_b, v_b = rt.astype(BF16), at.astype(BF16), v.astype(BF16)
        yield
        bt = bb * p_inv
        kt = k2 * p_inv
        bk_t = jnp.transpose(jnp.concatenate([bt, kt], axis=0)).astype(BF16)
        yield
        bhat_b = (bt * p_end).astype(BF16)
        khat_b = (kt * p_end).astype(BF16)
        out["local"] = (rt_b, at_b, bk_t, v_b, bhat_b, khat_b, at, rt, p_end)
        out["post"] = (bonus, g)

    def heads_stage(local, out):
        rt_b, at_b, bk_t, v_b, bhat_b, khat_b, at, rt, p_end = local
        heads = range(nh)
        sls = [slice(hh * HEAD, (hh + 1) * HEAD) for hh in heads]
        sc_b, m, vh_b, d = [], [], [], []
        for sl in sls:
            ar = jnp.concatenate([at_b[:, sl], rt_b[:, sl]], axis=0)
            sc = jnp.dot(ar, bk_t[sl, :], preferred_element_type=F32)
            sc = jnp.where(keep2, sc, 0.0)
            sc_b.append(sc.astype(BF16))
            m.append(sc[:C, :C])
            vh_b.append(v_b[:, sl])
        yield
        for hh in heads:
            zv = jnp.concatenate([zeros_h, vh_b[hh]], axis=0)
            x_loc = jnp.dot(sc_b[hh][:C], zv, preferred_element_type=F32)
            d.append(jnp.concatenate([at[:, sls[hh]], x_loc], axis=-1))
        yield
        nstep = int(math.log2(C))
        for i in range(nstep):
            lo = (2 ** i // ROW_TILE) * ROW_TILE
            for hh in heads:
                m_b = m[hh].astype(BF16)[lo:, :C - lo]
                d_b = d[hh].astype(BF16)[:C - lo]
                if i + 1 < nstep:
                    rhs = jnp.concatenate([d_b, m[hh].astype(BF16)[:C - lo]], axis=-1)
                    prod = jnp.dot(m_b, rhs, preferred_element_type=F32)
                    upd, m_new = prod[:, :2 * HEAD], prod[:, 2 * HEAD:]
                    if lo:
                        m_new = jnp.concatenate([jnp.zeros((lo, C), F32), m_new], axis=0)
                    m[hh] = m_new
                else:
                    upd = jnp.dot(m_b, d_b, preferred_element_type=F32)
                if lo:
                    upd = jnp.concatenate([jnp.zeros((lo, 2 * HEAD), F32), upd], axis=0)
                d[hh] = d[hh] + upd
            yield
        o1, wz = [], []
        for hh in heads:
            gmat = jnp.concatenate(
                [d[hh].astype(BF16), jnp.concatenate([zeros_h, vh_b[hh]], axis=-1)], axis=0)
            o1.append(jnp.dot(sc_b[hh][C:], gmat, preferred_element_type=F32))
            bkh = jnp.concatenate([bhat_b[:, sls[hh]], khat_b[:, sls[hh]]], axis=0)
            wz.append(_mm_tn(gmat, bkh))
        yield
        ys = []
        for hh in heads:
            rbar = rt[:, sls[hh]] + o1[hh][:, :HEAD]
            st = s_ref[hh]
            ys.append(_mm_nt(rbar, st) + o1[hh][:, HEAD:])
            s_ref[hh] = st * p_end[:, sls[hh]] + _mm(st, wz[hh][:HEAD]) + wz[hh][HEAD:]
        out["y"] = jnp.concatenate(ys, axis=-1)

    def tail(c, y, post):
        bonus, g = post
        mean = _seg_sum(y) * (1.0 / HEAD)
        yc = y - mean
        yield
        var = _seg_sum(yc * yc) * (1.0 / HEAD)
        yield
        yn = yc * lax.rsqrt(var + LNX_EPS) * lnw_ref[...] + lnb_ref[...]
        o_ref[chunk_rows(c), :] = ((yn + bonus) * g).astype(o_ref.dtype)

    def run_interleaved(*gens):
        alive = list(gens)
        while alive:
            for gen in list(alive):
                if next(gen, "done") == "done":
                    alive.remove(gen)

    def body(i, carry):
        local, y_prev, post_prev, post_cur = carry
        out = {}
        run_interleaved(heads_stage(local, out),
                        tail(jnp.maximum(i - 1, 0), y_prev, post_prev),
                        prep(jnp.minimum(i + 1, nchunk - 1), out))
        return out["local"], out["y"], post_cur, out["post"]

    first = {}
    run_interleaved(prep(0, first))
    zeros_w = jnp.zeros((C, r_ref.shape[1]), F32)
    _, y_last, post_last, _ = lax.fori_loop(
        0, nchunk, body, (first["local"], zeros_w, (zeros_w, zeros_w), first["post"]),
        unroll=2)
    run_interleaved(tail(nchunk - 1, y_last, post_last))

    last = pl.ds(r_ref.shape[0] - 1, 1)
    prev_ref[0:1, :] = r_ref[last, :]
    prev_ref[1:2, :] = k_ref[last, :]
    prev_ref[2:3, :] = v_ref[last, :]
    prevz_ref[0:1, :] = zl_ref[last, :]


def _rwkv(zm, zl, vfirst, prm, batch, seq, hg=HEAD_GROUP, ts=RWKV_ROWS):
    T = zm.shape[0]
    W = hg * HEAD
    ng = D_RWKV // W
    nt = seq // ts
    col0 = 2 * D_LRU // W
    has_vres = vfirst is not None

    def col(off):
        return pl.BlockSpec((ts, W), lambda b, g, t: (b * nt + t, off + g))

    vecg = pl.BlockSpec((1, W), lambda b, g, t: (0, g))
    vec_k = pl.BlockSpec((1, W), lambda b, g, t: (0, ng + g))
    vec_v = pl.BlockSpec((1, W), lambda b, g, t: (0, 2 * ng + g))
    vec_l = pl.BlockSpec((1, D_LORA), lambda b, g, t: (0, 0))

    def lora(rows):
        return pl.BlockSpec((rows, W), lambda b, g, t: (0, g))

    in_specs = [col(col0), col(col0 + ng), col(col0 + 2 * ng),
                pl.BlockSpec((ts, D_LORA), lambda b, g, t: (b * nt + t, 0))]
    args = [zm, zm, zm, zl]
    if has_vres:
        in_specs.append(col(0))
        args.append(vfirst)
    in_specs += [vecg, vec_k, vec_v, vec_l, vecg, lora(LANES), vecg, lora(LANES), lora(2 * LANES)]
    args += [prm["mu_rkv"], prm["mu_rkv"], prm["mu_rkv"], prm["mu_lora"],
             prm["w0"], prm["w2"], prm["a0"], prm["a2"], prm["g2"]]
    if has_vres:
        in_specs += [vecg, lora(LANES)]
        args += [prm["v0"], prm["v2"]]
    in_specs += [vecg] * 5
    args += [prm["kk"], prm["ka"], prm["rk"], prm["lnw"], prm["lnb"]]

    if has_vres:
        out_specs = col(0)
        out_shape = jax.ShapeDtypeStruct((T, D_RWKV), BF16)
    else:
        out_specs = [col(0), col(0)]
        out_shape = [jax.ShapeDtypeStruct((T, D_RWKV), BF16),
                     jax.ShapeDtypeStruct((T, D_RWKV), F32)]
    nbytes = 2 * ts * (5 * W * 4 + D_LORA * 4 + W * 2) + hg * HEAD * HEAD * 4
    res = pl.pallas_call(
        functools.partial(_rwkv_kernel, has_vres),
        grid=(batch, ng, nt),
        in_specs=in_specs,
        out_specs=out_specs,
        out_shape=out_shape,
        scratch_shapes=[pltpu.VMEM((hg, HEAD, HEAD), F32),
                        pltpu.VMEM((SUBLANES, W), F32),
                        pltpu.VMEM((SUBLANES, D_LORA), F32)],
        compiler_params=pltpu.CompilerParams(
            dimension_semantics=("arbitrary", "arbitrary", "arbitrary"),
            vmem_limit_bytes=_vmem_limit(nbytes)),
        name="rwkv7",
    )(*args)
    if has_vres:
        return res, vfirst
    return res[0], res[1]


def _ffn_kernel(tiles_per_seq, h_ref, g_ref, wg_ref, wu_ref, cw_ref, cb_ref, wd_ref,
                o_ref, u_ref, tail_ref):
    i = pl.program_id(0)
    j = pl.program_id(1)

    @pl.when(j == 0)
    def _():
        h = h_ref[...]
        u_ref[...] = _rms(h, g_ref[...]).astype(BF16)
        o_ref[...] = h

    @pl.when(i % tiles_per_seq == 0)
    def _():
        tail_ref[j] = jnp.zeros(tail_ref.shape[1:], F32)

    u = u_ref[...]
    gate = jnp.dot(u, wg_ref[...], preferred_element_type=F32)
    tm = gate.shape[0]
    tail = tail_ref[j]
    cw = cw_ref[...]
    conv = gate * cw[FFN_CONV - 1:FFN_CONV] + cb_ref[...]
    for d in range(1, FFN_CONV):
        conv = conv + _shift_rows(gate, d, tail) * cw[FFN_CONV - 1 - d:FFN_CONV - d]
    tail_ref[j] = gate[tm - SUBLANES:]
    up = jnp.dot(u, wu_ref[...], preferred_element_type=F32)
    act = (_gelu(conv) * up).astype(BF16)
    o_ref[...] += jnp.dot(act, wd_ref[...], preferred_element_type=F32)


def _ffn(h, g, wg, wu, cw, cb, wd, layer, seq, tm=512, tf=1024):
    T = h.shape[0]
    nf = D_FF // tf
    nbytes = (2 * (2 * tm * D_MODEL * 4 + 3 * D_MODEL * tf * 2) + tm * D_MODEL * 2
              + nf * SUBLANES * tf * 4 + 6 * tm * tf * 4)
    return pl.pallas_call(
        functools.partial(_ffn_kernel, seq // tm),
        grid=(T // tm, nf),
        in_specs=[
            pl.BlockSpec((tm, D_MODEL), lambda i, j: (i, 0)),
            pl.BlockSpec((1, D_MODEL), lambda i, j: (0, 0)),
            pl.BlockSpec((None, D_MODEL, tf), lambda i, j: (layer, 0, j)),
            pl.BlockSpec((None, D_MODEL, tf), lambda i, j: (layer, 0, j)),
            pl.BlockSpec((FFN_CONV, tf), lambda i, j: (0, j)),
            pl.BlockSpec((1, tf), lambda i, j: (0, j)),
            pl.BlockSpec((None, tf, D_MODEL), lambda i, j: (layer, j, 0)),
        ],
        out_specs=pl.BlockSpec((tm, D_MODEL), lambda i, j: (i, 0)),
        out_shape=jax.ShapeDtypeStruct((T, D_MODEL), F32),
        scratch_shapes=[pltpu.VMEM((tm, D_MODEL), BF16),
                        pltpu.VMEM((nf, SUBLANES, tf), F32)],
        compiler_params=pltpu.CompilerParams(
            dimension_semantics=("arbitrary", "arbitrary"),
            vmem_limit_bytes=_vmem_limit(nbytes)),
        name="ffn",
    )(h, g, wg, wu, cw, cb, wd)


def _ple_kernel(final, h_ref, p_ref, g_ref, wg_ref, wp_ref, gp_ref, gf_ref, o_ref):
    h = h_ref[...]
    u = _rms(h, g_ref[...]).astype(BF16)
    gate = _sigmoid(jnp.dot(u, wg_ref[...], preferred_element_type=F32))
    proj = jnp.dot(p_ref[...].astype(BF16), wp_ref[...], preferred_element_type=F32)
    out = h + _rms(gate * proj, gp_ref[...])
    if final:
        out = _rms(out, gf_ref[...])
    o_ref[...] = out


def _ple(h, p, g, wg, layer, wp, gp, gf, final, tm=512):
    T = h.shape[0]
    vec = pl.BlockSpec((1, D_MODEL), lambda i: (0, 0))
    nbytes = 2 * (2 * tm * D_MODEL * 4 + tm * D_PLE * 4 + D_MODEL * D_MODEL * 2
                  + D_PLE * D_MODEL * 2) + 4 * tm * D_MODEL * 4
    return pl.pallas_call(
        functools.partial(_ple_kernel, final),
        grid=(T // tm,),
        in_specs=[
            pl.BlockSpec((tm, D_MODEL), lambda i: (i, 0)),
            pl.BlockSpec((None, tm, D_PLE), lambda i: (layer, i, 0)),
            vec,
            pl.BlockSpec((None, D_MODEL, D_MODEL), lambda i: (layer, 0, 0)),
            pl.BlockSpec((D_PLE, D_MODEL), lambda i: (0, 0)),
            vec, vec,
        ],
        out_specs=pl.BlockSpec((tm, D_MODEL), lambda i: (i, 0)),
        out_shape=jax.ShapeDtypeStruct((T, D_MODEL), F32),
        compiler_params=pltpu.CompilerParams(
            dimension_semantics=("arbitrary",),
            vmem_limit_bytes=_vmem_limit(nbytes)),
        name="ple",
    )(h, p, g, wg, wp, gp, gf)


def _row(v):
    return v.reshape(1, -1).astype(F32)


def _pad_rows(w, top, total):
    return jnp.pad(w, ((top, total - top - w.shape[0]), (0, 0)))


def kernel(x, p, ln_mix, w_in, w_in_vres, mu_shift, mu_shift_vres, conv_a_w, conv_a_b, lru_wx, lru_bx, lru_wa, lru_ba, lru_lambda, lru_norm, rwkv_w0, rwkv_w2, rwkv_a0, rwkv_a2, rwkv_v0, rwkv_v2, rwkv_g2, rwkv_kk, rwkv_ka, rwkv_rk, rwkv_lnx_w, rwkv_lnx_b, w_o, ln_ffn, w_gate, w_up, conv_f_w, conv_f_b, w_down, ln_ple, w_ple_gate, w_ple_proj, ln_ple_post, ln_final):
    batch, seq, _ = x.shape
    depth = w_in.shape[0]
    T = batch * seq
    h = x.reshape(T, D_MODEL)
    n_lora = LORA_W + LORA_A + LORA_G
    vfirst = None
    w_in_b, w_o_b, w_gate_b, w_up_b, w_down_b, w_ple_gate_b = (
        w.astype(BF16) for w in (w_in, w_o, w_gate, w_up, w_down, w_ple_gate))
    for i in range(depth):
        lora_cols = [w_in_b[i, :, D_MAIN:]]
        mu_l = [mu_shift[i][3 * D_RWKV:]]
        if i > 0:
            lora_cols.append(w_in_vres[i - 1].astype(BF16))
            mu_l.append(mu_shift_vres[i - 1])
        w_lora = jnp.concatenate(lora_cols, axis=1)
        w_lora = jnp.pad(w_lora, ((0, 0), (0, D_LORA - w_lora.shape[1])))
        mu_lora = jnp.concatenate(mu_l, axis=0)
        mu_lora = jnp.pad(mu_lora, (0, D_LORA - mu_lora.shape[0]))

        zm, zl = _inproj(h, _row(ln_mix[i]), w_in_b, i, w_lora)

        prm = {
            "mu_rkv": _row(mu_shift[i][:3 * D_RWKV]),
            "mu_lora": _row(mu_lora),
            "w0": _row(rwkv_w0[i]),
            "w2": _pad_rows(rwkv_w2[i], 0, LANES),
            "a0": _row(rwkv_a0[i]),
            "a2": _pad_rows(rwkv_a2[i], LORA_W, LANES),
            "g2": _pad_rows(rwkv_g2[i], 0, 2 * LANES),
            "kk": _row(rwkv_kk[i]), "ka": _row(rwkv_ka[i]), "rk": _row(rwkv_rk[i]),
            "lnw": _row(rwkv_lnx_w[i]), "lnb": _row(rwkv_lnx_b[i]),
        }
        if i > 0:
            prm["v0"] = _row(rwkv_v0[i - 1])
            prm["v2"] = _pad_rows(rwkv_v2[i - 1], n_lora - 2 * LANES, LANES)
        out_b, vfirst = _rwkv(zm, zl, vfirst, prm, batch, seq)

        h = _lru_oproj(zm, out_b, h, conv_a_w[i], _row(conv_a_b[i]), lru_wx[i].astype(BF16),
                       _row(lru_bx[i]), lru_wa[i].astype(BF16), _row(lru_ba[i]),
                       _row(lru_lambda[i]), _row(lru_norm[i]), w_o_b, i, batch, seq)
        h = _ffn(h, _row(ln_ffn[i]), w_gate_b, w_up_b, conv_f_w[i], _row(conv_f_b[i]),
                 w_down_b, i, seq)
        h = _ple(h, p.reshape(depth, T, D_PLE), _row(ln_ple[i]), w_ple_gate_b, i,
                 w_ple_proj[i].astype(BF16), _row(ln_ple_post[i]), _row(ln_final),
                 final=(i == depth - 1))
    return h.reshape(batch, seq, D_MODEL)
```
